```python
import math
import jax, jax.numpy as jnp
from jax import lax
import numpy as np

D_MODEL = 1024
BATCH = 1
SEQ = 16384
DEPTH = 4

N_MIXERS = 4
HEAD_DIM = 64
N_HEADS = D_MODEL // HEAD_DIM
D_FF = 4 * D_MODEL
Q_BLOCK = 128
ROPE_THETA = 500000.0
ROT_DIM = HEAD_DIM // 4
RMS_EPS = 1e-6
NEG_INF = -1e30
FORCE_SCORE = 1e9
POOL_WINDOWS = (2, 4, 8, 16)
POOL_GROUP = D_MODEL // len(POOL_WINDOWS)
CONV_WIDTH = 3
NSA_KV_HEADS = 4
NSA_GROUP = N_HEADS // NSA_KV_HEADS
CMP_BLOCK = 32
CMP_STRIDE = 16
SLC_BLOCK = 64
SLC_TOPK = 16
WIN = 512
CMP_HIDDEN = 256

kernel_name = "hybrid_fox_pool_conv_nsa_trunk"


def rmsnorm(x, g):
    xf = x.astype(jnp.float32)
    y = xf * lax.rsqrt(jnp.mean(xf * xf, axis=-1, keepdims=True) + RMS_EPS)
    return (y * g.astype(jnp.float32)).astype(x.dtype)


def rope_tables(positions):
    inv = ROPE_THETA ** (-jnp.arange(0, ROT_DIM, 2, dtype=jnp.float32) / ROT_DIM)
    ang = positions.astype(jnp.float32)[:, None] * inv[None, :]
    return jnp.cos(ang), jnp.sin(ang)


def apply_partial_rope(x, cos, sin):
    half = ROT_DIM // 2
    x1, x2, xp = x[..., :half], x[..., half:ROT_DIM], x[..., ROT_DIM:]
    c = cos[None, :, None, :].astype(x.dtype)
    s = sin[None, :, None, :].astype(x.dtype)
    return jnp.concatenate([x1 * c - x2 * s, x2 * c + x1 * s, xp], axis=-1)


def sq_relu_mlp(h, w1, w2):
    return jnp.square(jax.nn.relu(h @ w1)) @ w2


def fox_attention(h, w_qkv, w_f, b_f, w_o):
    B, S, D = h.shape
    q, k, v = jnp.split(h @ w_qkv, 3, axis=-1)
    q = q.reshape(B, S, N_HEADS, HEAD_DIM)
    k = k.reshape(B, S, N_HEADS, HEAD_DIM)
    v = v.reshape(B, S, N_HEADS, HEAD_DIM)
    log_f = jax.nn.log_sigmoid((h @ w_f + b_f).astype(jnp.float32))
    cum = jnp.cumsum(log_f, axis=1)
    cum_t = cum.transpose(0, 2, 1)
    nblk = S // Q_BLOCK
    scale = HEAD_DIM ** -0.5
    key_pos = jnp.arange(S, dtype=jnp.int32)
    q_blocks = q.reshape(B, nblk, Q_BLOCK, N_HEADS, HEAD_DIM).swapaxes(0, 1)
    c_blocks = cum.reshape(B, nblk, Q_BLOCK, N_HEADS).swapaxes(0, 1)
    starts = jnp.arange(nblk, dtype=jnp.int32) * Q_BLOCK

    def block(args):
        q_i, c_i, s0 = args
        logits = jnp.einsum('bqhd,bkhd->bhqk', q_i, k, preferred_element_type=jnp.float32) * scale
        logits = logits + (c_i.transpose(0, 2, 1)[..., :, None] - cum_t[..., None, :])
        qpos = s0 + jnp.arange(Q_BLOCK, dtype=jnp.int32)
        mask = key_pos[None, :] <= qpos[:, None]
        p = jax.nn.softmax(jnp.where(mask, logits, NEG_INF), axis=-1)
        return jnp.einsum('bhqk,bkhd->bqhd', p.astype(v.dtype), v)

    o = lax.map(block, (q_blocks, c_blocks, starts))
    return o.swapaxes(0, 1).reshape(B, S, D) @ w_o


def pool_mixer(h, w_pool, pool_scale):
    B, S, D = h.shape
    hf = h.astype(jnp.float32)
    cs = jnp.concatenate([jnp.zeros((B, 1, D), jnp.float32), jnp.cumsum(hf, axis=1)], axis=1)
    t = jnp.arange(S, dtype=jnp.int32)
    diffs = []
    for g, w in enumerate(POOL_WINDOWS):
        sl = slice(g * POOL_GROUP, (g + 1) * POOL_GROUP)
        lo = jnp.maximum(t + 1 - w, 0)
        cnt = (t + 1 - lo).astype(jnp.float32)
        mean = (cs[:, 1:, sl] - cs[:, lo, sl]) / cnt[None, :, None]
        diffs.append(mean - hf[..., sl])
    d = jnp.stack(diffs, axis=2).astype(h.dtype)
    y = jnp.einsum('bsgi,gio->bsgo', d, w_pool).reshape(B, S, D)
    return y * pool_scale


def short_conv_mixer(h, w_in, conv_w, w_out):
    B, S, D = h.shape
    b_gate, c_gate, u = jnp.split(h @ w_in, 3, axis=-1)
    z = c_gate * u
    zp = jnp.pad(z, ((0, 0), (CONV_WIDTH - 1, 0), (0, 0)))
    conv = conv_w[0] * zp[:, 0:S]
    for j in range(1, CONV_WIDTH):
        conv = conv + conv_w[j] * zp[:, j:j + S]
    return (b_gate * conv) @ w_out


def nsa_attention(h, cos, sin, w_in, pe_k, w1_k, w2_k, pe_v, w1_v, w2_v, w_o):
    B, S, D = h.shape
    qd = N_HEADS * HEAD_DIM
    kvd = NSA_KV_HEADS * HEAD_DIM
    proj = h @ w_in
    q = proj[..., :qd].reshape(B, S, N_HEADS, HEAD_DIM)
    kv = proj[..., qd:qd + 6 * kvd].reshape(B, S, 6, NSA_KV_HEADS, HEAD_DIM)
    gates = jax.nn.sigmoid(proj[..., qd + 6 * kvd:].astype(jnp.float32)).reshape(B, S, 3, NSA_KV_HEADS, NSA_GROUP)
    k_cmp_raw, v_cmp_raw = kv[:, :, 0], kv[:, :, 1]
    k_slc, v_slc = apply_partial_rope(kv[:, :, 2], cos, sin), kv[:, :, 3]
    k_win, v_win = apply_partial_rope(kv[:, :, 4], cos, sin), kv[:, :, 5]
    q = apply_partial_rope(q, cos, sin).reshape(B, S, NSA_KV_HEADS, NSA_GROUP, HEAD_DIM)

    n_cmp = (S - CMP_BLOCK) // CMP_STRIDE + 1
    cmp_idx = np.arange(n_cmp)[:, None] * CMP_STRIDE + np.arange(CMP_BLOCK)[None, :]
    cmp_end = cmp_idx[:, -1]

    def compress(raw, pe, w1, w2):
        blocks = raw[:, cmp_idx] + pe[None, None, :, None, :]
        flat = blocks.transpose(0, 1, 3, 2, 4).reshape(B, n_cmp, NSA_KV_HEADS, CMP_BLOCK * HEAD_DIM)
        return jax.nn.gelu(flat @ w1) @ w2

    k_c = apply_partial_rope(compress(k_cmp_raw, pe_k, w1_k, w2_k), cos[cmp_end], sin[cmp_end])
    v_c = compress(v_cmp_raw, pe_v, w1_v, w2_v)
    cmp_end_j = jnp.asarray(cmp_end, dtype=jnp.int32)

    n_slc = S // SLC_BLOCK
    n_sel = min(SLC_TOPK, n_slc)
    slc_start = np.arange(n_slc) * SLC_BLOCK
    ci = np.arange(n_cmp)[:, None] * CMP_STRIDE
    overlap = jnp.asarray(((ci < slc_start[None, :] + SLC_BLOCK) & (ci + CMP_BLOCK > slc_start[None, :])).astype(np.float32))
    ks_blk = k_slc.reshape(B, n_slc, SLC_BLOCK, NSA_KV_HEADS, HEAD_DIM).transpose(0, 3, 1, 2, 4)
    vs_blk = v_slc.reshape(B, n_slc, SLC_BLOCK, NSA_KV_HEADS, HEAD_DIM).transpose(0, 3, 1, 2, 4)
    b_ix = jnp.arange(B)[:, None, None, None]
    h_ix = jnp.arange(NSA_KV_HEADS)[None, None, :, None]
    blk_ids = jnp.arange(n_slc, dtype=jnp.int32)[None, :]

    pad = jnp.zeros((B, WIN, NSA_KV_HEADS, HEAD_DIM), k_win.dtype)
    k_win_p = jnp.concatenate([pad, k_win], axis=1)
    v_win_p = jnp.concatenate([pad, v_win], axis=1)

    nblk = S // Q_BLOCK
    scale = HEAD_DIM ** -0.5
    q_blocks = q.reshape(B, nblk, Q_BLOCK, NSA_KV_HEADS, NSA_GROUP, HEAD_DIM).swapaxes(0, 1)
    g_blocks = gates.reshape(B, nblk, Q_BLOCK, 3, NSA_KV_HEADS, NSA_GROUP).swapaxes(0, 1)
    starts = jnp.arange(nblk, dtype=jnp.int32) * Q_BLOCK

    def block(args):
        q_i, g_i, s0 = args
        qpos = s0 + jnp.arange(Q_BLOCK, dtype=jnp.int32)
        sc = jnp.einsum('bqkgd,bnkd->bqkgn', q_i, k_c, preferred_element_type=jnp.float32) * scale
        cmask = cmp_end_j[None, :] <= qpos[:, None]
        sc = jnp.where(cmask[None, :, None, None, :], sc, NEG_INF)
        p_c = jnp.where(cmask.any(-1)[None, :, None, None, None], jax.nn.softmax(sc, axis=-1), 0.0)
        o_c = jnp.einsum('bqkgn,bnkd->bqkgd', p_c.astype(v_c.dtype), v_c)
        imp = jnp.einsum('bqkgn,nj->bqkj', p_c, overlap)
        cur = (qpos // SLC_BLOCK)[:, None]
        forced = (blk_ids == 0) | (blk_ids == cur) | (blk_ids == cur - 1)
        valid = blk_ids * SLC_BLOCK <= qpos[:, None]
        imp = jnp.where(forced[None, :, None, :], FORCE_SCORE, imp)
        imp = jnp.where(valid[None, :, None, :], imp, NEG_INF)
        top_val, top_idx = lax.top_k(imp, n_sel)
        ks = ks_blk[b_ix, h_ix, top_idx]
        vs = vs_blk[b_ix, h_ix, top_idx]
        ss = jnp.einsum('bqkgd,bqknld->bqkgnl', q_i, ks, preferred_element_type=jnp.float32) * scale
        tok = top_idx[..., None] * SLC_BLOCK + jnp.arange(SLC_BLOCK, dtype=jnp.int32)
        smask = (top_val > 0.5 * NEG_INF)[..., None] & (tok <= qpos[None, :, None, None, None])
        ss = jnp.where(smask[:, :, :, None], ss, NEG_INF).reshape(B, Q_BLOCK, NSA_KV_HEADS, NSA_GROUP, n_sel * SLC_BLOCK)
        p_s = jax.nn.softmax(ss, axis=-1)
        o_s = jnp.einsum('bqkgm,bqkmd->bqkgd', p_s.astype(vs.dtype),
                         vs.reshape(B, Q_BLOCK, NSA_KV_HEADS, n_sel * SLC_BLOCK, HEAD_DIM))
        kw = lax.dynamic_slice_in_dim(k_win_p, s0, WIN + Q_BLOCK, axis=1)
        vw = lax.dynamic_slice_in_dim(v_win_p, s0, WIN + Q_BLOCK, axis=1)
        kpos = s0 - WIN + jnp.arange(WIN + Q_BLOCK, dtype=jnp.int32)
        wmask = (kpos[None, :] <= qpos[:, None]) & (kpos[None, :] > qpos[:, None] - WIN) & (kpos[None, :] >= 0)
        sw = jnp.einsum('bqkgd,bmkd->bqkgm', q_i, kw, preferred_element_type=jnp.float32) * scale
        p_w = jax.nn.softmax(jnp.where(wmask[None, :, None, None, :], sw, NEG_INF), axis=-1)
        o_w = jnp.einsum('bqkgm,bmkd->bqkgd', p_w.astype(vw.dtype), vw)
        g = g_i.astype(o_c.dtype)[..., None]
        return g[:, :, 0] * o_c + g[:, :, 1] * o_s + g[:, :, 2] * o_w

    o = lax.map(block, (q_blocks, g_blocks, starts))
    return o.swapaxes(0, 1).reshape(B, S, qd) @ w_o


def setup_inputs(seed: int = 0) -> dict:
    key = jax.random.key(seed)
    ks = iter(jax.random.split(key, 64))

    def dense(shape, fan_in):
        return jax.random.normal(next(ks), shape, jnp.float32) * fan_in ** -0.5

    def gain():
        return 1.0 + 0.05 * jax.random.normal(next(ks), (D_MODEL,), jnp.float32)

    nsa_in_width = N_HEADS * HEAD_DIM + 6 * NSA_KV_HEADS * HEAD_DIM + 3 * N_HEADS
    return {
        "x": jax.random.normal(next(ks), (BATCH, SEQ, D_MODEL), jnp.float32),
        "positions": jnp.arange(SEQ, dtype=jnp.int32),
        "l0_norm_mix": gain(),
        "l0_fox_w_qkv": dense((D_MODEL, 3 * D_MODEL), D_MODEL),
        "l0_fox_w_f": dense((D_MODEL, N_HEADS), D_MODEL),
        "l0_fox_b_f": 4.0 + 0.5 * jax.random.normal(next(ks), (N_HEADS,), jnp.float32),
        "l0_fox_w_o": dense((D_MODEL, D_MODEL), D_MODEL),
        "l0_norm_mlp": gain(),
        "l0_mlp_w1": dense((D_MODEL, D_FF), D_MODEL),
        "l0_mlp_w2": dense((D_FF, D_MODEL), D_FF),
        "l1_norm_mix": gain(),
        "l1_pool_w": dense((len(POOL_WINDOWS), POOL_GROUP, POOL_GROUP), POOL_GROUP),
        "l1_pool_scale": 1.0 + 0.1 * jax.random.normal(next(ks), (D_MODEL,), jnp.float32),
        "l1_norm_mlp": gain(),
        "l1_mlp_w1": dense((D_MODEL, D_FF), D_MODEL),
        "l1_mlp_w2": dense((D_FF, D_MODEL), D_FF),
        "l2_norm_mix": gain(),
        "l2_conv_w_in": dense((D_MODEL, 3 * D_MODEL), D_MODEL),
        "l2_conv_w": dense((CONV_WIDTH, D_MODEL), CONV_WIDTH),
        "l2_conv_w_out": dense((D_MODEL, D_MODEL), D_MODEL),
        "l2_norm_mlp": gain(),
        "l2_mlp_w1": dense((D_MODEL, D_FF), D_MODEL),
        "l2_mlp_w2": dense((D_FF, D_MODEL), D_FF),
        "l3_norm_mix": gain(),
        "l3_nsa_w_in": dense((D_MODEL, nsa_in_width), D_MODEL),
        "l3_nsa_cmp_pe_k": 0.1 * jax.random.normal(next(ks), (CMP_BLOCK, HEAD_DIM), jnp.float32),
        "l3_nsa_cmp_w1_k": dense((CMP_BLOCK * HEAD_DIM, CMP_HIDDEN), CMP_BLOCK * HEAD_DIM),
        "l3_nsa_cmp_w2_k": dense((CMP_HIDDEN, HEAD_DIM), CMP_HIDDEN),
        "l3_nsa_cmp_pe_v": 0.1 * jax.random.normal(next(ks), (CMP_BLOCK, HEAD_DIM), jnp.float32),
        "l3_nsa_cmp_w1_v": dense((CMP_BLOCK * HEAD_DIM, CMP_HIDDEN), CMP_BLOCK * HEAD_DIM),
        "l3_nsa_cmp_w2_v": dense((CMP_HIDDEN, HEAD_DIM), CMP_HIDDEN),
        "l3_nsa_w_o": dense((N_HEADS * HEAD_DIM, D_MODEL), N_HEADS * HEAD_DIM),
        "l3_norm_mlp": gain(),
        "l3_mlp_w1": dense((D_MODEL, D_FF), D_MODEL),
        "l3_mlp_w2": dense((D_FF, D_MODEL), D_FF),
        "final_norm": gain(),
    }


def reference(x, positions,
              l0_norm_mix, l0_fox_w_qkv, l0_fox_w_f, l0_fox_b_f, l0_fox_w_o, l0_norm_mlp, l0_mlp_w1, l0_mlp_w2,
              l1_norm_mix, l1_pool_w, l1_pool_scale, l1_norm_mlp, l1_mlp_w1, l1_mlp_w2,
              l2_norm_mix, l2_conv_w_in, l2_conv_w, l2_conv_w_out, l2_norm_mlp, l2_mlp_w1, l2_mlp_w2,
              l3_norm_mix, l3_nsa_w_in, l3_nsa_cmp_pe_k, l3_nsa_cmp_w1_k, l3_nsa_cmp_w2_k,
              l3_nsa_cmp_pe_v, l3_nsa_cmp_w1_v, l3_nsa_cmp_w2_v, l3_nsa_w_o, l3_norm_mlp, l3_mlp_w1, l3_mlp_w2,
              final_norm):
    cos, sin = rope_tables(positions)
    mixers = (
        lambda t: fox_attention(t, l0_fox_w_qkv, l0_fox_w_f, l0_fox_b_f, l0_fox_w_o),
        lambda t: pool_mixer(t, l1_pool_w, l1_pool_scale),
        lambda t: short_conv_mixer(t, l2_conv_w_in, l2_conv_w, l2_conv_w_out),
        lambda t: nsa_attention(t, cos, sin, l3_nsa_w_in, l3_nsa_cmp_pe_k, l3_nsa_cmp_w1_k, l3_nsa_cmp_w2_k,
                                l3_nsa_cmp_pe_v, l3_nsa_cmp_w1_v, l3_nsa_cmp_w2_v, l3_nsa_w_o),
    )
    norm_mix = (l0_norm_mix, l1_norm_mix, l2_norm_mix, l3_norm_mix)
    norm_mlp = (l0_norm_mlp, l1_norm_mlp, l2_norm_mlp, l3_norm_mlp)
    mlp_w1 = (l0_mlp_w1, l1_mlp_w1, l2_mlp_w1, l3_mlp_w1)
    mlp_w2 = (l0_mlp_w2, l1_mlp_w2, l2_mlp_w2, l3_mlp_w2)
    h = x
    for i in range(DEPTH):
        h = h + mixers[i % N_MIXERS](rmsnorm(h, norm_mix[i]))
        h = h + sq_relu_mlp(rmsnorm(h, norm_mlp[i]), mlp_w1[i], mlp_w2[i])
    return rmsnorm(h, final_norm)
```

```python
import functools

import numpy as np
import jax
import jax.numpy as jnp
from jax import lax
from jax.experimental import pallas as pl
from jax.experimental.pallas import tpu as pltpu

F32 = jnp.float32
BF16 = jnp.bfloat16

D_MODEL = 1024
HEAD_DIM = 64
N_HEADS = D_MODEL // HEAD_DIM
D_FF = 4 * D_MODEL
ROPE_THETA = 500000.0
ROT_DIM = HEAD_DIM // 4
RMS_EPS = 1e-6
POOL_WINDOWS = (2, 4, 8, 16)
POOL_GROUP = D_MODEL // len(POOL_WINDOWS)
POOL_HALO = 16
CONV_WIDTH = 3
CONV_HALO = 8
NSA_KV_HEADS = 4
NSA_GROUP = N_HEADS // NSA_KV_HEADS
CMP_BLOCK = 32
CMP_STRIDE = 16
CMP_HIDDEN = 256
SLC_BLOCK = 64
SLC_SHIFT = 6
SLC_TOPK = 16
WIN = 512
FORCE_SCORE = 1e9
NEG = -1e30
LOG2E = 1.4426950408889634
QK_SCALE = HEAD_DIM ** -0.5 * LOG2E

LANES = 128
V7X_VMEM_LIMIT = 56 * 1024 * 1024
NSA_TILE = WIN // 2
NSA_PROJ_TILE = 2 * LANES
NSA_PROJ_WIDTH = 22 * LANES
NSA_ROPE_TILES = (0, 1, 2, 3, 6, 8)


def _cparams(semantics):
    return pltpu.CompilerParams(dimension_semantics=semantics, vmem_limit_bytes=V7X_VMEM_LIMIT)


def _rms(x, g):
    return x * lax.rsqrt(jnp.mean(x * x, axis=-1, keepdims=True) + RMS_EPS) * g


def _dot(a, b):
    return jnp.dot(a, b, preferred_element_type=F32)


def _dot_nt(a, b):
    return lax.dot_general(a, b, (((1,), (1,)), ((), ())), preferred_element_type=F32)


def _split3(x):
    hi = x.astype(BF16)
    r = x - hi.astype(F32)
    mid = r.astype(BF16)
    lo = (r - mid.astype(F32)).astype(BF16)
    return hi, mid, lo


def _rope(a, rc, rs1, rs2):
    return a * rc + pltpu.roll(a, LANES - ROT_DIM // 2, 1) * rs1 + pltpu.roll(a, ROT_DIM // 2, 1) * rs2


def _head_halves(q_blk):
    lane = lax.broadcasted_iota(jnp.int32, q_blk.shape, 1)
    qf = q_blk.astype(F32)
    return (jnp.where(lane < HEAD_DIM, qf, 0.0).astype(BF16),
            jnp.where(lane >= HEAD_DIM, qf, 0.0).astype(BF16))


def _online_update(s, v, m_ref, l_ref, acc_ref, idx):
    m_old = m_ref[idx]
    m_new = jnp.maximum(m_old, jnp.max(s, axis=1, keepdims=True))
    alpha = jnp.exp2(m_old - m_new)
    p = jnp.exp2(s - m_new)
    l_ref[idx] = alpha * l_ref[idx] + jnp.sum(p, axis=1, keepdims=True)
    acc_ref[idx] = alpha * acc_ref[idx] + _dot(p.astype(BF16), v)
    m_ref[idx] = m_new


def _mlp_kernel(x_ref, g_ref, w1_ref, w2_ref, *rest, nf, final):
    if final:
        fg_ref, o_ref, xn_ref = rest
    else:
        o_ref, xn_ref = rest
    f = pl.program_id(1)

    @pl.when(f == 0)
    def _():
        x = x_ref[...]
        xn_ref[...] = _rms(x, g_ref[...]).astype(BF16)
        o_ref[...] = x

    a = _dot(xn_ref[...], w1_ref[...])
    a = jnp.square(jnp.maximum(a, 0.0)).astype(BF16)
    o_ref[...] += _dot(a, w2_ref[...])

    if final:
        @pl.when(f == nf - 1)
        def _():
            o_ref[...] = _rms(o_ref[...], fg_ref[...])


def _mlp(h, g, w1, w2, final_g=None):
    S, D = h.shape
    F = w1.shape[1]
    tm = min(1024, S)
    tf = 1024
    nf = F // tf
    final = final_g is not None
    in_specs = [
        pl.BlockSpec((tm, D), lambda i, f: (i, 0)),
        pl.BlockSpec((1, D), lambda i, f: (0, 0)),
        pl.BlockSpec((D, tf), lambda i, f: (0, f)),
        pl.BlockSpec((tf, D), lambda i, f: (f, 0)),
    ]
    args = [h, g.reshape(1, D), w1, w2]
    if final:
        in_specs.append(pl.BlockSpec((1, D), lambda i, f: (0, 0)))
        args.append(final_g.reshape(1, D))
    return pl.pallas_call(
        functools.partial(_mlp_kernel, nf=nf, final=final),
        grid=(S // tm, nf),
        in_specs=in_specs,
        out_specs=pl.BlockSpec((tm, D), lambda i, f: (i, 0)),
        out_shape=jax.ShapeDtypeStruct((S, D), F32),
        scratch_shapes=[pltpu.VMEM((tm, D), BF16)],
        compiler_params=_cparams(("parallel", "arbitrary")),
        name="mlp",
    )(*args)


def _norm_matmul_kernel(x_ref, g_ref, w_ref, o_ref, xn_ref):
    @pl.when(pl.program_id(1) == 0)
    def _():
        xn_ref[...] = _rms(x_ref[...], g_ref[...]).astype(BF16)

    o_ref[...] = _dot(xn_ref[...], w_ref[...]).astype(o_ref.dtype)


def _norm_matmul(h, g, w, name):
    S, D = h.shape
    N = w.shape[1]
    tm = min(1024, S)
    tn = 512
    return pl.pallas_call(
        _norm_matmul_kernel,
        grid=(S // tm, N // tn),
        in_specs=[
            pl.BlockSpec((tm, D), lambda i, j: (i, 0)),
            pl.BlockSpec((1, D), lambda i, j: (0, 0)),
            pl.BlockSpec((D, tn), lambda i, j: (0, j)),
        ],
        out_specs=pl.BlockSpec((tm, tn), lambda i, j: (i, j)),
        out_shape=jax.ShapeDtypeStruct((S, N), BF16),
        scratch_shapes=[pltpu.VMEM((tm, D), BF16)],
        compiler_params=_cparams(("parallel", "arbitrary")),
        name=name,
    )(h, g.reshape(1, D), w)


def _matmul_res_kernel(a_ref, w_ref, r_ref, o_ref):
    o_ref[...] = r_ref[...] + _dot(a_ref[...], w_ref[...])


def _matmul_res(a, w, res, name):
    S, K = a.shape
    N = w.shape[1]
    tm = min(1024, S)
    return pl.pallas_call(
        _matmul_res_kernel,
        grid=(S // tm,),
        in_specs=[
            pl.BlockSpec((tm, K), lambda i: (i, 0)),
            pl.BlockSpec((K, N), lambda i: (0, 0)),
            pl.BlockSpec((tm, N), lambda i: (i, 0)),
        ],
        out_specs=pl.BlockSpec((tm, N), lambda i: (i, 0)),
        out_shape=jax.ShapeDtypeStruct((S, N), F32),
        compiler_params=_cparams(("parallel",)),
        name=name,
    )(a, w, res)


def _fox_gate_kernel(x_ref, g_ref, wf_ref, bf_ref, c_ref, carry_ref, *, tm):
    @pl.when(pl.program_id(0) == 0)
    def _():
        carry_ref[...] = jnp.zeros_like(carry_ref)

    xn = _rms(x_ref[...], g_ref[...]).astype(BF16)
    z = _dot(xn, wf_ref[...]) + bf_ref[...]
    logf = jnp.minimum(z, 0.0) - jnp.log(1.0 + jnp.exp(-jnp.abs(z)))
    row = lax.broadcasted_iota(jnp.int32, (tm, tm), 0)
    col = lax.broadcasted_iota(jnp.int32, (tm, tm), 1)
    tri = jnp.where(row >= col, 1.0, 0.0).astype(BF16)
    hi, mid, lo = _split3(logf)
    c = _dot(tri, hi) + _dot(tri, mid) + _dot(tri, lo) + carry_ref[...]
    carry_ref[...] = c[tm - 1:tm, :]
    c_ref[...] = c * LOG2E


def _fox_gate(h, g, wf, bf):
    S, D = h.shape
    tm = min(512, S)
    return pl.pallas_call(
        functools.partial(_fox_gate_kernel, tm=tm),
        grid=(S // tm,),
        in_specs=[
            pl.BlockSpec((tm, D), lambda i: (i, 0)),
            pl.BlockSpec((1, D), lambda i: (0, 0)),
            pl.BlockSpec((D, LANES), lambda i: (0, 0)),
            pl.BlockSpec((1, LANES), lambda i: (0, 0)),
        ],
        out_specs=pl.BlockSpec((tm, LANES), lambda i: (i, 0)),
        out_shape=jax.ShapeDtypeStruct((S, LANES), F32),
        scratch_shapes=[pltpu.VMEM((1, LANES), F32)],
        compiler_params=_cparams(("arbitrary",)),
        name="fox_gate",
    )(h, g.reshape(1, D), wf, bf)


def _fox_attn_kernel(q_ref, k_ref, v_ref, c_ref, o_ref, m_ref, l_ref, acc_ref, *, t):
    i = pl.program_id(1)
    qh = _head_halves(q_ref[...])
    m_ref[...] = jnp.full(m_ref.shape, NEG, F32)
    l_ref[...] = jnp.zeros(l_ref.shape, F32)
    acc_ref[...] = jnp.zeros(acc_ref.shape, F32)

    def tile(j, diag):
        off = pl.multiple_of(j * t, t)
        k = k_ref[pl.ds(off, t), :]
        v = v_ref[pl.ds(off, t), :]
        c = c_ref[0, j]
        if diag:
            causal = (lax.broadcasted_iota(jnp.int32, (t, t), 0)
                      >= lax.broadcasted_iota(jnp.int32, (t, t), 1))
        for h in range(2):
            s = _dot_nt(qh[h], k) - c[h:h + 1, :]
            if diag:
                s = jnp.where(causal, s, NEG)
            _online_update(s, v, m_ref, l_ref, acc_ref, h)

    def body(j, carry):
        tile(j, False)
        return carry

    lax.fori_loop(0, i, body, 0)
    tile(i, True)
    lane = lax.broadcasted_iota(jnp.int32, (t, LANES), 1)
    o = jnp.where(lane < HEAD_DIM, acc_ref[0] / l_ref[0], acc_ref[1] / l_ref[1])
    o_ref[...] = o.astype(o_ref.dtype)


def _fox_attn(qkv, c4):
    S = qkv.shape[0]
    t = min(512, S)
    n_pairs = N_HEADS // 2
    return pl.pallas_call(
        functools.partial(_fox_attn_kernel, t=t),
        grid=(n_pairs, S // t),
        in_specs=[
            pl.BlockSpec((t, LANES), lambda hp, i: (i, hp)),
            pl.BlockSpec((S, LANES), lambda hp, i: (0, n_pairs + hp)),
            pl.BlockSpec((S, LANES), lambda hp, i: (0, 2 * n_pairs + hp)),
            pl.BlockSpec((1, S // t, 2, t), lambda hp, i: (hp, 0, 0, 0)),
        ],
        out_specs=pl.BlockSpec((t, LANES), lambda hp, i: (i, hp)),
        out_shape=jax.ShapeDtypeStruct((S, D_MODEL), BF16),
        scratch_shapes=[pltpu.VMEM((2, t, 1), F32), pltpu.VMEM((2, t, 1), F32),
                        pltpu.VMEM((2, t, LANES), F32)],
        compiler_params=_cparams(("parallel", "arbitrary")),
        name="fox_attn",
    )(qkv, qkv, qkv, c4)


def _fox_layer(h, g, w_qkv, w_f, b_f, w_o):
    S = h.shape[0]
    t = min(512, S)
    w_qkv = jnp.concatenate([w_qkv[:, :D_MODEL] * QK_SCALE, w_qkv[:, D_MODEL:]], axis=1).astype(BF16)
    wf = jnp.pad(w_f, ((0, 0), (0, LANES - N_HEADS))).astype(BF16)
    bf = jnp.pad(b_f, (0, LANES - N_HEADS)).reshape(1, LANES)
    c = _fox_gate(h, g, wf, bf)
    c4 = c[:, :N_HEADS].T.reshape(N_HEADS // 2, 2, S // t, t).transpose(0, 2, 1, 3)
    qkv = _norm_matmul(h, g, w_qkv, "fox_qkv")
    o = _fox_attn(qkv, c4)
    return _matmul_res(o, w_o.astype(BF16), h, "fox_out")


def _pool_kernel(x_ref, halo_ref, g_ref, w_ref, sc_ref, o_ref, *, tm):
    i = pl.program_id(0)
    x = x_ref[...]
    g = g_ref[...]
    xn = _rms(x, g)
    hn = jnp.where(i > 0, _rms(halo_ref[...], g), 0.0)
    xe = jnp.concatenate([hn, xn], axis=0)
    tpos = i * tm + lax.broadcasted_iota(jnp.int32, (tm, 1), 0)
    for gi, w in enumerate(POOL_WINDOWS):
        sl = slice(gi * POOL_GROUP, (gi + 1) * POOL_GROUP)
        s = xe[:, sl]
        k = 1
        while k < w:
            s = s + pltpu.roll(s, k, 0)
            k *= 2
        cnt = jnp.minimum(tpos + 1, w).astype(F32)
        d = (s[POOL_HALO:, :] / cnt - xn[:, sl]).astype(BF16)
        o_ref[:, sl] = x[:, sl] + _dot(d, w_ref[gi]) * sc_ref[:, sl]


def _pool_layer(h, g, w_pool, pool_scale):
    S, D = h.shape
    tm = min(1024, S)
    return pl.pallas_call(
        functools.partial(_pool_kernel, tm=tm),
        grid=(S // tm,),
        in_specs=[
            pl.BlockSpec((tm, D), lambda i: (i, 0)),
            pl.BlockSpec((POOL_HALO, D), lambda i: (jnp.maximum(i * (tm // POOL_HALO) - 1, 0), 0)),
            pl.BlockSpec((1, D), lambda i: (0, 0)),
            pl.BlockSpec((len(POOL_WINDOWS), POOL_GROUP, POOL_GROUP), lambda i: (0, 0, 0)),
            pl.BlockSpec((1, D), lambda i: (0, 0)),
        ],
        out_specs=pl.BlockSpec((tm, D), lambda i: (i, 0)),
        out_shape=jax.ShapeDtypeStruct((S, D), F32),
        compiler_params=_cparams(("parallel",)),
        name="pool",
    )(h, h, g.reshape(1, D), w_pool.astype(BF16), pool_scale.reshape(1, D))


def _conv_in_kernel(x_ref, g_ref, wb_ref, wc_ref, wu_ref, b_ref, z_ref, xn_ref):
    @pl.when(pl.program_id(1) == 0)
    def _():
        xn_ref[...] = _rms(x_ref[...], g_ref[...]).astype(BF16)

    xn = xn_ref[...]
    b_ref[...] = _dot(xn, wb_ref[...])
    z_ref[...] = _dot(xn, wc_ref[...]) * _dot(xn, wu_ref[...])


def _conv_out_kernel(b_ref, z_ref, zh_ref, cw_ref, w_ref, r_ref, o_ref, *, tm):
    i = pl.program_id(0)
    z = z_ref[...]
    zh = jnp.where(i > 0, zh_ref[...], 0.0)
    row = lax.broadcasted_iota(jnp.int32, (tm, 1), 0)
    prev1 = zh[CONV_HALO - 1:CONV_HALO, :]
    prev2 = zh[CONV_HALO - 2:CONV_HALO - 1, :]
    z1 = jnp.where(row == 0, prev1, pltpu.roll(z, 1, 0))
    z2 = jnp.where(row == 0, prev2, jnp.where(row == 1, prev1, pltpu.roll(z, 2, 0)))
    cw = cw_ref[...]
    conv = cw[0:1, :] * z2 + cw[1:2, :] * z1 + cw[2:3, :] * z
    y = (b_ref[...] * conv).astype(BF16)
    o_ref[...] = r_ref[...] + _dot(y, w_ref[...])


def _conv_layer(h, g, w_in, conv_w, w_out):
    S, D = h.shape
    tm = min(1024, S)
    tn = 512
    nj = D // tn
    w_in = w_in.astype(BF16)
    b, z = pl.pallas_call(
        _conv_in_kernel,
        grid=(S // tm, nj),
        in_specs=[
            pl.BlockSpec((tm, D), lambda i, j: (i, 0)),
            pl.BlockSpec((1, D), lambda i, j: (0, 0)),
            pl.BlockSpec((D, tn), lambda i, j: (0, j)),
            pl.BlockSpec((D, tn), lambda i, j: (0, nj + j)),
            pl.BlockSpec((D, tn), lambda i, j: (0, 2 * nj + j)),
        ],
        out_specs=[pl.BlockSpec((tm, tn), lambda i, j: (i, j)),
                   pl.BlockSpec((tm, tn), lambda i, j: (i, j))],
        out_shape=[jax.ShapeDtypeStruct((S, D), F32), jax.ShapeDtypeStruct((S, D), F32)],
        scratch_shapes=[pltpu.VMEM((tm, D), BF16)],
        compiler_params=_cparams(("parallel", "arbitrary")),
        name="conv_in",
    )(h, g.reshape(1, D), w_in, w_in, w_in)
    cw = jnp.pad(conv_w, ((0, 8 - CONV_WIDTH), (0, 0)))
    return pl.pallas_call(
        functools.partial(_conv_out_kernel, tm=tm),
        grid=(S // tm,),
        in_specs=[
            pl.BlockSpec((tm, D), lambda i: (i, 0)),
            pl.BlockSpec((tm, D), lambda i: (i, 0)),
            pl.BlockSpec((CONV_HALO, D), lambda i: (jnp.maximum(i * (tm // CONV_HALO) - 1, 0), 0)),
            pl.BlockSpec((8, D), lambda i: (0, 0)),
            pl.BlockSpec((D, D), lambda i: (0, 0)),
            pl.BlockSpec((tm, D), lambda i: (i, 0)),
        ],
        out_specs=pl.BlockSpec((tm, D), lambda i: (i, 0)),
        out_shape=jax.ShapeDtypeStruct((S, D), F32),
        compiler_params=_cparams(("parallel",)),
        name="conv_out",
    )(b, z, z, cw, w_out.astype(BF16), h)


def _nsa_proj_kernel(x_ref, g_ref, w_ref, rc_ref, rs1_ref, rs2_ref, o_ref, xn_ref):
    j = pl.program_id(1)

    @pl.when(j == 0)
    def _():
        xn_ref[...] = _rms(x_ref[...], g_ref[...]).astype(BF16)

    a = _dot(xn_ref[...], w_ref[...])
    is_rope = functools.reduce(jnp.logical_or, [j == t for t in NSA_ROPE_TILES])

    @pl.when(is_rope)
    def _():
        rc, rs1, rs2 = rc_ref[...], rs1_ref[...], rs2_ref[...]
        o_ref[...] = jnp.concatenate(
            [_rope(a[:, :LANES], rc, rs1, rs2), _rope(a[:, LANES:], rc, rs1, rs2)], axis=1).astype(o_ref.dtype)

    @pl.when(jnp.logical_not(is_rope))
    def _():
        o_ref[...] = a.astype(o_ref.dtype)


def _nsa_proj(h, g, w_cat, rc, rs1, rs2):
    S, D = h.shape
    tm = min(1024, S)
    tn = NSA_PROJ_TILE
    return pl.pallas_call(
        _nsa_proj_kernel,
        grid=(S // tm, NSA_PROJ_WIDTH // tn),
        in_specs=[
            pl.BlockSpec((tm, D), lambda i, j: (i, 0)),
            pl.BlockSpec((1, D), lambda i, j: (0, 0)),
            pl.BlockSpec((D, tn), lambda i, j: (0, j)),
            pl.BlockSpec((tm, LANES), lambda i, j: (i, 0)),
            pl.BlockSpec((tm, LANES), lambda i, j: (i, 0)),
            pl.BlockSpec((tm, LANES), lambda i, j: (i, 0)),
        ],
        out_specs=pl.BlockSpec((tm, tn), lambda i, j: (i, j)),
        out_shape=jax.ShapeDtypeStruct((S, NSA_PROJ_WIDTH), BF16),
        scratch_shapes=[pltpu.VMEM((tm, D), BF16)],
        compiler_params=_cparams(("parallel", "arbitrary")),
        name="nsa_proj",
    )(h, g.reshape(1, D), w_cat, rc, rs1, rs2)


def _nsa_cmp_kernel(r_ref, w1_ref, pe_ref, w2_ref, rc_ref, rs1_ref, rs2_ref, o_ref, acc_ref, *, nc):
    kv = pl.program_id(0)
    hd = pl.program_id(1)
    half = CMP_STRIDE * HEAD_DIM
    r = r_ref[0, 0]
    w1 = w1_ref[0]
    first = _dot(r, w1[:half, :])
    second = _dot(r, w1[half:, :])
    pe_term = _dot(pe_ref[0], w1)[0:1, :]
    pre = first + pltpu.roll(second, nc - 1, 0) + pe_term
    ge = 0.5 * pre * (1.0 + jnp.tanh(0.7978845608028654 * (pre + 0.044715 * pre * pre * pre)))
    y = _dot(ge.astype(BF16), w2_ref[0, 0])

    @pl.when(hd == 0)
    def _():
        acc_ref[...] = y

    @pl.when(hd > 0)
    def _():
        acc_ref[...] += y

    @pl.when(hd == NSA_KV_HEADS - 1)
    def _():
        acc = acc_ref[...]

        @pl.when(kv == 0)
        def _():
            rc, rs1, rs2 = rc_ref[...], rs1_ref[...], rs2_ref[...]
            o_ref[0] = jnp.concatenate(
                [_rope(acc[:, :LANES], rc, rs1, rs2), _rope(acc[:, LANES:], rc, rs1, rs2)], axis=1).astype(o_ref.dtype)

        @pl.when(kv == 1)
        def _():
            o_ref[0] = acc.astype(o_ref.dtype)


def _nsa_cmp(r, w1, pe, w2p, rc, rs1, rs2):
    nc = r.shape[2]
    kvd = NSA_KV_HEADS * HEAD_DIM
    return pl.pallas_call(
        functools.partial(_nsa_cmp_kernel, nc=nc),
        grid=(2, NSA_KV_HEADS),
        in_specs=[
            pl.BlockSpec((1, 1, nc, CMP_STRIDE * HEAD_DIM), lambda a, b: (a, b, 0, 0)),
            pl.BlockSpec((1, CMP_BLOCK * HEAD_DIM, CMP_HIDDEN), lambda a, b: (a, 0, 0)),
            pl.BlockSpec((1, 8, CMP_BLOCK * HEAD_DIM), lambda a, b: (a, 0, 0)),
            pl.BlockSpec((1, 1, CMP_HIDDEN, kvd), lambda a, b: (a, b, 0, 0)),
            pl.BlockSpec((nc, LANES), lambda a, b: (0, 0)),
            pl.BlockSpec((nc, LANES), lambda a, b: (0, 0)),
            pl.BlockSpec((nc, LANES), lambda a, b: (0, 0)),
        ],
        out_specs=pl.BlockSpec((1, nc, kvd), lambda a, b: (a, 0, 0)),
        out_shape=jax.ShapeDtypeStruct((2, nc, kvd), BF16),
        scratch_shapes=[pltpu.VMEM((nc, kvd), F32)],
        compiler_params=_cparams(("parallel", "arbitrary")),
        name="nsa_cmp",
    )(r, w1, pe, w2p, rc, rs1, rs2)


def _nsa_cmp_attn_kernel(q_ref, kc_ref, vc_ref, ov_ref, oc_ref, sel_ref, *, t, nc, nsp):
    i = pl.program_id(1)
    qpos = i * t + lax.broadcasted_iota(jnp.int32, (t, 1), 0)
    cmp_end = CMP_STRIDE * lax.broadcasted_iota(jnp.int32, (1, nc), 1) + (CMP_BLOCK - 1)
    cmask = cmp_end <= qpos
    lane = lax.broadcasted_iota(jnp.int32, (t, LANES), 1)
    kc = kc_ref[0]
    vc = vc_ref[0]
    imp = [jnp.zeros((t, nc), F32), jnp.zeros((t, nc), F32)]
    for g in range(NSA_GROUP):
        qh = _head_halves(q_ref[:, g * LANES:(g + 1) * LANES])
        oc = []
        for h in range(2):
            s = jnp.where(cmask, _dot_nt(qh[h], kc), NEG)
            m = jnp.max(s, axis=1, keepdims=True)
            p = jnp.where(cmask, jnp.exp2(s - m), 0.0)
            l = jnp.sum(p, axis=1, keepdims=True)
            pn = p / jnp.maximum(l, 1e-30)
            imp[h] = imp[h] + pn
            oc.append(_dot(pn.astype(BF16), vc))
        oc_ref[:, g * LANES:(g + 1) * LANES] = jnp.where(lane < HEAD_DIM, oc[0], oc[1]).astype(oc_ref.dtype)

    ov = ov_ref[...]
    blk = lax.broadcasted_iota(jnp.int32, (1, nsp), 1)
    blk_f = blk.astype(F32)
    cur = jnp.right_shift(qpos, SLC_SHIFT)
    forced = jnp.logical_or(blk == 0, jnp.logical_or(blk == cur, blk == cur - 1))
    valid = blk * SLC_BLOCK <= qpos
    for h in range(2):
        hi, mid, lo = _split3(imp[h])
        score = _dot(hi, ov) + _dot(mid, ov) + _dot(lo, ov)
        vals = jnp.where(valid, jnp.where(forced, FORCE_SCORE, score), NEG)

        def pick_one(_, carry):
            vals, sel = carry
            mx = jnp.max(vals, axis=1, keepdims=True)
            first = jnp.min(jnp.where(vals == mx, blk_f, float(nsp)), axis=1, keepdims=True)
            pick = blk_f == first
            return jnp.where(pick, -jnp.inf, vals), jnp.where(pick, 1.0, sel)

        _, sel = lax.fori_loop(0, SLC_TOPK, pick_one, (vals, jnp.zeros((t, nsp), F32)))
        sel_ref[:, h * nsp:(h + 1) * nsp] = jnp.where(valid, sel, 0.0).astype(sel_ref.dtype)


def _nsa_cmp_attn(proj, kvc, ov):
    S = proj.shape[0]
    t = NSA_TILE
    nc, nsp = ov.shape
    return pl.pallas_call(
        functools.partial(_nsa_cmp_attn_kernel, t=t, nc=nc, nsp=nsp),
        grid=(NSA_KV_HEADS // 2, S // t),
        in_specs=[
            pl.BlockSpec((t, NSA_GROUP * LANES), lambda p, i: (i, p)),
            pl.BlockSpec((1, nc, LANES), lambda p, i: (0, 0, p)),
            pl.BlockSpec((1, nc, LANES), lambda p, i: (1, 0, p)),
            pl.BlockSpec((nc, nsp), lambda p, i: (0, 0)),
        ],
        out_specs=[pl.BlockSpec((t, NSA_GROUP * LANES), lambda p, i: (i, p)),
                   pl.BlockSpec((t, 2 * nsp), lambda p, i: (i, p))],
        out_shape=[jax.ShapeDtypeStruct((S, D_MODEL), BF16),
                   jax.ShapeDtypeStruct((S, NSA_KV_HEADS * nsp), BF16)],
        compiler_params=_cparams(("parallel", "parallel")),
        name="nsa_cmp_attn",
    )(proj, kvc, kvc, ov)


def _nsa_sel_kernel(q_ref, ks_ref, vs_ref, sel_ref, kw0_ref, kw1_ref, kw2_ref, vw0_ref, vw1_ref, vw2_ref,
                    gz_ref, e_ref, oc_ref, o_ref, m_ref, l_ref, acc_ref, *, t, nsp):
    i = pl.program_id(1)
    n_heads = 2 * NSA_GROUP
    lane = lax.broadcasted_iota(jnp.int32, (t, LANES), 1)
    low = lane < HEAD_DIM
    qs = []
    for g in range(NSA_GROUP):
        qs.extend(_head_halves(q_ref[:, g * LANES:(g + 1) * LANES]))
    sel_bias = [((sel_ref[:, h * nsp:(h + 1) * nsp].astype(F32) - 1.0) * -NEG).astype(BF16) for h in range(2)]
    m_ref[...] = jnp.full(m_ref.shape, NEG, F32)
    l_ref[...] = jnp.zeros(l_ref.shape, F32)
    acc_ref[...] = jnp.zeros(acc_ref.shape, F32)
    blk_row = lax.broadcasted_iota(jnp.int32, (nsp, t), 0)
    tok_col = lax.broadcasted_iota(jnp.int32, (nsp, t), 1)
    row = lax.broadcasted_iota(jnp.int32, (t, t), 0)
    col = lax.broadcasted_iota(jnp.int32, (t, t), 1)

    def tile(j, diag):
        off = pl.multiple_of(j * t, t)
        k = ks_ref[pl.ds(off, t), :]
        v = vs_ref[pl.ds(off, t), :]
        expand = jnp.where(blk_row == jnp.right_shift(off + tok_col, SLC_SHIFT), 1.0, 0.0).astype(BF16)
        for h in range(2):
            bias = _dot(sel_bias[h], expand)
            if diag:
                bias = jnp.where(row >= col, bias, NEG)
            for g in range(NSA_GROUP):
                idx = 2 * g + h
                _online_update(_dot_nt(qs[idx], k) + bias, v, m_ref, l_ref, acc_ref, idx)

    def body(j, carry):
        tile(j, False)
        return carry

    lax.fori_loop(0, i, body, 0)
    tile(i, True)

    n_win = WIN // t + 1
    k_win = jnp.concatenate([kw0_ref[...], kw1_ref[...], kw2_ref[...]], axis=0)
    v_win = jnp.concatenate([vw0_ref[...], vw1_ref[...], vw2_ref[...]], axis=0)
    qpos = i * t + lax.broadcasted_iota(jnp.int32, (t, n_win * t), 0)
    kpos = (i - (n_win - 1)) * t + lax.broadcasted_iota(jnp.int32, (t, n_win * t), 1)
    wmask = jnp.logical_and(jnp.logical_and(kpos <= qpos, kpos > qpos - WIN), kpos >= 0)
    ow = []
    for idx in range(n_heads):
        s = jnp.where(wmask, _dot_nt(qs[idx], k_win), NEG)
        m = jnp.max(s, axis=1, keepdims=True)
        p = jnp.exp2(s - m)
        l = jnp.sum(p, axis=1, keepdims=True)
        ow.append(_dot(p.astype(BF16), v_win) / l)

    gates = jax.nn.sigmoid(_dot(gz_ref[...], e_ref[0]))
    gw = NSA_GROUP * LANES
    for g in range(NSA_GROUP):
        sl = slice(g * LANES, (g + 1) * LANES)
        o_cmp = oc_ref[:, sl].astype(F32)
        o_slc = jnp.where(low, acc_ref[2 * g] / l_ref[2 * g], acc_ref[2 * g + 1] / l_ref[2 * g + 1])
        o_win = jnp.where(low, ow[2 * g], ow[2 * g + 1])
        out = (gates[:, g * LANES:(g + 1) * LANES] * o_cmp
               + gates[:, gw + g * LANES:gw + (g + 1) * LANES] * o_slc
               + gates[:, 2 * gw + g * LANES:2 * gw + (g + 1) * LANES] * o_win)
        o_ref[:, sl] = out.astype(o_ref.dtype)


def _nsa_sel(proj, sel, o_cmp, gate_expand):
    S = proj.shape[0]
    t = NSA_TILE
    nsp = sel.shape[1] // NSA_KV_HEADS
    n_heads = 2 * NSA_GROUP

    def win_spec(col0, back):
        return pl.BlockSpec((t, LANES), lambda p, i: (jnp.maximum(i - back, 0), col0 + p))

    return pl.pallas_call(
        functools.partial(_nsa_sel_kernel, t=t, nsp=nsp),
        grid=(NSA_KV_HEADS // 2, S // t),
        in_specs=[
            pl.BlockSpec((t, NSA_GROUP * LANES), lambda p, i: (i, p)),
            pl.BlockSpec((S, LANES), lambda p, i: (0, 12 + p)),
            pl.BlockSpec((S, LANES), lambda p, i: (0, 14 + p)),
            pl.BlockSpec((t, 2 * nsp), lambda p, i: (i, p)),
            win_spec(16, 2), win_spec(16, 1), win_spec(16, 0),
            win_spec(18, 2), win_spec(18, 1), win_spec(18, 0),
            pl.BlockSpec((t, NSA_PROJ_TILE), lambda p, i: (i, 10)),
            pl.BlockSpec((1, NSA_PROJ_TILE, 3 * NSA_GROUP * LANES), lambda p, i: (p, 0, 0)),
            pl.BlockSpec((t, NSA_GROUP * LANES), lambda p, i: (i, p)),
        ],
        out_specs=pl.BlockSpec((t, NSA_GROUP * LANES), lambda p, i: (i, p)),
        out_shape=jax.ShapeDtypeStruct((S, D_MODEL), BF16),
        scratch_shapes=[pltpu.VMEM((n_heads, t, 1), F32), pltpu.VMEM((n_heads, t, 1), F32),
                        pltpu.VMEM((n_heads, t, LANES), F32)],
        compiler_params=_cparams(("parallel", "arbitrary")),
        name="nsa_sel",
    )(proj, proj, proj, sel, proj, proj, proj, proj, proj, proj, proj, gate_expand, o_cmp)


def _rope_tables(positions):
    half = ROT_DIM // 2
    inv = ROPE_THETA ** (-jnp.arange(0, ROT_DIM, 2, dtype=F32) / ROT_DIM)
    ang = positions.astype(F32)[:, None] * inv[None, :]
    cos, sin = jnp.cos(ang), jnp.sin(ang)
    S = positions.shape[0]
    ones = jnp.ones((S, HEAD_DIM - ROT_DIM), F32)
    zeros_h = jnp.zeros((S, half), F32)
    zeros_r = jnp.zeros((S, HEAD_DIM - ROT_DIM), F32)
    rc = jnp.concatenate([cos, cos, ones], axis=1)
    rs1 = jnp.concatenate([-sin, zeros_h, zeros_r], axis=1)
    rs2 = jnp.concatenate([zeros_h, sin, zeros_r], axis=1)
    reps = LANES // HEAD_DIM
    return jnp.tile(rc, (1, reps)), jnp.tile(rs1, (1, reps)), jnp.tile(rs2, (1, reps))


def _pair_heads(w, axis):
    shape = w.shape
    w = w.reshape(shape[:axis] + (NSA_KV_HEADS // 2, 2, NSA_GROUP, HEAD_DIM) + shape[axis + 1:])
    w = jnp.swapaxes(w, axis + 1, axis + 2)
    return w.reshape(shape)


def _nsa_constants(S):
    nc = S // CMP_STRIDE
    n_cmp = (S - CMP_BLOCK) // CMP_STRIDE + 1
    ns = S // SLC_BLOCK
    nsp = -(-ns // LANES) * LANES
    ci = np.arange(nc)[:, None] * CMP_STRIDE
    st = np.arange(nsp)[None, :] * SLC_BLOCK
    ov = (ci < st + SLC_BLOCK) & (ci + CMP_BLOCK > st) & (np.arange(nc)[:, None] < n_cmp) & (np.arange(nsp)[None, :] < ns)
    e = np.zeros((NSA_KV_HEADS // 2, NSA_PROJ_TILE, 3 * NSA_GROUP * LANES), np.float32)
    for p in range(NSA_KV_HEADS // 2):
        for br in range(3):
            for g in range(NSA_GROUP):
                for hf in range(2):
                    r = br * N_HEADS + (2 * p + hf) * NSA_GROUP + g
                    c0 = br * NSA_GROUP * LANES + g * LANES + hf * HEAD_DIM
                    e[p, r, c0:c0 + HEAD_DIM] = 1.0
    cmp_end = np.minimum(np.arange(nc) * CMP_STRIDE + CMP_BLOCK - 1, S - 1)
    return nc, jnp.asarray(ov.astype(np.float32), BF16), jnp.asarray(e, BF16), cmp_end


def _nsa_layer(h, g, positions, w_in, pe_k, w1_k, w2_k, pe_v, w1_v, w2_v, w_o):
    S, D = h.shape
    qd = N_HEADS * HEAD_DIM
    kvd = NSA_KV_HEADS * HEAD_DIM
    nc, ov, gate_expand, cmp_end = _nsa_constants(S)
    rc, rs1, rs2 = _rope_tables(positions)

    wq = _pair_heads(w_in[:, :qd], 1) * QK_SCALE
    wg = jnp.pad(w_in[:, qd + 6 * kvd:], ((0, 0), (0, NSA_PROJ_TILE - 3 * N_HEADS)))
    w_cat = jnp.concatenate([wq, w_in[:, qd:qd + 6 * kvd], wg], axis=1).astype(BF16)
    proj = _nsa_proj(h, g, w_cat, rc, rs1, rs2)

    raw = proj[:, qd:qd + 2 * kvd].reshape(nc, CMP_STRIDE, 2, NSA_KV_HEADS, HEAD_DIM)
    raw = raw.transpose(2, 3, 0, 1, 4).reshape(2, NSA_KV_HEADS, nc, CMP_STRIDE * HEAD_DIM)
    w1 = jnp.stack([w1_k, w1_v]).astype(BF16)
    pe = jnp.stack([pe_k.reshape(1, -1), pe_v.reshape(1, -1)])
    pe = jnp.pad(pe, ((0, 0), (0, 7), (0, 0))).astype(BF16)
    w2 = jnp.stack([w2_k, w2_v])
    eye = jnp.eye(NSA_KV_HEADS, dtype=F32)
    w2p = (w2[:, None, :, None, :] * eye[None, :, None, :, None]).reshape(2, NSA_KV_HEADS, CMP_HIDDEN, kvd).astype(BF16)
    kvc = _nsa_cmp(raw, w1, pe, w2p, rc[cmp_end], rs1[cmp_end], rs2[cmp_end])

    o_cmp, sel = _nsa_cmp_attn(proj, kvc, ov)
    o = _nsa_sel(proj, sel, o_cmp, gate_expand)
    return _matmul_res(o, _pair_heads(w_o, 0).astype(BF16), h, "nsa_out")


def _trunk(x2, positions, p):
    h = _fox_layer(x2, p["l0_norm_mix"], p["l0_fox_w_qkv"], p["l0_fox_w_f"], p["l0_fox_b_f"], p["l0_fox_w_o"])
    h = _mlp(h, p["l0_norm_mlp"], p["l0_mlp_w1"].astype(BF16), p["l0_mlp_w2"].astype(BF16))
    h = _pool_layer(h, p["l1_norm_mix"], p["l1_pool_w"], p["l1_pool_scale"])
    h = _mlp(h, p["l1_norm_mlp"], p["l1_mlp_w1"].astype(BF16), p["l1_mlp_w2"].astype(BF16))
    h = _conv_layer(h, p["l2_norm_mix"], p["l2_conv_w_in"], p["l2_conv_w"], p["l2_conv_w_out"])
    h = _mlp(h, p["l2_norm_mlp"], p["l2_mlp_w1"].astype(BF16), p["l2_mlp_w2"].astype(BF16))
    h = _nsa_layer(h, p["l3_norm_mix"], positions, p["l3_nsa_w_in"], p["l3_nsa_cmp_pe_k"], p["l3_nsa_cmp_w1_k"],
                   p["l3_nsa_cmp_w2_k"], p["l3_nsa_cmp_pe_v"], p["l3_nsa_cmp_w1_v"], p["l3_nsa_cmp_w2_v"],
                   p["l3_nsa_w_o"])
    return _mlp(h, p["l3_norm_mlp"], p["l3_mlp_w1"].astype(BF16), p["l3_mlp_w2"].astype(BF16), p["final_norm"])


def kernel(x, positions, l0_norm_mix, l0_fox_w_qkv, l0_fox_w_f, l0_fox_b_f, l0_fox_w_o, l0_norm_mlp, l0_mlp_w1, l0_mlp_w2, l1_norm_mix, l1_pool_w, l1_pool_scale, l1_norm_mlp, l1_mlp_w1, l1_mlp_w2, l2_norm_mix, l2_conv_w_in, l2_conv_w, l2_conv_w_out, l2_norm_mlp, l2_mlp_w1, l2_mlp_w2, l3_norm_mix, l3_nsa_w_in, l3_nsa_cmp_pe_k, l3_nsa_cmp_w1_k, l3_nsa_cmp_w2_k, l3_nsa_cmp_pe_v, l3_nsa_cmp_w1_v, l3_nsa_cmp_w2_v, l3_nsa_w_o, l3_norm_mlp, l3_mlp_w1, l3_mlp_w2, final_norm):
    params = dict(locals())
    B, S, D = x.shape
    outs = [_trunk(x[b], positions, params) for b in range(B)]
    return jnp.stack(outs, axis=0)
```

```python
import functools

import numpy as np
import jax
import jax.numpy as jnp
from jax import lax
from jax.experimental import pallas as pl
from jax.experimental.pallas import tpu as pltpu

F32 = jnp.float32
BF16 = jnp.bfloat16

D_MODEL = 1024
HEAD_DIM = 64
N_HEADS = D_MODEL // HEAD_DIM
D_FF = 4 * D_MODEL
ROPE_THETA = 500000.0
ROT_DIM = HEAD_DIM // 4
RMS_EPS = 1e-6
POOL_WINDOWS = (2, 4, 8, 16)
POOL_GROUP = D_MODEL // len(POOL_WINDOWS)
POOL_HALO = 16
CONV_WIDTH = 3
CONV_HALO = 8
NSA_KV_HEADS = 4
NSA_GROUP = N_HEADS // NSA_KV_HEADS
CMP_BLOCK = 32
CMP_STRIDE = 16
CMP_HIDDEN = 256
SLC_BLOCK = 64
SLC_SHIFT = 6
SLC_TOPK = 16
WIN = 512
FORCE_SCORE = 1e9
NEG = -1e30
LOG2E = 1.4426950408889634
QK_SCALE = HEAD_DIM ** -0.5 * LOG2E

LANES = 128
V7X_VMEM_LIMIT = 56 * 1024 * 1024
NSA_TILE = WIN // 2
NSA_PROJ_TILE = 2 * LANES
NSA_PROJ_WIDTH = 22 * LANES
NSA_ROPE_TILES = (0, 1, 2, 3, 6, 8)


def _cparams(semantics):
    return pltpu.CompilerParams(dimension_semantics=semantics, vmem_limit_bytes=V7X_VMEM_LIMIT)


def _rms(x, g):
    return x * lax.rsqrt(jnp.mean(x * x, axis=-1, keepdims=True) + RMS_EPS) * g


def _dot(a, b):
    return jnp.dot(a, b, preferred_element_type=F32)


def _dot_nt(a, b):
    return lax.dot_general(a, b, (((1,), (1,)), ((), ())), preferred_element_type=F32)


def _split3(x):
    hi = x.astype(BF16)
    r = x - hi.astype(F32)
    mid = r.astype(BF16)
    lo = (r - mid.astype(F32)).astype(BF16)
    return hi, mid, lo


def _rope(a, rc, rs1, rs2):
    return a * rc + pltpu.roll(a, LANES - ROT_DIM // 2, 1) * rs1 + pltpu.roll(a, ROT_DIM // 2, 1) * rs2


def _head_halves(q_blk):
    lane = lax.broadcasted_iota(jnp.int32, q_blk.shape, 1)
    qf = q_blk.astype(F32)
    return (jnp.where(lane < HEAD_DIM, qf, 0.0).astype(BF16),
            jnp.where(lane >= HEAD_DIM, qf, 0.0).astype(BF16))


def _online_update(s, v, m_ref, l_ref, acc_ref, idx):
    reps = s.shape[1] // LANES
    m_old = m_ref[idx]
    m_new = jnp.maximum(m_old, jnp.max(s, axis=1, keepdims=True))
    alpha = jnp.exp2(m_old - m_new)
    p = jnp.exp2(s - jnp.tile(m_new, (1, reps)))
    psum = p[:, :LANES]
    for r in range(1, reps):
        psum = psum + p[:, r * LANES:(r + 1) * LANES]
    l_ref[idx] = alpha * l_ref[idx] + psum
    acc_ref[idx] = alpha * acc_ref[idx] + _dot(p.astype(BF16), v)
    m_ref[idx] = m_new


def _online_result(l_ref, acc_ref, idx):
    return acc_ref[idx] / jnp.sum(l_ref[idx], axis=1, keepdims=True)


def _mlp_kernel(x_ref, g_ref, w1_ref, w2_ref, *rest, nf, final):
    if final:
        fg_ref, o_ref, xn_ref = rest
    else:
        o_ref, xn_ref = rest
    f = pl.program_id(1)

    @pl.when(f == 0)
    def _():
        x = x_ref[...]
        xn_ref[...] = _rms(x, g_ref[...]).astype(BF16)
        o_ref[...] = x

    a = _dot(xn_ref[...], w1_ref[...])
    a = jnp.square(jnp.maximum(a, 0.0)).astype(BF16)
    o_ref[...] += _dot(a, w2_ref[...])

    if final:
        @pl.when(f == nf - 1)
        def _():
            o_ref[...] = _rms(o_ref[...], fg_ref[...])


def _mlp(h, g, w1, w2, final_g=None):
    S, D = h.shape
    F = w1.shape[1]
    tm = min(1024, S)
    tf = 1024
    nf = F // tf
    final = final_g is not None
    in_specs = [
        pl.BlockSpec((tm, D), lambda i, f: (i, 0)),
        pl.BlockSpec((1, D), lambda i, f: (0, 0)),
        pl.BlockSpec((D, tf), lambda i, f: (0, f)),
        pl.BlockSpec((tf, D), lambda i, f: (f, 0)),
    ]
    args = [h, g.reshape(1, D), w1, w2]
    if final:
        in_specs.append(pl.BlockSpec((1, D), lambda i, f: (0, 0)))
        args.append(final_g.reshape(1, D))
    return pl.pallas_call(
        functools.partial(_mlp_kernel, nf=nf, final=final),
        grid=(S // tm, nf),
        in_specs=in_specs,
        out_specs=pl.BlockSpec((tm, D), lambda i, f: (i, 0)),
        out_shape=jax.ShapeDtypeStruct((S, D), F32),
        scratch_shapes=[pltpu.VMEM((tm, D), BF16)],
        compiler_params=_cparams(("parallel", "arbitrary")),
        name="mlp",
    )(*args)


def _norm_matmul_kernel(x_ref, g_ref, w_ref, o_ref, xn_ref):
    @pl.when(pl.program_id(1) == 0)
    def _():
        xn_ref[...] = _rms(x_ref[...], g_ref[...]).astype(BF16)

    o_ref[...] = _dot(xn_ref[...], w_ref[...]).astype(o_ref.dtype)


def _norm_matmul(h, g, w, name):
    S, D = h.shape
    N = w.shape[1]
    tm = min(1024, S)
    tn = 512
    return pl.pallas_call(
        _norm_matmul_kernel,
        grid=(S // tm, N // tn),
        in_specs=[
            pl.BlockSpec((tm, D), lambda i, j: (i, 0)),
            pl.BlockSpec((1, D), lambda i, j: (0, 0)),
            pl.BlockSpec((D, tn), lambda i, j: (0, j)),
        ],
        out_specs=pl.BlockSpec((tm, tn), lambda i, j: (i, j)),
        out_shape=jax.ShapeDtypeStruct((S, N), BF16),
        scratch_shapes=[pltpu.VMEM((tm, D), BF16)],
        compiler_params=_cparams(("parallel", "arbitrary")),
        name=name,
    )(h, g.reshape(1, D), w)


def _matmul_res_kernel(a_ref, w_ref, r_ref, o_ref):
    o_ref[...] = r_ref[...] + _dot(a_ref[...], w_ref[...])


def _matmul_res(a, w, res, name):
    S, K = a.shape
    N = w.shape[1]
    tm = min(1024, S)
    return pl.pallas_call(
        _matmul_res_kernel,
        grid=(S // tm,),
        in_specs=[
            pl.BlockSpec((tm, K), lambda i: (i, 0)),
            pl.BlockSpec((K, N), lambda i: (0, 0)),
            pl.BlockSpec((tm, N), lambda i: (i, 0)),
        ],
        out_specs=pl.BlockSpec((tm, N), lambda i: (i, 0)),
        out_shape=jax.ShapeDtypeStruct((S, N), F32),
        compiler_params=_cparams(("parallel",)),
        name=name,
    )(a, w, res)


def _fox_gate_kernel(x_ref, g_ref, wf_ref, bf_ref, c_ref, carry_ref, *, tm):
    @pl.when(pl.program_id(0) == 0)
    def _():
        carry_ref[...] = jnp.zeros_like(carry_ref)

    xn = _rms(x_ref[...], g_ref[...]).astype(BF16)
    z = _dot(xn, wf_ref[...]) + bf_ref[...]
    logf = jnp.minimum(z, 0.0) - jnp.log(1.0 + jnp.exp(-jnp.abs(z)))
    row = lax.broadcasted_iota(jnp.int32, (tm, tm), 0)
    col = lax.broadcasted_iota(jnp.int32, (tm, tm), 1)
    tri = jnp.where(row >= col, 1.0, 0.0).astype(BF16)
    hi, mid, lo = _split3(logf)
    c = _dot(tri, hi) + _dot(tri, mid) + _dot(tri, lo) + carry_ref[...]
    carry_ref[...] = c[tm - 1:tm, :]
    c_ref[...] = c * LOG2E


def _fox_gate(h, g, wf, bf):
    S, D = h.shape
    tm = min(512, S)
    return pl.pallas_call(
        functools.partial(_fox_gate_kernel, tm=tm),
        grid=(S // tm,),
        in_specs=[
            pl.BlockSpec((tm, D), lambda i: (i, 0)),
            pl.BlockSpec((1, D), lambda i: (0, 0)),
            pl.BlockSpec((D, LANES), lambda i: (0, 0)),
            pl.BlockSpec((1, LANES), lambda i: (0, 0)),
        ],
        out_specs=pl.BlockSpec((tm, LANES), lambda i: (i, 0)),
        out_shape=jax.ShapeDtypeStruct((S, LANES), F32),
        scratch_shapes=[pltpu.VMEM((1, LANES), F32)],
        compiler_params=_cparams(("arbitrary",)),
        name="fox_gate",
    )(h, g.reshape(1, D), wf, bf)


def _fox_attn_kernel(q_ref, k_ref, v_ref, c_ref, o_ref, m_ref, l_ref, acc_ref, *, t):
    i = pl.program_id(1)
    qh = _head_halves(q_ref[...])
    m_ref[...] = jnp.full(m_ref.shape, NEG, F32)
    l_ref[...] = jnp.zeros(l_ref.shape, F32)
    acc_ref[...] = jnp.zeros(acc_ref.shape, F32)

    def tile(j, diag):
        off = pl.multiple_of(j * t, t)
        k = k_ref[pl.ds(off, t), :]
        v = v_ref[pl.ds(off, t), :]
        c = c_ref[0, j]
        if diag:
            causal = (lax.broadcasted_iota(jnp.int32, (t, t), 0)
                      >= lax.broadcasted_iota(jnp.int32, (t, t), 1))
        for h in range(2):
            s = _dot_nt(qh[h], k) - c[h:h + 1, :]
            if diag:
                s = jnp.where(causal, s, NEG)
            _online_update(s, v, m_ref, l_ref, acc_ref, h)

    def body(j, carry):
        tile(j, False)
        return carry

    lax.fori_loop(0, i, body, 0)
    tile(i, True)
    lane = lax.broadcasted_iota(jnp.int32, (t, LANES), 1)
    o = jnp.where(lane < HEAD_DIM, _online_result(l_ref, acc_ref, 0), _online_result(l_ref, acc_ref, 1))
    o_ref[...] = o.astype(o_ref.dtype)


def _fox_attn(qkv, c4):
    S = qkv.shape[0]
    t = min(512, S)
    n_pairs = N_HEADS // 2
    return pl.pallas_call(
        functools.partial(_fox_attn_kernel, t=t),
        grid=(n_pairs, S // t),
        in_specs=[
            pl.BlockSpec((t, LANES), lambda hp, i: (i, hp)),
            pl.BlockSpec((S, LANES), lambda hp, i: (0, n_pairs + hp)),
            pl.BlockSpec((S, LANES), lambda hp, i: (0, 2 * n_pairs + hp)),
            pl.BlockSpec((1, S // t, 2, t), lambda hp, i: (hp, 0, 0, 0)),
        ],
        out_specs=pl.BlockSpec((t, LANES), lambda hp, i: (i, hp)),
        out_shape=jax.ShapeDtypeStruct((S, D_MODEL), BF16),
        scratch_shapes=[pltpu.VMEM((2, t, LANES), F32)] * 3,
        compiler_params=_cparams(("parallel", "arbitrary")),
        name="fox_attn",
    )(qkv, qkv, qkv, c4)


def _fox_layer(h, g, w_qkv, w_f, b_f, w_o):
    S = h.shape[0]
    t = min(512, S)
    w_qkv = jnp.concatenate([w_qkv[:, :D_MODEL] * QK_SCALE, w_qkv[:, D_MODEL:]], axis=1).astype(BF16)
    wf = jnp.pad(w_f, ((0, 0), (0, LANES - N_HEADS))).astype(BF16)
    bf = jnp.pad(b_f, (0, LANES - N_HEADS)).reshape(1, LANES)
    c = _fox_gate(h, g, wf, bf)
    c4 = c[:, :N_HEADS].T.reshape(N_HEADS // 2, 2, S // t, t).transpose(0, 2, 1, 3)
    qkv = _norm_matmul(h, g, w_qkv, "fox_qkv")
    o = _fox_attn(qkv, c4)
    return _matmul_res(o, w_o.astype(BF16), h, "fox_out")


def _pool_kernel(x_ref, halo_ref, g_ref, w_ref, sc_ref, o_ref, *, tm):
    i = pl.program_id(0)
    x = x_ref[...]
    g = g_ref[...]
    xn = _rms(x, g)
    hn = jnp.where(i > 0, _rms(halo_ref[...], g), 0.0)
    xe = jnp.concatenate([hn, xn], axis=0)
    tpos = i * tm + lax.broadcasted_iota(jnp.int32, (tm, 1), 0)
    for gi, w in enumerate(POOL_WINDOWS):
        sl = slice(gi * POOL_GROUP, (gi + 1) * POOL_GROUP)
        s = xe[:, sl]
        k = 1
        while k < w:
            s = s + pltpu.roll(s, k, 0)
            k *= 2
        cnt = jnp.minimum(tpos + 1, w).astype(F32)
        d = (s[POOL_HALO:, :] / cnt - xn[:, sl]).astype(BF16)
        o_ref[:, sl] = x[:, sl] + _dot(d, w_ref[gi]) * sc_ref[:, sl]


def _pool_layer(h, g, w_pool, pool_scale):
    S, D = h.shape
    tm = min(1024, S)
    return pl.pallas_call(
        functools.partial(_pool_kernel, tm=tm),
        grid=(S // tm,),
        in_specs=[
            pl.BlockSpec((tm, D), lambda i: (i, 0)),
            pl.BlockSpec((POOL_HALO, D), lambda i: (jnp.maximum(i * (tm // POOL_HALO) - 1, 0), 0)),
            pl.BlockSpec((1, D), lambda i: (0, 0)),
            pl.BlockSpec((len(POOL_WINDOWS), POOL_GROUP, POOL_GROUP), lambda i: (0, 0, 0)),
            pl.BlockSpec((1, D), lambda i: (0, 0)),
        ],
        out_specs=pl.BlockSpec((tm, D), lambda i: (i, 0)),
        out_shape=jax.ShapeDtypeStruct((S, D), F32),
        compiler_params=_cparams(("parallel",)),
        name="pool",
    )(h, h, g.reshape(1, D), w_pool.astype(BF16), pool_scale.reshape(1, D))


def _conv_in_kernel(x_ref, g_ref, wb_ref, wc_ref, wu_ref, b_ref, z_ref, xn_ref):
    @pl.when(pl.program_id(1) == 0)
    def _():
        xn_ref[...] = _rms(x_ref[...], g_ref[...]).astype(BF16)

    xn = xn_ref[...]
    b_ref[...] = _dot(xn, wb_ref[...])
    z_ref[...] = _dot(xn, wc_ref[...]) * _dot(xn, wu_ref[...])


def _conv_out_kernel(b_ref, z_ref, zh_ref, cw_ref, w_ref, r_ref, o_ref, *, tm):
    i = pl.program_id(0)
    z = z_ref[...]
    zh = jnp.where(i > 0, zh_ref[...], 0.0)
    row = lax.broadcasted_iota(jnp.int32, (tm, 1), 0)
    prev1 = zh[CONV_HALO - 1:CONV_HALO, :]
    prev2 = zh[CONV_HALO - 2:CONV_HALO - 1, :]
    z1 = jnp.where(row == 0, prev1, pltpu.roll(z, 1, 0))
    z2 = jnp.where(row == 0, prev2, jnp.where(row == 1, prev1, pltpu.roll(z, 2, 0)))
    cw = cw_ref[...]
    conv = cw[0:1, :] * z2 + cw[1:2, :] * z1 + cw[2:3, :] * z
    y = (b_ref[...] * conv).astype(BF16)
    o_ref[...] = r_ref[...] + _dot(y, w_ref[...])


def _conv_layer(h, g, w_in, conv_w, w_out):
    S, D = h.shape
    tm = min(1024, S)
    tn = 512
    nj = D // tn
    w_in = w_in.astype(BF16)
    b, z = pl.pallas_call(
        _conv_in_kernel,
        grid=(S // tm, nj),
        in_specs=[
            pl.BlockSpec((tm, D), lambda i, j: (i, 0)),
            pl.BlockSpec((1, D), lambda i, j: (0, 0)),
            pl.BlockSpec((D, tn), lambda i, j: (0, j)),
            pl.BlockSpec((D, tn), lambda i, j: (0, nj + j)),
            pl.BlockSpec((D, tn), lambda i, j: (0, 2 * nj + j)),
        ],
        out_specs=[pl.BlockSpec((tm, tn), lambda i, j: (i, j)),
                   pl.BlockSpec((tm, tn), lambda i, j: (i, j))],
        out_shape=[jax.ShapeDtypeStruct((S, D), F32), jax.ShapeDtypeStruct((S, D), F32)],
        scratch_shapes=[pltpu.VMEM((tm, D), BF16)],
        compiler_params=_cparams(("parallel", "arbitrary")),
        name="conv_in",
    )(h, g.reshape(1, D), w_in, w_in, w_in)
    cw = jnp.pad(conv_w, ((0, 8 - CONV_WIDTH), (0, 0)))
    return pl.pallas_call(
        functools.partial(_conv_out_kernel, tm=tm),
        grid=(S // tm,),
        in_specs=[
            pl.BlockSpec((tm, D), lambda i: (i, 0)),
            pl.BlockSpec((tm, D), lambda i: (i, 0)),
            pl.BlockSpec((CONV_HALO, D), lambda i: (jnp.maximum(i * (tm // CONV_HALO) - 1, 0), 0)),
            pl.BlockSpec((8, D), lambda i: (0, 0)),
            pl.BlockSpec((D, D), lambda i: (0, 0)),
            pl.BlockSpec((tm, D), lambda i: (i, 0)),
        ],
        out_specs=pl.BlockSpec((tm, D), lambda i: (i, 0)),
        out_shape=jax.ShapeDtypeStruct((S, D), F32),
        compiler_params=_cparams(("parallel",)),
        name="conv_out",
    )(b, z, z, cw, w_out.astype(BF16), h)


def _nsa_proj_kernel(x_ref, g_ref, w_ref, rc_ref, rs1_ref, rs2_ref, o_ref, xn_ref):
    j = pl.program_id(1)

    @pl.when(j == 0)
    def _():
        xn_ref[...] = _rms(x_ref[...], g_ref[...]).astype(BF16)

    a = _dot(xn_ref[...], w_ref[...])
    is_rope = functools.reduce(jnp.logical_or, [j == t for t in NSA_ROPE_TILES])

    @pl.when(is_rope)
    def _():
        rc, rs1, rs2 = rc_ref[...], rs1_ref[...], rs2_ref[...]
        o_ref[...] = jnp.concatenate(
            [_rope(a[:, :LANES], rc, rs1, rs2), _rope(a[:, LANES:], rc, rs1, rs2)], axis=1).astype(o_ref.dtype)

    @pl.when(jnp.logical_not(is_rope))
    def _():
        o_ref[...] = a.astype(o_ref.dtype)


def _nsa_proj(h, g, w_cat, rc, rs1, rs2):
    S, D = h.shape
    tm = min(1024, S)
    tn = NSA_PROJ_TILE
    return pl.pallas_call(
        _nsa_proj_kernel,
        grid=(S // tm, NSA_PROJ_WIDTH // tn),
        in_specs=[
            pl.BlockSpec((tm, D), lambda i, j: (i, 0)),
            pl.BlockSpec((1, D), lambda i, j: (0, 0)),
            pl.BlockSpec((D, tn), lambda i, j: (0, j)),
            pl.BlockSpec((tm, LANES), lambda i, j: (i, 0)),
            pl.BlockSpec((tm, LANES), lambda i, j: (i, 0)),
            pl.BlockSpec((tm, LANES), lambda i, j: (i, 0)),
        ],
        out_specs=pl.BlockSpec((tm, tn), lambda i, j: (i, j)),
        out_shape=jax.ShapeDtypeStruct((S, NSA_PROJ_WIDTH), BF16),
        scratch_shapes=[pltpu.VMEM((tm, D), BF16)],
        compiler_params=_cparams(("parallel", "arbitrary")),
        name="nsa_proj",
    )(h, g.reshape(1, D), w_cat, rc, rs1, rs2)


def _nsa_cmp_kernel(r_ref, w1_ref, pe_ref, w2_ref, rc_ref, rs1_ref, rs2_ref, o_ref, acc_ref, *, nc):
    kv = pl.program_id(0)
    hd = pl.program_id(1)
    half = CMP_STRIDE * HEAD_DIM
    r = r_ref[0, 0]
    w1 = w1_ref[0]
    first = _dot(r, w1[:half, :])
    second = _dot(r, w1[half:, :])
    pe_term = _dot(pe_ref[0], w1)[0:1, :]
    pre = first + pltpu.roll(second, nc - 1, 0) + pe_term
    ge = 0.5 * pre * (1.0 + jnp.tanh(0.7978845608028654 * (pre + 0.044715 * pre * pre * pre)))
    y = _dot(ge.astype(BF16), w2_ref[0, 0])

    @pl.when(hd == 0)
    def _():
        acc_ref[...] = y

    @pl.when(hd > 0)
    def _():
        acc_ref[...] += y

    @pl.when(hd == NSA_KV_HEADS - 1)
    def _():
        acc = acc_ref[...]

        @pl.when(kv == 0)
        def _():
            rc, rs1, rs2 = rc_ref[...], rs1_ref[...], rs2_ref[...]
            o_ref[0] = jnp.concatenate(
                [_rope(acc[:, :LANES], rc, rs1, rs2), _rope(acc[:, LANES:], rc, rs1, rs2)], axis=1).astype(o_ref.dtype)

        @pl.when(kv == 1)
        def _():
            o_ref[0] = acc.astype(o_ref.dtype)


def _nsa_cmp(r, w1, pe, w2p, rc, rs1, rs2):
    nc = r.shape[2]
    kvd = NSA_KV_HEADS * HEAD_DIM
    return pl.pallas_call(
        functools.partial(_nsa_cmp_kernel, nc=nc),
        grid=(2, NSA_KV_HEADS),
        in_specs=[
            pl.BlockSpec((1, 1, nc, CMP_STRIDE * HEAD_DIM), lambda a, b: (a, b, 0, 0)),
            pl.BlockSpec((1, CMP_BLOCK * HEAD_DIM, CMP_HIDDEN), lambda a, b: (a, 0, 0)),
            pl.BlockSpec((1, 8, CMP_BLOCK * HEAD_DIM), lambda a, b: (a, 0, 0)),
            pl.BlockSpec((1, 1, CMP_HIDDEN, kvd), lambda a, b: (a, b, 0, 0)),
            pl.BlockSpec((nc, LANES), lambda a, b: (0, 0)),
            pl.BlockSpec((nc, LANES), lambda a, b: (0, 0)),
            pl.BlockSpec((nc, LANES), lambda a, b: (0, 0)),
        ],
        out_specs=pl.BlockSpec((1, nc, kvd), lambda a, b: (a, 0, 0)),
        out_shape=jax.ShapeDtypeStruct((2, nc, kvd), BF16),
        scratch_shapes=[pltpu.VMEM((nc, kvd), F32)],
        compiler_params=_cparams(("parallel", "arbitrary")),
        name="nsa_cmp",
    )(r, w1, pe, w2p, rc, rs1, rs2)


def _nsa_cmp_attn_kernel(q_ref, kc_ref, vc_ref, ov_ref, oc_ref, sel_ref, *, t, nc, nsp):
    i = pl.program_id(1)
    qpos = i * t + lax.broadcasted_iota(jnp.int32, (t, 1), 0)
    cmp_end = CMP_STRIDE * lax.broadcasted_iota(jnp.int32, (1, nc), 1) + (CMP_BLOCK - 1)
    cmask = cmp_end <= qpos
    lane = lax.broadcasted_iota(jnp.int32, (t, LANES), 1)
    kc = kc_ref[0]
    vc = vc_ref[0]
    imp = [jnp.zeros((t, nc), F32), jnp.zeros((t, nc), F32)]
    for g in range(NSA_GROUP):
        qh = _head_halves(q_ref[:, g * LANES:(g + 1) * LANES])
        oc = []
        for h in range(2):
            s = jnp.where(cmask, _dot_nt(qh[h], kc), NEG)
            m = jnp.max(s, axis=1, keepdims=True)
            p = jnp.where(cmask, jnp.exp2(s - m), 0.0)
            l = jnp.sum(p, axis=1, keepdims=True)
            pn = p / jnp.maximum(l, 1e-30)
            imp[h] = imp[h] + pn
            oc.append(_dot(pn.astype(BF16), vc))
        oc_ref[:, g * LANES:(g + 1) * LANES] = jnp.where(lane < HEAD_DIM, oc[0], oc[1]).astype(oc_ref.dtype)

    ov = ov_ref[...]
    blk = lax.broadcasted_iota(jnp.int32, (1, nsp), 1)
    blk_f = blk.astype(F32)
    cur = jnp.right_shift(qpos, SLC_SHIFT)
    forced = jnp.logical_or(blk == 0, jnp.logical_or(blk == cur, blk == cur - 1))
    valid = blk * SLC_BLOCK <= qpos
    for h in range(2):
        hi, mid, lo = _split3(imp[h])
        score = _dot(hi, ov) + _dot(mid, ov) + _dot(lo, ov)
        vals = jnp.where(valid, jnp.where(forced, FORCE_SCORE, score), NEG)

        def pick_one(_, carry):
            vals, sel = carry
            mx = jnp.max(vals, axis=1, keepdims=True)
            first = jnp.min(jnp.where(vals == mx, blk_f, float(nsp)), axis=1, keepdims=True)
            pick = blk_f == first
            return jnp.where(pick, -jnp.inf, vals), jnp.where(pick, 1.0, sel)

        _, sel = lax.fori_loop(0, SLC_TOPK, pick_one, (vals, jnp.zeros((t, nsp), F32)))
        sel_ref[:, h * nsp:(h + 1) * nsp] = jnp.where(valid, sel, 0.0).astype(sel_ref.dtype)


def _nsa_cmp_attn(proj, kvc, ov):
    S = proj.shape[0]
    t = NSA_TILE
    nc, nsp = ov.shape
    return pl.pallas_call(
        functools.partial(_nsa_cmp_attn_kernel, t=t, nc=nc, nsp=nsp),
        grid=(NSA_KV_HEADS // 2, S // t),
        in_specs=[
            pl.BlockSpec((t, NSA_GROUP * LANES), lambda p, i: (i, p)),
            pl.BlockSpec((1, nc, LANES), lambda p, i: (0, 0, p)),
            pl.BlockSpec((1, nc, LANES), lambda p, i: (1, 0, p)),
            pl.BlockSpec((nc, nsp), lambda p, i: (0, 0)),
        ],
        out_specs=[pl.BlockSpec((t, NSA_GROUP * LANES), lambda p, i: (i, p)),
                   pl.BlockSpec((t, 2 * nsp), lambda p, i: (i, p))],
        out_shape=[jax.ShapeDtypeStruct((S, D_MODEL), BF16),
                   jax.ShapeDtypeStruct((S, NSA_KV_HEADS * nsp), BF16)],
        compiler_params=_cparams(("parallel", "parallel")),
        name="nsa_cmp_attn",
    )(proj, kvc, kvc, ov)


def _nsa_sel_kernel(q_ref, ks_ref, vs_ref, sel_ref, kw0_ref, kw1_ref, kw2_ref, vw0_ref, vw1_ref, vw2_ref,
                    gz_ref, e_ref, oc_ref, o_ref, m_ref, l_ref, acc_ref, *, t, nsp):
    i = pl.program_id(1)
    n_heads = 2 * NSA_GROUP
    lane = lax.broadcasted_iota(jnp.int32, (t, LANES), 1)
    low = lane < HEAD_DIM
    qs = []
    for g in range(NSA_GROUP):
        qs.extend(_head_halves(q_ref[:, g * LANES:(g + 1) * LANES]))
    sel_bias = [((sel_ref[:, h * nsp:(h + 1) * nsp].astype(F32) - 1.0) * -NEG).astype(BF16) for h in range(2)]
    m_ref[...] = jnp.full(m_ref.shape, NEG, F32)
    l_ref[...] = jnp.zeros(l_ref.shape, F32)
    acc_ref[...] = jnp.zeros(acc_ref.shape, F32)
    blk_row = lax.broadcasted_iota(jnp.int32, (nsp, t), 0)
    tok_col = lax.broadcasted_iota(jnp.int32, (nsp, t), 1)
    row = lax.broadcasted_iota(jnp.int32, (t, t), 0)
    col = lax.broadcasted_iota(jnp.int32, (t, t), 1)

    def tile(j, diag):
        off = pl.multiple_of(j * t, t)
        k = ks_ref[pl.ds(off, t), :]
        v = vs_ref[pl.ds(off, t), :]
        expand = jnp.where(blk_row == jnp.right_shift(off + tok_col, SLC_SHIFT), 1.0, 0.0).astype(BF16)
        for h in range(2):
            bias = _dot(sel_bias[h], expand)
            if diag:
                bias = jnp.where(row >= col, bias, NEG)
            for g in range(NSA_GROUP):
                idx = 2 * g + h
                _online_update(_dot_nt(qs[idx], k) + bias, v, m_ref, l_ref, acc_ref, idx)

    def body(j, carry):
        tile(j, False)
        return carry

    lax.fori_loop(0, i, body, 0)
    tile(i, True)

    n_win = WIN // t + 1
    k_win = jnp.concatenate([kw0_ref[...], kw1_ref[...], kw2_ref[...]], axis=0)
    v_win = jnp.concatenate([vw0_ref[...], vw1_ref[...], vw2_ref[...]], axis=0)
    qpos = i * t + lax.broadcasted_iota(jnp.int32, (t, n_win * t), 0)
    kpos = (i - (n_win - 1)) * t + lax.broadcasted_iota(jnp.int32, (t, n_win * t), 1)
    wmask = jnp.logical_and(jnp.logical_and(kpos <= qpos, kpos > qpos - WIN), kpos >= 0)
    ow = []
    for idx in range(n_heads):
        s = jnp.where(wmask, _dot_nt(qs[idx], k_win), NEG)
        m = jnp.max(s, axis=1, keepdims=True)
        p = jnp.exp2(s - m)
        l = jnp.sum(p, axis=1, keepdims=True)
        ow.append(_dot(p.astype(BF16), v_win) / l)

    gates = jax.nn.sigmoid(_dot(gz_ref[...], e_ref[0]))
    gw = NSA_GROUP * LANES
    for g in range(NSA_GROUP):
        sl = slice(g * LANES, (g + 1) * LANES)
        o_cmp = oc_ref[:, sl].astype(F32)
        o_slc = jnp.where(low, _online_result(l_ref, acc_ref, 2 * g), _online_result(l_ref, acc_ref, 2 * g + 1))
        o_win = jnp.where(low, ow[2 * g], ow[2 * g + 1])
        out = (gates[:, g * LANES:(g + 1) * LANES] * o_cmp
               + gates[:, gw + g * LANES:gw + (g + 1) * LANES] * o_slc
               + gates[:, 2 * gw + g * LANES:2 * gw + (g + 1) * LANES] * o_win)
        o_ref[:, sl] = out.astype(o_ref.dtype)


def _nsa_sel(proj, sel, o_cmp, gate_expand):
    S = proj.shape[0]
    t = NSA_TILE
    nsp = sel.shape[1] // NSA_KV_HEADS
    n_heads = 2 * NSA_GROUP

    def win_spec(col0, back):
        return pl.BlockSpec((t, LANES), lambda p, i: (jnp.maximum(i - back, 0), col0 + p))

    return pl.pallas_call(
        functools.partial(_nsa_sel_kernel, t=t, nsp=nsp),
        grid=(NSA_KV_HEADS // 2, S // t),
        in_specs=[
            pl.BlockSpec((t, NSA_GROUP * LANES), lambda p, i: (i, p)),
            pl.BlockSpec((S, LANES), lambda p, i: (0, 12 + p)),
            pl.BlockSpec((S, LANES), lambda p, i: (0, 14 + p)),
            pl.BlockSpec((t, 2 * nsp), lambda p, i: (i, p)),
            win_spec(16, 2), win_spec(16, 1), win_spec(16, 0),
            win_spec(18, 2), win_spec(18, 1), win_spec(18, 0),
            pl.BlockSpec((t, NSA_PROJ_TILE), lambda p, i: (i, 10)),
            pl.BlockSpec((1, NSA_PROJ_TILE, 3 * NSA_GROUP * LANES), lambda p, i: (p, 0, 0)),
            pl.BlockSpec((t, NSA_GROUP * LANES), lambda p, i: (i, p)),
        ],
        out_specs=pl.BlockSpec((t, NSA_GROUP * LANES), lambda p, i: (i, p)),
        out_shape=jax.ShapeDtypeStruct((S, D_MODEL), BF16),
        scratch_shapes=[pltpu.VMEM((n_heads, t, LANES), F32)] * 3,
        compiler_params=_cparams(("parallel", "arbitrary")),
        name="nsa_sel",
    )(proj, proj, proj, sel, proj, proj, proj, proj, proj, proj, proj, gate_expand, o_cmp)


def _rope_tables(positions):
    half = ROT_DIM // 2
    inv = ROPE_THETA ** (-jnp.arange(0, ROT_DIM, 2, dtype=F32) / ROT_DIM)
    ang = positions.astype(F32)[:, None] * inv[None, :]
    cos, sin = jnp.cos(ang), jnp.sin(ang)
    S = positions.shape[0]
    ones = jnp.ones((S, HEAD_DIM - ROT_DIM), F32)
    zeros_h = jnp.zeros((S, half), F32)
    zeros_r = jnp.zeros((S, HEAD_DIM - ROT_DIM), F32)
    rc = jnp.concatenate([cos, cos, ones], axis=1)
    rs1 = jnp.concatenate([-sin, zeros_h, zeros_r], axis=1)
    rs2 = jnp.concatenate([zeros_h, sin, zeros_r], axis=1)
    reps = LANES // HEAD_DIM
    return jnp.tile(rc, (1, reps)), jnp.tile(rs1, (1, reps)), jnp.tile(rs2, (1, reps))


def _pair_heads(w, axis):
    shape = w.shape
    w = w.reshape(shape[:axis] + (NSA_KV_HEADS // 2, 2, NSA_GROUP, HEAD_DIM) + shape[axis + 1:])
    w = jnp.swapaxes(w, axis + 1, axis + 2)
    return w.reshape(shape)


def _nsa_constants(S):
    nc = S // CMP_STRIDE
    n_cmp = (S - CMP_BLOCK) // CMP_STRIDE + 1
    ns = S // SLC_BLOCK
    nsp = -(-ns // LANES) * LANES
    ci = np.arange(nc)[:, None] * CMP_STRIDE
    st = np.arange(nsp)[None, :] * SLC_BLOCK
    ov = (ci < st + SLC_BLOCK) & (ci + CMP_BLOCK > st) & (np.arange(nc)[:, None] < n_cmp) & (np.arange(nsp)[None, :] < ns)
    e = np.zeros((NSA_KV_HEADS // 2, NSA_PROJ_TILE, 3 * NSA_GROUP * LANES), np.float32)
    for p in range(NSA_KV_HEADS // 2):
        for br in range(3):
            for g in range(NSA_GROUP):
                for hf in range(2):
                    r = br * N_HEADS + (2 * p + hf) * NSA_GROUP + g
                    c0 = br * NSA_GROUP * LANES + g * LANES + hf * HEAD_DIM
                    e[p, r, c0:c0 + HEAD_DIM] = 1.0
    cmp_end = np.minimum(np.arange(nc) * CMP_STRIDE + CMP_BLOCK - 1, S - 1)
    return nc, jnp.asarray(ov.astype(np.float32), BF16), jnp.asarray(e, BF16), cmp_end


def _nsa_layer(h, g, positions, w_in, pe_k, w1_k, w2_k, pe_v, w1_v, w2_v, w_o):
    S, D = h.shape
    qd = N_HEADS * HEAD_DIM
    kvd = NSA_KV_HEADS * HEAD_DIM
    nc, ov, gate_expand, cmp_end = _nsa_constants(S)
    rc, rs1, rs2 = _rope_tables(positions)

    wq = _pair_heads(w_in[:, :qd], 1) * QK_SCALE
    wg = jnp.pad(w_in[:, qd + 6 * kvd:], ((0, 0), (0, NSA_PROJ_TILE - 3 * N_HEADS)))
    w_cat = jnp.concatenate([wq, w_in[:, qd:qd + 6 * kvd], wg], axis=1).astype(BF16)
    proj = _nsa_proj(h, g, w_cat, rc, rs1, rs2)

    raw = proj[:, qd:qd + 2 * kvd].reshape(nc, CMP_STRIDE, 2, NSA_KV_HEADS, HEAD_DIM)
    raw = raw.transpose(2, 3, 0, 1, 4).reshape(2, NSA_KV_HEADS, nc, CMP_STRIDE * HEAD_DIM)
    w1 = jnp.stack([w1_k, w1_v]).astype(BF16)
    pe = jnp.stack([pe_k.reshape(1, -1), pe_v.reshape(1, -1)])
    pe = jnp.pad(pe, ((0, 0), (0, 7), (0, 0))).astype(BF16)
    w2 = jnp.stack([w2_k, w2_v])
    eye = jnp.eye(NSA_KV_HEADS, dtype=F32)
    w2p = (w2[:, None, :, None, :] * eye[None, :, None, :, None]).reshape(2, NSA_KV_HEADS, CMP_HIDDEN, kvd).astype(BF16)
    kvc = _nsa_cmp(raw, w1, pe, w2p, rc[cmp_end], rs1[cmp_end], rs2[cmp_end])

    o_cmp, sel = _nsa_cmp_attn(proj, kvc, ov)
    o = _nsa_sel(proj, sel, o_cmp, gate_expand)
    return _matmul_res(o, _pair_heads(w_o, 0).astype(BF16), h, "nsa_out")


def _trunk(x2, positions, p):
    h = _fox_layer(x2, p["l0_norm_mix"], p["l0_fox_w_qkv"], p["l0_fox_w_f"], p["l0_fox_b_f"], p["l0_fox_w_o"])
    h = _mlp(h, p["l0_norm_mlp"], p["l0_mlp_w1"].astype(BF16), p["l0_mlp_w2"].astype(BF16))
    h = _pool_layer(h, p["l1_norm_mix"], p["l1_pool_w"], p["l1_pool_scale"])
    h = _mlp(h, p["l1_norm_mlp"], p["l1_mlp_w1"].astype(BF16), p["l1_mlp_w2"].astype(BF16))
    h = _conv_layer(h, p["l2_norm_mix"], p["l2_conv_w_in"], p["l2_conv_w"], p["l2_conv_w_out"])
    h = _mlp(h, p["l2_norm_mlp"], p["l2_mlp_w1"].astype(BF16), p["l2_mlp_w2"].astype(BF16))
    h = _nsa_layer(h, p["l3_norm_mix"], positions, p["l3_nsa_w_in"], p["l3_nsa_cmp_pe_k"], p["l3_nsa_cmp_w1_k"],
                   p["l3_nsa_cmp_w2_k"], p["l3_nsa_cmp_pe_v"], p["l3_nsa_cmp_w1_v"], p["l3_nsa_cmp_w2_v"],
                   p["l3_nsa_w_o"])
    return _mlp(h, p["l3_norm_mlp"], p["l3_mlp_w1"].astype(BF16), p["l3_mlp_w2"].astype(BF16), p["final_norm"])


def kernel(x, positions, l0_norm_mix, l0_fox_w_qkv, l0_fox_w_f, l0_fox_b_f, l0_fox_w_o, l0_norm_mlp, l0_mlp_w1, l0_mlp_w2, l1_norm_mix, l1_pool_w, l1_pool_scale, l1_norm_mlp, l1_mlp_w1, l1_mlp_w2, l2_norm_mix, l2_conv_w_in, l2_conv_w, l2_conv_w_out, l2_norm_mlp, l2_mlp_w1, l2_mlp_w2, l3_norm_mix, l3_nsa_w_in, l3_nsa_cmp_pe_k, l3_nsa_cmp_w1_k, l3_nsa_cmp_w2_k, l3_nsa_cmp_pe_v, l3_nsa_cmp_w1_v, l3_nsa_cmp_w2_v, l3_nsa_w_o, l3_norm_mlp, l3_mlp_w1, l3_mlp_w2, final_norm):
    params = dict(locals())
    B, S, D = x.shape
    outs = [_trunk(x[b], positions, params) for b in range(B)]
    return jnp.stack(outs, axis=0)
```

```python
import functools

import numpy as np
import jax
import jax.numpy as jnp
from jax import lax
from jax.experimental import pallas as pl
from jax.experimental.pallas import tpu as pltpu

F32 = jnp.float32
BF16 = jnp.bfloat16

D_MODEL = 1024
HEAD_DIM = 64
N_HEADS = D_MODEL // HEAD_DIM
D_FF = 4 * D_MODEL
ROPE_THETA = 500000.0
ROT_DIM = HEAD_DIM // 4
RMS_EPS = 1e-6
POOL_WINDOWS = (2, 4, 8, 16)
POOL_GROUP = D_MODEL // len(POOL_WINDOWS)
POOL_HALO = 16
CONV_WIDTH = 3
CONV_HALO = 8
NSA_KV_HEADS = 4
NSA_GROUP = N_HEADS // NSA_KV_HEADS
CMP_BLOCK = 32
CMP_STRIDE = 16
CMP_HIDDEN = 256
SLC_BLOCK = 64
SLC_SHIFT = 6
SLC_TOPK = 16
WIN = 512
FORCE_SCORE = 1e9
NEG = -1e30
LOG2E = 1.4426950408889634
QK_SCALE = HEAD_DIM ** -0.5 * LOG2E

LANES = 128
V7X_VMEM_LIMIT = 56 * 1024 * 1024
NSA_TILE = WIN // 2
NSA_PROJ_TILE = 2 * LANES
NSA_PROJ_WIDTH = 22 * LANES
NSA_ROPE_TILES = (0, 1, 2, 3, 6, 8)


def _cparams(semantics):
    return pltpu.CompilerParams(dimension_semantics=semantics, vmem_limit_bytes=V7X_VMEM_LIMIT)


def _rms(x, g):
    return x * lax.rsqrt(jnp.mean(x * x, axis=-1, keepdims=True) + RMS_EPS) * g


def _dot(a, b):
    return jnp.dot(a, b, preferred_element_type=F32)


def _dot_nt(a, b):
    return lax.dot_general(a, b, (((1,), (1,)), ((), ())), preferred_element_type=F32)


def _split3(x):
    hi = x.astype(BF16)
    r = x - hi.astype(F32)
    mid = r.astype(BF16)
    lo = (r - mid.astype(F32)).astype(BF16)
    return hi, mid, lo


def _rope(a, rc, rs1, rs2):
    return a * rc + pltpu.roll(a, LANES - ROT_DIM // 2, 1) * rs1 + pltpu.roll(a, ROT_DIM // 2, 1) * rs2


def _head_halves(q_blk):
    lane = lax.broadcasted_iota(jnp.int32, q_blk.shape, 1)
    qf = q_blk.astype(F32)
    return (jnp.where(lane < HEAD_DIM, qf, 0.0).astype(BF16),
            jnp.where(lane >= HEAD_DIM, qf, 0.0).astype(BF16))


def _online_update(s_ref, p_ref, v, m_ref, l_ref, acc_ref, idx, causal):
    t, tk = s_ref.shape[1:]
    reps = tk // LANES
    s = s_ref[idx]
    if causal:
        s = jnp.where(lax.broadcasted_iota(jnp.int32, (t, tk), 0) >= lax.broadcasted_iota(jnp.int32, (t, tk), 1),
                      s, NEG)
    m_old = m_ref[idx]
    m_new = jnp.maximum(m_old, jnp.max(s, axis=1, keepdims=True))
    alpha = jnp.exp2(m_old - m_new)
    p = jnp.exp2(s - jnp.tile(m_new, (1, reps)))
    psum = p[:, :LANES]
    for r in range(1, reps):
        psum = psum + p[:, r * LANES:(r + 1) * LANES]
    l_ref[idx] = alpha * l_ref[idx] + psum
    m_ref[idx] = m_new
    if p_ref is None:
        acc_ref[idx] = alpha * acc_ref[idx] + _dot(p.astype(BF16), v)
    else:
        p_ref[idx] = p.astype(BF16)
        acc_ref[idx] = alpha * acc_ref[idx] + _dot(p_ref[idx], v)


def _online_result(l_ref, acc_ref, idx):
    return acc_ref[idx] / jnp.sum(l_ref[idx], axis=1, keepdims=True)


def _pipelined_tiles(i, scores, consume):
    scores(0, 0)

    def body(jj, carry):
        j = 2 * jj
        scores(j + 1, 1)
        consume(j, 0, False)
        scores(j + 2, 0)
        consume(j + 1, 1, False)
        return carry

    lax.fori_loop(0, i // 2, body, 0)

    @pl.when(i % 2 == 0)
    def _():
        consume(i, 0, True)

    @pl.when(i % 2 == 1)
    def _():
        scores(i, 1)
        consume(i - 1, 0, False)
        consume(i, 1, True)


def _mlp_kernel(x_ref, g_ref, w1_ref, w2_ref, *rest, nf, final):
    if final:
        fg_ref, o_ref, xn_ref = rest
    else:
        o_ref, xn_ref = rest
    f = pl.program_id(1)

    @pl.when(f == 0)
    def _():
        x = x_ref[...]
        xn_ref[...] = _rms(x, g_ref[...]).astype(BF16)
        o_ref[...] = x

    a = _dot(xn_ref[...], w1_ref[...])
    a = jnp.square(jnp.maximum(a, 0.0)).astype(BF16)
    o_ref[...] += _dot(a, w2_ref[...])

    if final:
        @pl.when(f == nf - 1)
        def _():
            o_ref[...] = _rms(o_ref[...], fg_ref[...])


def _mlp(h, g, w1, w2, final_g=None):
    S, D = h.shape
    F = w1.shape[1]
    tm = min(1024, S)
    tf = 1024
    nf = F // tf
    final = final_g is not None
    in_specs = [
        pl.BlockSpec((tm, D), lambda i, f: (i, 0)),
        pl.BlockSpec((1, D), lambda i, f: (0, 0)),
        pl.BlockSpec((D, tf), lambda i, f: (0, f)),
        pl.BlockSpec((tf, D), lambda i, f: (f, 0)),
    ]
    args = [h, g.reshape(1, D), w1, w2]
    if final:
        in_specs.append(pl.BlockSpec((1, D), lambda i, f: (0, 0)))
        args.append(final_g.reshape(1, D))
    return pl.pallas_call(
        functools.partial(_mlp_kernel, nf=nf, final=final),
        grid=(S // tm, nf),
        in_specs=in_specs,
        out_specs=pl.BlockSpec((tm, D), lambda i, f: (i, 0)),
        out_shape=jax.ShapeDtypeStruct((S, D), F32),
        scratch_shapes=[pltpu.VMEM((tm, D), BF16)],
        compiler_params=_cparams(("parallel", "arbitrary")),
        name="mlp",
    )(*args)


def _norm_matmul_kernel(x_ref, g_ref, w_ref, o_ref, xn_ref):
    @pl.when(pl.program_id(1) == 0)
    def _():
        xn_ref[...] = _rms(x_ref[...], g_ref[...]).astype(BF16)

    o_ref[...] = _dot(xn_ref[...], w_ref[...]).astype(o_ref.dtype)


def _norm_matmul(h, g, w, name):
    S, D = h.shape
    N = w.shape[1]
    tm = min(1024, S)
    tn = 512
    return pl.pallas_call(
        _norm_matmul_kernel,
        grid=(S // tm, N // tn),
        in_specs=[
            pl.BlockSpec((tm, D), lambda i, j: (i, 0)),
            pl.BlockSpec((1, D), lambda i, j: (0, 0)),
            pl.BlockSpec((D, tn), lambda i, j: (0, j)),
        ],
        out_specs=pl.BlockSpec((tm, tn), lambda i, j: (i, j)),
        out_shape=jax.ShapeDtypeStruct((S, N), BF16),
        scratch_shapes=[pltpu.VMEM((tm, D), BF16)],
        compiler_params=_cparams(("parallel", "arbitrary")),
        name=name,
    )(h, g.reshape(1, D), w)


def _matmul_res_kernel(a_ref, w_ref, r_ref, o_ref):
    o_ref[...] = r_ref[...] + _dot(a_ref[...], w_ref[...])


def _matmul_res(a, w, res, name):
    S, K = a.shape
    N = w.shape[1]
    tm = min(1024, S)
    return pl.pallas_call(
        _matmul_res_kernel,
        grid=(S // tm,),
        in_specs=[
            pl.BlockSpec((tm, K), lambda i: (i, 0)),
            pl.BlockSpec((K, N), lambda i: (0, 0)),
            pl.BlockSpec((tm, N), lambda i: (i, 0)),
        ],
        out_specs=pl.BlockSpec((tm, N), lambda i: (i, 0)),
        out_shape=jax.ShapeDtypeStruct((S, N), F32),
        compiler_params=_cparams(("parallel",)),
        name=name,
    )(a, w, res)


def _fox_gate_kernel(x_ref, g_ref, wf_ref, bf_ref, c_ref, carry_ref, *, tm):
    @pl.when(pl.program_id(0) == 0)
    def _():
        carry_ref[...] = jnp.zeros_like(carry_ref)

    xn = _rms(x_ref[...], g_ref[...]).astype(BF16)
    z = _dot(xn, wf_ref[...]) + bf_ref[...]
    logf = jnp.minimum(z, 0.0) - jnp.log(1.0 + jnp.exp(-jnp.abs(z)))
    row = lax.broadcasted_iota(jnp.int32, (tm, tm), 0)
    col = lax.broadcasted_iota(jnp.int32, (tm, tm), 1)
    tri = jnp.where(row >= col, 1.0, 0.0).astype(BF16)
    hi, mid, lo = _split3(logf)
    c = _dot(tri, hi) + _dot(tri, mid) + _dot(tri, lo) + carry_ref[...]
    carry_ref[...] = c[tm - 1:tm, :]
    c_ref[...] = c * LOG2E


def _fox_gate(h, g, wf, bf):
    S, D = h.shape
    tm = min(512, S)
    return pl.pallas_call(
        functools.partial(_fox_gate_kernel, tm=tm),
        grid=(S // tm,),
        in_specs=[
            pl.BlockSpec((tm, D), lambda i: (i, 0)),
            pl.BlockSpec((1, D), lambda i: (0, 0)),
            pl.BlockSpec((D, LANES), lambda i: (0, 0)),
            pl.BlockSpec((1, LANES), lambda i: (0, 0)),
        ],
        out_specs=pl.BlockSpec((tm, LANES), lambda i: (i, 0)),
        out_shape=jax.ShapeDtypeStruct((S, LANES), F32),
        scratch_shapes=[pltpu.VMEM((1, LANES), F32)],
        compiler_params=_cparams(("arbitrary",)),
        name="fox_gate",
    )(h, g.reshape(1, D), wf, bf)


def _fox_attn_kernel(q_ref, k_ref, v_ref, c_ref, o_ref, s0_ref, s1_ref, m_ref, l_ref, acc_ref, *, t):
    i = pl.program_id(1)
    qh = _head_halves(q_ref[...])
    m_ref[...] = jnp.full(m_ref.shape, NEG, F32)
    l_ref[...] = jnp.zeros(l_ref.shape, F32)
    acc_ref[...] = jnp.zeros(acc_ref.shape, F32)
    s_bufs = (s0_ref, s1_ref)

    def scores(j, buf):
        k = k_ref[pl.ds(pl.multiple_of(j * t, t), t), :]
        c = c_ref[0, j]
        for h in range(2):
            s_bufs[buf][h] = _dot_nt(qh[h], k) - c[h:h + 1, :]

    def consume(j, buf, diag):
        v = v_ref[pl.ds(pl.multiple_of(j * t, t), t), :]
        for h in range(2):
            _online_update(s_bufs[buf], None, v, m_ref, l_ref, acc_ref, h, diag)

    _pipelined_tiles(i, scores, consume)
    lane = lax.broadcasted_iota(jnp.int32, (t, LANES), 1)
    o = jnp.where(lane < HEAD_DIM, _online_result(l_ref, acc_ref, 0), _online_result(l_ref, acc_ref, 1))
    o_ref[...] = o.astype(o_ref.dtype)


def _fox_attn(qkv, c4):
    S = qkv.shape[0]
    t = min(512, S)
    n_pairs = N_HEADS // 2
    return pl.pallas_call(
        functools.partial(_fox_attn_kernel, t=t),
        grid=(n_pairs, S // t),
        in_specs=[
            pl.BlockSpec((t, LANES), lambda hp, i: (i, hp)),
            pl.BlockSpec((S, LANES), lambda hp, i: (0, n_pairs + hp)),
            pl.BlockSpec((S, LANES), lambda hp, i: (0, 2 * n_pairs + hp)),
            pl.BlockSpec((1, S // t, 2, t), lambda hp, i: (hp, 0, 0, 0)),
        ],
        out_specs=pl.BlockSpec((t, LANES), lambda hp, i: (i, hp)),
        out_shape=jax.ShapeDtypeStruct((S, D_MODEL), BF16),
        scratch_shapes=[pltpu.VMEM((2, t, t), F32)] * 2 + [pltpu.VMEM((2, t, LANES), F32)] * 3,
        compiler_params=_cparams(("parallel", "arbitrary")),
        name="fox_attn",
    )(qkv, qkv, qkv, c4)


def _fox_layer(h, g, w_qkv, w_f, b_f, w_o):
    S = h.shape[0]
    t = min(512, S)
    w_qkv = jnp.concatenate([w_qkv[:, :D_MODEL] * QK_SCALE, w_qkv[:, D_MODEL:]], axis=1).astype(BF16)
    wf = jnp.pad(w_f, ((0, 0), (0, LANES - N_HEADS))).astype(BF16)
    bf = jnp.pad(b_f, (0, LANES - N_HEADS)).reshape(1, LANES)
    c = _fox_gate(h, g, wf, bf)
    c4 = c[:, :N_HEADS].T.reshape(N_HEADS // 2, 2, S // t, t).transpose(0, 2, 1, 3)
    qkv = _norm_matmul(h, g, w_qkv, "fox_qkv")
    o = _fox_attn(qkv, c4)
    return _matmul_res(o, w_o.astype(BF16), h, "fox_out")


def _pool_kernel(x_ref, halo_ref, g_ref, w_ref, sc_ref, o_ref, *, tm):
    i = pl.program_id(0)
    x = x_ref[...]
    g = g_ref[...]
    xn = _rms(x, g)
    hn = jnp.where(i > 0, _rms(halo_ref[...], g), 0.0)
    xe = jnp.concatenate([hn, xn], axis=0)
    tpos = i * tm + lax.broadcasted_iota(jnp.int32, (tm, 1), 0)
    for gi, w in enumerate(POOL_WINDOWS):
        sl = slice(gi * POOL_GROUP, (gi + 1) * POOL_GROUP)
        s = xe[:, sl]
        k = 1
        while k < w:
            s = s + pltpu.roll(s, k, 0)
            k *= 2
        cnt = jnp.minimum(tpos + 1, w).astype(F32)
        d = (s[POOL_HALO:, :] / cnt - xn[:, sl]).astype(BF16)
        o_ref[:, sl] = x[:, sl] + _dot(d, w_ref[gi]) * sc_ref[:, sl]


def _pool_layer(h, g, w_pool, pool_scale):
    S, D = h.shape
    tm = min(1024, S)
    return pl.pallas_call(
        functools.partial(_pool_kernel, tm=tm),
        grid=(S // tm,),
        in_specs=[
            pl.BlockSpec((tm, D), lambda i: (i, 0)),
            pl.BlockSpec((POOL_HALO, D), lambda i: (jnp.maximum(i * (tm // POOL_HALO) - 1, 0), 0)),
            pl.BlockSpec((1, D), lambda i: (0, 0)),
            pl.BlockSpec((len(POOL_WINDOWS), POOL_GROUP, POOL_GROUP), lambda i: (0, 0, 0)),
            pl.BlockSpec((1, D), lambda i: (0, 0)),
        ],
        out_specs=pl.BlockSpec((tm, D), lambda i: (i, 0)),
        out_shape=jax.ShapeDtypeStruct((S, D), F32),
        compiler_params=_cparams(("parallel",)),
        name="pool",
    )(h, h, g.reshape(1, D), w_pool.astype(BF16), pool_scale.reshape(1, D))


def _conv_in_kernel(x_ref, g_ref, wb_ref, wc_ref, wu_ref, b_ref, z_ref, xn_ref):
    @pl.when(pl.program_id(1) == 0)
    def _():
        xn_ref[...] = _rms(x_ref[...], g_ref[...]).astype(BF16)

    xn = xn_ref[...]
    b_ref[...] = _dot(xn, wb_ref[...])
    z_ref[...] = _dot(xn, wc_ref[...]) * _dot(xn, wu_ref[...])


def _conv_out_kernel(b_ref, z_ref, zh_ref, cw_ref, w_ref, r_ref, o_ref, *, tm):
    i = pl.program_id(0)
    z = z_ref[...]
    zh = jnp.where(i > 0, zh_ref[...], 0.0)
    row = lax.broadcasted_iota(jnp.int32, (tm, 1), 0)
    prev1 = zh[CONV_HALO - 1:CONV_HALO, :]
    prev2 = zh[CONV_HALO - 2:CONV_HALO - 1, :]
    z1 = jnp.where(row == 0, prev1, pltpu.roll(z, 1, 0))
    z2 = jnp.where(row == 0, prev2, jnp.where(row == 1, prev1, pltpu.roll(z, 2, 0)))
    cw = cw_ref[...]
    conv = cw[0:1, :] * z2 + cw[1:2, :] * z1 + cw[2:3, :] * z
    y = (b_ref[...] * conv).astype(BF16)
    o_ref[...] = r_ref[...] + _dot(y, w_ref[...])


def _conv_layer(h, g, w_in, conv_w, w_out):
    S, D = h.shape
    tm = min(1024, S)
    tn = 512
    nj = D // tn
    w_in = w_in.astype(BF16)
    b, z = pl.pallas_call(
        _conv_in_kernel,
        grid=(S // tm, nj),
        in_specs=[
            pl.BlockSpec((tm, D), lambda i, j: (i, 0)),
            pl.BlockSpec((1, D), lambda i, j: (0, 0)),
            pl.BlockSpec((D, tn), lambda i, j: (0, j)),
            pl.BlockSpec((D, tn), lambda i, j: (0, nj + j)),
            pl.BlockSpec((D, tn), lambda i, j: (0, 2 * nj + j)),
        ],
        out_specs=[pl.BlockSpec((tm, tn), lambda i, j: (i, j)),
                   pl.BlockSpec((tm, tn), lambda i, j: (i, j))],
        out_shape=[jax.ShapeDtypeStruct((S, D), F32), jax.ShapeDtypeStruct((S, D), F32)],
        scratch_shapes=[pltpu.VMEM((tm, D), BF16)],
        compiler_params=_cparams(("parallel", "arbitrary")),
        name="conv_in",
    )(h, g.reshape(1, D), w_in, w_in, w_in)
    cw = jnp.pad(conv_w, ((0, 8 - CONV_WIDTH), (0, 0)))
    return pl.pallas_call(
        functools.partial(_conv_out_kernel, tm=tm),
        grid=(S // tm,),
        in_specs=[
            pl.BlockSpec((tm, D), lambda i: (i, 0)),
            pl.BlockSpec((tm, D), lambda i: (i, 0)),
            pl.BlockSpec((CONV_HALO, D), lambda i: (jnp.maximum(i * (tm // CONV_HALO) - 1, 0), 0)),
            pl.BlockSpec((8, D), lambda i: (0, 0)),
            pl.BlockSpec((D, D), lambda i: (0, 0)),
            pl.BlockSpec((tm, D), lambda i: (i, 0)),
        ],
        out_specs=pl.BlockSpec((tm, D), lambda i: (i, 0)),
        out_shape=jax.ShapeDtypeStruct((S, D), F32),
        compiler_params=_cparams(("parallel",)),
        name="conv_out",
    )(b, z, z, cw, w_out.astype(BF16), h)


def _nsa_proj_kernel(x_ref, g_ref, w_ref, rc_ref, rs1_ref, rs2_ref, o_ref, xn_ref):
    j = pl.program_id(1)

    @pl.when(j == 0)
    def _():
        xn_ref[...] = _rms(x_ref[...], g_ref[...]).astype(BF16)

    a = _dot(xn_ref[...], w_ref[...])
    is_rope = functools.reduce(jnp.logical_or, [j == t for t in NSA_ROPE_TILES])

    @pl.when(is_rope)
    def _():
        rc, rs1, rs2 = rc_ref[...], rs1_ref[...], rs2_ref[...]
        o_ref[...] = jnp.concatenate(
            [_rope(a[:, :LANES], rc, rs1, rs2), _rope(a[:, LANES:], rc, rs1, rs2)], axis=1).astype(o_ref.dtype)

    @pl.when(jnp.logical_not(is_rope))
    def _():
        o_ref[...] = a.astype(o_ref.dtype)


def _nsa_proj(h, g, w_cat, rc, rs1, rs2):
    S, D = h.shape
    tm = min(1024, S)
    tn = NSA_PROJ_TILE
    return pl.pallas_call(
        _nsa_proj_kernel,
        grid=(S // tm, NSA_PROJ_WIDTH // tn),
        in_specs=[
            pl.BlockSpec((tm, D), lambda i, j: (i, 0)),
            pl.BlockSpec((1, D), lambda i, j: (0, 0)),
            pl.BlockSpec((D, tn), lambda i, j: (0, j)),
            pl.BlockSpec((tm, LANES), lambda i, j: (i, 0)),
            pl.BlockSpec((tm, LANES), lambda i, j: (i, 0)),
            pl.BlockSpec((tm, LANES), lambda i, j: (i, 0)),
        ],
        out_specs=pl.BlockSpec((tm, tn), lambda i, j: (i, j)),
        out_shape=jax.ShapeDtypeStruct((S, NSA_PROJ_WIDTH), BF16),
        scratch_shapes=[pltpu.VMEM((tm, D), BF16)],
        compiler_params=_cparams(("parallel", "arbitrary")),
        name="nsa_proj",
    )(h, g.reshape(1, D), w_cat, rc, rs1, rs2)


def _nsa_cmp_kernel(r_ref, w1_ref, pe_ref, w2_ref, rc_ref, rs1_ref, rs2_ref, o_ref, acc_ref, *, nc):
    kv = pl.program_id(0)
    hd = pl.program_id(1)
    half = CMP_STRIDE * HEAD_DIM
    r = r_ref[0, 0]
    w1 = w1_ref[0]
    first = _dot(r, w1[:half, :])
    second = _dot(r, w1[half:, :])
    pe_term = _dot(pe_ref[0], w1)[0:1, :]
    pre = first + pltpu.roll(second, nc - 1, 0) + pe_term
    ge = 0.5 * pre * (1.0 + jnp.tanh(0.7978845608028654 * (pre + 0.044715 * pre * pre * pre)))
    y = _dot(ge.astype(BF16), w2_ref[0, 0])

    @pl.when(hd == 0)
    def _():
        acc_ref[...] = y

    @pl.when(hd > 0)
    def _():
        acc_ref[...] += y

    @pl.when(hd == NSA_KV_HEADS - 1)
    def _():
        acc = acc_ref[...]

        @pl.when(kv == 0)
        def _():
            rc, rs1, rs2 = rc_ref[...], rs1_ref[...], rs2_ref[...]
            o_ref[0] = jnp.concatenate(
                [_rope(acc[:, :LANES], rc, rs1, rs2), _rope(acc[:, LANES:], rc, rs1, rs2)], axis=1).astype(o_ref.dtype)

        @pl.when(kv == 1)
        def _():
            o_ref[0] = acc.astype(o_ref.dtype)


def _nsa_cmp(r, w1, pe, w2p, rc, rs1, rs2):
    nc = r.shape[2]
    kvd = NSA_KV_HEADS * HEAD_DIM
    return pl.pallas_call(
        functools.partial(_nsa_cmp_kernel, nc=nc),
        grid=(2, NSA_KV_HEADS),
        in_specs=[
            pl.BlockSpec((1, 1, nc, CMP_STRIDE * HEAD_DIM), lambda a, b: (a, b, 0, 0)),
            pl.BlockSpec((1, CMP_BLOCK * HEAD_DIM, CMP_HIDDEN), lambda a, b: (a, 0, 0)),
            pl.BlockSpec((1, 8, CMP_BLOCK * HEAD_DIM), lambda a, b: (a, 0, 0)),
            pl.BlockSpec((1, 1, CMP_HIDDEN, kvd), lambda a, b: (a, b, 0, 0)),
            pl.BlockSpec((nc, LANES), lambda a, b: (0, 0)),
            pl.BlockSpec((nc, LANES), lambda a, b: (0, 0)),
            pl.BlockSpec((nc, LANES), lambda a, b: (0, 0)),
        ],
        out_specs=pl.BlockSpec((1, nc, kvd), lambda a, b: (a, 0, 0)),
        out_shape=jax.ShapeDtypeStruct((2, nc, kvd), BF16),
        scratch_shapes=[pltpu.VMEM((nc, kvd), F32)],
        compiler_params=_cparams(("parallel", "arbitrary")),
        name="nsa_cmp",
    )(r, w1, pe, w2p, rc, rs1, rs2)


def _nsa_cmp_attn_kernel(q_ref, kc_ref, vc_ref, ov_ref, oc_ref, sel_ref, *, t, nc, nsp):
    i = pl.program_id(0)
    qpos = i * t + lax.broadcasted_iota(jnp.int32, (t, 1), 0)
    cmp_end = CMP_STRIDE * lax.broadcasted_iota(jnp.int32, (1, nc), 1) + (CMP_BLOCK - 1)
    cmask = cmp_end <= qpos
    lane = lax.broadcasted_iota(jnp.int32, (t, LANES), 1)
    ov = ov_ref[...]
    blk = lax.broadcasted_iota(jnp.int32, (1, nsp), 1)
    blk_f = blk.astype(F32)
    cur = jnp.right_shift(qpos, SLC_SHIFT)
    forced = jnp.logical_or(blk == 0, jnp.logical_or(blk == cur, blk == cur - 1))
    valid = blk * SLC_BLOCK <= qpos
    vals = []
    for pair in range(NSA_KV_HEADS // 2):
        kc = kc_ref[0, :, pair * LANES:(pair + 1) * LANES]
        vc = vc_ref[0, :, pair * LANES:(pair + 1) * LANES]
        imp = [jnp.zeros((t, nc), F32), jnp.zeros((t, nc), F32)]
        for g in range(NSA_GROUP):
            blk_sl = slice((pair * NSA_GROUP + g) * LANES, (pair * NSA_GROUP + g + 1) * LANES)
            qh = _head_halves(q_ref[:, blk_sl])
            oc = []
            for h in range(2):
                s = jnp.where(cmask, _dot_nt(qh[h], kc), NEG)
                m = jnp.max(s, axis=1, keepdims=True)
                p = jnp.where(cmask, jnp.exp2(s - m), 0.0)
                l = jnp.sum(p, axis=1, keepdims=True)
                pn = p / jnp.maximum(l, 1e-30)
                imp[h] = imp[h] + pn
                oc.append(_dot(pn.astype(BF16), vc))
            oc_ref[:, blk_sl] = jnp.where(lane < HEAD_DIM, oc[0], oc[1]).astype(oc_ref.dtype)
        for h in range(2):
            hi, mid, lo = _split3(imp[h])
            score = _dot(hi, ov) + _dot(mid, ov) + _dot(lo, ov)
            vals.append(jnp.where(valid, jnp.where(forced, FORCE_SCORE, score), NEG))

    def pick_one(_, vals):
        out = []
        for v in vals:
            mx = jnp.max(v, axis=1, keepdims=True)
            first = jnp.min(jnp.where(v == mx, blk_f, float(nsp)), axis=1, keepdims=True)
            out.append(jnp.where(blk_f == first, -jnp.inf, v))
        return tuple(out)

    vals = lax.fori_loop(0, SLC_TOPK, pick_one, tuple(vals))
    for kvh in range(NSA_KV_HEADS):
        picked = jnp.logical_and(valid, vals[kvh] == -jnp.inf)
        sel_ref[:, kvh * nsp:(kvh + 1) * nsp] = jnp.where(picked, 1.0, 0.0).astype(sel_ref.dtype)


def _nsa_cmp_attn(proj, kvc, ov):
    S = proj.shape[0]
    t = NSA_TILE
    nc, nsp = ov.shape
    kvd = NSA_KV_HEADS * HEAD_DIM
    return pl.pallas_call(
        functools.partial(_nsa_cmp_attn_kernel, t=t, nc=nc, nsp=nsp),
        grid=(S // t,),
        in_specs=[
            pl.BlockSpec((t, D_MODEL), lambda i: (i, 0)),
            pl.BlockSpec((1, nc, kvd), lambda i: (0, 0, 0)),
            pl.BlockSpec((1, nc, kvd), lambda i: (1, 0, 0)),
            pl.BlockSpec((nc, nsp), lambda i: (0, 0)),
        ],
        out_specs=[pl.BlockSpec((t, D_MODEL), lambda i: (i, 0)),
                   pl.BlockSpec((t, NSA_KV_HEADS * nsp), lambda i: (i, 0))],
        out_shape=[jax.ShapeDtypeStruct((S, D_MODEL), BF16),
                   jax.ShapeDtypeStruct((S, NSA_KV_HEADS * nsp), BF16)],
        compiler_params=_cparams(("parallel",)),
        name="nsa_cmp_attn",
    )(proj, kvc, kvc, ov)


def _nsa_sel_kernel(q_ref, ks_ref, vs_ref, sel_ref, kw0_ref, kw1_ref, kw2_ref, vw0_ref, vw1_ref, vw2_ref,
                    gz_ref, e_ref, oc_ref, o_ref, s0_ref, s1_ref, p_ref, m_ref, l_ref, acc_ref, *, t, nsp):
    i = pl.program_id(1)
    n_heads = 2 * NSA_GROUP
    lane = lax.broadcasted_iota(jnp.int32, (t, LANES), 1)
    low = lane < HEAD_DIM
    qs = []
    for g in range(NSA_GROUP):
        qs.extend(_head_halves(q_ref[:, g * LANES:(g + 1) * LANES]))
    sel_bias = [((sel_ref[:, h * nsp:(h + 1) * nsp].astype(F32) - 1.0) * -NEG).astype(BF16) for h in range(2)]
    m_ref[...] = jnp.full(m_ref.shape, NEG, F32)
    l_ref[...] = jnp.zeros(l_ref.shape, F32)
    acc_ref[...] = jnp.zeros(acc_ref.shape, F32)
    blk_row = lax.broadcasted_iota(jnp.int32, (nsp, t), 0)
    tok_col = lax.broadcasted_iota(jnp.int32, (nsp, t), 1)
    s_bufs = (s0_ref, s1_ref)

    def scores(j, buf):
        off = pl.multiple_of(j * t, t)
        k = ks_ref[pl.ds(off, t), :]
        expand = jnp.where(blk_row == jnp.right_shift(off + tok_col, SLC_SHIFT), 1.0, 0.0).astype(BF16)
        for h in range(2):
            bias = _dot(sel_bias[h], expand)
            for g in range(NSA_GROUP):
                s_bufs[buf][2 * g + h] = _dot_nt(qs[2 * g + h], k) + bias

    def consume(j, buf, diag):
        v = vs_ref[pl.ds(pl.multiple_of(j * t, t), t), :]
        for idx in range(n_heads):
            _online_update(s_bufs[buf], p_ref, v, m_ref, l_ref, acc_ref, idx, diag)

    _pipelined_tiles(i, scores, consume)

    n_win = WIN // t + 1
    k_win = jnp.concatenate([kw0_ref[...], kw1_ref[...], kw2_ref[...]], axis=0)
    v_win = jnp.concatenate([vw0_ref[...], vw1_ref[...], vw2_ref[...]], axis=0)
    qpos = i * t + lax.broadcasted_iota(jnp.int32, (t, n_win * t), 0)
    kpos = (i - (n_win - 1)) * t + lax.broadcasted_iota(jnp.int32, (t, n_win * t), 1)
    wmask = jnp.logical_and(jnp.logical_and(kpos <= qpos, kpos > qpos - WIN), kpos >= 0)
    ow = []
    for idx in range(n_heads):
        s = jnp.where(wmask, _dot_nt(qs[idx], k_win), NEG)
        m = jnp.max(s, axis=1, keepdims=True)
        p = jnp.exp2(s - m)
        l = jnp.sum(p, axis=1, keepdims=True)
        ow.append(_dot(p.astype(BF16), v_win) / l)

    gates = jax.nn.sigmoid(_dot(gz_ref[...], e_ref[0]))
    gw = NSA_GROUP * LANES
    for g in range(NSA_GROUP):
        sl = slice(g * LANES, (g + 1) * LANES)
        o_cmp = oc_ref[:, sl].astype(F32)
        o_slc = jnp.where(low, _online_result(l_ref, acc_ref, 2 * g), _online_result(l_ref, acc_ref, 2 * g + 1))
        o_win = jnp.where(low, ow[2 * g], ow[2 * g + 1])
        out = (gates[:, g * LANES:(g + 1) * LANES] * o_cmp
               + gates[:, gw + g * LANES:gw + (g + 1) * LANES] * o_slc
               + gates[:, 2 * gw + g * LANES:2 * gw + (g + 1) * LANES] * o_win)
        o_ref[:, sl] = out.astype(o_ref.dtype)


def _nsa_sel(proj, sel, o_cmp, gate_expand):
    S = proj.shape[0]
    t = NSA_TILE
    nsp = sel.shape[1] // NSA_KV_HEADS
    n_heads = 2 * NSA_GROUP

    def win_spec(col0, back):
        return pl.BlockSpec((t, LANES), lambda p, i: (jnp.maximum(i - back, 0), col0 + p))

    return pl.pallas_call(
        functools.partial(_nsa_sel_kernel, t=t, nsp=nsp),
        grid=(NSA_KV_HEADS // 2, S // t),
        in_specs=[
            pl.BlockSpec((t, NSA_GROUP * LANES), lambda p, i: (i, p)),
            pl.BlockSpec((S, LANES), lambda p, i: (0, 12 + p)),
            pl.BlockSpec((S, LANES), lambda p, i: (0, 14 + p)),
            pl.BlockSpec((t, 2 * nsp), lambda p, i: (i, p)),
            win_spec(16, 2), win_spec(16, 1), win_spec(16, 0),
            win_spec(18, 2), win_spec(18, 1), win_spec(18, 0),
            pl.BlockSpec((t, NSA_PROJ_TILE), lambda p, i: (i, 10)),
            pl.BlockSpec((1, NSA_PROJ_TILE, 3 * NSA_GROUP * LANES), lambda p, i: (p, 0, 0)),
            pl.BlockSpec((t, NSA_GROUP * LANES), lambda p, i: (i, p)),
        ],
        out_specs=pl.BlockSpec((t, NSA_GROUP * LANES), lambda p, i: (i, p)),
        out_shape=jax.ShapeDtypeStruct((S, D_MODEL), BF16),
        scratch_shapes=([pltpu.VMEM((n_heads, t, t), F32)] * 2 + [pltpu.VMEM((n_heads, t, t), BF16)]
                        + [pltpu.VMEM((n_heads, t, LANES), F32)] * 3),
        compiler_params=_cparams(("parallel", "arbitrary")),
        name="nsa_sel",
    )(proj, proj, proj, sel, proj, proj, proj, proj, proj, proj, proj, gate_expand, o_cmp)


def _rope_tables(positions):
    half = ROT_DIM // 2
    inv = ROPE_THETA ** (-jnp.arange(0, ROT_DIM, 2, dtype=F32) / ROT_DIM)
    ang = positions.astype(F32)[:, None] * inv[None, :]
    cos, sin = jnp.cos(ang), jnp.sin(ang)
    S = positions.shape[0]
    ones = jnp.ones((S, HEAD_DIM - ROT_DIM), F32)
    zeros_h = jnp.zeros((S, half), F32)
    zeros_r = jnp.zeros((S, HEAD_DIM - ROT_DIM), F32)
    rc = jnp.concatenate([cos, cos, ones], axis=1)
    rs1 = jnp.concatenate([-sin, zeros_h, zeros_r], axis=1)
    rs2 = jnp.concatenate([zeros_h, sin, zeros_r], axis=1)
    reps = LANES // HEAD_DIM
    return jnp.tile(rc, (1, reps)), jnp.tile(rs1, (1, reps)), jnp.tile(rs2, (1, reps))


def _pair_heads(w, axis):
    shape = w.shape
    w = w.reshape(shape[:axis] + (NSA_KV_HEADS // 2, 2, NSA_GROUP, HEAD_DIM) + shape[axis + 1:])
    w = jnp.swapaxes(w, axis + 1, axis + 2)
    return w.reshape(shape)


def _nsa_constants(S):
    nc = S // CMP_STRIDE
    n_cmp = (S - CMP_BLOCK) // CMP_STRIDE + 1
    ns = S // SLC_BLOCK
    nsp = -(-ns // LANES) * LANES
    ci = np.arange(nc)[:, None] * CMP_STRIDE
    st = np.arange(nsp)[None, :] * SLC_BLOCK
    ov = (ci < st + SLC_BLOCK) & (ci + CMP_BLOCK > st) & (np.arange(nc)[:, None] < n_cmp) & (np.arange(nsp)[None, :] < ns)
    e = np.zeros((NSA_KV_HEADS // 2, NSA_PROJ_TILE, 3 * NSA_GROUP * LANES), np.float32)
    for p in range(NSA_KV_HEADS // 2):
        for br in range(3):
            for g in range(NSA_GROUP):
                for hf in range(2):
                    r = br * N_HEADS + (2 * p + hf) * NSA_GROUP + g
                    c0 = br * NSA_GROUP * LANES + g * LANES + hf * HEAD_DIM
                    e[p, r, c0:c0 + HEAD_DIM] = 1.0
    cmp_end = np.minimum(np.arange(nc) * CMP_STRIDE + CMP_BLOCK - 1, S - 1)
    return nc, jnp.asarray(ov.astype(np.float32), BF16), jnp.asarray(e, BF16), cmp_end


def _nsa_layer(h, g, positions, w_in, pe_k, w1_k, w2_k, pe_v, w1_v, w2_v, w_o):
    S, D = h.shape
    qd = N_HEADS * HEAD_DIM
    kvd = NSA_KV_HEADS * HEAD_DIM
    nc, ov, gate_expand, cmp_end = _nsa_constants(S)
    rc, rs1, rs2 = _rope_tables(positions)

    wq = _pair_heads(w_in[:, :qd], 1) * QK_SCALE
    wg = jnp.pad(w_in[:, qd + 6 * kvd:], ((0, 0), (0, NSA_PROJ_TILE - 3 * N_HEADS)))
    w_cat = jnp.concatenate([wq, w_in[:, qd:qd + 6 * kvd], wg], axis=1).astype(BF16)
    proj = _nsa_proj(h, g, w_cat, rc, rs1, rs2)

    raw = proj[:, qd:qd + 2 * kvd].reshape(nc, CMP_STRIDE, 2, NSA_KV_HEADS, HEAD_DIM)
    raw = raw.transpose(2, 3, 0, 1, 4).reshape(2, NSA_KV_HEADS, nc, CMP_STRIDE * HEAD_DIM)
    w1 = jnp.stack([w1_k, w1_v]).astype(BF16)
    pe = jnp.stack([pe_k.reshape(1, -1), pe_v.reshape(1, -1)])
    pe = jnp.pad(pe, ((0, 0), (0, 7), (0, 0))).astype(BF16)
    w2 = jnp.stack([w2_k, w2_v])
    eye = jnp.eye(NSA_KV_HEADS, dtype=F32)
    w2p = (w2[:, None, :, None, :] * eye[None, :, None, :, None]).reshape(2, NSA_KV_HEADS, CMP_HIDDEN, kvd).astype(BF16)
    kvc = _nsa_cmp(raw, w1, pe, w2p, rc[cmp_end], rs1[cmp_end], rs2[cmp_end])

    o_cmp, sel = _nsa_cmp_attn(proj, kvc, ov)
    o = _nsa_sel(proj, sel, o_cmp, gate_expand)
    return _matmul_res(o, _pair_heads(w_o, 0).astype(BF16), h, "nsa_out")


def _trunk(x2, positions, p):
    h = _fox_layer(x2, p["l0_norm_mix"], p["l0_fox_w_qkv"], p["l0_fox_w_f"], p["l0_fox_b_f"], p["l0_fox_w_o"])
    h = _mlp(h, p["l0_norm_mlp"], p["l0_mlp_w1"].astype(BF16), p["l0_mlp_w2"].astype(BF16))
    h = _pool_layer(h, p["l1_norm_mix"], p["l1_pool_w"], p["l1_pool_scale"])
    h = _mlp(h, p["l1_norm_mlp"], p["l1_mlp_w1"].astype(BF16), p["l1_mlp_w2"].astype(BF16))
    h = _conv_layer(h, p["l2_norm_mix"], p["l2_conv_w_in"], p["l2_conv_w"], p["l2_conv_w_out"])
    h = _mlp(h, p["l2_norm_mlp"], p["l2_mlp_w1"].astype(BF16), p["l2_mlp_w2"].astype(BF16))
    h = _nsa_layer(h, p["l3_norm_mix"], positions, p["l3_nsa_w_in"], p["l3_nsa_cmp_pe_k"], p["l3_nsa_cmp_w1_k"],
                   p["l3_nsa_cmp_w2_k"], p["l3_nsa_cmp_pe_v"], p["l3_nsa_cmp_w1_v"], p["l3_nsa_cmp_w2_v"],
                   p["l3_nsa_w_o"])
    return _mlp(h, p["l3_norm_mlp"], p["l3_mlp_w1"].astype(BF16), p["l3_mlp_w2"].astype(BF16), p["final_norm"])


def kernel(x, positions, l0_norm_mix, l0_fox_w_qkv, l0_fox_w_f, l0_fox_b_f, l0_fox_w_o, l0_norm_mlp, l0_mlp_w1, l0_mlp_w2, l1_norm_mix, l1_pool_w, l1_pool_scale, l1_norm_mlp, l1_mlp_w1, l1_mlp_w2, l2_norm_mix, l2_conv_w_in, l2_conv_w, l2_conv_w_out, l2_norm_mlp, l2_mlp_w1, l2_mlp_w2, l3_norm_mix, l3_nsa_w_in, l3_nsa_cmp_pe_k, l3_nsa_cmp_w1_k, l3_nsa_cmp_w2_k, l3_nsa_cmp_pe_v, l3_nsa_cmp_w1_v, l3_nsa_cmp_w2_v, l3_nsa_w_o, l3_norm_mlp, l3_mlp_w1, l3_mlp_w2, final_norm):
    params = dict(locals())
    B, S, D = x.shape
    outs = [_trunk(x[b], positions, params) for b in range(B)]
    return jnp.stack(outs, axis=0)
```

```python
import functools

import numpy as np
import jax
import jax.numpy as jnp
from jax import lax
from jax.experimental import pallas as pl
from jax.experimental.pallas import tpu as pltpu

F32 = jnp.float32
BF16 = jnp.bfloat16

D_MODEL = 1024
HEAD_DIM = 64
N_HEADS = D_MODEL // HEAD_DIM
D_FF = 4 * D_MODEL
ROPE_THETA = 500000.0
ROT_DIM = HEAD_DIM // 4
RMS_EPS = 1e-6
POOL_WINDOWS = (2, 4, 8, 16)
POOL_GROUP = D_MODEL // len(POOL_WINDOWS)
POOL_HALO = 16
CONV_WIDTH = 3
CONV_HALO = 8
NSA_KV_HEADS = 4
NSA_GROUP = N_HEADS // NSA_KV_HEADS
CMP_BLOCK = 32
CMP_STRIDE = 16
CMP_HIDDEN = 256
SLC_BLOCK = 64
SLC_SHIFT = 6
SLC_TOPK = 16
WIN = 512
FORCE_SCORE = 1e9
NEG = -1e30
LOG2E = 1.4426950408889634
QK_SCALE = HEAD_DIM ** -0.5 * LOG2E

LANES = 128
V7X_VMEM_LIMIT = 56 * 1024 * 1024
NSA_TILE = WIN // 2
NSA_PROJ_TILE = 2 * LANES
NSA_PROJ_WIDTH = 22 * LANES
NSA_ROPE_TILES = (0, 1, 2, 3, 6, 8)


def _cparams(semantics):
    return pltpu.CompilerParams(dimension_semantics=semantics, vmem_limit_bytes=V7X_VMEM_LIMIT)


def _rms(x, g):
    return x * lax.rsqrt(jnp.mean(x * x, axis=-1, keepdims=True) + RMS_EPS) * g


def _dot(a, b):
    return jnp.dot(a, b, preferred_element_type=F32)


def _dot_nt(a, b):
    return lax.dot_general(a, b, (((1,), (1,)), ((), ())), preferred_element_type=F32)


def _split3(x):
    hi = x.astype(BF16)
    r = x - hi.astype(F32)
    mid = r.astype(BF16)
    lo = (r - mid.astype(F32)).astype(BF16)
    return hi, mid, lo


def _rope(a, rc, rs1, rs2):
    return a * rc + pltpu.roll(a, LANES - ROT_DIM // 2, 1) * rs1 + pltpu.roll(a, ROT_DIM // 2, 1) * rs2


def _head_halves(q_blk):
    lane = lax.broadcasted_iota(jnp.int32, q_blk.shape, 1)
    qf = q_blk.astype(F32)
    return (jnp.where(lane < HEAD_DIM, qf, 0.0).astype(BF16),
            jnp.where(lane >= HEAD_DIM, qf, 0.0).astype(BF16))


def _online_update(s_ref, p_ref, v, m_ref, l_ref, acc_ref, idx, causal):
    t, tk = s_ref.shape[1:]
    reps = tk // LANES
    s = s_ref[idx]
    if causal:
        s = jnp.where(lax.broadcasted_iota(jnp.int32, (t, tk), 0) >= lax.broadcasted_iota(jnp.int32, (t, tk), 1),
                      s, NEG)
    m_old = m_ref[idx]
    m_new = jnp.maximum(m_old, jnp.max(s, axis=1, keepdims=True))
    alpha = jnp.exp2(m_old - m_new)
    p = jnp.exp2(s - jnp.tile(m_new, (1, reps)))
    psum = p[:, :LANES]
    for r in range(1, reps):
        psum = psum + p[:, r * LANES:(r + 1) * LANES]
    l_ref[idx] = alpha * l_ref[idx] + psum
    m_ref[idx] = m_new
    if p_ref is None:
        acc_ref[idx] = alpha * acc_ref[idx] + _dot(p.astype(BF16), v)
    else:
        p_ref[idx] = p.astype(BF16)
        acc_ref[idx] = alpha * acc_ref[idx] + _dot(p_ref[idx], v)


def _online_result(l_ref, acc_ref, idx):
    return acc_ref[idx] / jnp.sum(l_ref[idx], axis=1, keepdims=True)


def _pipelined_tiles(i, scores, consume):
    scores(0, 0)

    def body(jj, carry):
        j = 2 * jj
        scores(j + 1, 1)
        consume(j, 0, False)
        scores(j + 2, 0)
        consume(j + 1, 1, False)
        return carry

    lax.fori_loop(0, i // 2, body, 0)

    @pl.when(i % 2 == 0)
    def _():
        consume(i, 0, True)

    @pl.when(i % 2 == 1)
    def _():
        scores(i, 1)
        consume(i - 1, 0, False)
        consume(i, 1, True)


def _mlp_kernel(x_ref, g_ref, w1_ref, w2_ref, *rest, nf, final):
    if final:
        fg_ref, o_ref, xn_ref = rest
    else:
        o_ref, xn_ref = rest
    f = pl.program_id(1)

    @pl.when(f == 0)
    def _():
        x = x_ref[...]
        xn_ref[...] = _rms(x, g_ref[...]).astype(BF16)
        o_ref[...] = x

    a = _dot(xn_ref[...], w1_ref[...])
    a = jnp.square(jnp.maximum(a, 0.0)).astype(BF16)
    o_ref[...] += _dot(a, w2_ref[...])

    if final:
        @pl.when(f == nf - 1)
        def _():
            o_ref[...] = _rms(o_ref[...], fg_ref[...])


def _mlp(h, g, w1, w2, final_g=None):
    S, D = h.shape
    F = w1.shape[1]
    tm = min(1024, S)
    tf = 1024
    nf = F // tf
    final = final_g is not None
    in_specs = [
        pl.BlockSpec((tm, D), lambda i, f: (i, 0)),
        pl.BlockSpec((1, D), lambda i, f: (0, 0)),
        pl.BlockSpec((D, tf), lambda i, f: (0, f)),
        pl.BlockSpec((tf, D), lambda i, f: (f, 0)),
    ]
    args = [h, g.reshape(1, D), w1, w2]
    if final:
        in_specs.append(pl.BlockSpec((1, D), lambda i, f: (0, 0)))
        args.append(final_g.reshape(1, D))
    return pl.pallas_call(
        functools.partial(_mlp_kernel, nf=nf, final=final),
        grid=(S // tm, nf),
        in_specs=in_specs,
        out_specs=pl.BlockSpec((tm, D), lambda i, f: (i, 0)),
        out_shape=jax.ShapeDtypeStruct((S, D), F32),
        scratch_shapes=[pltpu.VMEM((tm, D), BF16)],
        compiler_params=_cparams(("parallel", "arbitrary")),
        name="mlp",
    )(*args)


def _norm_matmul_kernel(x_ref, g_ref, w_ref, o_ref, xn_ref):
    @pl.when(pl.program_id(1) == 0)
    def _():
        xn_ref[...] = _rms(x_ref[...], g_ref[...]).astype(BF16)

    o_ref[...] = _dot(xn_ref[...], w_ref[...]).astype(o_ref.dtype)


def _norm_matmul(h, g, w, name):
    S, D = h.shape
    N = w.shape[1]
    tm = min(1024, S)
    tn = 512
    return pl.pallas_call(
        _norm_matmul_kernel,
        grid=(S // tm, N // tn),
        in_specs=[
            pl.BlockSpec((tm, D), lambda i, j: (i, 0)),
            pl.BlockSpec((1, D), lambda i, j: (0, 0)),
            pl.BlockSpec((D, tn), lambda i, j: (0, j)),
        ],
        out_specs=pl.BlockSpec((tm, tn), lambda i, j: (i, j)),
        out_shape=jax.ShapeDtypeStruct((S, N), BF16),
        scratch_shapes=[pltpu.VMEM((tm, D), BF16)],
        compiler_params=_cparams(("parallel", "arbitrary")),
        name=name,
    )(h, g.reshape(1, D), w)


def _matmul_res_kernel(a_ref, w_ref, r_ref, o_ref):
    o_ref[...] = r_ref[...] + _dot(a_ref[...], w_ref[...])


def _matmul_res(a, w, res, name):
    S, K = a.shape
    N = w.shape[1]
    tm = min(1024, S)
    return pl.pallas_call(
        _matmul_res_kernel,
        grid=(S // tm,),
        in_specs=[
            pl.BlockSpec((tm, K), lambda i: (i, 0)),
            pl.BlockSpec((K, N), lambda i: (0, 0)),
            pl.BlockSpec((tm, N), lambda i: (i, 0)),
        ],
        out_specs=pl.BlockSpec((tm, N), lambda i: (i, 0)),
        out_shape=jax.ShapeDtypeStruct((S, N), F32),
        compiler_params=_cparams(("parallel",)),
        name=name,
    )(a, w, res)


def _fox_gate_kernel(x_ref, g_ref, wf_ref, bf_ref, c_ref, carry_ref, *, tm):
    @pl.when(pl.program_id(0) == 0)
    def _():
        carry_ref[...] = jnp.zeros_like(carry_ref)

    xn = _rms(x_ref[...], g_ref[...]).astype(BF16)
    z = _dot(xn, wf_ref[...]) + bf_ref[...]
    logf = jnp.minimum(z, 0.0) - jnp.log(1.0 + jnp.exp(-jnp.abs(z)))
    row = lax.broadcasted_iota(jnp.int32, (tm, tm), 0)
    col = lax.broadcasted_iota(jnp.int32, (tm, tm), 1)
    tri = jnp.where(row >= col, 1.0, 0.0).astype(BF16)
    hi, mid, lo = _split3(logf)
    c = _dot(tri, hi) + _dot(tri, mid) + _dot(tri, lo) + carry_ref[...]
    carry_ref[...] = c[tm - 1:tm, :]
    for n, term in enumerate(_split3(c * -LOG2E)):
        c_ref[n] = term


def _fox_gate(h, g, wf, bf):
    S, D = h.shape
    tm = min(512, S)
    return pl.pallas_call(
        functools.partial(_fox_gate_kernel, tm=tm),
        grid=(S // tm,),
        in_specs=[
            pl.BlockSpec((tm, D), lambda i: (i, 0)),
            pl.BlockSpec((1, D), lambda i: (0, 0)),
            pl.BlockSpec((D, LANES), lambda i: (0, 0)),
            pl.BlockSpec((1, LANES), lambda i: (0, 0)),
        ],
        out_specs=pl.BlockSpec((3, tm, LANES), lambda i: (0, i, 0)),
        out_shape=jax.ShapeDtypeStruct((3, S, LANES), BF16),
        scratch_shapes=[pltpu.VMEM((1, LANES), F32)],
        compiler_params=_cparams(("arbitrary",)),
        name="fox_gate",
    )(h, g.reshape(1, D), wf, bf)


def _fox_attn_kernel(q_ref, k_ref, v_ref, c_ref, o_ref, s0_ref, s1_ref, m_ref, l_ref, acc_ref, *, t):
    i = pl.program_id(1)
    qh = _head_halves(q_ref[...])
    m_ref[...] = jnp.full(m_ref.shape, NEG, F32)
    l_ref[...] = jnp.zeros(l_ref.shape, F32)
    acc_ref[...] = jnp.zeros(acc_ref.shape, F32)
    s_bufs = (s0_ref, s1_ref)

    def scores(j, buf):
        k = k_ref[pl.ds(pl.multiple_of(j * t, t), t), :]
        c = c_ref[0, j]
        for h in range(2):
            s_bufs[buf][h] = _dot_nt(qh[h], k) - c[h:h + 1, :]

    def consume(j, buf, diag):
        v = v_ref[pl.ds(pl.multiple_of(j * t, t), t), :]
        for h in range(2):
            _online_update(s_bufs[buf], None, v, m_ref, l_ref, acc_ref, h, diag)

    _pipelined_tiles(i, scores, consume)
    lane = lax.broadcasted_iota(jnp.int32, (t, LANES), 1)
    o = jnp.where(lane < HEAD_DIM, _online_result(l_ref, acc_ref, 0), _online_result(l_ref, acc_ref, 1))
    o_ref[...] = o.astype(o_ref.dtype)


def _fox_attn(qkv, c4):
    S = qkv.shape[0]
    t = min(512, S)
    n_pairs = N_HEADS // 2
    return pl.pallas_call(
        functools.partial(_fox_attn_kernel, t=t),
        grid=(n_pairs, S // t),
        in_specs=[
            pl.BlockSpec((t, LANES), lambda hp, i: (i, hp)),
            pl.BlockSpec((S, LANES), lambda hp, i: (0, n_pairs + hp)),
            pl.BlockSpec((S, LANES), lambda hp, i: (0, 2 * n_pairs + hp)),
            pl.BlockSpec((1, S // t, 2, t), lambda hp, i: (hp, 0, 0, 0)),
        ],
        out_specs=pl.BlockSpec((t, LANES), lambda hp, i: (i, hp)),
        out_shape=jax.ShapeDtypeStruct((S, D_MODEL), BF16),
        scratch_shapes=[pltpu.VMEM((2, t, t), F32)] * 2 + [pltpu.VMEM((2, t, LANES), F32)] * 3,
        compiler_params=_cparams(("parallel", "arbitrary")),
        name="fox_attn",
    )(qkv, qkv, qkv, c4)


def _online_update_t(s_ref, vt, m_ref, l_ref, acc_ref, idx, causal):
    tk, t = s_ref.shape[1:]
    s = s_ref[idx]
    if causal:
        s = jnp.where(lax.broadcasted_iota(jnp.int32, (tk, t), 0) <= lax.broadcasted_iota(jnp.int32, (tk, t), 1),
                      s, NEG)
    m_old = m_ref[idx]
    m_new = jnp.maximum(m_old, jnp.max(s, axis=0, keepdims=True))
    alpha = jnp.exp2(m_old - m_new)
    p = jnp.exp2(s - m_new)
    l_ref[idx] = alpha * l_ref[idx] + jnp.sum(p, axis=0, keepdims=True)
    m_ref[idx] = m_new
    acc_ref[idx] = alpha * acc_ref[idx] + _dot(vt, p.astype(BF16))


def _fox_attn_t_kernel(qt_ref, k_ref, vt_ref, ot_ref, s0_ref, s1_ref, m_ref, l_ref, acc_ref, *, t):
    i = pl.program_id(1)
    row = lax.broadcasted_iota(jnp.int32, (LANES, t), 0)
    qt = qt_ref[0].astype(F32)
    qa = []
    for h in range(2):
        head_rows = (row < HEAD_DIM) if h == 0 else (row >= HEAD_DIM)
        bias_rows = jnp.logical_and(row >= 3 * h, row < 3 * h + 3)
        qa.append(jnp.concatenate([jnp.where(head_rows, qt, 0.0), jnp.where(bias_rows, 1.0, 0.0)],
                                  axis=0).astype(BF16))
    m_ref[...] = jnp.full(m_ref.shape, NEG, F32)
    l_ref[...] = jnp.zeros(l_ref.shape, F32)
    acc_ref[...] = jnp.zeros(acc_ref.shape, F32)
    s_bufs = (s0_ref, s1_ref)

    def scores(j, buf):
        k = k_ref[0, pl.ds(pl.multiple_of(j * t, t), t), :]
        for h in range(2):
            s_bufs[buf][h] = _dot(k, qa[h])

    def consume(j, buf, diag):
        vt = vt_ref[0, j]
        for h in range(2):
            _online_update_t(s_bufs[buf], vt, m_ref, l_ref, acc_ref, h, diag)

    _pipelined_tiles(i, scores, consume)
    ot = jnp.where(row < HEAD_DIM, acc_ref[0] / l_ref[0], acc_ref[1] / l_ref[1])
    ot_ref[0] = ot.astype(ot_ref.dtype)


def _fox_attn_t(qt, k_aug, vt):
    n_pairs, _, S = qt.shape
    t = min(512, S)
    return pl.pallas_call(
        functools.partial(_fox_attn_t_kernel, t=t),
        grid=(n_pairs, S // t),
        in_specs=[
            pl.BlockSpec((1, LANES, t), lambda hp, i: (hp, 0, i)),
            pl.BlockSpec((1, S, 2 * LANES), lambda hp, i: (hp, 0, 0)),
            pl.BlockSpec((1, S // t, LANES, t), lambda hp, i: (hp, 0, 0, 0)),
        ],
        out_specs=pl.BlockSpec((1, LANES, t), lambda hp, i: (hp, 0, i)),
        out_shape=jax.ShapeDtypeStruct((n_pairs, LANES, S), BF16),
        scratch_shapes=([pltpu.VMEM((2, t, t), F32)] * 2 + [pltpu.VMEM((2, 1, t), F32)] * 2
                        + [pltpu.VMEM((2, LANES, t), F32)]),
        compiler_params=_cparams(("parallel", "arbitrary")),
        name="fox_attn",
    )(qt, k_aug, vt)


def _fox_layer(h, g, w_qkv, w_f, b_f, w_o):
    S = h.shape[0]
    t = min(512, S)
    w_qkv = jnp.concatenate([w_qkv[:, :D_MODEL] * QK_SCALE, w_qkv[:, D_MODEL:]], axis=1).astype(BF16)
    wf = jnp.pad(w_f, ((0, 0), (0, LANES - N_HEADS))).astype(BF16)
    bf = jnp.pad(b_f, (0, LANES - N_HEADS)).reshape(1, LANES)
    c_terms = _fox_gate(h, g, wf, bf)
    qkv = _norm_matmul(h, g, w_qkv, "fox_qkv")
    n_pairs = N_HEADS // 2
    qt = qkv[:, :D_MODEL].T.reshape(n_pairs, LANES, S)
    vt = qkv[:, 2 * D_MODEL:].T.reshape(n_pairs, LANES, S // t, t).transpose(0, 2, 1, 3)
    k = qkv[:, D_MODEL:2 * D_MODEL].reshape(S, n_pairs, LANES).transpose(1, 0, 2)
    c_terms = c_terms[:, :, :N_HEADS].transpose(1, 2, 0)
    c_terms = c_terms.reshape(S, n_pairs, 6).transpose(1, 0, 2)
    k_aug = jnp.concatenate([k, jnp.pad(c_terms, ((0, 0), (0, 0), (0, LANES - 6)))], axis=2)
    ot = _fox_attn_t(qt, k_aug, vt)
    o = ot.reshape(D_MODEL, S).T
    return _matmul_res(o, w_o.astype(BF16), h, "fox_out")


def _pool_kernel(x_ref, halo_ref, g_ref, w_ref, sc_ref, o_ref, *, tm):
    i = pl.program_id(0)
    x = x_ref[...]
    g = g_ref[...]
    xn = _rms(x, g)
    hn = jnp.where(i > 0, _rms(halo_ref[...], g), 0.0)
    xe = jnp.concatenate([hn, xn], axis=0)
    tpos = i * tm + lax.broadcasted_iota(jnp.int32, (tm, 1), 0)
    for gi, w in enumerate(POOL_WINDOWS):
        sl = slice(gi * POOL_GROUP, (gi + 1) * POOL_GROUP)
        s = xe[:, sl]
        k = 1
        while k < w:
            s = s + pltpu.roll(s, k, 0)
            k *= 2
        cnt = jnp.minimum(tpos + 1, w).astype(F32)
        d = (s[POOL_HALO:, :] / cnt - xn[:, sl]).astype(BF16)
        o_ref[:, sl] = x[:, sl] + _dot(d, w_ref[gi]) * sc_ref[:, sl]


def _pool_layer(h, g, w_pool, pool_scale):
    S, D = h.shape
    tm = min(1024, S)
    return pl.pallas_call(
        functools.partial(_pool_kernel, tm=tm),
        grid=(S // tm,),
        in_specs=[
            pl.BlockSpec((tm, D), lambda i: (i, 0)),
            pl.BlockSpec((POOL_HALO, D), lambda i: (jnp.maximum(i * (tm // POOL_HALO) - 1, 0), 0)),
            pl.BlockSpec((1, D), lambda i: (0, 0)),
            pl.BlockSpec((len(POOL_WINDOWS), POOL_GROUP, POOL_GROUP), lambda i: (0, 0, 0)),
            pl.BlockSpec((1, D), lambda i: (0, 0)),
        ],
        out_specs=pl.BlockSpec((tm, D), lambda i: (i, 0)),
        out_shape=jax.ShapeDtypeStruct((S, D), F32),
        compiler_params=_cparams(("parallel",)),
        name="pool",
    )(h, h, g.reshape(1, D), w_pool.astype(BF16), pool_scale.reshape(1, D))


def _conv_in_kernel(x_ref, g_ref, wb_ref, wc_ref, wu_ref, b_ref, z_ref, xn_ref):
    @pl.when(pl.program_id(1) == 0)
    def _():
        xn_ref[...] = _rms(x_ref[...], g_ref[...]).astype(BF16)

    xn = xn_ref[...]
    b_ref[...] = _dot(xn, wb_ref[...])
    z_ref[...] = _dot(xn, wc_ref[...]) * _dot(xn, wu_ref[...])


def _conv_out_kernel(b_ref, z_ref, zh_ref, cw_ref, w_ref, r_ref, o_ref, *, tm):
    i = pl.program_id(0)
    z = z_ref[...]
    zh = jnp.where(i > 0, zh_ref[...], 0.0)
    row = lax.broadcasted_iota(jnp.int32, (tm, 1), 0)
    prev1 = zh[CONV_HALO - 1:CONV_HALO, :]
    prev2 = zh[CONV_HALO - 2:CONV_HALO - 1, :]
    z1 = jnp.where(row == 0, prev1, pltpu.roll(z, 1, 0))
    z2 = jnp.where(row == 0, prev2, jnp.where(row == 1, prev1, pltpu.roll(z, 2, 0)))
    cw = cw_ref[...]
    conv = cw[0:1, :] * z2 + cw[1:2, :] * z1 + cw[2:3, :] * z
    y = (b_ref[...] * conv).astype(BF16)
    o_ref[...] = r_ref[...] + _dot(y, w_ref[...])


def _conv_layer(h, g, w_in, conv_w, w_out):
    S, D = h.shape
    tm = min(1024, S)
    tn = 512
    nj = D // tn
    w_in = w_in.astype(BF16)
    b, z = pl.pallas_call(
        _conv_in_kernel,
        grid=(S // tm, nj),
        in_specs=[
            pl.BlockSpec((tm, D), lambda i, j: (i, 0)),
            pl.BlockSpec((1, D), lambda i, j: (0, 0)),
            pl.BlockSpec((D, tn), lambda i, j: (0, j)),
            pl.BlockSpec((D, tn), lambda i, j: (0, nj + j)),
            pl.BlockSpec((D, tn), lambda i, j: (0, 2 * nj + j)),
        ],
        out_specs=[pl.BlockSpec((tm, tn), lambda i, j: (i, j)),
                   pl.BlockSpec((tm, tn), lambda i, j: (i, j))],
        out_shape=[jax.ShapeDtypeStruct((S, D), F32), jax.ShapeDtypeStruct((S, D), F32)],
        scratch_shapes=[pltpu.VMEM((tm, D), BF16)],
        compiler_params=_cparams(("parallel", "arbitrary")),
        name="conv_in",
    )(h, g.reshape(1, D), w_in, w_in, w_in)
    cw = jnp.pad(conv_w, ((0, 8 - CONV_WIDTH), (0, 0)))
    return pl.pallas_call(
        functools.partial(_conv_out_kernel, tm=tm),
        grid=(S // tm,),
        in_specs=[
            pl.BlockSpec((tm, D), lambda i: (i, 0)),
            pl.BlockSpec((tm, D), lambda i: (i, 0)),
            pl.BlockSpec((CONV_HALO, D), lambda i: (jnp.maximum(i * (tm // CONV_HALO) - 1, 0), 0)),
            pl.BlockSpec((8, D), lambda i: (0, 0)),
            pl.BlockSpec((D, D), lambda i: (0, 0)),
            pl.BlockSpec((tm, D), lambda i: (i, 0)),
        ],
        out_specs=pl.BlockSpec((tm, D), lambda i: (i, 0)),
        out_shape=jax.ShapeDtypeStruct((S, D), F32),
        compiler_params=_cparams(("parallel",)),
        name="conv_out",
    )(b, z, z, cw, w_out.astype(BF16), h)


def _nsa_proj_kernel(x_ref, g_ref, w_ref, rc_ref, rs1_ref, rs2_ref, o_ref, xn_ref):
    j = pl.program_id(1)

    @pl.when(j == 0)
    def _():
        xn_ref[...] = _rms(x_ref[...], g_ref[...]).astype(BF16)

    a = _dot(xn_ref[...], w_ref[...])
    is_rope = functools.reduce(jnp.logical_or, [j == t for t in NSA_ROPE_TILES])

    @pl.when(is_rope)
    def _():
        rc, rs1, rs2 = rc_ref[...], rs1_ref[...], rs2_ref[...]
        o_ref[...] = jnp.concatenate(
            [_rope(a[:, :LANES], rc, rs1, rs2), _rope(a[:, LANES:], rc, rs1, rs2)], axis=1).astype(o_ref.dtype)

    @pl.when(jnp.logical_not(is_rope))
    def _():
        o_ref[...] = a.astype(o_ref.dtype)


def _nsa_proj(h, g, w_cat, rc, rs1, rs2):
    S, D = h.shape
    tm = min(1024, S)
    tn = NSA_PROJ_TILE
    return pl.pallas_call(
        _nsa_proj_kernel,
        grid=(S // tm, NSA_PROJ_WIDTH // tn),
        in_specs=[
            pl.BlockSpec((tm, D), lambda i, j: (i, 0)),
            pl.BlockSpec((1, D), lambda i, j: (0, 0)),
            pl.BlockSpec((D, tn), lambda i, j: (0, j)),
            pl.BlockSpec((tm, LANES), lambda i, j: (i, 0)),
            pl.BlockSpec((tm, LANES), lambda i, j: (i, 0)),
            pl.BlockSpec((tm, LANES), lambda i, j: (i, 0)),
        ],
        out_specs=pl.BlockSpec((tm, tn), lambda i, j: (i, j)),
        out_shape=jax.ShapeDtypeStruct((S, NSA_PROJ_WIDTH), BF16),
        scratch_shapes=[pltpu.VMEM((tm, D), BF16)],
        compiler_params=_cparams(("parallel", "arbitrary")),
        name="nsa_proj",
    )(h, g.reshape(1, D), w_cat, rc, rs1, rs2)


def _nsa_cmp_kernel(r_ref, w1_ref, pe_ref, w2_ref, rc_ref, rs1_ref, rs2_ref, o_ref, acc_ref, *, nc):
    kv = pl.program_id(0)
    hd = pl.program_id(1)
    half = CMP_STRIDE * HEAD_DIM
    r = r_ref[0, 0]
    w1 = w1_ref[0]
    first = _dot(r, w1[:half, :])
    second = _dot(r, w1[half:, :])
    pe_term = _dot(pe_ref[0], w1)[0:1, :]
    pre = first + pltpu.roll(second, nc - 1, 0) + pe_term
    ge = 0.5 * pre * (1.0 + jnp.tanh(0.7978845608028654 * (pre + 0.044715 * pre * pre * pre)))
    y = _dot(ge.astype(BF16), w2_ref[0, 0])

    @pl.when(hd == 0)
    def _():
        acc_ref[...] = y

    @pl.when(hd > 0)
    def _():
        acc_ref[...] += y

    @pl.when(hd == NSA_KV_HEADS - 1)
    def _():
        acc = acc_ref[...]

        @pl.when(kv == 0)
        def _():
            rc, rs1, rs2 = rc_ref[...], rs1_ref[...], rs2_ref[...]
            o_ref[0] = jnp.concatenate(
                [_rope(acc[:, :LANES], rc, rs1, rs2), _rope(acc[:, LANES:], rc, rs1, rs2)], axis=1).astype(o_ref.dtype)

        @pl.when(kv == 1)
        def _():
            o_ref[0] = acc.astype(o_ref.dtype)


def _nsa_cmp(r, w1, pe, w2p, rc, rs1, rs2):
    nc = r.shape[2]
    kvd = NSA_KV_HEADS * HEAD_DIM
    return pl.pallas_call(
        functools.partial(_nsa_cmp_kernel, nc=nc),
        grid=(2, NSA_KV_HEADS),
        in_specs=[
            pl.BlockSpec((1, 1, nc, CMP_STRIDE * HEAD_DIM), lambda a, b: (a, b, 0, 0)),
            pl.BlockSpec((1, CMP_BLOCK * HEAD_DIM, CMP_HIDDEN), lambda a, b: (a, 0, 0)),
            pl.BlockSpec((1, 8, CMP_BLOCK * HEAD_DIM), lambda a, b: (a, 0, 0)),
            pl.BlockSpec((1, 1, CMP_HIDDEN, kvd), lambda a, b: (a, b, 0, 0)),
            pl.BlockSpec((nc, LANES), lambda a, b: (0, 0)),
            pl.BlockSpec((nc, LANES), lambda a, b: (0, 0)),
            pl.BlockSpec((nc, LANES), lambda a, b: (0, 0)),
        ],
        out_specs=pl.BlockSpec((1, nc, kvd), lambda a, b: (a, 0, 0)),
        out_shape=jax.ShapeDtypeStruct((2, nc, kvd), BF16),
        scratch_shapes=[pltpu.VMEM((nc, kvd), F32)],
        compiler_params=_cparams(("parallel", "arbitrary")),
        name="nsa_cmp",
    )(r, w1, pe, w2p, rc, rs1, rs2)


def _nsa_cmp_attn_kernel(q_ref, kc_ref, vc_ref, ov_ref, oc_ref, sel_ref, *, t, nc, nsp):
    i = pl.program_id(0)
    qpos = i * t + lax.broadcasted_iota(jnp.int32, (t, 1), 0)
    cmp_end = CMP_STRIDE * lax.broadcasted_iota(jnp.int32, (1, nc), 1) + (CMP_BLOCK - 1)
    cmask = cmp_end <= qpos
    any_valid = qpos >= CMP_BLOCK - 1
    lane = lax.broadcasted_iota(jnp.int32, (t, LANES), 1)
    ov = ov_ref[...]
    blk = lax.broadcasted_iota(jnp.int32, (1, nsp), 1)
    cur = jnp.right_shift(qpos, SLC_SHIFT)
    forced = jnp.logical_or(blk == 0, jnp.logical_or(blk == cur, blk == cur - 1))
    valid = blk * SLC_BLOCK <= qpos
    vals = []
    for pair in range(NSA_KV_HEADS // 2):
        kc = kc_ref[0, :, pair * LANES:(pair + 1) * LANES]
        vc = vc_ref[0, :, pair * LANES:(pair + 1) * LANES]
        imp = [jnp.zeros((t, nc), F32), jnp.zeros((t, nc), F32)]
        for g in range(NSA_GROUP):
            blk_sl = slice((pair * NSA_GROUP + g) * LANES, (pair * NSA_GROUP + g + 1) * LANES)
            qh = _head_halves(q_ref[:, blk_sl])
            oc = []
            for h in range(2):
                s = jnp.where(cmask, _dot_nt(qh[h], kc), NEG)
                p = jnp.exp2(s - jnp.max(s, axis=1, keepdims=True))
                l = jnp.sum(p, axis=1, keepdims=True)
                pn = p * jnp.where(any_valid, 1.0 / l, 0.0)
                imp[h] = imp[h] + pn
                oc.append(_dot(pn.astype(BF16), vc))
            oc_ref[:, blk_sl] = jnp.where(lane < HEAD_DIM, oc[0], oc[1]).astype(oc_ref.dtype)
        for h in range(2):
            hi, mid, lo = _split3(imp[h])
            score = _dot(hi, ov) + _dot(mid, ov) + _dot(lo, ov)
            vals.append(jnp.where(valid, jnp.where(forced, FORCE_SCORE, score), NEG))

    blk_col = lax.broadcasted_iota(jnp.int32, (nsp, t), 0).astype(F32)

    def pick_one(_, vals_t):
        out = []
        for v in vals_t:
            mx = jnp.max(v, axis=0, keepdims=True)
            first = jnp.min(jnp.where(v == mx, blk_col, float(nsp)), axis=0, keepdims=True)
            out.append(jnp.where(blk_col == first, -jnp.inf, v))
        return tuple(out)

    vals_t = lax.fori_loop(0, SLC_TOPK, pick_one, tuple(v.T for v in vals))
    for kvh in range(NSA_KV_HEADS):
        picked = jnp.where(vals_t[kvh] == -jnp.inf, 1.0, 0.0).T
        sel_ref[:, kvh * nsp:(kvh + 1) * nsp] = jnp.where(valid, picked, 0.0).astype(sel_ref.dtype)


def _nsa_cmp_attn(proj, kvc, ov):
    S = proj.shape[0]
    t = NSA_TILE
    nc, nsp = ov.shape
    kvd = NSA_KV_HEADS * HEAD_DIM
    return pl.pallas_call(
        functools.partial(_nsa_cmp_attn_kernel, t=t, nc=nc, nsp=nsp),
        grid=(S // t,),
        in_specs=[
            pl.BlockSpec((t, D_MODEL), lambda i: (i, 0)),
            pl.BlockSpec((1, nc, kvd), lambda i: (0, 0, 0)),
            pl.BlockSpec((1, nc, kvd), lambda i: (1, 0, 0)),
            pl.BlockSpec((nc, nsp), lambda i: (0, 0)),
        ],
        out_specs=[pl.BlockSpec((t, D_MODEL), lambda i: (i, 0)),
                   pl.BlockSpec((t, NSA_KV_HEADS * nsp), lambda i: (i, 0))],
        out_shape=[jax.ShapeDtypeStruct((S, D_MODEL), BF16),
                   jax.ShapeDtypeStruct((S, NSA_KV_HEADS * nsp), BF16)],
        compiler_params=_cparams(("parallel",)),
        name="nsa_cmp_attn",
    )(proj, kvc, kvc, ov)


def _nsa_sel_kernel(q_ref, ks_ref, vs_ref, sel_ref, kw0_ref, kw1_ref, kw2_ref, vw0_ref, vw1_ref, vw2_ref,
                    gz_ref, e_ref, oc_ref, o_ref, s0_ref, s1_ref, p_ref, m_ref, l_ref, acc_ref, *, t, nsp):
    i = pl.program_id(1)
    n_heads = 2 * NSA_GROUP
    lane = lax.broadcasted_iota(jnp.int32, (t, LANES), 1)
    low = lane < HEAD_DIM
    qs = []
    for g in range(NSA_GROUP):
        qs.extend(_head_halves(q_ref[:, g * LANES:(g + 1) * LANES]))
    sel_bias = [((sel_ref[:, h * nsp:(h + 1) * nsp].astype(F32) - 1.0) * -NEG).astype(BF16) for h in range(2)]
    m_ref[...] = jnp.full(m_ref.shape, NEG, F32)
    l_ref[...] = jnp.zeros(l_ref.shape, F32)
    acc_ref[...] = jnp.zeros(acc_ref.shape, F32)
    blk_row = lax.broadcasted_iota(jnp.int32, (nsp, t), 0)
    tok_col = lax.broadcasted_iota(jnp.int32, (nsp, t), 1)
    s_bufs = (s0_ref, s1_ref)

    def scores(j, buf):
        off = pl.multiple_of(j * t, t)
        k = ks_ref[pl.ds(off, t), :]
        expand = jnp.where(blk_row == jnp.right_shift(off + tok_col, SLC_SHIFT), 1.0, 0.0).astype(BF16)
        for h in range(2):
            bias = _dot(sel_bias[h], expand)
            for g in range(NSA_GROUP):
                s_bufs[buf][2 * g + h] = _dot_nt(qs[2 * g + h], k) + bias

    def consume(j, buf, diag):
        v = vs_ref[pl.ds(pl.multiple_of(j * t, t), t), :]
        for idx in range(n_heads):
            _online_update(s_bufs[buf], p_ref, v, m_ref, l_ref, acc_ref, idx, diag)

    _pipelined_tiles(i, scores, consume)

    n_win = WIN // t + 1
    k_win = jnp.concatenate([kw0_ref[...], kw1_ref[...], kw2_ref[...]], axis=0)
    v_win = jnp.concatenate([vw0_ref[...], vw1_ref[...], vw2_ref[...]], axis=0)
    qpos = i * t + lax.broadcasted_iota(jnp.int32, (t, n_win * t), 0)
    kpos = (i - (n_win - 1)) * t + lax.broadcasted_iota(jnp.int32, (t, n_win * t), 1)
    wmask = jnp.logical_and(jnp.logical_and(kpos <= qpos, kpos > qpos - WIN), kpos >= 0)
    ow = []
    for idx in range(n_heads):
        s = jnp.where(wmask, _dot_nt(qs[idx], k_win), NEG)
        m = jnp.max(s, axis=1, keepdims=True)
        p = jnp.exp2(s - m)
        l = jnp.sum(p, axis=1, keepdims=True)
        ow.append(_dot(p.astype(BF16), v_win) / l)

    gates = jax.nn.sigmoid(_dot(gz_ref[...], e_ref[0]))
    gw = NSA_GROUP * LANES
    for g in range(NSA_GROUP):
        sl = slice(g * LANES, (g + 1) * LANES)
        o_cmp = oc_ref[:, sl].astype(F32)
        o_slc = jnp.where(low, _online_result(l_ref, acc_ref, 2 * g), _online_result(l_ref, acc_ref, 2 * g + 1))
        o_win = jnp.where(low, ow[2 * g], ow[2 * g + 1])
        out = (gates[:, g * LANES:(g + 1) * LANES] * o_cmp
               + gates[:, gw + g * LANES:gw + (g + 1) * LANES] * o_slc
               + gates[:, 2 * gw + g * LANES:2 * gw + (g + 1) * LANES] * o_win)
        o_ref[:, sl] = out.astype(o_ref.dtype)


def _nsa_sel(proj, sel, o_cmp, gate_expand):
    S = proj.shape[0]
    t = NSA_TILE
    nsp = sel.shape[1] // NSA_KV_HEADS
    n_heads = 2 * NSA_GROUP

    def win_spec(col0, back):
        return pl.BlockSpec((t, LANES), lambda p, i: (jnp.maximum(i - back, 0), col0 + p))

    return pl.pallas_call(
        functools.partial(_nsa_sel_kernel, t=t, nsp=nsp),
        grid=(NSA_KV_HEADS // 2, S // t),
        in_specs=[
            pl.BlockSpec((t, NSA_GROUP * LANES), lambda p, i: (i, p)),
            pl.BlockSpec((S, LANES), lambda p, i: (0, 12 + p)),
            pl.BlockSpec((S, LANES), lambda p, i: (0, 14 + p)),
            pl.BlockSpec((t, 2 * nsp), lambda p, i: (i, p)),
            win_spec(16, 2), win_spec(16, 1), win_spec(16, 0),
            win_spec(18, 2), win_spec(18, 1), win_spec(18, 0),
            pl.BlockSpec((t, NSA_PROJ_TILE), lambda p, i: (i, 10)),
            pl.BlockSpec((1, NSA_PROJ_TILE, 3 * NSA_GROUP * LANES), lambda p, i: (p, 0, 0)),
            pl.BlockSpec((t, NSA_GROUP * LANES), lambda p, i: (i, p)),
        ],
        out_specs=pl.BlockSpec((t, NSA_GROUP * LANES), lambda p, i: (i, p)),
        out_shape=jax.ShapeDtypeStruct((S, D_MODEL), BF16),
        scratch_shapes=([pltpu.VMEM((n_heads, t, t), F32)] * 2 + [pltpu.VMEM((n_heads, t, t), BF16)]
                        + [pltpu.VMEM((n_heads, t, LANES), F32)] * 3),
        compiler_params=_cparams(("parallel", "arbitrary")),
        name="nsa_sel",
    )(proj, proj, proj, sel, proj, proj, proj, proj, proj, proj, proj, gate_expand, o_cmp)


def _nsa_sel_t_kernel(qt_ref, ks_ref, vst_ref, selt_ref, kw0_ref, kw1_ref, kw2_ref, vwt0_ref, vwt1_ref, vwt2_ref,
                      gzt_ref, et_ref, oct_ref, ot_ref, s0_ref, s1_ref, m_ref, l_ref, acc_ref, *, t, nsp):
    i = pl.program_id(1)
    n_heads = 2 * NSA_GROUP
    row = lax.broadcasted_iota(jnp.int32, (LANES, t), 0)
    low = row < HEAD_DIM
    qs = []
    for g in range(NSA_GROUP):
        qb = qt_ref[0, g * LANES:(g + 1) * LANES, :].astype(F32)
        qs.append(jnp.where(low, qb, 0.0).astype(BF16))
        qs.append(jnp.where(low, 0.0, qb).astype(BF16))
    sel_bias = [((selt_ref[h * nsp:(h + 1) * nsp, :].astype(F32) - 1.0) * -NEG).astype(BF16) for h in range(2)]
    m_ref[...] = jnp.full(m_ref.shape, NEG, F32)
    l_ref[...] = jnp.zeros(l_ref.shape, F32)
    acc_ref[...] = jnp.zeros(acc_ref.shape, F32)
    tok_row = lax.broadcasted_iota(jnp.int32, (t, nsp), 0)
    blk_col = lax.broadcasted_iota(jnp.int32, (t, nsp), 1)
    s_bufs = (s0_ref, s1_ref)

    def scores(j, buf):
        off = pl.multiple_of(j * t, t)
        k = ks_ref[pl.ds(off, t), :]
        expand = jnp.where(blk_col == jnp.right_shift(off + tok_row, SLC_SHIFT), 1.0, 0.0).astype(BF16)
        for h in range(2):
            bias = _dot(expand, sel_bias[h])
            for g in range(NSA_GROUP):
                s_bufs[buf][2 * g + h] = _dot(k, qs[2 * g + h]) + bias

    def consume(j, buf, diag):
        vt = vst_ref[0, j]
        for idx in range(n_heads):
            _online_update_t(s_bufs[buf], vt, m_ref, l_ref, acc_ref, idx, diag)

    _pipelined_tiles(i, scores, consume)

    n_win = WIN // t + 1
    k_win = jnp.concatenate([kw0_ref[...], kw1_ref[...], kw2_ref[...]], axis=0)
    vt_win = jnp.concatenate([vwt0_ref[0], vwt1_ref[0], vwt2_ref[0]], axis=1)
    kpos = (i - (n_win - 1)) * t + lax.broadcasted_iota(jnp.int32, (n_win * t, t), 0)
    qpos = i * t + lax.broadcasted_iota(jnp.int32, (n_win * t, t), 1)
    wmask = jnp.logical_and(jnp.logical_and(kpos <= qpos, kpos > qpos - WIN), kpos >= 0)
    ow = []
    for idx in range(n_heads):
        s = jnp.where(wmask, _dot(k_win, qs[idx]), NEG)
        p = jnp.exp2(s - jnp.max(s, axis=0, keepdims=True))
        l = jnp.sum(p, axis=0, keepdims=True)
        ow.append(_dot(vt_win, p.astype(BF16)) / l)

    gates = jax.nn.sigmoid(_dot(et_ref[0], gzt_ref[...]))
    gw = NSA_GROUP * LANES
    for g in range(NSA_GROUP):
        rows = slice(g * LANES, (g + 1) * LANES)
        o_cmp = oct_ref[0, rows, :].astype(F32)
        o_slc = jnp.where(low, acc_ref[2 * g] / l_ref[2 * g], acc_ref[2 * g + 1] / l_ref[2 * g + 1])
        o_win = jnp.where(low, ow[2 * g], ow[2 * g + 1])
        out = (gates[g * LANES:(g + 1) * LANES, :] * o_cmp
               + gates[gw + g * LANES:gw + (g + 1) * LANES, :] * o_slc
               + gates[2 * gw + g * LANES:2 * gw + (g + 1) * LANES, :] * o_win)
        ot_ref[0, rows, :] = out.astype(ot_ref.dtype)


def _nsa_sel_t(proj, qt, vst, vwt, gzt, selt, oct, gate_expand_t):
    S = proj.shape[0]
    t = NSA_TILE
    nsp = selt.shape[0] // NSA_KV_HEADS
    n_heads = 2 * NSA_GROUP
    gl = NSA_GROUP * LANES

    def kwin_spec(back):
        return pl.BlockSpec((t, LANES), lambda p, i: (jnp.maximum(i - back, 0), 16 + p))

    def vwin_spec(back):
        return pl.BlockSpec((1, LANES, t), lambda p, i: (p, 0, jnp.maximum(i - back, 0)))

    return pl.pallas_call(
        functools.partial(_nsa_sel_t_kernel, t=t, nsp=nsp),
        grid=(NSA_KV_HEADS // 2, S // t),
        in_specs=[
            pl.BlockSpec((1, gl, t), lambda p, i: (p, 0, i)),
            pl.BlockSpec((S, LANES), lambda p, i: (0, 12 + p)),
            pl.BlockSpec((1, S // t, LANES, t), lambda p, i: (p, 0, 0, 0)),
            pl.BlockSpec((2 * nsp, t), lambda p, i: (p, i)),
            kwin_spec(2), kwin_spec(1), kwin_spec(0),
            vwin_spec(2), vwin_spec(1), vwin_spec(0),
            pl.BlockSpec((NSA_PROJ_TILE, t), lambda p, i: (0, i)),
            pl.BlockSpec((1, 3 * gl, NSA_PROJ_TILE), lambda p, i: (p, 0, 0)),
            pl.BlockSpec((1, gl, t), lambda p, i: (p, 0, i)),
        ],
        out_specs=pl.BlockSpec((1, gl, t), lambda p, i: (p, 0, i)),
        out_shape=jax.ShapeDtypeStruct((NSA_KV_HEADS // 2, gl, S), BF16),
        scratch_shapes=([pltpu.VMEM((n_heads, t, t), F32)] * 2 + [pltpu.VMEM((n_heads, 1, t), F32)] * 2
                        + [pltpu.VMEM((n_heads, LANES, t), F32)]),
        compiler_params=_cparams(("parallel", "arbitrary")),
        name="nsa_sel",
    )(qt, proj, vst, selt, proj, proj, proj, vwt, vwt, vwt, gzt, gate_expand_t, oct)


def _rope_tables(positions):
    half = ROT_DIM // 2
    inv = ROPE_THETA ** (-jnp.arange(0, ROT_DIM, 2, dtype=F32) / ROT_DIM)
    ang = positions.astype(F32)[:, None] * inv[None, :]
    cos, sin = jnp.cos(ang), jnp.sin(ang)
    S = positions.shape[0]
    ones = jnp.ones((S, HEAD_DIM - ROT_DIM), F32)
    zeros_h = jnp.zeros((S, half), F32)
    zeros_r = jnp.zeros((S, HEAD_DIM - ROT_DIM), F32)
    rc = jnp.concatenate([cos, cos, ones], axis=1)
    rs1 = jnp.concatenate([-sin, zeros_h, zeros_r], axis=1)
    rs2 = jnp.concatenate([zeros_h, sin, zeros_r], axis=1)
    reps = LANES // HEAD_DIM
    return jnp.tile(rc, (1, reps)), jnp.tile(rs1, (1, reps)), jnp.tile(rs2, (1, reps))


def _pair_heads(w, axis):
    shape = w.shape
    w = w.reshape(shape[:axis] + (NSA_KV_HEADS // 2, 2, NSA_GROUP, HEAD_DIM) + shape[axis + 1:])
    w = jnp.swapaxes(w, axis + 1, axis + 2)
    return w.reshape(shape)


def _nsa_constants(S):
    nc = S // CMP_STRIDE
    n_cmp = (S - CMP_BLOCK) // CMP_STRIDE + 1
    ns = S // SLC_BLOCK
    nsp = -(-ns // LANES) * LANES
    ci = np.arange(nc)[:, None] * CMP_STRIDE
    st = np.arange(nsp)[None, :] * SLC_BLOCK
    ov = (ci < st + SLC_BLOCK) & (ci + CMP_BLOCK > st) & (np.arange(nc)[:, None] < n_cmp) & (np.arange(nsp)[None, :] < ns)
    e = np.zeros((NSA_KV_HEADS // 2, NSA_PROJ_TILE, 3 * NSA_GROUP * LANES), np.float32)
    for p in range(NSA_KV_HEADS // 2):
        for br in range(3):
            for g in range(NSA_GROUP):
                for hf in range(2):
                    r = br * N_HEADS + (2 * p + hf) * NSA_GROUP + g
                    c0 = br * NSA_GROUP * LANES + g * LANES + hf * HEAD_DIM
                    e[p, r, c0:c0 + HEAD_DIM] = 1.0
    cmp_end = np.minimum(np.arange(nc) * CMP_STRIDE + CMP_BLOCK - 1, S - 1)
    return nc, jnp.asarray(ov.astype(np.float32), BF16), jnp.asarray(e, BF16), cmp_end


def _nsa_layer(h, g, positions, w_in, pe_k, w1_k, w2_k, pe_v, w1_v, w2_v, w_o):
    S, D = h.shape
    qd = N_HEADS * HEAD_DIM
    kvd = NSA_KV_HEADS * HEAD_DIM
    nc, ov, gate_expand, cmp_end = _nsa_constants(S)
    rc, rs1, rs2 = _rope_tables(positions)

    wq = _pair_heads(w_in[:, :qd], 1) * QK_SCALE
    wg = jnp.pad(w_in[:, qd + 6 * kvd:], ((0, 0), (0, NSA_PROJ_TILE - 3 * N_HEADS)))
    w_cat = jnp.concatenate([wq, w_in[:, qd:qd + 6 * kvd], wg], axis=1).astype(BF16)
    proj = _nsa_proj(h, g, w_cat, rc, rs1, rs2)

    raw = proj[:, qd:qd + 2 * kvd].reshape(nc, CMP_STRIDE, 2, NSA_KV_HEADS, HEAD_DIM)
    raw = raw.transpose(2, 3, 0, 1, 4).reshape(2, NSA_KV_HEADS, nc, CMP_STRIDE * HEAD_DIM)
    w1 = jnp.stack([w1_k, w1_v]).astype(BF16)
    pe = jnp.stack([pe_k.reshape(1, -1), pe_v.reshape(1, -1)])
    pe = jnp.pad(pe, ((0, 0), (0, 7), (0, 0))).astype(BF16)
    w2 = jnp.stack([w2_k, w2_v])
    eye = jnp.eye(NSA_KV_HEADS, dtype=F32)
    w2p = (w2[:, None, :, None, :] * eye[None, :, None, :, None]).reshape(2, NSA_KV_HEADS, CMP_HIDDEN, kvd).astype(BF16)
    kvc = _nsa_cmp(raw, w1, pe, w2p, rc[cmp_end], rs1[cmp_end], rs2[cmp_end])

    o_cmp, sel = _nsa_cmp_attn(proj, kvc, ov)
    t = NSA_TILE
    n_pair = NSA_KV_HEADS // 2
    qt = proj[:, :qd].T.reshape(n_pair, NSA_GROUP * LANES, S)
    vst = proj[:, 14 * LANES:16 * LANES].T.reshape(n_pair, LANES, S // t, t).transpose(0, 2, 1, 3)
    vwt = proj[:, 18 * LANES:20 * LANES].T.reshape(n_pair, LANES, S)
    gzt = proj[:, 20 * LANES:].T
    oct = o_cmp.T.reshape(n_pair, NSA_GROUP * LANES, S)
    ot = _nsa_sel_t(proj, qt, vst, vwt, gzt, sel.T, oct, gate_expand.transpose(0, 2, 1))
    o = ot.reshape(qd, S).T
    return _matmul_res(o, _pair_heads(w_o, 0).astype(BF16), h, "nsa_out")


def _trunk(x2, positions, p):
    h = _fox_layer(x2, p["l0_norm_mix"], p["l0_fox_w_qkv"], p["l0_fox_w_f"], p["l0_fox_b_f"], p["l0_fox_w_o"])
    h = _mlp(h, p["l0_norm_mlp"], p["l0_mlp_w1"].astype(BF16), p["l0_mlp_w2"].astype(BF16))
    h = _pool_layer(h, p["l1_norm_mix"], p["l1_pool_w"], p["l1_pool_scale"])
    h = _mlp(h, p["l1_norm_mlp"], p["l1_mlp_w1"].astype(BF16), p["l1_mlp_w2"].astype(BF16))
    h = _conv_layer(h, p["l2_norm_mix"], p["l2_conv_w_in"], p["l2_conv_w"], p["l2_conv_w_out"])
    h = _mlp(h, p["l2_norm_mlp"], p["l2_mlp_w1"].astype(BF16), p["l2_mlp_w2"].astype(BF16))
    h = _nsa_layer(h, p["l3_norm_mix"], positions, p["l3_nsa_w_in"], p["l3_nsa_cmp_pe_k"], p["l3_nsa_cmp_w1_k"],
                   p["l3_nsa_cmp_w2_k"], p["l3_nsa_cmp_pe_v"], p["l3_nsa_cmp_w1_v"], p["l3_nsa_cmp_w2_v"],
                   p["l3_nsa_w_o"])
    return _mlp(h, p["l3_norm_mlp"], p["l3_mlp_w1"].astype(BF16), p["l3_mlp_w2"].astype(BF16), p["final_norm"])


def kernel(x, positions, l0_norm_mix, l0_fox_w_qkv, l0_fox_w_f, l0_fox_b_f, l0_fox_w_o, l0_norm_mlp, l0_mlp_w1, l0_mlp_w2, l1_norm_mix, l1_pool_w, l1_pool_scale, l1_norm_mlp, l1_mlp_w1, l1_mlp_w2, l2_norm_mix, l2_conv_w_in, l2_conv_w, l2_conv_w_out, l2_norm_mlp, l2_mlp_w1, l2_mlp_w2, l3_norm_mix, l3_nsa_w_in, l3_nsa_cmp_pe_k, l3_nsa_cmp_w1_k, l3_nsa_cmp_w2_k, l3_nsa_cmp_pe_v, l3_nsa_cmp_w1_v, l3_nsa_cmp_w2_v, l3_nsa_w_o, l3_norm_mlp, l3_mlp_w1, l3_mlp_w2, final_norm):
    params = dict(locals())
    B, S, D = x.shape
    outs = [_trunk(x[b], positions, params) for b in range(B)]
    return jnp.stack(outs, axis=0)
```

```python
import functools

import numpy as np
import jax
import jax.numpy as jnp
from jax import lax
from jax.experimental import pallas as pl
from jax.experimental.pallas import tpu as pltpu

F32 = jnp.float32
BF16 = jnp.bfloat16

D_MODEL = 1024
HEAD_DIM = 64
N_HEADS = D_MODEL // HEAD_DIM
D_FF = 4 * D_MODEL
ROPE_THETA = 500000.0
ROT_DIM = HEAD_DIM // 4
RMS_EPS = 1e-6
POOL_WINDOWS = (2, 4, 8, 16)
POOL_GROUP = D_MODEL // len(POOL_WINDOWS)
POOL_HALO = 16
CONV_WIDTH = 3
CONV_HALO = 8
NSA_KV_HEADS = 4
NSA_GROUP = N_HEADS // NSA_KV_HEADS
CMP_BLOCK = 32
CMP_STRIDE = 16
CMP_HIDDEN = 256
SLC_BLOCK = 64
SLC_SHIFT = 6
SLC_TOPK = 16
WIN = 512
FORCE_SCORE = 1e9
NEG = -1e30
LOG2E = 1.4426950408889634
QK_SCALE = HEAD_DIM ** -0.5 * LOG2E

LANES = 128
V7X_VMEM_LIMIT = 56 * 1024 * 1024
NSA_TILE = WIN // 2
SEL_GROUP = 16
NSA_PROJ_TILE = 2 * LANES
NSA_PROJ_WIDTH = 22 * LANES
NSA_ROPE_TILES = (0, 1, 2, 3, 6, 8)


def _cparams(semantics):
    return pltpu.CompilerParams(dimension_semantics=semantics, vmem_limit_bytes=V7X_VMEM_LIMIT)


def _rms(x, g):
    return x * lax.rsqrt(jnp.mean(x * x, axis=-1, keepdims=True) + RMS_EPS) * g


def _dot(a, b):
    return jnp.dot(a, b, preferred_element_type=F32)


def _dot_nt(a, b):
    return lax.dot_general(a, b, (((1,), (1,)), ((), ())), preferred_element_type=F32)


def _split3(x):
    hi = x.astype(BF16)
    r = x - hi.astype(F32)
    mid = r.astype(BF16)
    lo = (r - mid.astype(F32)).astype(BF16)
    return hi, mid, lo


def _rope(a, rc, rs1, rs2):
    return a * rc + pltpu.roll(a, LANES - ROT_DIM // 2, 1) * rs1 + pltpu.roll(a, ROT_DIM // 2, 1) * rs2


def _head_halves(q_blk):
    lane = lax.broadcasted_iota(jnp.int32, q_blk.shape, 1)
    qf = q_blk.astype(F32)
    return (jnp.where(lane < HEAD_DIM, qf, 0.0).astype(BF16),
            jnp.where(lane >= HEAD_DIM, qf, 0.0).astype(BF16))


def _online_update(s_ref, p_ref, v, m_ref, l_ref, acc_ref, idx, causal):
    t, tk = s_ref.shape[1:]
    reps = tk // LANES
    s = s_ref[idx]
    if causal:
        s = jnp.where(lax.broadcasted_iota(jnp.int32, (t, tk), 0) >= lax.broadcasted_iota(jnp.int32, (t, tk), 1),
                      s, NEG)
    m_old = m_ref[idx]
    m_new = jnp.maximum(m_old, jnp.max(s, axis=1, keepdims=True))
    alpha = jnp.exp2(m_old - m_new)
    p = jnp.exp2(s - jnp.tile(m_new, (1, reps)))
    psum = p[:, :LANES]
    for r in range(1, reps):
        psum = psum + p[:, r * LANES:(r + 1) * LANES]
    l_ref[idx] = alpha * l_ref[idx] + psum
    m_ref[idx] = m_new
    if p_ref is None:
        acc_ref[idx] = alpha * acc_ref[idx] + _dot(p.astype(BF16), v)
    else:
        p_ref[idx] = p.astype(BF16)
        acc_ref[idx] = alpha * acc_ref[idx] + _dot(p_ref[idx], v)


def _online_result(l_ref, acc_ref, idx):
    return acc_ref[idx] / jnp.sum(l_ref[idx], axis=1, keepdims=True)


def _pipelined_tiles(i, scores, consume, pairs_per_step=1):
    scores(0, 0)

    def pair(j):
        scores(j + 1, 1)
        consume(j, 0, False)
        scores(j + 2, 0)
        consume(j + 1, 1, False)

    def body(jj, carry):
        for u in range(pairs_per_step):
            pair(2 * (pairs_per_step * jj + u))
        return carry

    n_pairs = i // 2
    n_steps = n_pairs // pairs_per_step
    lax.fori_loop(0, n_steps, body, 0)
    if pairs_per_step > 1:
        def tail(jj, carry):
            pair(2 * jj)
            return carry

        lax.fori_loop(n_steps * pairs_per_step, n_pairs, tail, 0)

    @pl.when(i % 2 == 0)
    def _():
        consume(i, 0, True)

    @pl.when(i % 2 == 1)
    def _():
        scores(i, 1)
        consume(i - 1, 0, False)
        consume(i, 1, True)


def _mlp_kernel(x_ref, g_ref, w1_ref, w2_ref, *rest, nf, final):
    if final:
        fg_ref, o_ref, xn_ref = rest
    else:
        o_ref, xn_ref = rest
    f = pl.program_id(1)

    @pl.when(f == 0)
    def _():
        x = x_ref[...]
        xn_ref[...] = _rms(x, g_ref[...]).astype(BF16)
        o_ref[...] = x

    a = _dot(xn_ref[...], w1_ref[...])
    a = jnp.square(jnp.maximum(a, 0.0)).astype(BF16)
    o_ref[...] += _dot(a, w2_ref[...])

    if final:
        @pl.when(f == nf - 1)
        def _():
            o_ref[...] = _rms(o_ref[...], fg_ref[...])


def _mlp(h, g, w1, w2, final_g=None):
    S, D = h.shape
    F = w1.shape[1]
    tm = min(1024, S)
    tf = 1024
    nf = F // tf
    final = final_g is not None
    in_specs = [
        pl.BlockSpec((tm, D), lambda i, f: (i, 0)),
        pl.BlockSpec((1, D), lambda i, f: (0, 0)),
        pl.BlockSpec((D, tf), lambda i, f: (0, f)),
        pl.BlockSpec((tf, D), lambda i, f: (f, 0)),
    ]
    args = [h, g.reshape(1, D), w1, w2]
    if final:
        in_specs.append(pl.BlockSpec((1, D), lambda i, f: (0, 0)))
        args.append(final_g.reshape(1, D))
    return pl.pallas_call(
        functools.partial(_mlp_kernel, nf=nf, final=final),
        grid=(S // tm, nf),
        in_specs=in_specs,
        out_specs=pl.BlockSpec((tm, D), lambda i, f: (i, 0)),
        out_shape=jax.ShapeDtypeStruct((S, D), F32),
        scratch_shapes=[pltpu.VMEM((tm, D), BF16)],
        compiler_params=_cparams(("parallel", "arbitrary")),
        name="mlp",
    )(*args)


def _norm_matmul_kernel(x_ref, g_ref, w_ref, o_ref, xn_ref):
    @pl.when(pl.program_id(1) == 0)
    def _():
        xn_ref[...] = _rms(x_ref[...], g_ref[...]).astype(BF16)

    o_ref[...] = _dot(xn_ref[...], w_ref[...]).astype(o_ref.dtype)


def _norm_matmul(h, g, w, name):
    S, D = h.shape
    N = w.shape[1]
    tm = min(1024, S)
    tn = 512
    return pl.pallas_call(
        _norm_matmul_kernel,
        grid=(S // tm, N // tn),
        in_specs=[
            pl.BlockSpec((tm, D), lambda i, j: (i, 0)),
            pl.BlockSpec((1, D), lambda i, j: (0, 0)),
            pl.BlockSpec((D, tn), lambda i, j: (0, j)),
        ],
        out_specs=pl.BlockSpec((tm, tn), lambda i, j: (i, j)),
        out_shape=jax.ShapeDtypeStruct((S, N), BF16),
        scratch_shapes=[pltpu.VMEM((tm, D), BF16)],
        compiler_params=_cparams(("parallel", "arbitrary")),
        name=name,
    )(h, g.reshape(1, D), w)


def _matmul_res_kernel(a_ref, w_ref, r_ref, o_ref):
    o_ref[...] = r_ref[...] + _dot(a_ref[...], w_ref[...])


def _matmul_res(a, w, res, name):
    S, K = a.shape
    N = w.shape[1]
    tm = min(1024, S)
    return pl.pallas_call(
        _matmul_res_kernel,
        grid=(S // tm,),
        in_specs=[
            pl.BlockSpec((tm, K), lambda i: (i, 0)),
            pl.BlockSpec((K, N), lambda i: (0, 0)),
            pl.BlockSpec((tm, N), lambda i: (i, 0)),
        ],
        out_specs=pl.BlockSpec((tm, N), lambda i: (i, 0)),
        out_shape=jax.ShapeDtypeStruct((S, N), F32),
        compiler_params=_cparams(("parallel",)),
        name=name,
    )(a, w, res)


def _fox_gate_kernel(x_ref, g_ref, wf_ref, bf_ref, c_ref, carry_ref, *, tm):
    @pl.when(pl.program_id(0) == 0)
    def _():
        carry_ref[...] = jnp.zeros_like(carry_ref)

    xn = _rms(x_ref[...], g_ref[...]).astype(BF16)
    z = _dot(xn, wf_ref[...]) + bf_ref[...]
    logf = jnp.minimum(z, 0.0) - jnp.log(1.0 + jnp.exp(-jnp.abs(z)))
    row = lax.broadcasted_iota(jnp.int32, (tm, tm), 0)
    col = lax.broadcasted_iota(jnp.int32, (tm, tm), 1)
    tri = jnp.where(row >= col, 1.0, 0.0).astype(BF16)
    hi, mid, lo = _split3(logf)
    c = _dot(tri, hi) + _dot(tri, mid) + _dot(tri, lo) + carry_ref[...]
    carry_ref[...] = c[tm - 1:tm, :]
    for n, term in enumerate(_split3(c * -LOG2E)):
        c_ref[n] = term


def _fox_gate(h, g, wf, bf):
    S, D = h.shape
    tm = min(512, S)
    return pl.pallas_call(
        functools.partial(_fox_gate_kernel, tm=tm),
        grid=(S // tm,),
        in_specs=[
            pl.BlockSpec((tm, D), lambda i: (i, 0)),
            pl.BlockSpec((1, D), lambda i: (0, 0)),
            pl.BlockSpec((D, LANES), lambda i: (0, 0)),
            pl.BlockSpec((1, LANES), lambda i: (0, 0)),
        ],
        out_specs=pl.BlockSpec((3, tm, LANES), lambda i: (0, i, 0)),
        out_shape=jax.ShapeDtypeStruct((3, S, LANES), BF16),
        scratch_shapes=[pltpu.VMEM((1, LANES), F32)],
        compiler_params=_cparams(("arbitrary",)),
        name="fox_gate",
    )(h, g.reshape(1, D), wf, bf)


def _fox_attn_kernel(q_ref, k_ref, v_ref, c_ref, o_ref, s0_ref, s1_ref, m_ref, l_ref, acc_ref, *, t):
    i = pl.program_id(1)
    qh = _head_halves(q_ref[...])
    m_ref[...] = jnp.full(m_ref.shape, NEG, F32)
    l_ref[...] = jnp.zeros(l_ref.shape, F32)
    acc_ref[...] = jnp.zeros(acc_ref.shape, F32)
    s_bufs = (s0_ref, s1_ref)

    def scores(j, buf):
        k = k_ref[pl.ds(pl.multiple_of(j * t, t), t), :]
        c = c_ref[0, j]
        for h in range(2):
            s_bufs[buf][h] = _dot_nt(qh[h], k) - c[h:h + 1, :]

    def consume(j, buf, diag):
        v = v_ref[pl.ds(pl.multiple_of(j * t, t), t), :]
        for h in range(2):
            _online_update(s_bufs[buf], None, v, m_ref, l_ref, acc_ref, h, diag)

    _pipelined_tiles(i, scores, consume)
    lane = lax.broadcasted_iota(jnp.int32, (t, LANES), 1)
    o = jnp.where(lane < HEAD_DIM, _online_result(l_ref, acc_ref, 0), _online_result(l_ref, acc_ref, 1))
    o_ref[...] = o.astype(o_ref.dtype)


def _fox_attn(qkv, c4):
    S = qkv.shape[0]
    t = min(512, S)
    n_pairs = N_HEADS // 2
    return pl.pallas_call(
        functools.partial(_fox_attn_kernel, t=t),
        grid=(n_pairs, S // t),
        in_specs=[
            pl.BlockSpec((t, LANES), lambda hp, i: (i, hp)),
            pl.BlockSpec((S, LANES), lambda hp, i: (0, n_pairs + hp)),
            pl.BlockSpec((S, LANES), lambda hp, i: (0, 2 * n_pairs + hp)),
            pl.BlockSpec((1, S // t, 2, t), lambda hp, i: (hp, 0, 0, 0)),
        ],
        out_specs=pl.BlockSpec((t, LANES), lambda hp, i: (i, hp)),
        out_shape=jax.ShapeDtypeStruct((S, D_MODEL), BF16),
        scratch_shapes=[pltpu.VMEM((2, t, t), F32)] * 2 + [pltpu.VMEM((2, t, LANES), F32)] * 3,
        compiler_params=_cparams(("parallel", "arbitrary")),
        name="fox_attn",
    )(qkv, qkv, qkv, c4)


def _online_update_t(s_ref, vt, m_ref, l_ref, acc_ref, idx, causal):
    tk, t = s_ref.shape[1:]
    s = s_ref[idx]
    if causal:
        s = jnp.where(lax.broadcasted_iota(jnp.int32, (tk, t), 0) <= lax.broadcasted_iota(jnp.int32, (tk, t), 1),
                      s, NEG)
    m_old = m_ref[idx]
    m_new = jnp.maximum(m_old, jnp.max(s, axis=0, keepdims=True))
    alpha = jnp.exp2(m_old - m_new)
    p = jnp.exp2(s - m_new)
    l_ref[idx] = alpha * l_ref[idx] + jnp.sum(p, axis=0, keepdims=True)
    m_ref[idx] = m_new
    acc_ref[idx] = alpha * acc_ref[idx] + _dot(vt, p.astype(BF16))


def _fox_attn_t_kernel(qt_ref, k_ref, vt_ref, ot_ref, s0_ref, s1_ref, m_ref, l_ref, acc_ref, *, t):
    i = pl.program_id(1)
    row = lax.broadcasted_iota(jnp.int32, (LANES, t), 0)
    qt = qt_ref[0].astype(F32)
    qa = []
    for h in range(2):
        head_rows = (row < HEAD_DIM) if h == 0 else (row >= HEAD_DIM)
        bias_rows = jnp.logical_and(row >= 3 * h, row < 3 * h + 3)
        qa.append(jnp.concatenate([jnp.where(head_rows, qt, 0.0), jnp.where(bias_rows, 1.0, 0.0)],
                                  axis=0).astype(BF16))
    m_ref[...] = jnp.full(m_ref.shape, NEG, F32)
    l_ref[...] = jnp.zeros(l_ref.shape, F32)
    acc_ref[...] = jnp.zeros(acc_ref.shape, F32)
    s_bufs = (s0_ref, s1_ref)

    def scores(j, buf):
        k = k_ref[0, pl.ds(pl.multiple_of(j * t, t), t), :]
        for h in range(2):
            s_bufs[buf][h] = _dot(k, qa[h])

    def consume(j, buf, diag):
        vt = vt_ref[0, j]
        for h in range(2):
            _online_update_t(s_bufs[buf], vt, m_ref, l_ref, acc_ref, h, diag)

    _pipelined_tiles(i, scores, consume, pairs_per_step=2)
    ot = jnp.where(row < HEAD_DIM, acc_ref[0] / l_ref[0], acc_ref[1] / l_ref[1])
    ot_ref[0] = ot.astype(ot_ref.dtype)


def _fox_attn_t(qt, k_aug, vt):
    n_pairs, _, S = qt.shape
    t = min(512, S)
    return pl.pallas_call(
        functools.partial(_fox_attn_t_kernel, t=t),
        grid=(n_pairs, S // t),
        in_specs=[
            pl.BlockSpec((1, LANES, t), lambda hp, i: (hp, 0, i)),
            pl.BlockSpec((1, S, 2 * LANES), lambda hp, i: (hp, 0, 0)),
            pl.BlockSpec((1, S // t, LANES, t), lambda hp, i: (hp, 0, 0, 0)),
        ],
        out_specs=pl.BlockSpec((1, LANES, t), lambda hp, i: (hp, 0, i)),
        out_shape=jax.ShapeDtypeStruct((n_pairs, LANES, S), BF16),
        scratch_shapes=([pltpu.VMEM((2, t, t), F32)] * 2 + [pltpu.VMEM((2, 1, t), F32)] * 2
                        + [pltpu.VMEM((2, LANES, t), F32)]),
        compiler_params=_cparams(("parallel", "arbitrary")),
        name="fox_attn",
    )(qt, k_aug, vt)


def _fox_layer(h, g, w_qkv, w_f, b_f, w_o):
    S = h.shape[0]
    t = min(512, S)
    w_qkv = jnp.concatenate([w_qkv[:, :D_MODEL] * QK_SCALE, w_qkv[:, D_MODEL:]], axis=1).astype(BF16)
    wf = jnp.pad(w_f, ((0, 0), (0, LANES - N_HEADS))).astype(BF16)
    bf = jnp.pad(b_f, (0, LANES - N_HEADS)).reshape(1, LANES)
    c_terms = _fox_gate(h, g, wf, bf)
    qkv = _norm_matmul(h, g, w_qkv, "fox_qkv")
    n_pairs = N_HEADS // 2
    qt = qkv[:, :D_MODEL].T.reshape(n_pairs, LANES, S)
    vt = qkv[:, 2 * D_MODEL:].T.reshape(n_pairs, LANES, S // t, t).transpose(0, 2, 1, 3)
    k = qkv[:, D_MODEL:2 * D_MODEL].reshape(S, n_pairs, LANES).transpose(1, 0, 2)
    c_terms = c_terms[:, :, :N_HEADS].transpose(1, 2, 0)
    c_terms = c_terms.reshape(S, n_pairs, 6).transpose(1, 0, 2)
    k_aug = jnp.concatenate([k, jnp.pad(c_terms, ((0, 0), (0, 0), (0, LANES - 6)))], axis=2)
    ot = _fox_attn_t(qt, k_aug, vt)
    o = ot.reshape(D_MODEL, S).T
    return _matmul_res(o, w_o.astype(BF16), h, "fox_out")


def _pool_kernel(x_ref, halo_ref, g_ref, w_ref, sc_ref, o_ref, *, tm):
    i = pl.program_id(0)
    x = x_ref[...]
    g = g_ref[...]
    xn = _rms(x, g)
    hn = jnp.where(i > 0, _rms(halo_ref[...], g), 0.0)
    xe = jnp.concatenate([hn, xn], axis=0)
    tpos = i * tm + lax.broadcasted_iota(jnp.int32, (tm, 1), 0)
    for gi, w in enumerate(POOL_WINDOWS):
        sl = slice(gi * POOL_GROUP, (gi + 1) * POOL_GROUP)
        s = xe[:, sl]
        k = 1
        while k < w:
            s = s + pltpu.roll(s, k, 0)
            k *= 2
        cnt = jnp.minimum(tpos + 1, w).astype(F32)
        d = (s[POOL_HALO:, :] / cnt - xn[:, sl]).astype(BF16)
        o_ref[:, sl] = x[:, sl] + _dot(d, w_ref[gi]) * sc_ref[:, sl]


def _pool_layer(h, g, w_pool, pool_scale):
    S, D = h.shape
    tm = min(1024, S)
    return pl.pallas_call(
        functools.partial(_pool_kernel, tm=tm),
        grid=(S // tm,),
        in_specs=[
            pl.BlockSpec((tm, D), lambda i: (i, 0)),
            pl.BlockSpec((POOL_HALO, D), lambda i: (jnp.maximum(i * (tm // POOL_HALO) - 1, 0), 0)),
            pl.BlockSpec((1, D), lambda i: (0, 0)),
            pl.BlockSpec((len(POOL_WINDOWS), POOL_GROUP, POOL_GROUP), lambda i: (0, 0, 0)),
            pl.BlockSpec((1, D), lambda i: (0, 0)),
        ],
        out_specs=pl.BlockSpec((tm, D), lambda i: (i, 0)),
        out_shape=jax.ShapeDtypeStruct((S, D), F32),
        compiler_params=_cparams(("parallel",)),
        name="pool",
    )(h, h, g.reshape(1, D), w_pool.astype(BF16), pool_scale.reshape(1, D))


def _conv_in_kernel(x_ref, g_ref, wb_ref, wc_ref, wu_ref, b_ref, z_ref, xn_ref):
    @pl.when(pl.program_id(1) == 0)
    def _():
        xn_ref[...] = _rms(x_ref[...], g_ref[...]).astype(BF16)

    xn = xn_ref[...]
    b_ref[...] = _dot(xn, wb_ref[...])
    z_ref[...] = _dot(xn, wc_ref[...]) * _dot(xn, wu_ref[...])


def _conv_out_kernel(b_ref, z_ref, zh_ref, cw_ref, w_ref, r_ref, o_ref, *, tm):
    i = pl.program_id(0)
    z = z_ref[...]
    zh = jnp.where(i > 0, zh_ref[...], 0.0)
    row = lax.broadcasted_iota(jnp.int32, (tm, 1), 0)
    prev1 = zh[CONV_HALO - 1:CONV_HALO, :]
    prev2 = zh[CONV_HALO - 2:CONV_HALO - 1, :]
    z1 = jnp.where(row == 0, prev1, pltpu.roll(z, 1, 0))
    z2 = jnp.where(row == 0, prev2, jnp.where(row == 1, prev1, pltpu.roll(z, 2, 0)))
    cw = cw_ref[...]
    conv = cw[0:1, :] * z2 + cw[1:2, :] * z1 + cw[2:3, :] * z
    y = (b_ref[...] * conv).astype(BF16)
    o_ref[...] = r_ref[...] + _dot(y, w_ref[...])


def _conv_layer(h, g, w_in, conv_w, w_out):
    S, D = h.shape
    tm = min(1024, S)
    tn = 512
    nj = D // tn
    w_in = w_in.astype(BF16)
    b, z = pl.pallas_call(
        _conv_in_kernel,
        grid=(S // tm, nj),
        in_specs=[
            pl.BlockSpec((tm, D), lambda i, j: (i, 0)),
            pl.BlockSpec((1, D), lambda i, j: (0, 0)),
            pl.BlockSpec((D, tn), lambda i, j: (0, j)),
            pl.BlockSpec((D, tn), lambda i, j: (0, nj + j)),
            pl.BlockSpec((D, tn), lambda i, j: (0, 2 * nj + j)),
        ],
        out_specs=[pl.BlockSpec((tm, tn), lambda i, j: (i, j)),
                   pl.BlockSpec((tm, tn), lambda i, j: (i, j))],
        out_shape=[jax.ShapeDtypeStruct((S, D), F32), jax.ShapeDtypeStruct((S, D), F32)],
        scratch_shapes=[pltpu.VMEM((tm, D), BF16)],
        compiler_params=_cparams(("parallel", "arbitrary")),
        name="conv_in",
    )(h, g.reshape(1, D), w_in, w_in, w_in)
    cw = jnp.pad(conv_w, ((0, 8 - CONV_WIDTH), (0, 0)))
    return pl.pallas_call(
        functools.partial(_conv_out_kernel, tm=tm),
        grid=(S // tm,),
        in_specs=[
            pl.BlockSpec((tm, D), lambda i: (i, 0)),
            pl.BlockSpec((tm, D), lambda i: (i, 0)),
            pl.BlockSpec((CONV_HALO, D), lambda i: (jnp.maximum(i * (tm // CONV_HALO) - 1, 0), 0)),
            pl.BlockSpec((8, D), lambda i: (0, 0)),
            pl.BlockSpec((D, D), lambda i: (0, 0)),
            pl.BlockSpec((tm, D), lambda i: (i, 0)),
        ],
        out_specs=pl.BlockSpec((tm, D), lambda i: (i, 0)),
        out_shape=jax.ShapeDtypeStruct((S, D), F32),
        compiler_params=_cparams(("parallel",)),
        name="conv_out",
    )(b, z, z, cw, w_out.astype(BF16), h)


def _nsa_proj_kernel(x_ref, g_ref, w_ref, rc_ref, rs1_ref, rs2_ref, o_ref, xn_ref):
    j = pl.program_id(1)

    @pl.when(j == 0)
    def _():
        xn_ref[...] = _rms(x_ref[...], g_ref[...]).astype(BF16)

    a = _dot(xn_ref[...], w_ref[...])
    is_rope = functools.reduce(jnp.logical_or, [j == t for t in NSA_ROPE_TILES])

    @pl.when(is_rope)
    def _():
        rc, rs1, rs2 = rc_ref[...], rs1_ref[...], rs2_ref[...]
        o_ref[...] = jnp.concatenate(
            [_rope(a[:, :LANES], rc, rs1, rs2), _rope(a[:, LANES:], rc, rs1, rs2)], axis=1).astype(o_ref.dtype)

    @pl.when(jnp.logical_not(is_rope))
    def _():
        o_ref[...] = a.astype(o_ref.dtype)


def _nsa_proj(h, g, w_cat, rc, rs1, rs2):
    S, D = h.shape
    tm = min(1024, S)
    tn = NSA_PROJ_TILE
    return pl.pallas_call(
        _nsa_proj_kernel,
        grid=(S // tm, NSA_PROJ_WIDTH // tn),
        in_specs=[
            pl.BlockSpec((tm, D), lambda i, j: (i, 0)),
            pl.BlockSpec((1, D), lambda i, j: (0, 0)),
            pl.BlockSpec((D, tn), lambda i, j: (0, j)),
            pl.BlockSpec((tm, LANES), lambda i, j: (i, 0)),
            pl.BlockSpec((tm, LANES), lambda i, j: (i, 0)),
            pl.BlockSpec((tm, LANES), lambda i, j: (i, 0)),
        ],
        out_specs=pl.BlockSpec((tm, tn), lambda i, j: (i, j)),
        out_shape=jax.ShapeDtypeStruct((S, NSA_PROJ_WIDTH), BF16),
        scratch_shapes=[pltpu.VMEM((tm, D), BF16)],
        compiler_params=_cparams(("parallel", "arbitrary")),
        name="nsa_proj",
    )(h, g.reshape(1, D), w_cat, rc, rs1, rs2)


def _nsa_cmp_kernel(r_ref, w1_ref, pe_ref, w2_ref, rc_ref, rs1_ref, rs2_ref, o_ref, acc_ref, *, nc):
    kv = pl.program_id(0)
    hd = pl.program_id(1)
    half = CMP_STRIDE * HEAD_DIM
    r = r_ref[0, 0]
    w1 = w1_ref[0]
    first = _dot(r, w1[:half, :])
    second = _dot(r, w1[half:, :])
    pe_term = _dot(pe_ref[0], w1)[0:1, :]
    pre = first + pltpu.roll(second, nc - 1, 0) + pe_term
    ge = 0.5 * pre * (1.0 + jnp.tanh(0.7978845608028654 * (pre + 0.044715 * pre * pre * pre)))
    y = _dot(ge.astype(BF16), w2_ref[0, 0])

    @pl.when(hd == 0)
    def _():
        acc_ref[...] = y

    @pl.when(hd > 0)
    def _():
        acc_ref[...] += y

    @pl.when(hd == NSA_KV_HEADS - 1)
    def _():
        acc = acc_ref[...]

        @pl.when(kv == 0)
        def _():
            rc, rs1, rs2 = rc_ref[...], rs1_ref[...], rs2_ref[...]
            o_ref[0] = jnp.concatenate(
                [_rope(acc[:, :LANES], rc, rs1, rs2), _rope(acc[:, LANES:], rc, rs1, rs2)], axis=1).astype(o_ref.dtype)

        @pl.when(kv == 1)
        def _():
            o_ref[0] = acc.astype(o_ref.dtype)


def _nsa_cmp(r, w1, pe, w2p, rc, rs1, rs2):
    nc = r.shape[2]
    kvd = NSA_KV_HEADS * HEAD_DIM
    return pl.pallas_call(
        functools.partial(_nsa_cmp_kernel, nc=nc),
        grid=(2, NSA_KV_HEADS),
        in_specs=[
            pl.BlockSpec((1, 1, nc, CMP_STRIDE * HEAD_DIM), lambda a, b: (a, b, 0, 0)),
            pl.BlockSpec((1, CMP_BLOCK * HEAD_DIM, CMP_HIDDEN), lambda a, b: (a, 0, 0)),
            pl.BlockSpec((1, 8, CMP_BLOCK * HEAD_DIM), lambda a, b: (a, 0, 0)),
            pl.BlockSpec((1, 1, CMP_HIDDEN, kvd), lambda a, b: (a, b, 0, 0)),
            pl.BlockSpec((nc, LANES), lambda a, b: (0, 0)),
            pl.BlockSpec((nc, LANES), lambda a, b: (0, 0)),
            pl.BlockSpec((nc, LANES), lambda a, b: (0, 0)),
        ],
        out_specs=pl.BlockSpec((1, nc, kvd), lambda a, b: (a, 0, 0)),
        out_shape=jax.ShapeDtypeStruct((2, nc, kvd), BF16),
        scratch_shapes=[pltpu.VMEM((nc, kvd), F32)],
        compiler_params=_cparams(("parallel", "arbitrary")),
        name="nsa_cmp",
    )(r, w1, pe, w2p, rc, rs1, rs2)


def _nsa_cmp_attn_kernel(q_ref, kc_ref, vc_ref, ov_ref, oc_ref, sel_ref, *, t, nc, nsp):
    i = pl.program_id(0)
    qpos = i * t + lax.broadcasted_iota(jnp.int32, (t, 1), 0)
    cmp_end = CMP_STRIDE * lax.broadcasted_iota(jnp.int32, (1, nc), 1) + (CMP_BLOCK - 1)
    cmask = cmp_end <= qpos
    any_valid = qpos >= CMP_BLOCK - 1
    lane = lax.broadcasted_iota(jnp.int32, (t, LANES), 1)
    ov = ov_ref[...]
    blk = lax.broadcasted_iota(jnp.int32, (1, nsp), 1)
    cur = jnp.right_shift(qpos, SLC_SHIFT)
    forced = jnp.logical_or(blk == 0, jnp.logical_or(blk == cur, blk == cur - 1))
    valid = blk * SLC_BLOCK <= qpos
    vals = []
    for pair in range(NSA_KV_HEADS // 2):
        kc = kc_ref[0, :, pair * LANES:(pair + 1) * LANES]
        vc = vc_ref[0, :, pair * LANES:(pair + 1) * LANES]
        imp = [jnp.zeros((t, nc), F32), jnp.zeros((t, nc), F32)]
        for g in range(NSA_GROUP):
            blk_sl = slice((pair * NSA_GROUP + g) * LANES, (pair * NSA_GROUP + g + 1) * LANES)
            qh = _head_halves(q_ref[:, blk_sl])
            oc = []
            for h in range(2):
                s = jnp.where(cmask, _dot_nt(qh[h], kc), NEG)
                p = jnp.exp2(s - jnp.max(s, axis=1, keepdims=True))
                l = jnp.sum(p, axis=1, keepdims=True)
                pn = p * jnp.where(any_valid, 1.0 / l, 0.0)
                imp[h] = imp[h] + pn
                oc.append(_dot(pn.astype(BF16), vc))
            oc_ref[:, blk_sl] = jnp.where(lane < HEAD_DIM, oc[0], oc[1]).astype(oc_ref.dtype)
        for h in range(2):
            hi, mid, lo = _split3(imp[h])
            score = _dot(hi, ov) + _dot(mid, ov) + _dot(lo, ov)
            vals.append(jnp.where(valid, jnp.where(forced, FORCE_SCORE, score), NEG))

    blk_col = lax.broadcasted_iota(jnp.int32, (nsp, t), 0).astype(F32)

    def pick_one(_, vals_t):
        out = []
        for v in vals_t:
            mx = jnp.max(v, axis=0, keepdims=True)
            first = jnp.min(jnp.where(v == mx, blk_col, float(nsp)), axis=0, keepdims=True)
            out.append(jnp.where(blk_col == first, -jnp.inf, v))
        return tuple(out)

    vals_t = lax.fori_loop(0, SLC_TOPK, pick_one, tuple(v.T for v in vals))
    for kvh in range(NSA_KV_HEADS):
        picked = jnp.where(vals_t[kvh] == -jnp.inf, 1.0, 0.0).T
        sel_ref[:, kvh * nsp:(kvh + 1) * nsp] = jnp.where(valid, picked, 0.0).astype(sel_ref.dtype)


def _nsa_cmp_attn(proj, kvc, ov):
    S = proj.shape[0]
    t = NSA_TILE
    nc, nsp = ov.shape
    kvd = NSA_KV_HEADS * HEAD_DIM
    return pl.pallas_call(
        functools.partial(_nsa_cmp_attn_kernel, t=t, nc=nc, nsp=nsp),
        grid=(S // t,),
        in_specs=[
            pl.BlockSpec((t, D_MODEL), lambda i: (i, 0)),
            pl.BlockSpec((1, nc, kvd), lambda i: (0, 0, 0)),
            pl.BlockSpec((1, nc, kvd), lambda i: (1, 0, 0)),
            pl.BlockSpec((nc, nsp), lambda i: (0, 0)),
        ],
        out_specs=[pl.BlockSpec((t, D_MODEL), lambda i: (i, 0)),
                   pl.BlockSpec((t, NSA_KV_HEADS * nsp), lambda i: (i, 0))],
        out_shape=[jax.ShapeDtypeStruct((S, D_MODEL), BF16),
                   jax.ShapeDtypeStruct((S, NSA_KV_HEADS * nsp), BF16)],
        compiler_params=_cparams(("parallel",)),
        name="nsa_cmp_attn",
    )(proj, kvc, kvc, ov)


def _nsa_sel_kernel(q_ref, ks_ref, vs_ref, sel_ref, kw0_ref, kw1_ref, kw2_ref, vw0_ref, vw1_ref, vw2_ref,
                    gz_ref, e_ref, oc_ref, o_ref, s0_ref, s1_ref, p_ref, m_ref, l_ref, acc_ref, *, t, nsp):
    i = pl.program_id(1)
    n_heads = 2 * NSA_GROUP
    lane = lax.broadcasted_iota(jnp.int32, (t, LANES), 1)
    low = lane < HEAD_DIM
    qs = []
    for g in range(NSA_GROUP):
        qs.extend(_head_halves(q_ref[:, g * LANES:(g + 1) * LANES]))
    sel_bias = [((sel_ref[:, h * nsp:(h + 1) * nsp].astype(F32) - 1.0) * -NEG).astype(BF16) for h in range(2)]
    m_ref[...] = jnp.full(m_ref.shape, NEG, F32)
    l_ref[...] = jnp.zeros(l_ref.shape, F32)
    acc_ref[...] = jnp.zeros(acc_ref.shape, F32)
    blk_row = lax.broadcasted_iota(jnp.int32, (nsp, t), 0)
    tok_col = lax.broadcasted_iota(jnp.int32, (nsp, t), 1)
    s_bufs = (s0_ref, s1_ref)

    def scores(j, buf):
        off = pl.multiple_of(j * t, t)
        k = ks_ref[pl.ds(off, t), :]
        expand = jnp.where(blk_row == jnp.right_shift(off + tok_col, SLC_SHIFT), 1.0, 0.0).astype(BF16)
        for h in range(2):
            bias = _dot(sel_bias[h], expand)
            for g in range(NSA_GROUP):
                s_bufs[buf][2 * g + h] = _dot_nt(qs[2 * g + h], k) + bias

    def consume(j, buf, diag):
        v = vs_ref[pl.ds(pl.multiple_of(j * t, t), t), :]
        for idx in range(n_heads):
            _online_update(s_bufs[buf], p_ref, v, m_ref, l_ref, acc_ref, idx, diag)

    _pipelined_tiles(i, scores, consume)

    n_win = WIN // t + 1
    k_win = jnp.concatenate([kw0_ref[...], kw1_ref[...], kw2_ref[...]], axis=0)
    v_win = jnp.concatenate([vw0_ref[...], vw1_ref[...], vw2_ref[...]], axis=0)
    qpos = i * t + lax.broadcasted_iota(jnp.int32, (t, n_win * t), 0)
    kpos = (i - (n_win - 1)) * t + lax.broadcasted_iota(jnp.int32, (t, n_win * t), 1)
    wmask = jnp.logical_and(jnp.logical_and(kpos <= qpos, kpos > qpos - WIN), kpos >= 0)
    ow = []
    for idx in range(n_heads):
        s = jnp.where(wmask, _dot_nt(qs[idx], k_win), NEG)
        m = jnp.max(s, axis=1, keepdims=True)
        p = jnp.exp2(s - m)
        l = jnp.sum(p, axis=1, keepdims=True)
        ow.append(_dot(p.astype(BF16), v_win) / l)

    gates = jax.nn.sigmoid(_dot(gz_ref[...], e_ref[0]))
    gw = NSA_GROUP * LANES
    for g in range(NSA_GROUP):
        sl = slice(g * LANES, (g + 1) * LANES)
        o_cmp = oc_ref[:, sl].astype(F32)
        o_slc = jnp.where(low, _online_result(l_ref, acc_ref, 2 * g), _online_result(l_ref, acc_ref, 2 * g + 1))
        o_win = jnp.where(low, ow[2 * g], ow[2 * g + 1])
        out = (gates[:, g * LANES:(g + 1) * LANES] * o_cmp
               + gates[:, gw + g * LANES:gw + (g + 1) * LANES] * o_slc
               + gates[:, 2 * gw + g * LANES:2 * gw + (g + 1) * LANES] * o_win)
        o_ref[:, sl] = out.astype(o_ref.dtype)


def _nsa_sel(proj, sel, o_cmp, gate_expand):
    S = proj.shape[0]
    t = NSA_TILE
    nsp = sel.shape[1] // NSA_KV_HEADS
    n_heads = 2 * NSA_GROUP

    def win_spec(col0, back):
        return pl.BlockSpec((t, LANES), lambda p, i: (jnp.maximum(i - back, 0), col0 + p))

    return pl.pallas_call(
        functools.partial(_nsa_sel_kernel, t=t, nsp=nsp),
        grid=(NSA_KV_HEADS // 2, S // t),
        in_specs=[
            pl.BlockSpec((t, NSA_GROUP * LANES), lambda p, i: (i, p)),
            pl.BlockSpec((S, LANES), lambda p, i: (0, 12 + p)),
            pl.BlockSpec((S, LANES), lambda p, i: (0, 14 + p)),
            pl.BlockSpec((t, 2 * nsp), lambda p, i: (i, p)),
            win_spec(16, 2), win_spec(16, 1), win_spec(16, 0),
            win_spec(18, 2), win_spec(18, 1), win_spec(18, 0),
            pl.BlockSpec((t, NSA_PROJ_TILE), lambda p, i: (i, 10)),
            pl.BlockSpec((1, NSA_PROJ_TILE, 3 * NSA_GROUP * LANES), lambda p, i: (p, 0, 0)),
            pl.BlockSpec((t, NSA_GROUP * LANES), lambda p, i: (i, p)),
        ],
        out_specs=pl.BlockSpec((t, NSA_GROUP * LANES), lambda p, i: (i, p)),
        out_shape=jax.ShapeDtypeStruct((S, D_MODEL), BF16),
        scratch_shapes=([pltpu.VMEM((n_heads, t, t), F32)] * 2 + [pltpu.VMEM((n_heads, t, t), BF16)]
                        + [pltpu.VMEM((n_heads, t, LANES), F32)] * 3),
        compiler_params=_cparams(("parallel", "arbitrary")),
        name="nsa_sel",
    )(proj, proj, proj, sel, proj, proj, proj, proj, proj, proj, proj, gate_expand, o_cmp)


def _nsa_sel_t_kernel(qt_ref, ks_ref, vst_ref, selt_ref, kw0_ref, kw1_ref, kw2_ref, vwt0_ref, vwt1_ref, vwt2_ref,
                      gzt_ref, et_ref, oct_ref, ot_ref, s0_ref, s1_ref, bias_ref, m_ref, l_ref, acc_ref, *, t):
    i = pl.program_id(1)
    n_heads = 2 * NSA_GROUP
    row = lax.broadcasted_iota(jnp.int32, (LANES, t), 0)
    low = row < HEAD_DIM
    qs = []
    for g in range(NSA_GROUP):
        qb = qt_ref[0, g * LANES:(g + 1) * LANES, :].astype(F32)
        qs.append(jnp.where(low, qb, 0.0).astype(BF16))
        qs.append(jnp.where(low, 0.0, qb).astype(BF16))
    bias_ref[...] = ((selt_ref[...].astype(F32) - 1.0) * -NEG).astype(BF16)
    m_ref[...] = jnp.full(m_ref.shape, NEG, F32)
    l_ref[...] = jnp.zeros(l_ref.shape, F32)
    acc_ref[...] = jnp.zeros(acc_ref.shape, F32)
    tok_blk = jnp.right_shift(lax.broadcasted_iota(jnp.int32, (t, LANES), 0), SLC_SHIFT)
    lane_col = lax.broadcasted_iota(jnp.int32, (t, LANES), 1)
    zero_rows = jnp.zeros((LANES - SEL_GROUP, t), BF16)
    tiles_per_group = SEL_GROUP * SLC_BLOCK // t
    s_bufs = (s0_ref, s1_ref)

    def scores(j, buf):
        k = ks_ref[pl.ds(pl.multiple_of(j * t, t), t), :]
        grp = j // tiles_per_group
        first_blk = (j % tiles_per_group) * (t // SLC_BLOCK)
        expand = jnp.where(lane_col == first_blk + tok_blk, 1.0, 0.0).astype(BF16)
        lhs = jnp.concatenate([k, expand], axis=1)
        for h in range(2):
            blk_bias = bias_ref[h, grp]
            for g in range(NSA_GROUP):
                rhs = jnp.concatenate([qs[2 * g + h], blk_bias, zero_rows], axis=0)
                s_bufs[buf][2 * g + h] = _dot(lhs, rhs)

    def consume(j, buf, diag):
        vt = vst_ref[0, j]
        for idx in range(n_heads):
            _online_update_t(s_bufs[buf], vt, m_ref, l_ref, acc_ref, idx, diag)

    _pipelined_tiles(i, scores, consume)

    n_win = WIN // t + 1
    k_win = jnp.concatenate([kw0_ref[...], kw1_ref[...], kw2_ref[...]], axis=0)
    vt_win = jnp.concatenate([vwt0_ref[0], vwt1_ref[0], vwt2_ref[0]], axis=1)
    kpos = (i - (n_win - 1)) * t + lax.broadcasted_iota(jnp.int32, (n_win * t, t), 0)
    qpos = i * t + lax.broadcasted_iota(jnp.int32, (n_win * t, t), 1)
    wmask = jnp.logical_and(jnp.logical_and(kpos <= qpos, kpos > qpos - WIN), kpos >= 0)
    ow = []
    for idx in range(n_heads):
        s = jnp.where(wmask, _dot(k_win, qs[idx]), NEG)
        p = jnp.exp2(s - jnp.max(s, axis=0, keepdims=True))
        l = jnp.sum(p, axis=0, keepdims=True)
        ow.append(_dot(vt_win, p.astype(BF16)) / l)

    gates = jax.nn.sigmoid(_dot(et_ref[0], gzt_ref[...]))
    gw = NSA_GROUP * LANES
    for g in range(NSA_GROUP):
        rows = slice(g * LANES, (g + 1) * LANES)
        o_cmp = oct_ref[0, rows, :].astype(F32)
        o_slc = jnp.where(low, acc_ref[2 * g] / l_ref[2 * g], acc_ref[2 * g + 1] / l_ref[2 * g + 1])
        o_win = jnp.where(low, ow[2 * g], ow[2 * g + 1])
        out = (gates[g * LANES:(g + 1) * LANES, :] * o_cmp
               + gates[gw + g * LANES:gw + (g + 1) * LANES, :] * o_slc
               + gates[2 * gw + g * LANES:2 * gw + (g + 1) * LANES, :] * o_win)
        ot_ref[0, rows, :] = out.astype(ot_ref.dtype)


def _nsa_sel_t(proj, qt, vst, vwt, gzt, selt, oct, gate_expand_t):
    S = proj.shape[0]
    t = NSA_TILE
    n_grp = selt.shape[1]
    n_heads = 2 * NSA_GROUP
    gl = NSA_GROUP * LANES

    def kwin_spec(back):
        return pl.BlockSpec((t, LANES), lambda p, i: (jnp.maximum(i - back, 0), 16 + p))

    def vwin_spec(back):
        return pl.BlockSpec((1, LANES, t), lambda p, i: (p, 0, jnp.maximum(i - back, 0)))

    return pl.pallas_call(
        functools.partial(_nsa_sel_t_kernel, t=t),
        grid=(NSA_KV_HEADS // 2, S // t),
        in_specs=[
            pl.BlockSpec((1, gl, t), lambda p, i: (p, 0, i)),
            pl.BlockSpec((S, LANES), lambda p, i: (0, 12 + p)),
            pl.BlockSpec((1, S // t, LANES, t), lambda p, i: (p, 0, 0, 0)),
            pl.BlockSpec((2, n_grp, SEL_GROUP, t), lambda p, i: (p, 0, 0, i)),
            kwin_spec(2), kwin_spec(1), kwin_spec(0),
            vwin_spec(2), vwin_spec(1), vwin_spec(0),
            pl.BlockSpec((NSA_PROJ_TILE, t), lambda p, i: (0, i)),
            pl.BlockSpec((1, 3 * gl, NSA_PROJ_TILE), lambda p, i: (p, 0, 0)),
            pl.BlockSpec((1, gl, t), lambda p, i: (p, 0, i)),
        ],
        out_specs=pl.BlockSpec((1, gl, t), lambda p, i: (p, 0, i)),
        out_shape=jax.ShapeDtypeStruct((NSA_KV_HEADS // 2, gl, S), BF16),
        scratch_shapes=([pltpu.VMEM((n_heads, t, t), F32)] * 2 + [pltpu.VMEM((2, n_grp, SEL_GROUP, t), BF16)]
                        + [pltpu.VMEM((n_heads, 1, t), F32)] * 2 + [pltpu.VMEM((n_heads, LANES, t), F32)]),
        compiler_params=_cparams(("parallel", "arbitrary")),
        name="nsa_sel",
    )(qt, proj, vst, selt, proj, proj, proj, vwt, vwt, vwt, gzt, gate_expand_t, oct)


def _rope_tables(positions):
    half = ROT_DIM // 2
    inv = ROPE_THETA ** (-jnp.arange(0, ROT_DIM, 2, dtype=F32) / ROT_DIM)
    ang = positions.astype(F32)[:, None] * inv[None, :]
    cos, sin = jnp.cos(ang), jnp.sin(ang)
    S = positions.shape[0]
    ones = jnp.ones((S, HEAD_DIM - ROT_DIM), F32)
    zeros_h = jnp.zeros((S, half), F32)
    zeros_r = jnp.zeros((S, HEAD_DIM - ROT_DIM), F32)
    rc = jnp.concatenate([cos, cos, ones], axis=1)
    rs1 = jnp.concatenate([-sin, zeros_h, zeros_r], axis=1)
    rs2 = jnp.concatenate([zeros_h, sin, zeros_r], axis=1)
    reps = LANES // HEAD_DIM
    return jnp.tile(rc, (1, reps)), jnp.tile(rs1, (1, reps)), jnp.tile(rs2, (1, reps))


def _pair_heads(w, axis):
    shape = w.shape
    w = w.reshape(shape[:axis] + (NSA_KV_HEADS // 2, 2, NSA_GROUP, HEAD_DIM) + shape[axis + 1:])
    w = jnp.swapaxes(w, axis + 1, axis + 2)
    return w.reshape(shape)


def _nsa_constants(S):
    nc = S // CMP_STRIDE
    n_cmp = (S - CMP_BLOCK) // CMP_STRIDE + 1
    ns = S // SLC_BLOCK
    nsp = -(-ns // LANES) * LANES
    ci = np.arange(nc)[:, None] * CMP_STRIDE
    st = np.arange(nsp)[None, :] * SLC_BLOCK
    ov = (ci < st + SLC_BLOCK) & (ci + CMP_BLOCK > st) & (np.arange(nc)[:, None] < n_cmp) & (np.arange(nsp)[None, :] < ns)
    e = np.zeros((NSA_KV_HEADS // 2, NSA_PROJ_TILE, 3 * NSA_GROUP * LANES), np.float32)
    for p in range(NSA_KV_HEADS // 2):
        for br in range(3):
            for g in range(NSA_GROUP):
                for hf in range(2):
                    r = br * N_HEADS + (2 * p + hf) * NSA_GROUP + g
                    c0 = br * NSA_GROUP * LANES + g * LANES + hf * HEAD_DIM
                    e[p, r, c0:c0 + HEAD_DIM] = 1.0
    cmp_end = np.minimum(np.arange(nc) * CMP_STRIDE + CMP_BLOCK - 1, S - 1)
    return nc, jnp.asarray(ov.astype(np.float32), BF16), jnp.asarray(e, BF16), cmp_end


def _nsa_layer(h, g, positions, w_in, pe_k, w1_k, w2_k, pe_v, w1_v, w2_v, w_o):
    S, D = h.shape
    qd = N_HEADS * HEAD_DIM
    kvd = NSA_KV_HEADS * HEAD_DIM
    nc, ov, gate_expand, cmp_end = _nsa_constants(S)
    rc, rs1, rs2 = _rope_tables(positions)

    wq = _pair_heads(w_in[:, :qd], 1) * QK_SCALE
    wg = jnp.pad(w_in[:, qd + 6 * kvd:], ((0, 0), (0, NSA_PROJ_TILE - 3 * N_HEADS)))
    w_cat = jnp.concatenate([wq, w_in[:, qd:qd + 6 * kvd], wg], axis=1).astype(BF16)
    proj = _nsa_proj(h, g, w_cat, rc, rs1, rs2)

    raw = proj[:, qd:qd + 2 * kvd].reshape(nc, CMP_STRIDE, 2, NSA_KV_HEADS, HEAD_DIM)
    raw = raw.transpose(2, 3, 0, 1, 4).reshape(2, NSA_KV_HEADS, nc, CMP_STRIDE * HEAD_DIM)
    w1 = jnp.stack([w1_k, w1_v]).astype(BF16)
    pe = jnp.stack([pe_k.reshape(1, -1), pe_v.reshape(1, -1)])
    pe = jnp.pad(pe, ((0, 0), (0, 7), (0, 0))).astype(BF16)
    w2 = jnp.stack([w2_k, w2_v])
    eye = jnp.eye(NSA_KV_HEADS, dtype=F32)
    w2p = (w2[:, None, :, None, :] * eye[None, :, None, :, None]).reshape(2, NSA_KV_HEADS, CMP_HIDDEN, kvd).astype(BF16)
    kvc = _nsa_cmp(raw, w1, pe, w2p, rc[cmp_end], rs1[cmp_end], rs2[cmp_end])

    o_cmp, sel = _nsa_cmp_attn(proj, kvc, ov)
    t = NSA_TILE
    n_pair = NSA_KV_HEADS // 2
    qt = proj[:, :qd].T.reshape(n_pair, NSA_GROUP * LANES, S)
    vst = proj[:, 14 * LANES:16 * LANES].T.reshape(n_pair, LANES, S // t, t).transpose(0, 2, 1, 3)
    vwt = proj[:, 18 * LANES:20 * LANES].T.reshape(n_pair, LANES, S)
    gzt = proj[:, 20 * LANES:].T
    oct = o_cmp.T.reshape(n_pair, NSA_GROUP * LANES, S)
    selt = sel.T.reshape(NSA_KV_HEADS, -1, SEL_GROUP, S)
    ot = _nsa_sel_t(proj, qt, vst, vwt, gzt, selt, oct, gate_expand.transpose(0, 2, 1))
    o = ot.reshape(qd, S).T
    return _matmul_res(o, _pair_heads(w_o, 0).astype(BF16), h, "nsa_out")


def _trunk(x2, positions, p):
    h = _fox_layer(x2, p["l0_norm_mix"], p["l0_fox_w_qkv"], p["l0_fox_w_f"], p["l0_fox_b_f"], p["l0_fox_w_o"])
    h = _mlp(h, p["l0_norm_mlp"], p["l0_mlp_w1"].astype(BF16), p["l0_mlp_w2"].astype(BF16))
    h = _pool_layer(h, p["l1_norm_mix"], p["l1_pool_w"], p["l1_pool_scale"])
    h = _mlp(h, p["l1_norm_mlp"], p["l1_mlp_w1"].astype(BF16), p["l1_mlp_w2"].astype(BF16))
    h = _conv_layer(h, p["l2_norm_mix"], p["l2_conv_w_in"], p["l2_conv_w"], p["l2_conv_w_out"])
    h = _mlp(h, p["l2_norm_mlp"], p["l2_mlp_w1"].astype(BF16), p["l2_mlp_w2"].astype(BF16))
    h = _nsa_layer(h, p["l3_norm_mix"], positions, p["l3_nsa_w_in"], p["l3_nsa_cmp_pe_k"], p["l3_nsa_cmp_w1_k"],
                   p["l3_nsa_cmp_w2_k"], p["l3_nsa_cmp_pe_v"], p["l3_nsa_cmp_w1_v"], p["l3_nsa_cmp_w2_v"],
                   p["l3_nsa_w_o"])
    return _mlp(h, p["l3_norm_mlp"], p["l3_mlp_w1"].astype(BF16), p["l3_mlp_w2"].astype(BF16), p["final_norm"])


def kernel(x, positions, l0_norm_mix, l0_fox_w_qkv, l0_fox_w_f, l0_fox_b_f, l0_fox_w_o, l0_norm_mlp, l0_mlp_w1, l0_mlp_w2, l1_norm_mix, l1_pool_w, l1_pool_scale, l1_norm_mlp, l1_mlp_w1, l1_mlp_w2, l2_norm_mix, l2_conv_w_in, l2_conv_w, l2_conv_w_out, l2_norm_mlp, l2_mlp_w1, l2_mlp_w2, l3_norm_mix, l3_nsa_w_in, l3_nsa_cmp_pe_k, l3_nsa_cmp_w1_k, l3_nsa_cmp_w2_k, l3_nsa_cmp_pe_v, l3_nsa_cmp_w1_v, l3_nsa_cmp_w2_v, l3_nsa_w_o, l3_norm_mlp, l3_mlp_w1, l3_mlp_w2, final_norm):
    params = dict(locals())
    B, S, D = x.shape
    outs = [_trunk(x[b], positions, params) for b in range(B)]
    return jnp.stack(outs, axis=0)
```

```python
import functools

import numpy as np
import jax
import jax.numpy as jnp
from jax import lax
from jax.experimental import pallas as pl
from jax.experimental.pallas import tpu as pltpu

F32 = jnp.float32
BF16 = jnp.bfloat16

D_MODEL = 1024
HEAD_DIM = 64
N_HEADS = D_MODEL // HEAD_DIM
D_FF = 4 * D_MODEL
ROPE_THETA = 500000.0
ROT_DIM = HEAD_DIM // 4
RMS_EPS = 1e-6
POOL_WINDOWS = (2, 4, 8, 16)
POOL_GROUP = D_MODEL // len(POOL_WINDOWS)
POOL_HALO = 16
CONV_WIDTH = 3
CONV_HALO = 8
NSA_KV_HEADS = 4
NSA_GROUP = N_HEADS // NSA_KV_HEADS
CMP_BLOCK = 32
CMP_STRIDE = 16
CMP_HIDDEN = 256
SLC_BLOCK = 64
SLC_SHIFT = 6
SLC_TOPK = 16
WIN = 512
FORCE_SCORE = 1e9
NEG = -1e30
LOG2E = 1.4426950408889634
QK_SCALE = HEAD_DIM ** -0.5 * LOG2E

LANES = 128
V7X_VMEM_LIMIT = 56 * 1024 * 1024
FOX_TILE = 512
FOX_BIAS_TERMS = 3
NSA_TILE = WIN // 2
SEL_GROUP = 16
NSA_PROJ_TILE = 2 * LANES
NSA_NAT_WIDTH = 8 * LANES
NSA_NAT_ROPE_TILES = (2, 3)
NSA_T_ROWS = 14 * LANES
NSA_T_ROPE_TILES = (0, 1, 2, 3)


def _cparams(semantics):
    return pltpu.CompilerParams(dimension_semantics=semantics, vmem_limit_bytes=V7X_VMEM_LIMIT)


def _rms(x, g):
    return x * lax.rsqrt(jnp.mean(x * x, axis=-1, keepdims=True) + RMS_EPS) * g


def _dot(a, b):
    return jnp.dot(a, b, preferred_element_type=F32)


def _dot_nt(a, b):
    return lax.dot_general(a, b, (((1,), (1,)), ((), ())), preferred_element_type=F32)


def _dot_tn(a, b):
    return lax.dot_general(a, b, (((0,), (0,)), ((), ())), preferred_element_type=F32)


def _split3(x):
    hi = x.astype(BF16)
    r = x - hi.astype(F32)
    mid = r.astype(BF16)
    lo = (r - mid.astype(F32)).astype(BF16)
    return hi, mid, lo


def _rope(a, rc, rs1, rs2, axis):
    half = ROT_DIM // 2
    return a * rc + pltpu.roll(a, LANES - half, axis) * rs1 + pltpu.roll(a, half, axis) * rs2


def _head_rows(qt_blk):
    low = lax.broadcasted_iota(jnp.int32, qt_blk.shape, 0) < HEAD_DIM
    qf = qt_blk.astype(F32)
    return jnp.where(low, qf, 0.0).astype(BF16), jnp.where(low, 0.0, qf).astype(BF16)


def _online_update(s_ref, vt, m_ref, l_ref, acc_ref, idx, causal):
    tk, t = s_ref.shape[1:]
    s = s_ref[idx]
    if causal:
        s = jnp.where(lax.broadcasted_iota(jnp.int32, (tk, t), 0) <= lax.broadcasted_iota(jnp.int32, (tk, t), 1),
                      s, NEG)
    m_old = m_ref[idx]
    m_new = jnp.maximum(m_old, jnp.max(s, axis=0, keepdims=True))
    alpha = jnp.exp2(m_old - m_new)
    p = jnp.exp2(s - m_new)
    l_ref[idx] = alpha * l_ref[idx] + jnp.sum(p, axis=0, keepdims=True)
    m_ref[idx] = m_new
    acc_ref[idx] = alpha * acc_ref[idx] + _dot(vt, p.astype(BF16))


def _pipelined_tiles(i, scores, consume, pairs_per_step=1):
    scores(0, 0)

    def pair(j):
        scores(j + 1, 1)
        consume(j, 0, False)
        scores(j + 2, 0)
        consume(j + 1, 1, False)

    def body(jj, carry):
        for u in range(pairs_per_step):
            pair(2 * (pairs_per_step * jj + u))
        return carry

    n_pairs = i // 2
    n_steps = n_pairs // pairs_per_step
    lax.fori_loop(0, n_steps, body, 0)
    if pairs_per_step > 1:
        def tail(jj, carry):
            pair(2 * jj)
            return carry

        lax.fori_loop(n_steps * pairs_per_step, n_pairs, tail, 0)

    @pl.when(i % 2 == 0)
    def _():
        consume(i, 0, True)

    @pl.when(i % 2 == 1)
    def _():
        scores(i, 1)
        consume(i - 1, 0, False)
        consume(i, 1, True)


def _mlp_kernel(x_ref, g_ref, w1_ref, w2_ref, *rest, nf, final):
    if final:
        fg_ref, o_ref, xn_ref = rest
    else:
        o_ref, xn_ref = rest
    f = pl.program_id(1)

    @pl.when(f == 0)
    def _():
        x = x_ref[...]
        xn_ref[...] = _rms(x, g_ref[...]).astype(BF16)
        o_ref[...] = x

    a = _dot(xn_ref[...], w1_ref[...])
    a = jnp.square(jnp.maximum(a, 0.0)).astype(BF16)
    o_ref[...] += _dot(a, w2_ref[...])

    if final:
        @pl.when(f == nf - 1)
        def _():
            o_ref[...] = _rms(o_ref[...], fg_ref[...])


def _mlp(h, g, w1, w2, final_g=None):
    S, D = h.shape
    F = w1.shape[1]
    tm = min(1024, S)
    tf = 1024
    nf = F // tf
    final = final_g is not None
    in_specs = [
        pl.BlockSpec((tm, D), lambda i, f: (i, 0)),
        pl.BlockSpec((1, D), lambda i, f: (0, 0)),
        pl.BlockSpec((D, tf), lambda i, f: (0, f)),
        pl.BlockSpec((tf, D), lambda i, f: (f, 0)),
    ]
    args = [h, g.reshape(1, D), w1, w2]
    if final:
        in_specs.append(pl.BlockSpec((1, D), lambda i, f: (0, 0)))
        args.append(final_g.reshape(1, D))
    return pl.pallas_call(
        functools.partial(_mlp_kernel, nf=nf, final=final),
        grid=(S // tm, nf),
        in_specs=in_specs,
        out_specs=pl.BlockSpec((tm, D), lambda i, f: (i, 0)),
        out_shape=jax.ShapeDtypeStruct((S, D), F32),
        scratch_shapes=[pltpu.VMEM((tm, D), BF16)],
        compiler_params=_cparams(("parallel", "arbitrary")),
        name="mlp",
    )(*args)


def _proj_kernel(x_ref, g_ref, w_ref, *rest, key_major, rope_tiles):
    if rope_tiles:
        rc_ref, rs1_ref, rs2_ref, o_ref, xn_ref = rest
    else:
        o_ref, xn_ref = rest
    j = pl.program_id(1)

    @pl.when(j == 0)
    def _():
        xn_ref[...] = _rms(x_ref[...], g_ref[...]).astype(BF16)

    a = _dot_nt(w_ref[...], xn_ref[...]) if key_major else _dot(xn_ref[...], w_ref[...])
    if not rope_tiles:
        o_ref[...] = a.astype(o_ref.dtype)
        return
    axis = 0 if key_major else 1
    is_rope = functools.reduce(jnp.logical_or, [j == t for t in rope_tiles])

    @pl.when(is_rope)
    def _():
        rc, rs1, rs2 = rc_ref[...], rs1_ref[...], rs2_ref[...]
        blocks = [lax.slice_in_dim(a, b * LANES, (b + 1) * LANES, axis=axis) for b in range(a.shape[axis] // LANES)]
        o_ref[...] = jnp.concatenate([_rope(blk, rc, rs1, rs2, axis) for blk in blocks], axis=axis).astype(o_ref.dtype)

    @pl.when(jnp.logical_not(is_rope))
    def _():
        o_ref[...] = a.astype(o_ref.dtype)


def _proj(h, g, w, name, *, key_major, tn, rope=None, rope_tiles=()):
    S, D = h.shape
    tm = min(1024, S)
    if key_major:
        N = w.shape[0]
        w_spec = pl.BlockSpec((tn, D), lambda i, j: (j, 0))
        rope_spec = pl.BlockSpec((LANES, tm), lambda i, j: (0, i))
        out_spec = pl.BlockSpec((tn, tm), lambda i, j: (j, i))
        out_shape = jax.ShapeDtypeStruct((N, S), BF16)
    else:
        N = w.shape[1]
        w_spec = pl.BlockSpec((D, tn), lambda i, j: (0, j))
        rope_spec = pl.BlockSpec((tm, LANES), lambda i, j: (i, 0))
        out_spec = pl.BlockSpec((tm, tn), lambda i, j: (i, j))
        out_shape = jax.ShapeDtypeStruct((S, N), BF16)
    in_specs = [pl.BlockSpec((tm, D), lambda i, j: (i, 0)), pl.BlockSpec((1, D), lambda i, j: (0, 0)), w_spec]
    args = [h, g.reshape(1, D), w]
    if rope_tiles:
        in_specs += [rope_spec] * 3
        args += list(rope)
    return pl.pallas_call(
        functools.partial(_proj_kernel, key_major=key_major, rope_tiles=tuple(rope_tiles)),
        grid=(S // tm, N // tn),
        in_specs=in_specs,
        out_specs=out_spec,
        out_shape=out_shape,
        scratch_shapes=[pltpu.VMEM((tm, D), BF16)],
        compiler_params=_cparams(("parallel", "arbitrary")),
        name=name,
    )(*args)


def _matmul_res_kernel(at_ref, w_ref, r_ref, o_ref):
    o_ref[...] = r_ref[...] + _dot_tn(at_ref[...], w_ref[...])


def _matmul_res(at, w, res, name):
    K, S = at.shape
    N = w.shape[1]
    tm = min(1024, S)
    return pl.pallas_call(
        _matmul_res_kernel,
        grid=(S // tm,),
        in_specs=[
            pl.BlockSpec((K, tm), lambda i: (0, i)),
            pl.BlockSpec((K, N), lambda i: (0, 0)),
            pl.BlockSpec((tm, N), lambda i: (i, 0)),
        ],
        out_specs=pl.BlockSpec((tm, N), lambda i: (i, 0)),
        out_shape=jax.ShapeDtypeStruct((S, N), F32),
        compiler_params=_cparams(("parallel",)),
        name=name,
    )(at, w, res)


def _fox_gate_kernel(x_ref, g_ref, wf_ref, bf_ref, place_ref, c_ref, carry_ref, *, tm):
    @pl.when(pl.program_id(0) == 0)
    def _():
        carry_ref[...] = jnp.zeros_like(carry_ref)

    xn = _rms(x_ref[...], g_ref[...]).astype(BF16)
    z = _dot(xn, wf_ref[...]) + bf_ref[...]
    logf = jnp.minimum(z, 0.0) - jnp.log(1.0 + jnp.exp(-jnp.abs(z)))
    row = lax.broadcasted_iota(jnp.int32, (tm, tm), 0)
    col = lax.broadcasted_iota(jnp.int32, (tm, tm), 1)
    tri = jnp.where(row >= col, 1.0, 0.0).astype(BF16)
    hi, mid, lo = _split3(logf)
    c = _dot(tri, hi) + _dot(tri, mid) + _dot(tri, lo) + carry_ref[...]
    carry_ref[...] = c[tm - 1:tm, :]
    terms = jnp.concatenate(_split3(c * -LOG2E), axis=1)
    c_ref[...] = _dot(terms, place_ref[...]).astype(c_ref.dtype)


def _fox_gate(h, g, wf, bf, place):
    S, D = h.shape
    tm = min(512, S)
    return pl.pallas_call(
        functools.partial(_fox_gate_kernel, tm=tm),
        grid=(S // tm,),
        in_specs=[
            pl.BlockSpec((tm, D), lambda i: (i, 0)),
            pl.BlockSpec((1, D), lambda i: (0, 0)),
            pl.BlockSpec((D, LANES), lambda i: (0, 0)),
            pl.BlockSpec((1, LANES), lambda i: (0, 0)),
            pl.BlockSpec((FOX_BIAS_TERMS * LANES, D_MODEL), lambda i: (0, 0)),
        ],
        out_specs=pl.BlockSpec((tm, D_MODEL), lambda i: (i, 0)),
        out_shape=jax.ShapeDtypeStruct((S, D_MODEL), BF16),
        scratch_shapes=[pltpu.VMEM((1, LANES), F32)],
        compiler_params=_cparams(("arbitrary",)),
        name="fox_gate",
    )(h, g.reshape(1, D), wf, bf, place)


def _fox_attn_kernel(qt_ref, k_ref, c_ref, vt_ref, ot_ref, s0_ref, s1_ref, m_ref, l_ref, acc_ref, *, t):
    i = pl.program_id(1)
    row = lax.broadcasted_iota(jnp.int32, (LANES, t), 0)
    q_heads = _head_rows(qt_ref[...])
    qa = []
    for h in range(2):
        ones_rows = jnp.logical_and(row >= FOX_BIAS_TERMS * h, row < FOX_BIAS_TERMS * (h + 1))
        qa.append(jnp.concatenate([q_heads[h], jnp.where(ones_rows, 1.0, 0.0).astype(BF16)], axis=0))
    m_ref[...] = jnp.full(m_ref.shape, NEG, F32)
    l_ref[...] = jnp.zeros(l_ref.shape, F32)
    acc_ref[...] = jnp.zeros(acc_ref.shape, F32)
    s_bufs = (s0_ref, s1_ref)

    def scores(j, buf):
        rows = pl.ds(pl.multiple_of(j * t, t), t)
        lhs = jnp.concatenate([k_ref[rows, :], c_ref[rows, :]], axis=1)
        for h in range(2):
            s_bufs[buf][h] = _dot(lhs, qa[h])

    def consume(j, buf, diag):
        vt = vt_ref[:, pl.ds(pl.multiple_of(j * t, t), t)]
        for h in range(2):
            _online_update(s_bufs[buf], vt, m_ref, l_ref, acc_ref, h, diag)

    _pipelined_tiles(i, scores, consume, pairs_per_step=2)
    ot = jnp.where(row < HEAD_DIM, acc_ref[0] / l_ref[0], acc_ref[1] / l_ref[1])
    ot_ref[...] = ot.astype(ot_ref.dtype)


def _fox_attn(qvt, k, c_terms):
    S = k.shape[0]
    t = min(FOX_TILE, S)
    n_pairs = N_HEADS // 2
    return pl.pallas_call(
        functools.partial(_fox_attn_kernel, t=t),
        grid=(n_pairs, S // t),
        in_specs=[
            pl.BlockSpec((LANES, t), lambda hp, i: (hp, i)),
            pl.BlockSpec((S, LANES), lambda hp, i: (0, hp)),
            pl.BlockSpec((S, LANES), lambda hp, i: (0, hp)),
            pl.BlockSpec((LANES, S), lambda hp, i: (n_pairs + hp, 0)),
        ],
        out_specs=pl.BlockSpec((LANES, t), lambda hp, i: (hp, i)),
        out_shape=jax.ShapeDtypeStruct((D_MODEL, S), BF16),
        scratch_shapes=([pltpu.VMEM((2, t, t), F32)] * 2 + [pltpu.VMEM((2, 1, t), F32)] * 2
                        + [pltpu.VMEM((2, LANES, t), F32)]),
        compiler_params=_cparams(("parallel", "arbitrary")),
        name="fox_attn",
    )(qvt, k, c_terms, qvt)


def _fox_bias_placement():
    place = np.zeros((FOX_BIAS_TERMS * LANES, D_MODEL), np.float32)
    for head in range(N_HEADS):
        for n in range(FOX_BIAS_TERMS):
            place[n * LANES + head, (head // 2) * LANES + FOX_BIAS_TERMS * (head % 2) + n] = 1.0
    return jnp.asarray(place, BF16)


def _fox_layer(h, g, w_qkv, w_f, b_f, w_o):
    wf = jnp.pad(w_f, ((0, 0), (0, LANES - N_HEADS))).astype(BF16)
    bf = jnp.pad(b_f, (0, LANES - N_HEADS)).reshape(1, LANES)
    c_terms = _fox_gate(h, g, wf, bf, _fox_bias_placement())
    w_qv = jnp.concatenate([w_qkv[:, :D_MODEL] * QK_SCALE, w_qkv[:, 2 * D_MODEL:]], axis=1).T.astype(BF16)
    qvt = _proj(h, g, w_qv, "fox_qv", key_major=True, tn=512)
    k = _proj(h, g, w_qkv[:, D_MODEL:2 * D_MODEL].astype(BF16), "fox_k", key_major=False, tn=512)
    ot = _fox_attn(qvt, k, c_terms)
    return _matmul_res(ot, w_o.astype(BF16), h, "fox_out")


def _pool_kernel(x_ref, halo_ref, g_ref, w_ref, sc_ref, o_ref, *, tm):
    i = pl.program_id(0)
    x = x_ref[...]
    g = g_ref[...]
    xn = _rms(x, g)
    hn = jnp.where(i > 0, _rms(halo_ref[...], g), 0.0)
    xe = jnp.concatenate([hn, xn], axis=0)
    tpos = i * tm + lax.broadcasted_iota(jnp.int32, (tm, 1), 0)
    for gi, w in enumerate(POOL_WINDOWS):
        sl = slice(gi * POOL_GROUP, (gi + 1) * POOL_GROUP)
        s = xe[:, sl]
        k = 1
        while k < w:
            s = s + pltpu.roll(s, k, 0)
            k *= 2
        cnt = jnp.minimum(tpos + 1, w).astype(F32)
        d = (s[POOL_HALO:, :] / cnt - xn[:, sl]).astype(BF16)
        o_ref[:, sl] = x[:, sl] + _dot(d, w_ref[gi]) * sc_ref[:, sl]


def _pool_layer(h, g, w_pool, pool_scale):
    S, D = h.shape
    tm = min(1024, S)
    return pl.pallas_call(
        functools.partial(_pool_kernel, tm=tm),
        grid=(S // tm,),
        in_specs=[
            pl.BlockSpec((tm, D), lambda i: (i, 0)),
            pl.BlockSpec((POOL_HALO, D), lambda i: (jnp.maximum(i * (tm // POOL_HALO) - 1, 0), 0)),
            pl.BlockSpec((1, D), lambda i: (0, 0)),
            pl.BlockSpec((len(POOL_WINDOWS), POOL_GROUP, POOL_GROUP), lambda i: (0, 0, 0)),
            pl.BlockSpec((1, D), lambda i: (0, 0)),
        ],
        out_specs=pl.BlockSpec((tm, D), lambda i: (i, 0)),
        out_shape=jax.ShapeDtypeStruct((S, D), F32),
        compiler_params=_cparams(("parallel",)),
        name="pool",
    )(h, h, g.reshape(1, D), w_pool.astype(BF16), pool_scale.reshape(1, D))


def _conv_in_kernel(x_ref, g_ref, wb_ref, wc_ref, wu_ref, b_ref, z_ref, xn_ref):
    @pl.when(pl.program_id(1) == 0)
    def _():
        xn_ref[...] = _rms(x_ref[...], g_ref[...]).astype(BF16)

    xn = xn_ref[...]
    b_ref[...] = _dot(xn, wb_ref[...])
    z_ref[...] = _dot(xn, wc_ref[...]) * _dot(xn, wu_ref[...])


def _conv_out_kernel(b_ref, z_ref, zh_ref, cw_ref, w_ref, r_ref, o_ref, *, tm):
    i = pl.program_id(0)
    z = z_ref[...]
    zh = jnp.where(i > 0, zh_ref[...], 0.0)
    row = lax.broadcasted_iota(jnp.int32, (tm, 1), 0)
    prev1 = zh[CONV_HALO - 1:CONV_HALO, :]
    prev2 = zh[CONV_HALO - 2:CONV_HALO - 1, :]
    z1 = jnp.where(row == 0, prev1, pltpu.roll(z, 1, 0))
    z2 = jnp.where(row == 0, prev2, jnp.where(row == 1, prev1, pltpu.roll(z, 2, 0)))
    cw = cw_ref[...]
    conv = cw[0:1, :] * z2 + cw[1:2, :] * z1 + cw[2:3, :] * z
    y = (b_ref[...] * conv).astype(BF16)
    o_ref[...] = r_ref[...] + _dot(y, w_ref[...])


def _conv_layer(h, g, w_in, conv_w, w_out):
    S, D = h.shape
    tm = min(1024, S)
    tn = 512
    nj = D // tn
    w_in = w_in.astype(BF16)
    b, z = pl.pallas_call(
        _conv_in_kernel,
        grid=(S // tm, nj),
        in_specs=[
            pl.BlockSpec((tm, D), lambda i, j: (i, 0)),
            pl.BlockSpec((1, D), lambda i, j: (0, 0)),
            pl.BlockSpec((D, tn), lambda i, j: (0, j)),
            pl.BlockSpec((D, tn), lambda i, j: (0, nj + j)),
            pl.BlockSpec((D, tn), lambda i, j: (0, 2 * nj + j)),
        ],
        out_specs=[pl.BlockSpec((tm, tn), lambda i, j: (i, j)),
                   pl.BlockSpec((tm, tn), lambda i, j: (i, j))],
        out_shape=[jax.ShapeDtypeStruct((S, D), F32), jax.ShapeDtypeStruct((S, D), F32)],
        scratch_shapes=[pltpu.VMEM((tm, D), BF16)],
        compiler_params=_cparams(("parallel", "arbitrary")),
        name="conv_in",
    )(h, g.reshape(1, D), w_in, w_in, w_in)
    cw = jnp.pad(conv_w, ((0, 8 - CONV_WIDTH), (0, 0)))
    return pl.pallas_call(
        functools.partial(_conv_out_kernel, tm=tm),
        grid=(S // tm,),
        in_specs=[
            pl.BlockSpec((tm, D), lambda i: (i, 0)),
            pl.BlockSpec((tm, D), lambda i: (i, 0)),
            pl.BlockSpec((CONV_HALO, D), lambda i: (jnp.maximum(i * (tm // CONV_HALO) - 1, 0), 0)),
            pl.BlockSpec((8, D), lambda i: (0, 0)),
            pl.BlockSpec((D, D), lambda i: (0, 0)),
            pl.BlockSpec((tm, D), lambda i: (i, 0)),
        ],
        out_specs=pl.BlockSpec((tm, D), lambda i: (i, 0)),
        out_shape=jax.ShapeDtypeStruct((S, D), F32),
        compiler_params=_cparams(("parallel",)),
        name="conv_out",
    )(b, z, z, cw, w_out.astype(BF16), h)


def _nsa_cmp_kernel(r_ref, w1_ref, pe_ref, w2_ref, rc_ref, rs1_ref, rs2_ref, o_ref, acc_ref, *, nc):
    kv = pl.program_id(0)
    hd = pl.program_id(1)
    half = CMP_STRIDE * HEAD_DIM
    r = r_ref[0, 0]
    w1 = w1_ref[0]
    first = _dot(r, w1[:half, :])
    second = _dot(r, w1[half:, :])
    pe_term = _dot(pe_ref[0], w1)[0:1, :]
    pre = first + pltpu.roll(second, nc - 1, 0) + pe_term
    ge = 0.5 * pre * (1.0 + jnp.tanh(0.7978845608028654 * (pre + 0.044715 * pre * pre * pre)))
    y = _dot(ge.astype(BF16), w2_ref[0, 0])

    @pl.when(hd == 0)
    def _():
        acc_ref[...] = y

    @pl.when(hd > 0)
    def _():
        acc_ref[...] += y

    @pl.when(hd == NSA_KV_HEADS - 1)
    def _():
        acc = acc_ref[...]

        @pl.when(kv == 0)
        def _():
            rc, rs1, rs2 = rc_ref[...], rs1_ref[...], rs2_ref[...]
            o_ref[0] = jnp.concatenate(
                [_rope(acc[:, :LANES], rc, rs1, rs2, 1), _rope(acc[:, LANES:], rc, rs1, rs2, 1)],
                axis=1).astype(o_ref.dtype)

        @pl.when(kv == 1)
        def _():
            o_ref[0] = acc.astype(o_ref.dtype)


def _nsa_cmp(r, w1, pe, w2p, rc, rs1, rs2):
    nc = r.shape[2]
    kvd = NSA_KV_HEADS * HEAD_DIM
    return pl.pallas_call(
        functools.partial(_nsa_cmp_kernel, nc=nc),
        grid=(2, NSA_KV_HEADS),
        in_specs=[
            pl.BlockSpec((1, 1, nc, CMP_STRIDE * HEAD_DIM), lambda a, b: (a, b, 0, 0)),
            pl.BlockSpec((1, CMP_BLOCK * HEAD_DIM, CMP_HIDDEN), lambda a, b: (a, 0, 0)),
            pl.BlockSpec((1, 8, CMP_BLOCK * HEAD_DIM), lambda a, b: (a, 0, 0)),
            pl.BlockSpec((1, 1, CMP_HIDDEN, kvd), lambda a, b: (a, b, 0, 0)),
            pl.BlockSpec((nc, LANES), lambda a, b: (0, 0)),
            pl.BlockSpec((nc, LANES), lambda a, b: (0, 0)),
            pl.BlockSpec((nc, LANES), lambda a, b: (0, 0)),
        ],
        out_specs=pl.BlockSpec((1, nc, kvd), lambda a, b: (a, 0, 0)),
        out_shape=jax.ShapeDtypeStruct((2, nc, kvd), BF16),
        scratch_shapes=[pltpu.VMEM((nc, kvd), F32)],
        compiler_params=_cparams(("parallel", "arbitrary")),
        name="nsa_cmp",
    )(r, w1, pe, w2p, rc, rs1, rs2)


def _nsa_cmp_attn_kernel(qt_ref, kc_ref, vct_ref, ovt_ref, oct_ref, selt_ref, *, t, nc, nsp):
    i = pl.program_id(0)
    qpos = i * t + lax.broadcasted_iota(jnp.int32, (1, t), 1)
    cmp_end = CMP_STRIDE * lax.broadcasted_iota(jnp.int32, (nc, t), 0) + (CMP_BLOCK - 1)
    cmask = cmp_end <= qpos
    any_valid = qpos >= CMP_BLOCK - 1
    low = lax.broadcasted_iota(jnp.int32, (LANES, t), 0) < HEAD_DIM
    ovt = ovt_ref[...]
    blk = lax.broadcasted_iota(jnp.int32, (nsp, t), 0)
    blk_f = blk.astype(F32)
    cur = jnp.right_shift(qpos, SLC_SHIFT)
    forced = jnp.logical_or(blk == 0, jnp.logical_or(blk == cur, blk == cur - 1))
    valid = blk * SLC_BLOCK <= qpos
    vals = []
    for pair in range(NSA_KV_HEADS // 2):
        kc = kc_ref[0, :, pair * LANES:(pair + 1) * LANES]
        vct = vct_ref[pair * LANES:(pair + 1) * LANES, :]
        imp = [jnp.zeros((nc, t), F32), jnp.zeros((nc, t), F32)]
        for g in range(NSA_GROUP):
            rows = slice((pair * NSA_GROUP + g) * LANES, (pair * NSA_GROUP + g + 1) * LANES)
            qh = _head_rows(qt_ref[rows, :])
            oc = []
            for h in range(2):
                s = jnp.where(cmask, _dot(kc, qh[h]), NEG)
                p = jnp.exp2(s - jnp.max(s, axis=0, keepdims=True))
                l = jnp.sum(p, axis=0, keepdims=True)
                pn = p * jnp.where(any_valid, 1.0 / l, 0.0)
                imp[h] = imp[h] + pn
                oc.append(_dot(vct, pn.astype(BF16)))
            oct_ref[rows, :] = jnp.where(low, oc[0], oc[1]).astype(oct_ref.dtype)
        for h in range(2):
            hi, mid, lo = _split3(imp[h])
            score = _dot(ovt, hi) + _dot(ovt, mid) + _dot(ovt, lo)
            vals.append(jnp.where(valid, jnp.where(forced, FORCE_SCORE, score), NEG))

    def pick_one(_, vals):
        out = []
        for v in vals:
            mx = jnp.max(v, axis=0, keepdims=True)
            first = jnp.min(jnp.where(v == mx, blk_f, float(nsp)), axis=0, keepdims=True)
            out.append(jnp.where(blk_f == first, -jnp.inf, v))
        return tuple(out)

    vals = lax.fori_loop(0, SLC_TOPK, pick_one, tuple(vals))
    for kvh in range(NSA_KV_HEADS):
        picked = jnp.logical_and(valid, vals[kvh] == -jnp.inf)
        selt_ref[kvh] = jnp.where(picked, 1.0, 0.0).astype(selt_ref.dtype)


def _nsa_cmp_attn(proj_t, kvc, vct, ovt):
    S = proj_t.shape[1]
    t = NSA_TILE
    nsp, nc = ovt.shape
    kvd = NSA_KV_HEADS * HEAD_DIM
    return pl.pallas_call(
        functools.partial(_nsa_cmp_attn_kernel, t=t, nc=nc, nsp=nsp),
        grid=(S // t,),
        in_specs=[
            pl.BlockSpec((D_MODEL, t), lambda i: (0, i)),
            pl.BlockSpec((1, nc, kvd), lambda i: (0, 0, 0)),
            pl.BlockSpec((kvd, nc), lambda i: (0, 0)),
            pl.BlockSpec((nsp, nc), lambda i: (0, 0)),
        ],
        out_specs=[pl.BlockSpec((D_MODEL, t), lambda i: (0, i)),
                   pl.BlockSpec((NSA_KV_HEADS, nsp, t), lambda i: (0, 0, i))],
        out_shape=[jax.ShapeDtypeStruct((D_MODEL, S), BF16),
                   jax.ShapeDtypeStruct((NSA_KV_HEADS, nsp, S), BF16)],
        compiler_params=_cparams(("parallel",)),
        name="nsa_cmp_attn",
    )(proj_t, kvc, vct, ovt)


def _nsa_sel_kernel(qt_ref, ks_ref, vst_ref, selt_ref, kw0_ref, kw1_ref, kw2_ref, vwt0_ref, vwt1_ref, vwt2_ref,
                    gzt_ref, et_ref, oct_ref, ot_ref, s0_ref, s1_ref, bias_ref, m_ref, l_ref, acc_ref, *, t):
    i = pl.program_id(1)
    n_heads = 2 * NSA_GROUP
    low = lax.broadcasted_iota(jnp.int32, (LANES, t), 0) < HEAD_DIM
    qs = []
    for g in range(NSA_GROUP):
        qs.extend(_head_rows(qt_ref[g * LANES:(g + 1) * LANES, :]))
    bias_ref[...] = ((selt_ref[...].astype(F32) - 1.0) * -NEG).astype(BF16)
    m_ref[...] = jnp.full(m_ref.shape, NEG, F32)
    l_ref[...] = jnp.zeros(l_ref.shape, F32)
    acc_ref[...] = jnp.zeros(acc_ref.shape, F32)
    tok_blk = jnp.right_shift(lax.broadcasted_iota(jnp.int32, (t, LANES), 0), SLC_SHIFT)
    lane_col = lax.broadcasted_iota(jnp.int32, (t, LANES), 1)
    zero_rows = jnp.zeros((LANES - SEL_GROUP, t), BF16)
    tiles_per_group = SEL_GROUP * SLC_BLOCK // t
    s_bufs = (s0_ref, s1_ref)

    def scores(j, buf):
        k = ks_ref[pl.ds(pl.multiple_of(j * t, t), t), :]
        grp = j // tiles_per_group
        first_blk = (j % tiles_per_group) * (t // SLC_BLOCK)
        expand = jnp.where(lane_col == first_blk + tok_blk, 1.0, 0.0).astype(BF16)
        lhs = jnp.concatenate([k, expand], axis=1)
        for h in range(2):
            blk_bias = bias_ref[h, grp]
            for g in range(NSA_GROUP):
                rhs = jnp.concatenate([qs[2 * g + h], blk_bias, zero_rows], axis=0)
                s_bufs[buf][2 * g + h] = _dot(lhs, rhs)

    def consume(j, buf, diag):
        vt = vst_ref[:, pl.ds(pl.multiple_of(j * t, t), t)]
        for idx in range(n_heads):
            _online_update(s_bufs[buf], vt, m_ref, l_ref, acc_ref, idx, diag)

    _pipelined_tiles(i, scores, consume)

    n_win = WIN // t + 1
    k_win = jnp.concatenate([kw0_ref[...], kw1_ref[...], kw2_ref[...]], axis=0)
    vt_win = jnp.concatenate([vwt0_ref[...], vwt1_ref[...], vwt2_ref[...]], axis=1)
    kpos = (i - (n_win - 1)) * t + lax.broadcasted_iota(jnp.int32, (n_win * t, t), 0)
    qpos = i * t + lax.broadcasted_iota(jnp.int32, (n_win * t, t), 1)
    wmask = jnp.logical_and(jnp.logical_and(kpos <= qpos, kpos > qpos - WIN), kpos >= 0)
    ow = []
    for idx in range(n_heads):
        s = jnp.where(wmask, _dot(k_win, qs[idx]), NEG)
        p = jnp.exp2(s - jnp.max(s, axis=0, keepdims=True))
        l = jnp.sum(p, axis=0, keepdims=True)
        ow.append(_dot(vt_win, p.astype(BF16)) / l)

    gates = jax.nn.sigmoid(_dot(et_ref[0], gzt_ref[...]))
    gw = NSA_GROUP * LANES
    for g in range(NSA_GROUP):
        rows = slice(g * LANES, (g + 1) * LANES)
        o_cmp = oct_ref[rows, :].astype(F32)
        o_slc = jnp.where(low, acc_ref[2 * g] / l_ref[2 * g], acc_ref[2 * g + 1] / l_ref[2 * g + 1])
        o_win = jnp.where(low, ow[2 * g], ow[2 * g + 1])
        out = (gates[g * LANES:(g + 1) * LANES, :] * o_cmp
               + gates[gw + g * LANES:gw + (g + 1) * LANES, :] * o_slc
               + gates[2 * gw + g * LANES:2 * gw + (g + 1) * LANES, :] * o_win)
        ot_ref[rows, :] = out.astype(ot_ref.dtype)


def _nsa_sel(nat, proj_t, selt, oct, gate_expand_t):
    S = nat.shape[0]
    t = NSA_TILE
    n_grp = selt.shape[1]
    n_heads = 2 * NSA_GROUP
    gl = NSA_GROUP * LANES

    def kwin_spec(back):
        return pl.BlockSpec((t, LANES), lambda p, i: (jnp.maximum(i - back, 0), 6 + p))

    def vwin_spec(back):
        return pl.BlockSpec((LANES, t), lambda p, i: (10 + p, jnp.maximum(i - back, 0)))

    return pl.pallas_call(
        functools.partial(_nsa_sel_kernel, t=t),
        grid=(NSA_KV_HEADS // 2, S // t),
        in_specs=[
            pl.BlockSpec((gl, t), lambda p, i: (p, i)),
            pl.BlockSpec((S, LANES), lambda p, i: (0, 4 + p)),
            pl.BlockSpec((LANES, S), lambda p, i: (8 + p, 0)),
            pl.BlockSpec((2, n_grp, SEL_GROUP, t), lambda p, i: (p, 0, 0, i)),
            kwin_spec(2), kwin_spec(1), kwin_spec(0),
            vwin_spec(2), vwin_spec(1), vwin_spec(0),
            pl.BlockSpec((NSA_PROJ_TILE, t), lambda p, i: (6, i)),
            pl.BlockSpec((1, 3 * gl, NSA_PROJ_TILE), lambda p, i: (p, 0, 0)),
            pl.BlockSpec((gl, t), lambda p, i: (p, i)),
        ],
        out_specs=pl.BlockSpec((gl, t), lambda p, i: (p, i)),
        out_shape=jax.ShapeDtypeStruct((D_MODEL, S), BF16),
        scratch_shapes=([pltpu.VMEM((n_heads, t, t), F32)] * 2 + [pltpu.VMEM((2, n_grp, SEL_GROUP, t), BF16)]
                        + [pltpu.VMEM((n_heads, 1, t), F32)] * 2 + [pltpu.VMEM((n_heads, LANES, t), F32)]),
        compiler_params=_cparams(("parallel", "arbitrary")),
        name="nsa_sel",
    )(proj_t, nat, proj_t, selt, nat, nat, nat, proj_t, proj_t, proj_t, proj_t, gate_expand_t, oct)


def _rope_tables(positions):
    half = ROT_DIM // 2
    inv = ROPE_THETA ** (-jnp.arange(0, ROT_DIM, 2, dtype=F32) / ROT_DIM)
    ang = positions.astype(F32)[:, None] * inv[None, :]
    cos, sin = jnp.cos(ang), jnp.sin(ang)
    S = positions.shape[0]
    ones = jnp.ones((S, HEAD_DIM - ROT_DIM), F32)
    zeros_h = jnp.zeros((S, half), F32)
    zeros_r = jnp.zeros((S, HEAD_DIM - ROT_DIM), F32)
    rc = jnp.concatenate([cos, cos, ones], axis=1)
    rs1 = jnp.concatenate([-sin, zeros_h, zeros_r], axis=1)
    rs2 = jnp.concatenate([zeros_h, sin, zeros_r], axis=1)
    reps = LANES // HEAD_DIM
    return jnp.tile(rc, (1, reps)), jnp.tile(rs1, (1, reps)), jnp.tile(rs2, (1, reps))


def _pair_heads(w, axis):
    shape = w.shape
    w = w.reshape(shape[:axis] + (NSA_KV_HEADS // 2, 2, NSA_GROUP, HEAD_DIM) + shape[axis + 1:])
    w = jnp.swapaxes(w, axis + 1, axis + 2)
    return w.reshape(shape)


def _nsa_constants(S):
    nc = S // CMP_STRIDE
    n_cmp = (S - CMP_BLOCK) // CMP_STRIDE + 1
    ns = S // SLC_BLOCK
    nsp = -(-ns // LANES) * LANES
    ci = np.arange(nc)[None, :] * CMP_STRIDE
    st = np.arange(nsp)[:, None] * SLC_BLOCK
    ovt = (ci < st + SLC_BLOCK) & (ci + CMP_BLOCK > st) & (np.arange(nc)[None, :] < n_cmp) & (np.arange(nsp)[:, None] < ns)
    e = np.zeros((NSA_KV_HEADS // 2, 3 * NSA_GROUP * LANES, NSA_PROJ_TILE), np.float32)
    for p in range(NSA_KV_HEADS // 2):
        for br in range(3):
            for g in range(NSA_GROUP):
                for hf in range(2):
                    src = br * N_HEADS + (2 * p + hf) * NSA_GROUP + g
                    r0 = br * NSA_GROUP * LANES + g * LANES + hf * HEAD_DIM
                    e[p, r0:r0 + HEAD_DIM, src] = 1.0
    cmp_end = np.minimum(np.arange(nc) * CMP_STRIDE + CMP_BLOCK - 1, S - 1)
    return nc, jnp.asarray(ovt.astype(np.float32), BF16), jnp.asarray(e, BF16), cmp_end


def _nsa_layer(h, g, positions, w_in, pe_k, w1_k, w2_k, pe_v, w1_v, w2_v, w_o):
    S, D = h.shape
    qd = N_HEADS * HEAD_DIM
    kvd = NSA_KV_HEADS * HEAD_DIM
    nc, ovt, gate_expand_t, cmp_end = _nsa_constants(S)
    rc, rs1, rs2 = _rope_tables(positions)

    def kv_piece(n):
        return w_in[:, qd + n * kvd:qd + (n + 1) * kvd]

    w_nat = jnp.concatenate([kv_piece(0), kv_piece(1), kv_piece(2), kv_piece(4)], axis=1).astype(BF16)
    wg = jnp.pad(w_in[:, qd + 6 * kvd:], ((0, 0), (0, NSA_PROJ_TILE - 3 * N_HEADS)))
    w_t = jnp.concatenate([_pair_heads(w_in[:, :qd], 1) * QK_SCALE, kv_piece(3), kv_piece(5), wg], axis=1).T.astype(BF16)
    nat = _proj(h, g, w_nat, "nsa_proj", key_major=False, tn=NSA_PROJ_TILE,
                rope=(rc, rs1, rs2), rope_tiles=NSA_NAT_ROPE_TILES)
    proj_t = _proj(h, g, w_t, "nsa_proj_t", key_major=True, tn=NSA_PROJ_TILE,
                   rope=(rc.T, rs1.T, rs2.T), rope_tiles=NSA_T_ROPE_TILES)

    raw = nat[:, :2 * kvd].reshape(nc, CMP_STRIDE, 2, NSA_KV_HEADS, HEAD_DIM)
    raw = raw.transpose(2, 3, 0, 1, 4).reshape(2, NSA_KV_HEADS, nc, CMP_STRIDE * HEAD_DIM)
    w1 = jnp.stack([w1_k, w1_v]).astype(BF16)
    pe = jnp.stack([pe_k.reshape(1, -1), pe_v.reshape(1, -1)])
    pe = jnp.pad(pe, ((0, 0), (0, 7), (0, 0))).astype(BF16)
    w2 = jnp.stack([w2_k, w2_v])
    eye = jnp.eye(NSA_KV_HEADS, dtype=F32)
    w2p = (w2[:, None, :, None, :] * eye[None, :, None, :, None]).reshape(2, NSA_KV_HEADS, CMP_HIDDEN, kvd).astype(BF16)
    kvc = _nsa_cmp(raw, w1, pe, w2p, rc[cmp_end], rs1[cmp_end], rs2[cmp_end])

    oct, selt = _nsa_cmp_attn(proj_t, kvc, kvc[1].T, ovt)
    selt = selt.reshape(NSA_KV_HEADS, -1, SEL_GROUP, S)
    ot = _nsa_sel(nat, proj_t, selt, oct, gate_expand_t)
    return _matmul_res(ot, _pair_heads(w_o, 0).astype(BF16), h, "nsa_out")


def _trunk(x2, positions, p):
    h = _fox_layer(x2, p["l0_norm_mix"], p["l0_fox_w_qkv"], p["l0_fox_w_f"], p["l0_fox_b_f"], p["l0_fox_w_o"])
    h = _mlp(h, p["l0_norm_mlp"], p["l0_mlp_w1"].astype(BF16), p["l0_mlp_w2"].astype(BF16))
    h = _pool_layer(h, p["l1_norm_mix"], p["l1_pool_w"], p["l1_pool_scale"])
    h = _mlp(h, p["l1_norm_mlp"], p["l1_mlp_w1"].astype(BF16), p["l1_mlp_w2"].astype(BF16))
    h = _conv_layer(h, p["l2_norm_mix"], p["l2_conv_w_in"], p["l2_conv_w"], p["l2_conv_w_out"])
    h = _mlp(h, p["l2_norm_mlp"], p["l2_mlp_w1"].astype(BF16), p["l2_mlp_w2"].astype(BF16))
    h = _nsa_layer(h, p["l3_norm_mix"], positions, p["l3_nsa_w_in"], p["l3_nsa_cmp_pe_k"], p["l3_nsa_cmp_w1_k"],
                   p["l3_nsa_cmp_w2_k"], p["l3_nsa_cmp_pe_v"], p["l3_nsa_cmp_w1_v"], p["l3_nsa_cmp_w2_v"],
                   p["l3_nsa_w_o"])
    return _mlp(h, p["l3_norm_mlp"], p["l3_mlp_w1"].astype(BF16), p["l3_mlp_w2"].astype(BF16), p["final_norm"])


def kernel(x, positions, l0_norm_mix, l0_fox_w_qkv, l0_fox_w_f, l0_fox_b_f, l0_fox_w_o, l0_norm_mlp, l0_mlp_w1, l0_mlp_w2, l1_norm_mix, l1_pool_w, l1_pool_scale, l1_norm_mlp, l1_mlp_w1, l1_mlp_w2, l2_norm_mix, l2_conv_w_in, l2_conv_w, l2_conv_w_out, l2_norm_mlp, l2_mlp_w1, l2_mlp_w2, l3_norm_mix, l3_nsa_w_in, l3_nsa_cmp_pe_k, l3_nsa_cmp_w1_k, l3_nsa_cmp_w2_k, l3_nsa_cmp_pe_v, l3_nsa_cmp_w1_v, l3_nsa_cmp_w2_v, l3_nsa_w_o, l3_norm_mlp, l3_mlp_w1, l3_mlp_w2, final_norm):
    params = dict(locals())
    B, S, D = x.shape
    outs = [_trunk(x[b], positions, params) for b in range(B)]
    return jnp.stack(outs, axis=0)
```

```python
import functools

import numpy as np
import jax
import jax.numpy as jnp
from jax import lax
from jax.experimental import pallas as pl
from jax.experimental.pallas import tpu as pltpu

F32 = jnp.float32
BF16 = jnp.bfloat16

D_MODEL = 1024
HEAD_DIM = 64
N_HEADS = D_MODEL // HEAD_DIM
D_FF = 4 * D_MODEL
ROPE_THETA = 500000.0
ROT_DIM = HEAD_DIM // 4
RMS_EPS = 1e-6
POOL_WINDOWS = (2, 4, 8, 16)
POOL_GROUP = D_MODEL // len(POOL_WINDOWS)
POOL_HALO = 16
CONV_WIDTH = 3
CONV_HALO = 8
NSA_KV_HEADS = 4
NSA_GROUP = N_HEADS // NSA_KV_HEADS
CMP_BLOCK = 32
CMP_STRIDE = 16
CMP_HIDDEN = 256
SLC_BLOCK = 64
SLC_SHIFT = 6
SLC_TOPK = 16
WIN = 512
FORCE_SCORE = 1e9
NEG = -1e30
LOG2E = 1.4426950408889634
QK_SCALE = HEAD_DIM ** -0.5 * LOG2E

LANES = 128
V7X_VMEM_LIMIT = 56 * 1024 * 1024
FOX_TILE = 512
FOX_BIAS_TERMS = 3
NSA_TILE = WIN // 2
SEL_GROUP = 16
SUM_ROWS = 16
NSA_PROJ_TILE = 2 * LANES
NSA_NAT_WIDTH = 8 * LANES
NSA_NAT_ROPE_TILES = (2, 3)
NSA_T_ROWS = 14 * LANES
NSA_T_ROPE_TILES = (0, 1, 2, 3)


def _cparams(semantics):
    return pltpu.CompilerParams(dimension_semantics=semantics, vmem_limit_bytes=V7X_VMEM_LIMIT)


def _rms(x, g):
    return x * lax.rsqrt(jnp.mean(x * x, axis=-1, keepdims=True) + RMS_EPS) * g


def _dot(a, b):
    return jnp.dot(a, b, preferred_element_type=F32)


def _dot_nt(a, b):
    return lax.dot_general(a, b, (((1,), (1,)), ((), ())), preferred_element_type=F32)


def _dot_tn(a, b):
    return lax.dot_general(a, b, (((0,), (0,)), ((), ())), preferred_element_type=F32)


def _split3(x):
    hi = x.astype(BF16)
    r = x - hi.astype(F32)
    mid = r.astype(BF16)
    lo = (r - mid.astype(F32)).astype(BF16)
    return hi, mid, lo


def _rope(a, rc, rs1, rs2, axis):
    half = ROT_DIM // 2
    return a * rc + pltpu.roll(a, LANES - half, axis) * rs1 + pltpu.roll(a, half, axis) * rs2


def _head_rows(qt_blk):
    low = lax.broadcasted_iota(jnp.int32, qt_blk.shape, 0) < HEAD_DIM
    qf = qt_blk.astype(F32)
    return jnp.where(low, qf, 0.0).astype(BF16), jnp.where(low, 0.0, qf).astype(BF16)


def _with_ones_rows(vt):
    return jnp.concatenate([vt, jnp.ones((SUM_ROWS, vt.shape[1]), vt.dtype)], axis=0)


def _online_update(s_ref, vt_ones, m_ref, acc_ref, idx, causal):
    tk, t = s_ref.shape[1:]
    s = s_ref[idx]
    if causal:
        s = jnp.where(lax.broadcasted_iota(jnp.int32, (tk, t), 0) <= lax.broadcasted_iota(jnp.int32, (tk, t), 1),
                      s, NEG)
    m_old = m_ref[idx]
    m_new = jnp.maximum(m_old, jnp.max(s, axis=0, keepdims=True))
    p = jnp.exp2(s - m_new).astype(BF16)
    m_ref[idx] = m_new
    acc_ref[idx] = jnp.exp2(m_old - m_new) * acc_ref[idx] + _dot(vt_ones, p)


def _online_result(acc_ref, idx):
    acc = acc_ref[idx]
    return acc[:LANES, :] / acc[LANES:LANES + 1, :]


def _pipelined_tiles(i, scores, consume, pairs_per_step=1):
    scores(0, 0)

    def pair(j):
        scores(j + 1, 1)
        consume(j, 0, False)
        scores(j + 2, 0)
        consume(j + 1, 1, False)

    def body(jj, carry):
        for u in range(pairs_per_step):
            pair(2 * (pairs_per_step * jj + u))
        return carry

    n_pairs = i // 2
    n_steps = n_pairs // pairs_per_step
    lax.fori_loop(0, n_steps, body, 0)
    if pairs_per_step > 1:
        def tail(jj, carry):
            pair(2 * jj)
            return carry

        lax.fori_loop(n_steps * pairs_per_step, n_pairs, tail, 0)

    @pl.when(i % 2 == 0)
    def _():
        consume(i, 0, True)

    @pl.when(i % 2 == 1)
    def _():
        scores(i, 1)
        consume(i - 1, 0, False)
        consume(i, 1, True)


def _mlp_kernel(x_ref, g_ref, w1_ref, w2_ref, *rest, nf, final):
    if final:
        fg_ref, o_ref, xn_ref = rest
    else:
        o_ref, xn_ref = rest
    f = pl.program_id(1)

    @pl.when(f == 0)
    def _():
        x = x_ref[...]
        xn_ref[...] = _rms(x, g_ref[...]).astype(BF16)
        o_ref[...] = x

    a = _dot(xn_ref[...], w1_ref[...])
    a = jnp.square(jnp.maximum(a, 0.0)).astype(BF16)
    o_ref[...] += _dot(a, w2_ref[...])

    if final:
        @pl.when(f == nf - 1)
        def _():
            o_ref[...] = _rms(o_ref[...], fg_ref[...])


def _mlp(h, g, w1, w2, final_g=None):
    S, D = h.shape
    F = w1.shape[1]
    tm = min(1024, S)
    tf = 1024
    nf = F // tf
    final = final_g is not None
    in_specs = [
        pl.BlockSpec((tm, D), lambda i, f: (i, 0)),
        pl.BlockSpec((1, D), lambda i, f: (0, 0)),
        pl.BlockSpec((D, tf), lambda i, f: (0, f)),
        pl.BlockSpec((tf, D), lambda i, f: (f, 0)),
    ]
    args = [h, g.reshape(1, D), w1, w2]
    if final:
        in_specs.append(pl.BlockSpec((1, D), lambda i, f: (0, 0)))
        args.append(final_g.reshape(1, D))
    return pl.pallas_call(
        functools.partial(_mlp_kernel, nf=nf, final=final),
        grid=(S // tm, nf),
        in_specs=in_specs,
        out_specs=pl.BlockSpec((tm, D), lambda i, f: (i, 0)),
        out_shape=jax.ShapeDtypeStruct((S, D), F32),
        scratch_shapes=[pltpu.VMEM((tm, D), BF16)],
        compiler_params=_cparams(("parallel", "arbitrary")),
        name="mlp",
    )(*args)


def _proj_kernel(x_ref, g_ref, w_ref, *rest, key_major, rope_tiles):
    if rope_tiles:
        rc_ref, rs1_ref, rs2_ref, o_ref, xn_ref = rest
    else:
        o_ref, xn_ref = rest
    j = pl.program_id(1)

    @pl.when(j == 0)
    def _():
        xn_ref[...] = _rms(x_ref[...], g_ref[...]).astype(BF16)

    a = _dot_nt(w_ref[...], xn_ref[...]) if key_major else _dot(xn_ref[...], w_ref[...])
    if not rope_tiles:
        o_ref[...] = a.astype(o_ref.dtype)
        return
    axis = 0 if key_major else 1
    is_rope = functools.reduce(jnp.logical_or, [j == t for t in rope_tiles])

    @pl.when(is_rope)
    def _():
        rc, rs1, rs2 = rc_ref[...], rs1_ref[...], rs2_ref[...]
        blocks = [lax.slice_in_dim(a, b * LANES, (b + 1) * LANES, axis=axis) for b in range(a.shape[axis] // LANES)]
        o_ref[...] = jnp.concatenate([_rope(blk, rc, rs1, rs2, axis) for blk in blocks], axis=axis).astype(o_ref.dtype)

    @pl.when(jnp.logical_not(is_rope))
    def _():
        o_ref[...] = a.astype(o_ref.dtype)


def _proj(h, g, w, name, *, key_major, tn, rope=None, rope_tiles=()):
    S, D = h.shape
    tm = min(1024, S)
    if key_major:
        N = w.shape[0]
        w_spec = pl.BlockSpec((tn, D), lambda i, j: (j, 0))
        rope_spec = pl.BlockSpec((LANES, tm), lambda i, j: (0, i))
        out_spec = pl.BlockSpec((tn, tm), lambda i, j: (j, i))
        out_shape = jax.ShapeDtypeStruct((N, S), BF16)
    else:
        N = w.shape[1]
        w_spec = pl.BlockSpec((D, tn), lambda i, j: (0, j))
        rope_spec = pl.BlockSpec((tm, LANES), lambda i, j: (i, 0))
        out_spec = pl.BlockSpec((tm, tn), lambda i, j: (i, j))
        out_shape = jax.ShapeDtypeStruct((S, N), BF16)
    in_specs = [pl.BlockSpec((tm, D), lambda i, j: (i, 0)), pl.BlockSpec((1, D), lambda i, j: (0, 0)), w_spec]
    args = [h, g.reshape(1, D), w]
    if rope_tiles:
        in_specs += [rope_spec] * 3
        args += list(rope)
    return pl.pallas_call(
        functools.partial(_proj_kernel, key_major=key_major, rope_tiles=tuple(rope_tiles)),
        grid=(S // tm, N // tn),
        in_specs=in_specs,
        out_specs=out_spec,
        out_shape=out_shape,
        scratch_shapes=[pltpu.VMEM((tm, D), BF16)],
        compiler_params=_cparams(("parallel", "arbitrary")),
        name=name,
    )(*args)


def _matmul_res_kernel(at_ref, w_ref, r_ref, o_ref):
    o_ref[...] = r_ref[...] + _dot_tn(at_ref[...], w_ref[...])


def _matmul_res(at, w, res, name):
    K, S = at.shape
    N = w.shape[1]
    tm = min(1024, S)
    return pl.pallas_call(
        _matmul_res_kernel,
        grid=(S // tm,),
        in_specs=[
            pl.BlockSpec((K, tm), lambda i: (0, i)),
            pl.BlockSpec((K, N), lambda i: (0, 0)),
            pl.BlockSpec((tm, N), lambda i: (i, 0)),
        ],
        out_specs=pl.BlockSpec((tm, N), lambda i: (i, 0)),
        out_shape=jax.ShapeDtypeStruct((S, N), F32),
        compiler_params=_cparams(("parallel",)),
        name=name,
    )(at, w, res)


def _fox_gate_kernel(x_ref, g_ref, wf_ref, bf_ref, place_ref, c_ref, carry_ref, *, tm):
    @pl.when(pl.program_id(0) == 0)
    def _():
        carry_ref[...] = jnp.zeros_like(carry_ref)

    xn = _rms(x_ref[...], g_ref[...]).astype(BF16)
    z = _dot(xn, wf_ref[...]) + bf_ref[...]
    logf = jnp.minimum(z, 0.0) - jnp.log(1.0 + jnp.exp(-jnp.abs(z)))
    row = lax.broadcasted_iota(jnp.int32, (tm, tm), 0)
    col = lax.broadcasted_iota(jnp.int32, (tm, tm), 1)
    tri = jnp.where(row >= col, 1.0, 0.0).astype(BF16)
    hi, mid, lo = _split3(logf)
    c = _dot(tri, hi) + _dot(tri, mid) + _dot(tri, lo) + carry_ref[...]
    carry_ref[...] = c[tm - 1:tm, :]
    terms = jnp.concatenate(_split3(c * -LOG2E), axis=1)
    c_ref[...] = _dot(terms, place_ref[...]).astype(c_ref.dtype)


def _fox_gate(h, g, wf, bf, place):
    S, D = h.shape
    tm = min(512, S)
    return pl.pallas_call(
        functools.partial(_fox_gate_kernel, tm=tm),
        grid=(S // tm,),
        in_specs=[
            pl.BlockSpec((tm, D), lambda i: (i, 0)),
            pl.BlockSpec((1, D), lambda i: (0, 0)),
            pl.BlockSpec((D, LANES), lambda i: (0, 0)),
            pl.BlockSpec((1, LANES), lambda i: (0, 0)),
            pl.BlockSpec((FOX_BIAS_TERMS * LANES, D_MODEL), lambda i: (0, 0)),
        ],
        out_specs=pl.BlockSpec((tm, D_MODEL), lambda i: (i, 0)),
        out_shape=jax.ShapeDtypeStruct((S, D_MODEL), BF16),
        scratch_shapes=[pltpu.VMEM((1, LANES), F32)],
        compiler_params=_cparams(("arbitrary",)),
        name="fox_gate",
    )(h, g.reshape(1, D), wf, bf, place)


def _fox_attn_kernel(qt_ref, k_ref, c_ref, vt_ref, ot_ref, s0_ref, s1_ref, m_ref, acc_ref, *, t):
    i = pl.program_id(1)
    row = lax.broadcasted_iota(jnp.int32, (LANES, t), 0)
    q_heads = _head_rows(qt_ref[...])
    qa = []
    for h in range(2):
        ones_rows = jnp.logical_and(row >= FOX_BIAS_TERMS * h, row < FOX_BIAS_TERMS * (h + 1))
        qa.append(jnp.concatenate([q_heads[h], jnp.where(ones_rows, 1.0, 0.0).astype(BF16)], axis=0))
    m_ref[...] = jnp.full(m_ref.shape, NEG, F32)
    acc_ref[...] = jnp.zeros(acc_ref.shape, F32)
    s_bufs = (s0_ref, s1_ref)

    def scores(j, buf):
        rows = pl.ds(pl.multiple_of(j * t, t), t)
        lhs = jnp.concatenate([k_ref[rows, :], c_ref[rows, :]], axis=1)
        for h in range(2):
            s_bufs[buf][h] = _dot(lhs, qa[h])

    def consume(j, buf, diag):
        vt = _with_ones_rows(vt_ref[:, pl.ds(pl.multiple_of(j * t, t), t)])
        for h in range(2):
            _online_update(s_bufs[buf], vt, m_ref, acc_ref, h, diag)

    _pipelined_tiles(i, scores, consume, pairs_per_step=2)
    ot = jnp.where(row < HEAD_DIM, _online_result(acc_ref, 0), _online_result(acc_ref, 1))
    ot_ref[...] = ot.astype(ot_ref.dtype)


def _fox_attn(qvt, k, c_terms):
    S = k.shape[0]
    t = min(FOX_TILE, S)
    n_pairs = N_HEADS // 2
    return pl.pallas_call(
        functools.partial(_fox_attn_kernel, t=t),
        grid=(n_pairs, S // t),
        in_specs=[
            pl.BlockSpec((LANES, t), lambda hp, i: (hp, i)),
            pl.BlockSpec((S, LANES), lambda hp, i: (0, hp)),
            pl.BlockSpec((S, LANES), lambda hp, i: (0, hp)),
            pl.BlockSpec((LANES, S), lambda hp, i: (n_pairs + hp, 0)),
        ],
        out_specs=pl.BlockSpec((LANES, t), lambda hp, i: (hp, i)),
        out_shape=jax.ShapeDtypeStruct((D_MODEL, S), BF16),
        scratch_shapes=([pltpu.VMEM((2, t, t), F32)] * 2 + [pltpu.VMEM((2, 1, t), F32)]
                        + [pltpu.VMEM((2, LANES + SUM_ROWS, t), F32)]),
        compiler_params=_cparams(("parallel", "arbitrary")),
        name="fox_attn",
    )(qvt, k, c_terms, qvt)


def _fox_bias_placement():
    place = np.zeros((FOX_BIAS_TERMS * LANES, D_MODEL), np.float32)
    for head in range(N_HEADS):
        for n in range(FOX_BIAS_TERMS):
            place[n * LANES + head, (head // 2) * LANES + FOX_BIAS_TERMS * (head % 2) + n] = 1.0
    return jnp.asarray(place, BF16)


def _fox_layer(h, g, w_qkv, w_f, b_f, w_o):
    wf = jnp.pad(w_f, ((0, 0), (0, LANES - N_HEADS))).astype(BF16)
    bf = jnp.pad(b_f, (0, LANES - N_HEADS)).reshape(1, LANES)
    c_terms = _fox_gate(h, g, wf, bf, _fox_bias_placement())
    w_qv = jnp.concatenate([w_qkv[:, :D_MODEL] * QK_SCALE, w_qkv[:, 2 * D_MODEL:]], axis=1).T.astype(BF16)
    qvt = _proj(h, g, w_qv, "fox_qv", key_major=True, tn=512)
    k = _proj(h, g, w_qkv[:, D_MODEL:2 * D_MODEL].astype(BF16), "fox_k", key_major=False, tn=512)
    ot = _fox_attn(qvt, k, c_terms)
    return _matmul_res(ot, w_o.astype(BF16), h, "fox_out")


def _pool_kernel(x_ref, halo_ref, g_ref, w_ref, sc_ref, o_ref, *, tm):
    i = pl.program_id(0)
    x = x_ref[...]
    g = g_ref[...]
    xn = _rms(x, g)
    hn = jnp.where(i > 0, _rms(halo_ref[...], g), 0.0)
    xe = jnp.concatenate([hn, xn], axis=0)
    tpos = i * tm + lax.broadcasted_iota(jnp.int32, (tm, 1), 0)
    for gi, w in enumerate(POOL_WINDOWS):
        sl = slice(gi * POOL_GROUP, (gi + 1) * POOL_GROUP)
        s = xe[:, sl]
        k = 1
        while k < w:
            s = s + pltpu.roll(s, k, 0)
            k *= 2
        cnt = jnp.minimum(tpos + 1, w).astype(F32)
        d = (s[POOL_HALO:, :] / cnt - xn[:, sl]).astype(BF16)
        o_ref[:, sl] = x[:, sl] + _dot(d, w_ref[gi]) * sc_ref[:, sl]


def _pool_layer(h, g, w_pool, pool_scale):
    S, D = h.shape
    tm = min(1024, S)
    return pl.pallas_call(
        functools.partial(_pool_kernel, tm=tm),
        grid=(S // tm,),
        in_specs=[
            pl.BlockSpec((tm, D), lambda i: (i, 0)),
            pl.BlockSpec((POOL_HALO, D), lambda i: (jnp.maximum(i * (tm // POOL_HALO) - 1, 0), 0)),
            pl.BlockSpec((1, D), lambda i: (0, 0)),
            pl.BlockSpec((len(POOL_WINDOWS), POOL_GROUP, POOL_GROUP), lambda i: (0, 0, 0)),
            pl.BlockSpec((1, D), lambda i: (0, 0)),
        ],
        out_specs=pl.BlockSpec((tm, D), lambda i: (i, 0)),
        out_shape=jax.ShapeDtypeStruct((S, D), F32),
        compiler_params=_cparams(("parallel",)),
        name="pool",
    )(h, h, g.reshape(1, D), w_pool.astype(BF16), pool_scale.reshape(1, D))


def _conv_in_kernel(x_ref, g_ref, wb_ref, wc_ref, wu_ref, b_ref, z_ref, xn_ref):
    @pl.when(pl.program_id(1) == 0)
    def _():
        xn_ref[...] = _rms(x_ref[...], g_ref[...]).astype(BF16)

    xn = xn_ref[...]
    b_ref[...] = _dot(xn, wb_ref[...])
    z_ref[...] = _dot(xn, wc_ref[...]) * _dot(xn, wu_ref[...])


def _conv_out_kernel(b_ref, z_ref, zh_ref, cw_ref, w_ref, r_ref, o_ref, *, tm):
    i = pl.program_id(0)
    z = z_ref[...]
    zh = jnp.where(i > 0, zh_ref[...], 0.0)
    row = lax.broadcasted_iota(jnp.int32, (tm, 1), 0)
    prev1 = zh[CONV_HALO - 1:CONV_HALO, :]
    prev2 = zh[CONV_HALO - 2:CONV_HALO - 1, :]
    z1 = jnp.where(row == 0, prev1, pltpu.roll(z, 1, 0))
    z2 = jnp.where(row == 0, prev2, jnp.where(row == 1, prev1, pltpu.roll(z, 2, 0)))
    cw = cw_ref[...]
    conv = cw[0:1, :] * z2 + cw[1:2, :] * z1 + cw[2:3, :] * z
    y = (b_ref[...] * conv).astype(BF16)
    o_ref[...] = r_ref[...] + _dot(y, w_ref[...])


def _conv_layer(h, g, w_in, conv_w, w_out):
    S, D = h.shape
    tm = min(1024, S)
    tn = 512
    nj = D // tn
    w_in = w_in.astype(BF16)
    b, z = pl.pallas_call(
        _conv_in_kernel,
        grid=(S // tm, nj),
        in_specs=[
            pl.BlockSpec((tm, D), lambda i, j: (i, 0)),
            pl.BlockSpec((1, D), lambda i, j: (0, 0)),
            pl.BlockSpec((D, tn), lambda i, j: (0, j)),
            pl.BlockSpec((D, tn), lambda i, j: (0, nj + j)),
            pl.BlockSpec((D, tn), lambda i, j: (0, 2 * nj + j)),
        ],
        out_specs=[pl.BlockSpec((tm, tn), lambda i, j: (i, j)),
                   pl.BlockSpec((tm, tn), lambda i, j: (i, j))],
        out_shape=[jax.ShapeDtypeStruct((S, D), F32), jax.ShapeDtypeStruct((S, D), F32)],
        scratch_shapes=[pltpu.VMEM((tm, D), BF16)],
        compiler_params=_cparams(("parallel", "arbitrary")),
        name="conv_in",
    )(h, g.reshape(1, D), w_in, w_in, w_in)
    cw = jnp.pad(conv_w, ((0, 8 - CONV_WIDTH), (0, 0)))
    return pl.pallas_call(
        functools.partial(_conv_out_kernel, tm=tm),
        grid=(S // tm,),
        in_specs=[
            pl.BlockSpec((tm, D), lambda i: (i, 0)),
            pl.BlockSpec((tm, D), lambda i: (i, 0)),
            pl.BlockSpec((CONV_HALO, D), lambda i: (jnp.maximum(i * (tm // CONV_HALO) - 1, 0), 0)),
            pl.BlockSpec((8, D), lambda i: (0, 0)),
            pl.BlockSpec((D, D), lambda i: (0, 0)),
            pl.BlockSpec((tm, D), lambda i: (i, 0)),
        ],
        out_specs=pl.BlockSpec((tm, D), lambda i: (i, 0)),
        out_shape=jax.ShapeDtypeStruct((S, D), F32),
        compiler_params=_cparams(("parallel",)),
        name="conv_out",
    )(b, z, z, cw, w_out.astype(BF16), h)


def _nsa_cmp_kernel(r_ref, w1_ref, pe_ref, w2_ref, rc_ref, rs1_ref, rs2_ref, o_ref, acc_ref, *, nc):
    kv = pl.program_id(0)
    hd = pl.program_id(1)
    half = CMP_STRIDE * HEAD_DIM
    r = r_ref[0, 0]
    w1 = w1_ref[0]
    first = _dot(r, w1[:half, :])
    second = _dot(r, w1[half:, :])
    pe_term = _dot(pe_ref[0], w1)[0:1, :]
    pre = first + pltpu.roll(second, nc - 1, 0) + pe_term
    ge = 0.5 * pre * (1.0 + jnp.tanh(0.7978845608028654 * (pre + 0.044715 * pre * pre * pre)))
    y = _dot(ge.astype(BF16), w2_ref[0, 0])

    @pl.when(hd == 0)
    def _():
        acc_ref[...] = y

    @pl.when(hd > 0)
    def _():
        acc_ref[...] += y

    @pl.when(hd == NSA_KV_HEADS - 1)
    def _():
        acc = acc_ref[...]

        @pl.when(kv == 0)
        def _():
            rc, rs1, rs2 = rc_ref[...], rs1_ref[...], rs2_ref[...]
            o_ref[0] = jnp.concatenate(
                [_rope(acc[:, :LANES], rc, rs1, rs2, 1), _rope(acc[:, LANES:], rc, rs1, rs2, 1)],
                axis=1).astype(o_ref.dtype)

        @pl.when(kv == 1)
        def _():
            o_ref[0] = acc.astype(o_ref.dtype)


def _nsa_cmp(r, w1, pe, w2p, rc, rs1, rs2):
    nc = r.shape[2]
    kvd = NSA_KV_HEADS * HEAD_DIM
    return pl.pallas_call(
        functools.partial(_nsa_cmp_kernel, nc=nc),
        grid=(2, NSA_KV_HEADS),
        in_specs=[
            pl.BlockSpec((1, 1, nc, CMP_STRIDE * HEAD_DIM), lambda a, b: (a, b, 0, 0)),
            pl.BlockSpec((1, CMP_BLOCK * HEAD_DIM, CMP_HIDDEN), lambda a, b: (a, 0, 0)),
            pl.BlockSpec((1, 8, CMP_BLOCK * HEAD_DIM), lambda a, b: (a, 0, 0)),
            pl.BlockSpec((1, 1, CMP_HIDDEN, kvd), lambda a, b: (a, b, 0, 0)),
            pl.BlockSpec((nc, LANES), lambda a, b: (0, 0)),
            pl.BlockSpec((nc, LANES), lambda a, b: (0, 0)),
            pl.BlockSpec((nc, LANES), lambda a, b: (0, 0)),
        ],
        out_specs=pl.BlockSpec((1, nc, kvd), lambda a, b: (a, 0, 0)),
        out_shape=jax.ShapeDtypeStruct((2, nc, kvd), BF16),
        scratch_shapes=[pltpu.VMEM((nc, kvd), F32)],
        compiler_params=_cparams(("parallel", "arbitrary")),
        name="nsa_cmp",
    )(r, w1, pe, w2p, rc, rs1, rs2)


def _nsa_cmp_attn_kernel(qt_ref, kc_ref, vct_ref, ovt_ref, oct_ref, selt_ref, *, t, nc, nsp):
    i = pl.program_id(0)
    qpos = i * t + lax.broadcasted_iota(jnp.int32, (1, t), 1)
    cmp_end = CMP_STRIDE * lax.broadcasted_iota(jnp.int32, (nc, t), 0) + (CMP_BLOCK - 1)
    cmask = cmp_end <= qpos
    any_valid = qpos >= CMP_BLOCK - 1
    low = lax.broadcasted_iota(jnp.int32, (LANES, t), 0) < HEAD_DIM
    ovt = ovt_ref[...]
    blk = lax.broadcasted_iota(jnp.int32, (nsp, t), 0)
    blk_f = blk.astype(F32)
    cur = jnp.right_shift(qpos, SLC_SHIFT)
    forced = jnp.logical_or(blk == 0, jnp.logical_or(blk == cur, blk == cur - 1))
    valid = blk * SLC_BLOCK <= qpos
    vals = []
    for pair in range(NSA_KV_HEADS // 2):
        kc = kc_ref[0, :, pair * LANES:(pair + 1) * LANES]
        vct = vct_ref[pair * LANES:(pair + 1) * LANES, :]
        imp = [jnp.zeros((nc, t), F32), jnp.zeros((nc, t), F32)]
        for g in range(NSA_GROUP):
            rows = slice((pair * NSA_GROUP + g) * LANES, (pair * NSA_GROUP + g + 1) * LANES)
            qh = _head_rows(qt_ref[rows, :])
            oc = []
            for h in range(2):
                s = jnp.where(cmask, _dot(kc, qh[h]), NEG)
                p = jnp.exp2(s - jnp.max(s, axis=0, keepdims=True))
                l = jnp.sum(p, axis=0, keepdims=True)
                pn = p * jnp.where(any_valid, 1.0 / l, 0.0)
                imp[h] = imp[h] + pn
                oc.append(_dot(vct, pn.astype(BF16)))
            oct_ref[rows, :] = jnp.where(low, oc[0], oc[1]).astype(oct_ref.dtype)
        for h in range(2):
            hi, mid, lo = _split3(imp[h])
            score = _dot(ovt, hi) + _dot(ovt, mid) + _dot(ovt, lo)
            vals.append(jnp.where(valid, jnp.where(forced, FORCE_SCORE, score), NEG))

    def pick_one(_, vals):
        out = []
        for v in vals:
            mx = jnp.max(v, axis=0, keepdims=True)
            first = jnp.min(jnp.where(v == mx, blk_f, float(nsp)), axis=0, keepdims=True)
            out.append(jnp.where(blk_f == first, -jnp.inf, v))
        return tuple(out)

    vals = lax.fori_loop(0, SLC_TOPK, pick_one, tuple(vals))
    for kvh in range(NSA_KV_HEADS):
        picked = jnp.logical_and(valid, vals[kvh] == -jnp.inf)
        selt_ref[kvh] = jnp.where(picked, 1.0, 0.0).astype(selt_ref.dtype)


def _nsa_cmp_attn(proj_t, kvc, vct, ovt):
    S = proj_t.shape[1]
    t = NSA_TILE
    nsp, nc = ovt.shape
    kvd = NSA_KV_HEADS * HEAD_DIM
    return pl.pallas_call(
        functools.partial(_nsa_cmp_attn_kernel, t=t, nc=nc, nsp=nsp),
        grid=(S // t,),
        in_specs=[
            pl.BlockSpec((D_MODEL, t), lambda i: (0, i)),
            pl.BlockSpec((1, nc, kvd), lambda i: (0, 0, 0)),
            pl.BlockSpec((kvd, nc), lambda i: (0, 0)),
            pl.BlockSpec((nsp, nc), lambda i: (0, 0)),
        ],
        out_specs=[pl.BlockSpec((D_MODEL, t), lambda i: (0, i)),
                   pl.BlockSpec((NSA_KV_HEADS, nsp, t), lambda i: (0, 0, i))],
        out_shape=[jax.ShapeDtypeStruct((D_MODEL, S), BF16),
                   jax.ShapeDtypeStruct((NSA_KV_HEADS, nsp, S), BF16)],
        compiler_params=_cparams(("parallel",)),
        name="nsa_cmp_attn",
    )(proj_t, kvc, vct, ovt)


def _nsa_sel_kernel(qt_ref, ks_ref, vst_ref, selt_ref, kw0_ref, kw1_ref, kw2_ref, vwt0_ref, vwt1_ref, vwt2_ref,
                    gzt_ref, et_ref, oct_ref, ot_ref, s0_ref, s1_ref, bias_ref, m_ref, acc_ref, *, t):
    i = pl.program_id(1)
    n_heads = 2 * NSA_GROUP
    low = lax.broadcasted_iota(jnp.int32, (LANES, t), 0) < HEAD_DIM
    qs = []
    for g in range(NSA_GROUP):
        qs.extend(_head_rows(qt_ref[g * LANES:(g + 1) * LANES, :]))
    bias_ref[...] = ((selt_ref[...].astype(F32) - 1.0) * -NEG).astype(BF16)
    m_ref[...] = jnp.full(m_ref.shape, NEG, F32)
    acc_ref[...] = jnp.zeros(acc_ref.shape, F32)
    tok_blk = jnp.right_shift(lax.broadcasted_iota(jnp.int32, (t, LANES), 0), SLC_SHIFT)
    lane_col = lax.broadcasted_iota(jnp.int32, (t, LANES), 1)
    zero_rows = jnp.zeros((LANES - SEL_GROUP, t), BF16)
    tiles_per_group = SEL_GROUP * SLC_BLOCK // t
    s_bufs = (s0_ref, s1_ref)

    def scores(j, buf):
        k = ks_ref[pl.ds(pl.multiple_of(j * t, t), t), :]
        grp = j // tiles_per_group
        first_blk = (j % tiles_per_group) * (t // SLC_BLOCK)
        expand = jnp.where(lane_col == first_blk + tok_blk, 1.0, 0.0).astype(BF16)
        lhs = jnp.concatenate([k, expand], axis=1)
        for h in range(2):
            blk_bias = bias_ref[h, grp]
            for g in range(NSA_GROUP):
                rhs = jnp.concatenate([qs[2 * g + h], blk_bias, zero_rows], axis=0)
                s_bufs[buf][2 * g + h] = _dot(lhs, rhs)

    def consume(j, buf, diag):
        vt = _with_ones_rows(vst_ref[:, pl.ds(pl.multiple_of(j * t, t), t)])
        for idx in range(n_heads):
            _online_update(s_bufs[buf], vt, m_ref, acc_ref, idx, diag)

    _pipelined_tiles(i, scores, consume)

    n_win = WIN // t + 1
    k_win = jnp.concatenate([kw0_ref[...], kw1_ref[...], kw2_ref[...]], axis=0)
    vt_win = jnp.concatenate([vwt0_ref[...], vwt1_ref[...], vwt2_ref[...]], axis=1)
    kpos = (i - (n_win - 1)) * t + lax.broadcasted_iota(jnp.int32, (n_win * t, t), 0)
    qpos = i * t + lax.broadcasted_iota(jnp.int32, (n_win * t, t), 1)
    wmask = jnp.logical_and(jnp.logical_and(kpos <= qpos, kpos > qpos - WIN), kpos >= 0)
    ow = []
    for idx in range(n_heads):
        s = jnp.where(wmask, _dot(k_win, qs[idx]), NEG)
        p = jnp.exp2(s - jnp.max(s, axis=0, keepdims=True))
        l = jnp.sum(p, axis=0, keepdims=True)
        ow.append(_dot(vt_win, p.astype(BF16)) / l)

    gates = jax.nn.sigmoid(_dot(et_ref[0], gzt_ref[...]))
    gw = NSA_GROUP * LANES
    for g in range(NSA_GROUP):
        rows = slice(g * LANES, (g + 1) * LANES)
        o_cmp = oct_ref[rows, :].astype(F32)
        o_slc = jnp.where(low, _online_result(acc_ref, 2 * g), _online_result(acc_ref, 2 * g + 1))
        o_win = jnp.where(low, ow[2 * g], ow[2 * g + 1])
        out = (gates[g * LANES:(g + 1) * LANES, :] * o_cmp
               + gates[gw + g * LANES:gw + (g + 1) * LANES, :] * o_slc
               + gates[2 * gw + g * LANES:2 * gw + (g + 1) * LANES, :] * o_win)
        ot_ref[rows, :] = out.astype(ot_ref.dtype)


def _nsa_sel(nat, proj_t, selt, oct, gate_expand_t):
    S = nat.shape[0]
    t = NSA_TILE
    n_grp = selt.shape[1]
    n_heads = 2 * NSA_GROUP
    gl = NSA_GROUP * LANES

    def kwin_spec(back):
        return pl.BlockSpec((t, LANES), lambda p, i: (jnp.maximum(i - back, 0), 6 + p))

    def vwin_spec(back):
        return pl.BlockSpec((LANES, t), lambda p, i: (10 + p, jnp.maximum(i - back, 0)))

    return pl.pallas_call(
        functools.partial(_nsa_sel_kernel, t=t),
        grid=(NSA_KV_HEADS // 2, S // t),
        in_specs=[
            pl.BlockSpec((gl, t), lambda p, i: (p, i)),
            pl.BlockSpec((S, LANES), lambda p, i: (0, 4 + p)),
            pl.BlockSpec((LANES, S), lambda p, i: (8 + p, 0)),
            pl.BlockSpec((2, n_grp, SEL_GROUP, t), lambda p, i: (p, 0, 0, i)),
            kwin_spec(2), kwin_spec(1), kwin_spec(0),
            vwin_spec(2), vwin_spec(1), vwin_spec(0),
            pl.BlockSpec((NSA_PROJ_TILE, t), lambda p, i: (6, i)),
            pl.BlockSpec((1, 3 * gl, NSA_PROJ_TILE), lambda p, i: (p, 0, 0)),
            pl.BlockSpec((gl, t), lambda p, i: (p, i)),
        ],
        out_specs=pl.BlockSpec((gl, t), lambda p, i: (p, i)),
        out_shape=jax.ShapeDtypeStruct((D_MODEL, S), BF16),
        scratch_shapes=([pltpu.VMEM((n_heads, t, t), F32)] * 2 + [pltpu.VMEM((2, n_grp, SEL_GROUP, t), BF16)]
                        + [pltpu.VMEM((n_heads, 1, t), F32)] + [pltpu.VMEM((n_heads, LANES + SUM_ROWS, t), F32)]),
        compiler_params=_cparams(("parallel", "arbitrary")),
        name="nsa_sel",
    )(proj_t, nat, proj_t, selt, nat, nat, nat, proj_t, proj_t, proj_t, proj_t, gate_expand_t, oct)


def _rope_tables(positions):
    half = ROT_DIM // 2
    inv = ROPE_THETA ** (-jnp.arange(0, ROT_DIM, 2, dtype=F32) / ROT_DIM)
    ang = positions.astype(F32)[:, None] * inv[None, :]
    cos, sin = jnp.cos(ang), jnp.sin(ang)
    S = positions.shape[0]
    ones = jnp.ones((S, HEAD_DIM - ROT_DIM), F32)
    zeros_h = jnp.zeros((S, half), F32)
    zeros_r = jnp.zeros((S, HEAD_DIM - ROT_DIM), F32)
    rc = jnp.concatenate([cos, cos, ones], axis=1)
    rs1 = jnp.concatenate([-sin, zeros_h, zeros_r], axis=1)
    rs2 = jnp.concatenate([zeros_h, sin, zeros_r], axis=1)
    reps = LANES // HEAD_DIM
    return jnp.tile(rc, (1, reps)), jnp.tile(rs1, (1, reps)), jnp.tile(rs2, (1, reps))


def _pair_heads(w, axis):
    shape = w.shape
    w = w.reshape(shape[:axis] + (NSA_KV_HEADS // 2, 2, NSA_GROUP, HEAD_DIM) + shape[axis + 1:])
    w = jnp.swapaxes(w, axis + 1, axis + 2)
    return w.reshape(shape)


def _nsa_constants(S):
    nc = S // CMP_STRIDE
    n_cmp = (S - CMP_BLOCK) // CMP_STRIDE + 1
    ns = S // SLC_BLOCK
    nsp = -(-ns // LANES) * LANES
    ci = np.arange(nc)[None, :] * CMP_STRIDE
    st = np.arange(nsp)[:, None] * SLC_BLOCK
    ovt = (ci < st + SLC_BLOCK) & (ci + CMP_BLOCK > st) & (np.arange(nc)[None, :] < n_cmp) & (np.arange(nsp)[:, None] < ns)
    e = np.zeros((NSA_KV_HEADS // 2, 3 * NSA_GROUP * LANES, NSA_PROJ_TILE), np.float32)
    for p in range(NSA_KV_HEADS // 2):
        for br in range(3):
            for g in range(NSA_GROUP):
                for hf in range(2):
                    src = br * N_HEADS + (2 * p + hf) * NSA_GROUP + g
                    r0 = br * NSA_GROUP * LANES + g * LANES + hf * HEAD_DIM
                    e[p, r0:r0 + HEAD_DIM, src] = 1.0
    cmp_end = np.minimum(np.arange(nc) * CMP_STRIDE + CMP_BLOCK - 1, S - 1)
    return nc, jnp.asarray(ovt.astype(np.float32), BF16), jnp.asarray(e, BF16), cmp_end


def _nsa_layer(h, g, positions, w_in, pe_k, w1_k, w2_k, pe_v, w1_v, w2_v, w_o):
    S, D = h.shape
    qd = N_HEADS * HEAD_DIM
    kvd = NSA_KV_HEADS * HEAD_DIM
    nc, ovt, gate_expand_t, cmp_end = _nsa_constants(S)
    rc, rs1, rs2 = _rope_tables(positions)

    def kv_piece(n):
        return w_in[:, qd + n * kvd:qd + (n + 1) * kvd]

    w_nat = jnp.concatenate([kv_piece(0), kv_piece(1), kv_piece(2), kv_piece(4)], axis=1).astype(BF16)
    wg = jnp.pad(w_in[:, qd + 6 * kvd:], ((0, 0), (0, NSA_PROJ_TILE - 3 * N_HEADS)))
    w_t = jnp.concatenate([_pair_heads(w_in[:, :qd], 1) * QK_SCALE, kv_piece(3), kv_piece(5), wg], axis=1).T.astype(BF16)
    nat = _proj(h, g, w_nat, "nsa_proj", key_major=False, tn=NSA_PROJ_TILE,
                rope=(rc, rs1, rs2), rope_tiles=NSA_NAT_ROPE_TILES)
    proj_t = _proj(h, g, w_t, "nsa_proj_t", key_major=True, tn=NSA_PROJ_TILE,
                   rope=(rc.T, rs1.T, rs2.T), rope_tiles=NSA_T_ROPE_TILES)

    raw = nat[:, :2 * kvd].reshape(nc, CMP_STRIDE, 2, NSA_KV_HEADS, HEAD_DIM)
    raw = raw.transpose(2, 3, 0, 1, 4).reshape(2, NSA_KV_HEADS, nc, CMP_STRIDE * HEAD_DIM)
    w1 = jnp.stack([w1_k, w1_v]).astype(BF16)
    pe = jnp.stack([pe_k.reshape(1, -1), pe_v.reshape(1, -1)])
    pe = jnp.pad(pe, ((0, 0), (0, 7), (0, 0))).astype(BF16)
    w2 = jnp.stack([w2_k, w2_v])
    eye = jnp.eye(NSA_KV_HEADS, dtype=F32)
    w2p = (w2[:, None, :, None, :] * eye[None, :, None, :, None]).reshape(2, NSA_KV_HEADS, CMP_HIDDEN, kvd).astype(BF16)
    kvc = _nsa_cmp(raw, w1, pe, w2p, rc[cmp_end], rs1[cmp_end], rs2[cmp_end])

    oct, selt = _nsa_cmp_attn(proj_t, kvc, kvc[1].T, ovt)
    selt = selt.reshape(NSA_KV_HEADS, -1, SEL_GROUP, S)
    ot = _nsa_sel(nat, proj_t, selt, oct, gate_expand_t)
    return _matmul_res(ot, _pair_heads(w_o, 0).astype(BF16), h, "nsa_out")


def _trunk(x2, positions, p):
    h = _fox_layer(x2, p["l0_norm_mix"], p["l0_fox_w_qkv"], p["l0_fox_w_f"], p["l0_fox_b_f"], p["l0_fox_w_o"])
    h = _mlp(h, p["l0_norm_mlp"], p["l0_mlp_w1"].astype(BF16), p["l0_mlp_w2"].astype(BF16))
    h = _pool_layer(h, p["l1_norm_mix"], p["l1_pool_w"], p["l1_pool_scale"])
    h = _mlp(h, p["l1_norm_mlp"], p["l1_mlp_w1"].astype(BF16), p["l1_mlp_w2"].astype(BF16))
    h = _conv_layer(h, p["l2_norm_mix"], p["l2_conv_w_in"], p["l2_conv_w"], p["l2_conv_w_out"])
    h = _mlp(h, p["l2_norm_mlp"], p["l2_mlp_w1"].astype(BF16), p["l2_mlp_w2"].astype(BF16))
    h = _nsa_layer(h, p["l3_norm_mix"], positions, p["l3_nsa_w_in"], p["l3_nsa_cmp_pe_k"], p["l3_nsa_cmp_w1_k"],
                   p["l3_nsa_cmp_w2_k"], p["l3_nsa_cmp_pe_v"], p["l3_nsa_cmp_w1_v"], p["l3_nsa_cmp_w2_v"],
                   p["l3_nsa_w_o"])
    return _mlp(h, p["l3_norm_mlp"], p["l3_mlp_w1"].astype(BF16), p["l3_mlp_w2"].astype(BF16), p["final_norm"])


def kernel(x, positions, l0_norm_mix, l0_fox_w_qkv, l0_fox_w_f, l0_fox_b_f, l0_fox_w_o, l0_norm_mlp, l0_mlp_w1, l0_mlp_w2, l1_norm_mix, l1_pool_w, l1_pool_scale, l1_norm_mlp, l1_mlp_w1, l1_mlp_w2, l2_norm_mix, l2_conv_w_in, l2_conv_w, l2_conv_w_out, l2_norm_mlp, l2_mlp_w1, l2_mlp_w2, l3_norm_mix, l3_nsa_w_in, l3_nsa_cmp_pe_k, l3_nsa_cmp_w1_k, l3_nsa_cmp_w2_k, l3_nsa_cmp_pe_v, l3_nsa_cmp_w1_v, l3_nsa_cmp_w2_v, l3_nsa_w_o, l3_norm_mlp, l3_mlp_w1, l3_mlp_w2, final_norm):
    params = dict(locals())
    B, S, D = x.shape
    outs = [_trunk(x[b], positions, params) for b in range(B)]
    return jnp.stack(outs, axis=0)
```

```python
import functools

import numpy as np
import jax
import jax.numpy as jnp
from jax import lax
from jax.experimental import pallas as pl
from jax.experimental.pallas import tpu as pltpu

F32 = jnp.float32
BF16 = jnp.bfloat16

D_MODEL = 1024
HEAD_DIM = 64
N_HEADS = D_MODEL // HEAD_DIM
D_FF = 4 * D_MODEL
ROPE_THETA = 500000.0
ROT_DIM = HEAD_DIM // 4
RMS_EPS = 1e-6
POOL_WINDOWS = (2, 4, 8, 16)
POOL_GROUP = D_MODEL // len(POOL_WINDOWS)
POOL_HALO = 16
CONV_WIDTH = 3
CONV_HALO = 8
NSA_KV_HEADS = 4
NSA_GROUP = N_HEADS // NSA_KV_HEADS
CMP_BLOCK = 32
CMP_STRIDE = 16
CMP_HIDDEN = 256
SLC_BLOCK = 64
SLC_SHIFT = 6
SLC_TOPK = 16
WIN = 512
FORCE_SCORE = 1e9
NEG = -1e30
LOG2E = 1.4426950408889634
QK_SCALE = HEAD_DIM ** -0.5 * LOG2E

LANES = 128
V7X_VMEM_LIMIT = 56 * 1024 * 1024
FOX_TILE = 512
FOX_BIAS_TERMS = 3
NSA_TILE = WIN // 2
SEL_GROUP = 16
SUM_ROWS = 16
CMP_SPANS = 4
NSA_PROJ_TILE = 2 * LANES
NSA_NAT_WIDTH = 8 * LANES
NSA_NAT_ROPE_TILES = (2, 3)
NSA_T_ROWS = 14 * LANES
NSA_T_ROPE_TILES = (0, 1, 2, 3)


def _cparams(semantics):
    return pltpu.CompilerParams(dimension_semantics=semantics, vmem_limit_bytes=V7X_VMEM_LIMIT)


def _rms(x, g):
    return x * lax.rsqrt(jnp.mean(x * x, axis=-1, keepdims=True) + RMS_EPS) * g


def _dot(a, b):
    return jnp.dot(a, b, preferred_element_type=F32)


def _dot_nt(a, b):
    return lax.dot_general(a, b, (((1,), (1,)), ((), ())), preferred_element_type=F32)


def _dot_tn(a, b):
    return lax.dot_general(a, b, (((0,), (0,)), ((), ())), preferred_element_type=F32)


def _split3(x):
    hi = x.astype(BF16)
    r = x - hi.astype(F32)
    mid = r.astype(BF16)
    lo = (r - mid.astype(F32)).astype(BF16)
    return hi, mid, lo


def _rope(a, rc, rs1, rs2, axis):
    half = ROT_DIM // 2
    return a * rc + pltpu.roll(a, LANES - half, axis) * rs1 + pltpu.roll(a, half, axis) * rs2


def _head_rows(qt_blk):
    low = lax.broadcasted_iota(jnp.int32, qt_blk.shape, 0) < HEAD_DIM
    qf = qt_blk.astype(F32)
    return jnp.where(low, qf, 0.0).astype(BF16), jnp.where(low, 0.0, qf).astype(BF16)


def _with_ones_rows(vt):
    return jnp.concatenate([vt, jnp.ones((SUM_ROWS, vt.shape[1]), vt.dtype)], axis=0)


def _online_update(s_ref, vt_ones, m_ref, acc_ref, idx, causal):
    tk, t = s_ref.shape[1:]
    s = s_ref[idx]
    if causal:
        s = jnp.where(lax.broadcasted_iota(jnp.int32, (tk, t), 0) <= lax.broadcasted_iota(jnp.int32, (tk, t), 1),
                      s, NEG)
    m_old = m_ref[idx]
    m_new = jnp.maximum(m_old, jnp.max(s, axis=0, keepdims=True))
    p = jnp.exp2(s - m_new).astype(BF16)
    m_ref[idx] = m_new
    acc_ref[idx] = jnp.exp2(m_old - m_new) * acc_ref[idx] + _dot(vt_ones, p)


def _online_result(acc_ref, idx):
    acc = acc_ref[idx]
    return acc[:LANES, :] / acc[LANES:LANES + 1, :]


def _pipelined_tiles(i, scores, consume, pairs_per_step=1):
    scores(0, 0)

    def pair(j):
        scores(j + 1, 1)
        consume(j, 0, False)
        scores(j + 2, 0)
        consume(j + 1, 1, False)

    def body(jj, carry):
        for u in range(pairs_per_step):
            pair(2 * (pairs_per_step * jj + u))
        return carry

    n_pairs = i // 2
    n_steps = n_pairs // pairs_per_step
    lax.fori_loop(0, n_steps, body, 0)
    if pairs_per_step > 1:
        def tail(jj, carry):
            pair(2 * jj)
            return carry

        lax.fori_loop(n_steps * pairs_per_step, n_pairs, tail, 0)

    @pl.when(i % 2 == 0)
    def _():
        consume(i, 0, True)

    @pl.when(i % 2 == 1)
    def _():
        scores(i, 1)
        consume(i - 1, 0, False)
        consume(i, 1, True)


def _mlp_kernel(x_ref, g_ref, w1_ref, w2_ref, *rest, nf, final):
    if final:
        fg_ref, o_ref, xn_ref = rest
    else:
        o_ref, xn_ref = rest
    f = pl.program_id(1)

    @pl.when(f == 0)
    def _():
        x = x_ref[...]
        xn_ref[...] = _rms(x, g_ref[...]).astype(BF16)
        o_ref[...] = x

    a = _dot(xn_ref[...], w1_ref[...])
    a = jnp.square(jnp.maximum(a, 0.0)).astype(BF16)
    o_ref[...] += _dot(a, w2_ref[...])

    if final:
        @pl.when(f == nf - 1)
        def _():
            o_ref[...] = _rms(o_ref[...], fg_ref[...])


def _mlp(h, g, w1, w2, final_g=None):
    S, D = h.shape
    F = w1.shape[1]
    tm = min(1024, S)
    tf = 1024
    nf = F // tf
    final = final_g is not None
    in_specs = [
        pl.BlockSpec((tm, D), lambda i, f: (i, 0)),
        pl.BlockSpec((1, D), lambda i, f: (0, 0)),
        pl.BlockSpec((D, tf), lambda i, f: (0, f)),
        pl.BlockSpec((tf, D), lambda i, f: (f, 0)),
    ]
    args = [h, g.reshape(1, D), w1, w2]
    if final:
        in_specs.append(pl.BlockSpec((1, D), lambda i, f: (0, 0)))
        args.append(final_g.reshape(1, D))
    return pl.pallas_call(
        functools.partial(_mlp_kernel, nf=nf, final=final),
        grid=(S // tm, nf),
        in_specs=in_specs,
        out_specs=pl.BlockSpec((tm, D), lambda i, f: (i, 0)),
        out_shape=jax.ShapeDtypeStruct((S, D), F32),
        scratch_shapes=[pltpu.VMEM((tm, D), BF16)],
        compiler_params=_cparams(("parallel", "arbitrary")),
        name="mlp",
    )(*args)


def _proj_kernel(x_ref, g_ref, w_ref, *rest, key_major, rope_tiles):
    if rope_tiles:
        rc_ref, rs1_ref, rs2_ref, o_ref, xn_ref = rest
    else:
        o_ref, xn_ref = rest
    j = pl.program_id(1)

    @pl.when(j == 0)
    def _():
        xn_ref[...] = _rms(x_ref[...], g_ref[...]).astype(BF16)

    a = _dot_nt(w_ref[...], xn_ref[...]) if key_major else _dot(xn_ref[...], w_ref[...])
    if not rope_tiles:
        o_ref[...] = a.astype(o_ref.dtype)
        return
    axis = 0 if key_major else 1
    is_rope = functools.reduce(jnp.logical_or, [j == t for t in rope_tiles])

    @pl.when(is_rope)
    def _():
        rc, rs1, rs2 = rc_ref[...], rs1_ref[...], rs2_ref[...]
        blocks = [lax.slice_in_dim(a, b * LANES, (b + 1) * LANES, axis=axis) for b in range(a.shape[axis] // LANES)]
        o_ref[...] = jnp.concatenate([_rope(blk, rc, rs1, rs2, axis) for blk in blocks], axis=axis).astype(o_ref.dtype)

    @pl.when(jnp.logical_not(is_rope))
    def _():
        o_ref[...] = a.astype(o_ref.dtype)


def _proj(h, g, w, name, *, key_major, tn, rope=None, rope_tiles=()):
    S, D = h.shape
    tm = min(1024, S)
    if key_major:
        N = w.shape[0]
        w_spec = pl.BlockSpec((tn, D), lambda i, j: (j, 0))
        rope_spec = pl.BlockSpec((LANES, tm), lambda i, j: (0, i))
        out_spec = pl.BlockSpec((tn, tm), lambda i, j: (j, i))
        out_shape = jax.ShapeDtypeStruct((N, S), BF16)
    else:
        N = w.shape[1]
        w_spec = pl.BlockSpec((D, tn), lambda i, j: (0, j))
        rope_spec = pl.BlockSpec((tm, LANES), lambda i, j: (i, 0))
        out_spec = pl.BlockSpec((tm, tn), lambda i, j: (i, j))
        out_shape = jax.ShapeDtypeStruct((S, N), BF16)
    in_specs = [pl.BlockSpec((tm, D), lambda i, j: (i, 0)), pl.BlockSpec((1, D), lambda i, j: (0, 0)), w_spec]
    args = [h, g.reshape(1, D), w]
    if rope_tiles:
        in_specs += [rope_spec] * 3
        args += list(rope)
    return pl.pallas_call(
        functools.partial(_proj_kernel, key_major=key_major, rope_tiles=tuple(rope_tiles)),
        grid=(S // tm, N // tn),
        in_specs=in_specs,
        out_specs=out_spec,
        out_shape=out_shape,
        scratch_shapes=[pltpu.VMEM((tm, D), BF16)],
        compiler_params=_cparams(("parallel", "arbitrary")),
        name=name,
    )(*args)


def _matmul_res_kernel(at_ref, w_ref, r_ref, o_ref):
    o_ref[...] = r_ref[...] + _dot_tn(at_ref[...], w_ref[...])


def _matmul_res(at, w, res, name):
    K, S = at.shape
    N = w.shape[1]
    tm = min(1024, S)
    return pl.pallas_call(
        _matmul_res_kernel,
        grid=(S // tm,),
        in_specs=[
            pl.BlockSpec((K, tm), lambda i: (0, i)),
            pl.BlockSpec((K, N), lambda i: (0, 0)),
            pl.BlockSpec((tm, N), lambda i: (i, 0)),
        ],
        out_specs=pl.BlockSpec((tm, N), lambda i: (i, 0)),
        out_shape=jax.ShapeDtypeStruct((S, N), F32),
        compiler_params=_cparams(("parallel",)),
        name=name,
    )(at, w, res)


def _fox_gate_kernel(x_ref, g_ref, wf_ref, bf_ref, place_ref, c_ref, carry_ref, *, tm):
    @pl.when(pl.program_id(0) == 0)
    def _():
        carry_ref[...] = jnp.zeros_like(carry_ref)

    xn = _rms(x_ref[...], g_ref[...]).astype(BF16)
    z = _dot(xn, wf_ref[...]) + bf_ref[...]
    logf = jnp.minimum(z, 0.0) - jnp.log(1.0 + jnp.exp(-jnp.abs(z)))
    row = lax.broadcasted_iota(jnp.int32, (tm, tm), 0)
    col = lax.broadcasted_iota(jnp.int32, (tm, tm), 1)
    tri = jnp.where(row >= col, 1.0, 0.0).astype(BF16)
    hi, mid, lo = _split3(logf)
    c = _dot(tri, hi) + _dot(tri, mid) + _dot(tri, lo) + carry_ref[...]
    carry_ref[...] = c[tm - 1:tm, :]
    terms = jnp.concatenate(_split3(c * -LOG2E), axis=1)
    c_ref[...] = _dot(terms, place_ref[...]).astype(c_ref.dtype)


def _fox_gate(h, g, wf, bf, place):
    S, D = h.shape
    tm = min(512, S)
    return pl.pallas_call(
        functools.partial(_fox_gate_kernel, tm=tm),
        grid=(S // tm,),
        in_specs=[
            pl.BlockSpec((tm, D), lambda i: (i, 0)),
            pl.BlockSpec((1, D), lambda i: (0, 0)),
            pl.BlockSpec((D, LANES), lambda i: (0, 0)),
            pl.BlockSpec((1, LANES), lambda i: (0, 0)),
            pl.BlockSpec((FOX_BIAS_TERMS * LANES, D_MODEL), lambda i: (0, 0)),
        ],
        out_specs=pl.BlockSpec((tm, D_MODEL), lambda i: (i, 0)),
        out_shape=jax.ShapeDtypeStruct((S, D_MODEL), BF16),
        scratch_shapes=[pltpu.VMEM((1, LANES), F32)],
        compiler_params=_cparams(("arbitrary",)),
        name="fox_gate",
    )(h, g.reshape(1, D), wf, bf, place)


def _fox_attn_kernel(qt_ref, k_ref, c_ref, vt_ref, ot_ref, s0_ref, s1_ref, m_ref, acc_ref, *, t):
    i = pl.program_id(1)
    row = lax.broadcasted_iota(jnp.int32, (LANES, t), 0)
    q_heads = _head_rows(qt_ref[...])
    qa = []
    for h in range(2):
        ones_rows = jnp.logical_and(row >= FOX_BIAS_TERMS * h, row < FOX_BIAS_TERMS * (h + 1))
        qa.append(jnp.concatenate([q_heads[h], jnp.where(ones_rows, 1.0, 0.0).astype(BF16)], axis=0))
    m_ref[...] = jnp.full(m_ref.shape, NEG, F32)
    acc_ref[...] = jnp.zeros(acc_ref.shape, F32)
    s_bufs = (s0_ref, s1_ref)

    def scores(j, buf):
        rows = pl.ds(pl.multiple_of(j * t, t), t)
        lhs = jnp.concatenate([k_ref[rows, :], c_ref[rows, :]], axis=1)
        for h in range(2):
            s_bufs[buf][h] = _dot(lhs, qa[h])

    def consume(j, buf, diag):
        vt = _with_ones_rows(vt_ref[:, pl.ds(pl.multiple_of(j * t, t), t)])
        for h in range(2):
            _online_update(s_bufs[buf], vt, m_ref, acc_ref, h, diag)

    _pipelined_tiles(i, scores, consume, pairs_per_step=2)
    ot = jnp.where(row < HEAD_DIM, _online_result(acc_ref, 0), _online_result(acc_ref, 1))
    ot_ref[...] = ot.astype(ot_ref.dtype)


def _fox_attn(qvt, k, c_terms):
    S = k.shape[0]
    t = min(FOX_TILE, S)
    n_pairs = N_HEADS // 2
    return pl.pallas_call(
        functools.partial(_fox_attn_kernel, t=t),
        grid=(n_pairs, S // t),
        in_specs=[
            pl.BlockSpec((LANES, t), lambda hp, i: (hp, i)),
            pl.BlockSpec((S, LANES), lambda hp, i: (0, hp)),
            pl.BlockSpec((S, LANES), lambda hp, i: (0, hp)),
            pl.BlockSpec((LANES, S), lambda hp, i: (n_pairs + hp, 0)),
        ],
        out_specs=pl.BlockSpec((LANES, t), lambda hp, i: (hp, i)),
        out_shape=jax.ShapeDtypeStruct((D_MODEL, S), BF16),
        scratch_shapes=([pltpu.VMEM((2, t, t), F32)] * 2 + [pltpu.VMEM((2, 1, t), F32)]
                        + [pltpu.VMEM((2, LANES + SUM_ROWS, t), F32)]),
        compiler_params=_cparams(("parallel", "arbitrary")),
        name="fox_attn",
    )(qvt, k, c_terms, qvt)


def _fox_bias_placement():
    place = np.zeros((FOX_BIAS_TERMS * LANES, D_MODEL), np.float32)
    for head in range(N_HEADS):
        for n in range(FOX_BIAS_TERMS):
            place[n * LANES + head, (head // 2) * LANES + FOX_BIAS_TERMS * (head % 2) + n] = 1.0
    return jnp.asarray(place, BF16)


def _fox_layer(h, g, w_qkv, w_f, b_f, w_o):
    wf = jnp.pad(w_f, ((0, 0), (0, LANES - N_HEADS))).astype(BF16)
    bf = jnp.pad(b_f, (0, LANES - N_HEADS)).reshape(1, LANES)
    c_terms = _fox_gate(h, g, wf, bf, _fox_bias_placement())
    w_qv = jnp.concatenate([w_qkv[:, :D_MODEL] * QK_SCALE, w_qkv[:, 2 * D_MODEL:]], axis=1).T.astype(BF16)
    qvt = _proj(h, g, w_qv, "fox_qv", key_major=True, tn=512)
    k = _proj(h, g, w_qkv[:, D_MODEL:2 * D_MODEL].astype(BF16), "fox_k", key_major=False, tn=512)
    ot = _fox_attn(qvt, k, c_terms)
    return _matmul_res(ot, w_o.astype(BF16), h, "fox_out")


def _pool_kernel(x_ref, halo_ref, g_ref, w_ref, sc_ref, o_ref, *, tm):
    i = pl.program_id(0)
    x = x_ref[...]
    g = g_ref[...]
    xn = _rms(x, g)
    hn = jnp.where(i > 0, _rms(halo_ref[...], g), 0.0)
    xe = jnp.concatenate([hn, xn], axis=0)
    tpos = i * tm + lax.broadcasted_iota(jnp.int32, (tm, 1), 0)
    for gi, w in enumerate(POOL_WINDOWS):
        sl = slice(gi * POOL_GROUP, (gi + 1) * POOL_GROUP)
        s = xe[:, sl]
        k = 1
        while k < w:
            s = s + pltpu.roll(s, k, 0)
            k *= 2
        cnt = jnp.minimum(tpos + 1, w).astype(F32)
        d = (s[POOL_HALO:, :] / cnt - xn[:, sl]).astype(BF16)
        o_ref[:, sl] = x[:, sl] + _dot(d, w_ref[gi]) * sc_ref[:, sl]


def _pool_layer(h, g, w_pool, pool_scale):
    S, D = h.shape
    tm = min(1024, S)
    return pl.pallas_call(
        functools.partial(_pool_kernel, tm=tm),
        grid=(S // tm,),
        in_specs=[
            pl.BlockSpec((tm, D), lambda i: (i, 0)),
            pl.BlockSpec((POOL_HALO, D), lambda i: (jnp.maximum(i * (tm // POOL_HALO) - 1, 0), 0)),
            pl.BlockSpec((1, D), lambda i: (0, 0)),
            pl.BlockSpec((len(POOL_WINDOWS), POOL_GROUP, POOL_GROUP), lambda i: (0, 0, 0)),
            pl.BlockSpec((1, D), lambda i: (0, 0)),
        ],
        out_specs=pl.BlockSpec((tm, D), lambda i: (i, 0)),
        out_shape=jax.ShapeDtypeStruct((S, D), F32),
        compiler_params=_cparams(("parallel",)),
        name="pool",
    )(h, h, g.reshape(1, D), w_pool.astype(BF16), pool_scale.reshape(1, D))


def _conv_in_kernel(x_ref, g_ref, wb_ref, wc_ref, wu_ref, b_ref, z_ref, xn_ref):
    @pl.when(pl.program_id(1) == 0)
    def _():
        xn_ref[...] = _rms(x_ref[...], g_ref[...]).astype(BF16)

    xn = xn_ref[...]
    b_ref[...] = _dot(xn, wb_ref[...])
    z_ref[...] = _dot(xn, wc_ref[...]) * _dot(xn, wu_ref[...])


def _conv_out_kernel(b_ref, z_ref, zh_ref, cw_ref, w_ref, r_ref, o_ref, *, tm):
    i = pl.program_id(0)
    z = z_ref[...]
    zh = jnp.where(i > 0, zh_ref[...], 0.0)
    row = lax.broadcasted_iota(jnp.int32, (tm, 1), 0)
    prev1 = zh[CONV_HALO - 1:CONV_HALO, :]
    prev2 = zh[CONV_HALO - 2:CONV_HALO - 1, :]
    z1 = jnp.where(row == 0, prev1, pltpu.roll(z, 1, 0))
    z2 = jnp.where(row == 0, prev2, jnp.where(row == 1, prev1, pltpu.roll(z, 2, 0)))
    cw = cw_ref[...]
    conv = cw[0:1, :] * z2 + cw[1:2, :] * z1 + cw[2:3, :] * z
    y = (b_ref[...] * conv).astype(BF16)
    o_ref[...] = r_ref[...] + _dot(y, w_ref[...])


def _conv_layer(h, g, w_in, conv_w, w_out):
    S, D = h.shape
    tm = min(1024, S)
    tn = 512
    nj = D // tn
    w_in = w_in.astype(BF16)
    b, z = pl.pallas_call(
        _conv_in_kernel,
        grid=(S // tm, nj),
        in_specs=[
            pl.BlockSpec((tm, D), lambda i, j: (i, 0)),
            pl.BlockSpec((1, D), lambda i, j: (0, 0)),
            pl.BlockSpec((D, tn), lambda i, j: (0, j)),
            pl.BlockSpec((D, tn), lambda i, j: (0, nj + j)),
            pl.BlockSpec((D, tn), lambda i, j: (0, 2 * nj + j)),
        ],
        out_specs=[pl.BlockSpec((tm, tn), lambda i, j: (i, j)),
                   pl.BlockSpec((tm, tn), lambda i, j: (i, j))],
        out_shape=[jax.ShapeDtypeStruct((S, D), F32), jax.ShapeDtypeStruct((S, D), F32)],
        scratch_shapes=[pltpu.VMEM((tm, D), BF16)],
        compiler_params=_cparams(("parallel", "arbitrary")),
        name="conv_in",
    )(h, g.reshape(1, D), w_in, w_in, w_in)
    cw = jnp.pad(conv_w, ((0, 8 - CONV_WIDTH), (0, 0)))
    return pl.pallas_call(
        functools.partial(_conv_out_kernel, tm=tm),
        grid=(S // tm,),
        in_specs=[
            pl.BlockSpec((tm, D), lambda i: (i, 0)),
            pl.BlockSpec((tm, D), lambda i: (i, 0)),
            pl.BlockSpec((CONV_HALO, D), lambda i: (jnp.maximum(i * (tm // CONV_HALO) - 1, 0), 0)),
            pl.BlockSpec((8, D), lambda i: (0, 0)),
            pl.BlockSpec((D, D), lambda i: (0, 0)),
            pl.BlockSpec((tm, D), lambda i: (i, 0)),
        ],
        out_specs=pl.BlockSpec((tm, D), lambda i: (i, 0)),
        out_shape=jax.ShapeDtypeStruct((S, D), F32),
        compiler_params=_cparams(("parallel",)),
        name="conv_out",
    )(b, z, z, cw, w_out.astype(BF16), h)


def _nsa_cmp_kernel(r_ref, w1_ref, pe_ref, w2_ref, rc_ref, rs1_ref, rs2_ref, o_ref, acc_ref, *, nc):
    kv = pl.program_id(0)
    hd = pl.program_id(1)
    half = CMP_STRIDE * HEAD_DIM
    r = r_ref[0, 0]
    w1 = w1_ref[0]
    first = _dot(r, w1[:half, :])
    second = _dot(r, w1[half:, :])
    pe_term = _dot(pe_ref[0], w1)[0:1, :]
    pre = first + pltpu.roll(second, nc - 1, 0) + pe_term
    ge = 0.5 * pre * (1.0 + jnp.tanh(0.7978845608028654 * (pre + 0.044715 * pre * pre * pre)))
    y = _dot(ge.astype(BF16), w2_ref[0, 0])

    @pl.when(hd == 0)
    def _():
        acc_ref[...] = y

    @pl.when(hd > 0)
    def _():
        acc_ref[...] += y

    @pl.when(hd == NSA_KV_HEADS - 1)
    def _():
        acc = acc_ref[...]

        @pl.when(kv == 0)
        def _():
            rc, rs1, rs2 = rc_ref[...], rs1_ref[...], rs2_ref[...]
            o_ref[0] = jnp.concatenate(
                [_rope(acc[:, :LANES], rc, rs1, rs2, 1), _rope(acc[:, LANES:], rc, rs1, rs2, 1)],
                axis=1).astype(o_ref.dtype)

        @pl.when(kv == 1)
        def _():
            o_ref[0] = acc.astype(o_ref.dtype)


def _nsa_cmp(r, w1, pe, w2p, rc, rs1, rs2):
    nc = r.shape[2]
    kvd = NSA_KV_HEADS * HEAD_DIM
    return pl.pallas_call(
        functools.partial(_nsa_cmp_kernel, nc=nc),
        grid=(2, NSA_KV_HEADS),
        in_specs=[
            pl.BlockSpec((1, 1, nc, CMP_STRIDE * HEAD_DIM), lambda a, b: (a, b, 0, 0)),
            pl.BlockSpec((1, CMP_BLOCK * HEAD_DIM, CMP_HIDDEN), lambda a, b: (a, 0, 0)),
            pl.BlockSpec((1, 8, CMP_BLOCK * HEAD_DIM), lambda a, b: (a, 0, 0)),
            pl.BlockSpec((1, 1, CMP_HIDDEN, kvd), lambda a, b: (a, b, 0, 0)),
            pl.BlockSpec((nc, LANES), lambda a, b: (0, 0)),
            pl.BlockSpec((nc, LANES), lambda a, b: (0, 0)),
            pl.BlockSpec((nc, LANES), lambda a, b: (0, 0)),
        ],
        out_specs=pl.BlockSpec((1, nc, kvd), lambda a, b: (a, 0, 0)),
        out_shape=jax.ShapeDtypeStruct((2, nc, kvd), BF16),
        scratch_shapes=[pltpu.VMEM((nc, kvd), F32)],
        compiler_params=_cparams(("parallel", "arbitrary")),
        name="nsa_cmp",
    )(r, w1, pe, w2p, rc, rs1, rs2)


def _nsa_cmp_attn_body(i, qt_ref, kc_ref, vct_ref, ovt_ref, oct_ref, selt_ref, *, t, nc, nsp):
    qpos = i * t + lax.broadcasted_iota(jnp.int32, (1, t), 1)
    cmp_end = CMP_STRIDE * lax.broadcasted_iota(jnp.int32, (nc, t), 0) + (CMP_BLOCK - 1)
    cmask = cmp_end <= qpos
    any_valid = qpos >= CMP_BLOCK - 1
    low = lax.broadcasted_iota(jnp.int32, (LANES, t), 0) < HEAD_DIM
    ovt = ovt_ref[:nsp, :nc]
    blk = lax.broadcasted_iota(jnp.int32, (nsp, t), 0)
    blk_f = blk.astype(F32)
    cur = jnp.right_shift(qpos, SLC_SHIFT)
    forced = jnp.logical_or(blk == 0, jnp.logical_or(blk == cur, blk == cur - 1))
    valid = blk * SLC_BLOCK <= qpos
    vals = []
    for pair in range(NSA_KV_HEADS // 2):
        kc = kc_ref[0, :nc, pair * LANES:(pair + 1) * LANES]
        vct = vct_ref[pair * LANES:(pair + 1) * LANES, :nc]
        imp = [jnp.zeros((nc, t), F32), jnp.zeros((nc, t), F32)]
        for g in range(NSA_GROUP):
            rows = slice((pair * NSA_GROUP + g) * LANES, (pair * NSA_GROUP + g + 1) * LANES)
            qh = _head_rows(qt_ref[rows, :])
            oc = []
            for h in range(2):
                s = jnp.where(cmask, _dot(kc, qh[h]), NEG)
                p = jnp.exp2(s - jnp.max(s, axis=0, keepdims=True))
                l = jnp.sum(p, axis=0, keepdims=True)
                pn = p * jnp.where(any_valid, 1.0 / l, 0.0)
                imp[h] = imp[h] + pn
                oc.append(_dot(vct, pn.astype(BF16)))
            oct_ref[rows, :] = jnp.where(low, oc[0], oc[1]).astype(oct_ref.dtype)
        for h in range(2):
            hi, mid, lo = _split3(imp[h])
            score = _dot(ovt, hi) + _dot(ovt, mid) + _dot(ovt, lo)
            vals.append(jnp.where(valid, jnp.where(forced, FORCE_SCORE, score), NEG))

    def pick_one(_, vals):
        out = []
        for v in vals:
            mx = jnp.max(v, axis=0, keepdims=True)
            first = jnp.min(jnp.where(v == mx, blk_f, float(nsp)), axis=0, keepdims=True)
            out.append(jnp.where(blk_f == first, -jnp.inf, v))
        return tuple(out)

    vals = lax.fori_loop(0, SLC_TOPK, pick_one, tuple(vals))
    for kvh in range(NSA_KV_HEADS):
        picked = jnp.logical_and(valid, vals[kvh] == -jnp.inf)
        selt_ref[kvh, :nsp, :] = jnp.where(picked, 1.0, 0.0).astype(selt_ref.dtype)
        if nsp < selt_ref.shape[1]:
            selt_ref[kvh, nsp:, :] = jnp.zeros((selt_ref.shape[1] - nsp, t), selt_ref.dtype)


def _nsa_cmp_attn_kernel(qt_ref, kc_ref, vct_ref, ovt_ref, oct_ref, selt_ref, *, t, n_steps, n_spans):
    i = pl.program_id(0)
    nsp, nc = ovt_ref.shape
    for q in range(n_spans):
        @pl.when(jnp.logical_and(i >= q * n_steps // n_spans, i < (q + 1) * n_steps // n_spans))
        def _(q=q):
            _nsa_cmp_attn_body(i, qt_ref, kc_ref, vct_ref, ovt_ref, oct_ref, selt_ref, t=t,
                               nc=(q + 1) * nc // n_spans, nsp=(q + 1) * nsp // n_spans)


def _nsa_cmp_attn(proj_t, kvc, vct, ovt):
    S = proj_t.shape[1]
    t = NSA_TILE
    nsp, nc = ovt.shape
    kvd = NSA_KV_HEADS * HEAD_DIM
    n_spans = max(1, min(CMP_SPANS, nc // NSA_PROJ_TILE))
    return pl.pallas_call(
        functools.partial(_nsa_cmp_attn_kernel, t=t, n_steps=S // t, n_spans=n_spans),
        grid=(S // t,),
        in_specs=[
            pl.BlockSpec((D_MODEL, t), lambda i: (0, i)),
            pl.BlockSpec((1, nc, kvd), lambda i: (0, 0, 0)),
            pl.BlockSpec((kvd, nc), lambda i: (0, 0)),
            pl.BlockSpec((nsp, nc), lambda i: (0, 0)),
        ],
        out_specs=[pl.BlockSpec((D_MODEL, t), lambda i: (0, i)),
                   pl.BlockSpec((NSA_KV_HEADS, nsp, t), lambda i: (0, 0, i))],
        out_shape=[jax.ShapeDtypeStruct((D_MODEL, S), BF16),
                   jax.ShapeDtypeStruct((NSA_KV_HEADS, nsp, S), BF16)],
        compiler_params=_cparams(("parallel",)),
        name="nsa_cmp_attn",
    )(proj_t, kvc, vct, ovt)


def _nsa_sel_kernel(qt_ref, ks_ref, vst_ref, selt_ref, kw0_ref, kw1_ref, kw2_ref, vwt0_ref, vwt1_ref, vwt2_ref,
                    gzt_ref, et_ref, oct_ref, ot_ref, s0_ref, s1_ref, bias_ref, m_ref, acc_ref, *, t):
    i = pl.program_id(1)
    n_heads = 2 * NSA_GROUP
    low = lax.broadcasted_iota(jnp.int32, (LANES, t), 0) < HEAD_DIM
    qs = []
    for g in range(NSA_GROUP):
        qs.extend(_head_rows(qt_ref[g * LANES:(g + 1) * LANES, :]))
    bias_ref[...] = ((selt_ref[...].astype(F32) - 1.0) * -NEG).astype(BF16)
    m_ref[...] = jnp.full(m_ref.shape, NEG, F32)
    acc_ref[...] = jnp.zeros(acc_ref.shape, F32)
    tok_blk = jnp.right_shift(lax.broadcasted_iota(jnp.int32, (t, LANES), 0), SLC_SHIFT)
    lane_col = lax.broadcasted_iota(jnp.int32, (t, LANES), 1)
    zero_rows = jnp.zeros((LANES - SEL_GROUP, t), BF16)
    tiles_per_group = SEL_GROUP * SLC_BLOCK // t
    s_bufs = (s0_ref, s1_ref)

    def scores(j, buf):
        k = ks_ref[pl.ds(pl.multiple_of(j * t, t), t), :]
        grp = j // tiles_per_group
        first_blk = (j % tiles_per_group) * (t // SLC_BLOCK)
        expand = jnp.where(lane_col == first_blk + tok_blk, 1.0, 0.0).astype(BF16)
        lhs = jnp.concatenate([k, expand], axis=1)
        for h in range(2):
            blk_bias = bias_ref[h, grp]
            for g in range(NSA_GROUP):
                rhs = jnp.concatenate([qs[2 * g + h], blk_bias, zero_rows], axis=0)
                s_bufs[buf][2 * g + h] = _dot(lhs, rhs)

    def consume(j, buf, diag):
        vt = _with_ones_rows(vst_ref[:, pl.ds(pl.multiple_of(j * t, t), t)])
        for idx in range(n_heads):
            _online_update(s_bufs[buf], vt, m_ref, acc_ref, idx, diag)

    _pipelined_tiles(i, scores, consume)

    n_win = WIN // t + 1
    k_win = jnp.concatenate([kw0_ref[...], kw1_ref[...], kw2_ref[...]], axis=0)
    vt_win = jnp.concatenate([vwt0_ref[...], vwt1_ref[...], vwt2_ref[...]], axis=1)
    kpos = (i - (n_win - 1)) * t + lax.broadcasted_iota(jnp.int32, (n_win * t, t), 0)
    qpos = i * t + lax.broadcasted_iota(jnp.int32, (n_win * t, t), 1)
    wmask = jnp.logical_and(jnp.logical_and(kpos <= qpos, kpos > qpos - WIN), kpos >= 0)
    ow = []
    for idx in range(n_heads):
        s = jnp.where(wmask, _dot(k_win, qs[idx]), NEG)
        p = jnp.exp2(s - jnp.max(s, axis=0, keepdims=True))
        l = jnp.sum(p, axis=0, keepdims=True)
        ow.append(_dot(vt_win, p.astype(BF16)) / l)

    gates = jax.nn.sigmoid(_dot(et_ref[0], gzt_ref[...]))
    gw = NSA_GROUP * LANES
    for g in range(NSA_GROUP):
        rows = slice(g * LANES, (g + 1) * LANES)
        o_cmp = oct_ref[rows, :].astype(F32)
        o_slc = jnp.where(low, _online_result(acc_ref, 2 * g), _online_result(acc_ref, 2 * g + 1))
        o_win = jnp.where(low, ow[2 * g], ow[2 * g + 1])
        out = (gates[g * LANES:(g + 1) * LANES, :] * o_cmp
               + gates[gw + g * LANES:gw + (g + 1) * LANES, :] * o_slc
               + gates[2 * gw + g * LANES:2 * gw + (g + 1) * LANES, :] * o_win)
        ot_ref[rows, :] = out.astype(ot_ref.dtype)


def _nsa_sel(nat, proj_t, selt, oct, gate_expand_t):
    S = nat.shape[0]
    t = NSA_TILE
    n_grp = selt.shape[1]
    n_heads = 2 * NSA_GROUP
    gl = NSA_GROUP * LANES

    def kwin_spec(back):
        return pl.BlockSpec((t, LANES), lambda p, i: (jnp.maximum(i - back, 0), 6 + p))

    def vwin_spec(back):
        return pl.BlockSpec((LANES, t), lambda p, i: (10 + p, jnp.maximum(i - back, 0)))

    return pl.pallas_call(
        functools.partial(_nsa_sel_kernel, t=t),
        grid=(NSA_KV_HEADS // 2, S // t),
        in_specs=[
            pl.BlockSpec((gl, t), lambda p, i: (p, i)),
            pl.BlockSpec((S, LANES), lambda p, i: (0, 4 + p)),
            pl.BlockSpec((LANES, S), lambda p, i: (8 + p, 0)),
            pl.BlockSpec((2, n_grp, SEL_GROUP, t), lambda p, i: (p, 0, 0, i)),
            kwin_spec(2), kwin_spec(1), kwin_spec(0),
            vwin_spec(2), vwin_spec(1), vwin_spec(0),
            pl.BlockSpec((NSA_PROJ_TILE, t), lambda p, i: (6, i)),
            pl.BlockSpec((1, 3 * gl, NSA_PROJ_TILE), lambda p, i: (p, 0, 0)),
            pl.BlockSpec((gl, t), lambda p, i: (p, i)),
        ],
        out_specs=pl.BlockSpec((gl, t), lambda p, i: (p, i)),
        out_shape=jax.ShapeDtypeStruct((D_MODEL, S), BF16),
        scratch_shapes=([pltpu.VMEM((n_heads, t, t), F32)] * 2 + [pltpu.VMEM((2, n_grp, SEL_GROUP, t), BF16)]
                        + [pltpu.VMEM((n_heads, 1, t), F32)] + [pltpu.VMEM((n_heads, LANES + SUM_ROWS, t), F32)]),
        compiler_params=_cparams(("parallel", "arbitrary")),
        name="nsa_sel",
    )(proj_t, nat, proj_t, selt, nat, nat, nat, proj_t, proj_t, proj_t, proj_t, gate_expand_t, oct)


def _rope_tables(positions):
    half = ROT_DIM // 2
    inv = ROPE_THETA ** (-jnp.arange(0, ROT_DIM, 2, dtype=F32) / ROT_DIM)
    ang = positions.astype(F32)[:, None] * inv[None, :]
    cos, sin = jnp.cos(ang), jnp.sin(ang)
    S = positions.shape[0]
    ones = jnp.ones((S, HEAD_DIM - ROT_DIM), F32)
    zeros_h = jnp.zeros((S, half), F32)
    zeros_r = jnp.zeros((S, HEAD_DIM - ROT_DIM), F32)
    rc = jnp.concatenate([cos, cos, ones], axis=1)
    rs1 = jnp.concatenate([-sin, zeros_h, zeros_r], axis=1)
    rs2 = jnp.concatenate([zeros_h, sin, zeros_r], axis=1)
    reps = LANES // HEAD_DIM
    return jnp.tile(rc, (1, reps)), jnp.tile(rs1, (1, reps)), jnp.tile(rs2, (1, reps))


def _pair_heads(w, axis):
    shape = w.shape
    w = w.reshape(shape[:axis] + (NSA_KV_HEADS // 2, 2, NSA_GROUP, HEAD_DIM) + shape[axis + 1:])
    w = jnp.swapaxes(w, axis + 1, axis + 2)
    return w.reshape(shape)


def _nsa_constants(S):
    nc = S // CMP_STRIDE
    n_cmp = (S - CMP_BLOCK) // CMP_STRIDE + 1
    ns = S // SLC_BLOCK
    nsp = -(-ns // LANES) * LANES
    ci = np.arange(nc)[None, :] * CMP_STRIDE
    st = np.arange(nsp)[:, None] * SLC_BLOCK
    ovt = (ci < st + SLC_BLOCK) & (ci + CMP_BLOCK > st) & (np.arange(nc)[None, :] < n_cmp) & (np.arange(nsp)[:, None] < ns)
    e = np.zeros((NSA_KV_HEADS // 2, 3 * NSA_GROUP * LANES, NSA_PROJ_TILE), np.float32)
    for p in range(NSA_KV_HEADS // 2):
        for br in range(3):
            for g in range(NSA_GROUP):
                for hf in range(2):
                    src = br * N_HEADS + (2 * p + hf) * NSA_GROUP + g
                    r0 = br * NSA_GROUP * LANES + g * LANES + hf * HEAD_DIM
                    e[p, r0:r0 + HEAD_DIM, src] = 1.0
    cmp_end = np.minimum(np.arange(nc) * CMP_STRIDE + CMP_BLOCK - 1, S - 1)
    return nc, jnp.asarray(ovt.astype(np.float32), BF16), jnp.asarray(e, BF16), cmp_end


def _nsa_layer(h, g, positions, w_in, pe_k, w1_k, w2_k, pe_v, w1_v, w2_v, w_o):
    S, D = h.shape
    qd = N_HEADS * HEAD_DIM
    kvd = NSA_KV_HEADS * HEAD_DIM
    nc, ovt, gate_expand_t, cmp_end = _nsa_constants(S)
    rc, rs1, rs2 = _rope_tables(positions)

    def kv_piece(n):
        return w_in[:, qd + n * kvd:qd + (n + 1) * kvd]

    w_nat = jnp.concatenate([kv_piece(0), kv_piece(1), kv_piece(2), kv_piece(4)], axis=1).astype(BF16)
    wg = jnp.pad(w_in[:, qd + 6 * kvd:], ((0, 0), (0, NSA_PROJ_TILE - 3 * N_HEADS)))
    w_t = jnp.concatenate([_pair_heads(w_in[:, :qd], 1) * QK_SCALE, kv_piece(3), kv_piece(5), wg], axis=1).T.astype(BF16)
    nat = _proj(h, g, w_nat, "nsa_proj", key_major=False, tn=NSA_PROJ_TILE,
                rope=(rc, rs1, rs2), rope_tiles=NSA_NAT_ROPE_TILES)
    proj_t = _proj(h, g, w_t, "nsa_proj_t", key_major=True, tn=NSA_PROJ_TILE,
                   rope=(rc.T, rs1.T, rs2.T), rope_tiles=NSA_T_ROPE_TILES)

    raw = nat[:, :2 * kvd].reshape(nc, CMP_STRIDE, 2, NSA_KV_HEADS, HEAD_DIM)
    raw = raw.transpose(2, 3, 0, 1, 4).reshape(2, NSA_KV_HEADS, nc, CMP_STRIDE * HEAD_DIM)
    w1 = jnp.stack([w1_k, w1_v]).astype(BF16)
    pe = jnp.stack([pe_k.reshape(1, -1), pe_v.reshape(1, -1)])
    pe = jnp.pad(pe, ((0, 0), (0, 7), (0, 0))).astype(BF16)
    w2 = jnp.stack([w2_k, w2_v])
    eye = jnp.eye(NSA_KV_HEADS, dtype=F32)
    w2p = (w2[:, None, :, None, :] * eye[None, :, None, :, None]).reshape(2, NSA_KV_HEADS, CMP_HIDDEN, kvd).astype(BF16)
    kvc = _nsa_cmp(raw, w1, pe, w2p, rc[cmp_end], rs1[cmp_end], rs2[cmp_end])

    oct, selt = _nsa_cmp_attn(proj_t, kvc, kvc[1].T, ovt)
    selt = selt.reshape(NSA_KV_HEADS, -1, SEL_GROUP, S)
    ot = _nsa_sel(nat, proj_t, selt, oct, gate_expand_t)
    return _matmul_res(ot, _pair_heads(w_o, 0).astype(BF16), h, "nsa_out")


def _trunk(x2, positions, p):
    h = _fox_layer(x2, p["l0_norm_mix"], p["l0_fox_w_qkv"], p["l0_fox_w_f"], p["l0_fox_b_f"], p["l0_fox_w_o"])
    h = _mlp(h, p["l0_norm_mlp"], p["l0_mlp_w1"].astype(BF16), p["l0_mlp_w2"].astype(BF16))
    h = _pool_layer(h, p["l1_norm_mix"], p["l1_pool_w"], p["l1_pool_scale"])
    h = _mlp(h, p["l1_norm_mlp"], p["l1_mlp_w1"].astype(BF16), p["l1_mlp_w2"].astype(BF16))
    h = _conv_layer(h, p["l2_norm_mix"], p["l2_conv_w_in"], p["l2_conv_w"], p["l2_conv_w_out"])
    h = _mlp(h, p["l2_norm_mlp"], p["l2_mlp_w1"].astype(BF16), p["l2_mlp_w2"].astype(BF16))
    h = _nsa_layer(h, p["l3_norm_mix"], positions, p["l3_nsa_w_in"], p["l3_nsa_cmp_pe_k"], p["l3_nsa_cmp_w1_k"],
                   p["l3_nsa_cmp_w2_k"], p["l3_nsa_cmp_pe_v"], p["l3_nsa_cmp_w1_v"], p["l3_nsa_cmp_w2_v"],
                   p["l3_nsa_w_o"])
    return _mlp(h, p["l3_norm_mlp"], p["l3_mlp_w1"].astype(BF16), p["l3_mlp_w2"].astype(BF16), p["final_norm"])


def kernel(x, positions, l0_norm_mix, l0_fox_w_qkv, l0_fox_w_f, l0_fox_b_f, l0_fox_w_o, l0_norm_mlp, l0_mlp_w1, l0_mlp_w2, l1_norm_mix, l1_pool_w, l1_pool_scale, l1_norm_mlp, l1_mlp_w1, l1_mlp_w2, l2_norm_mix, l2_conv_w_in, l2_conv_w, l2_conv_w_out, l2_norm_mlp, l2_mlp_w1, l2_mlp_w2, l3_norm_mix, l3_nsa_w_in, l3_nsa_cmp_pe_k, l3_nsa_cmp_w1_k, l3_nsa_cmp_w2_k, l3_nsa_cmp_pe_v, l3_nsa_cmp_w1_v, l3_nsa_cmp_w2_v, l3_nsa_w_o, l3_norm_mlp, l3_mlp_w1, l3_mlp_w2, final_norm):
    params = dict(locals())
    B, S, D = x.shape
    outs = [_trunk(x[b], positions, params) for b in range(B)]
    return jnp.stack(outs, axis=0)
```

```python
import functools

import numpy as np
import jax
import jax.numpy as jnp
from jax import lax
from jax.experimental import pallas as pl
from jax.experimental.pallas import tpu as pltpu

F32 = jnp.float32
BF16 = jnp.bfloat16

D_MODEL = 1024
HEAD_DIM = 64
N_HEADS = D_MODEL // HEAD_DIM
D_FF = 4 * D_MODEL
ROPE_THETA = 500000.0
ROT_DIM = HEAD_DIM // 4
RMS_EPS = 1e-6
POOL_WINDOWS = (2, 4, 8, 16)
POOL_GROUP = D_MODEL // len(POOL_WINDOWS)
POOL_HALO = 16
CONV_WIDTH = 3
CONV_HALO = 8
NSA_KV_HEADS = 4
NSA_GROUP = N_HEADS // NSA_KV_HEADS
CMP_BLOCK = 32
CMP_STRIDE = 16
CMP_HIDDEN = 256
SLC_BLOCK = 64
SLC_SHIFT = 6
SLC_TOPK = 16
WIN = 512
FORCE_SCORE = 1e9
NEG = -1e30
LOG2E = 1.4426950408889634
QK_SCALE = HEAD_DIM ** -0.5 * LOG2E

LANES = 128
V7X_VMEM_LIMIT = 56 * 1024 * 1024
FOX_TILE = 512
FOX_BIAS_TERMS = 3
NSA_TILE = WIN // 2
SEL_GROUP = 16
SUM_ROWS = 16
CMP_SPANS = 4
NSA_PROJ_TILE = 2 * LANES
NSA_NAT_WIDTH = 8 * LANES
NSA_NAT_ROPE_TILES = (1,)
NSA_T_ROWS = 14 * LANES
NSA_T_ROPE_TILES = (0, 1, 2, 3)


def _cparams(semantics):
    return pltpu.CompilerParams(dimension_semantics=semantics, vmem_limit_bytes=V7X_VMEM_LIMIT)


def _rms(x, g):
    return x * lax.rsqrt(jnp.mean(x * x, axis=-1, keepdims=True) + RMS_EPS) * g


def _dot(a, b):
    return jnp.dot(a, b, preferred_element_type=F32)


def _dot_nt(a, b):
    return lax.dot_general(a, b, (((1,), (1,)), ((), ())), preferred_element_type=F32)


def _dot_tn(a, b):
    return lax.dot_general(a, b, (((0,), (0,)), ((), ())), preferred_element_type=F32)


def _split3(x):
    hi = x.astype(BF16)
    r = x - hi.astype(F32)
    mid = r.astype(BF16)
    lo = (r - mid.astype(F32)).astype(BF16)
    return hi, mid, lo


def _rope(a, rc, rs1, rs2, axis):
    half = ROT_DIM // 2
    return a * rc + pltpu.roll(a, LANES - half, axis) * rs1 + pltpu.roll(a, half, axis) * rs2


def _head_rows(qt_blk):
    low = lax.broadcasted_iota(jnp.int32, qt_blk.shape, 0) < HEAD_DIM
    qf = qt_blk.astype(F32)
    return jnp.where(low, qf, 0.0).astype(BF16), jnp.where(low, 0.0, qf).astype(BF16)


def _with_ones_rows(vt):
    return jnp.concatenate([vt, jnp.ones((SUM_ROWS, vt.shape[1]), vt.dtype)], axis=0)


def _online_update(s_ref, vt_ones, m_ref, acc_ref, idx, causal):
    tk, t = s_ref.shape[1:]
    s = s_ref[idx]
    if causal:
        s = jnp.where(lax.broadcasted_iota(jnp.int32, (tk, t), 0) <= lax.broadcasted_iota(jnp.int32, (tk, t), 1),
                      s, NEG)
    m_old = m_ref[idx]
    m_new = jnp.maximum(m_old, jnp.max(s, axis=0, keepdims=True))
    p = jnp.exp2(s - m_new).astype(BF16)
    m_ref[idx] = m_new
    acc_ref[idx] = jnp.exp2(m_old - m_new) * acc_ref[idx] + _dot(vt_ones, p)


def _online_result(acc_ref, idx):
    acc = acc_ref[idx]
    return acc[:LANES, :] / acc[LANES:LANES + 1, :]


def _pipelined_tiles(i, scores, consume, pairs_per_step=1):
    scores(0, 0)

    def pair(j):
        scores(j + 1, 1)
        consume(j, 0, False)
        scores(j + 2, 0)
        consume(j + 1, 1, False)

    def body(jj, carry):
        for u in range(pairs_per_step):
            pair(2 * (pairs_per_step * jj + u))
        return carry

    n_pairs = i // 2
    n_steps = n_pairs // pairs_per_step
    lax.fori_loop(0, n_steps, body, 0)
    if pairs_per_step > 1:
        def tail(jj, carry):
            pair(2 * jj)
            return carry

        lax.fori_loop(n_steps * pairs_per_step, n_pairs, tail, 0)

    @pl.when(i % 2 == 0)
    def _():
        consume(i, 0, True)

    @pl.when(i % 2 == 1)
    def _():
        scores(i, 1)
        consume(i - 1, 0, False)
        consume(i, 1, True)


def _mlp_kernel(x_ref, g_ref, w1_ref, w2_ref, *rest, nf, final):
    if final:
        fg_ref, o_ref, xn_ref = rest
    else:
        o_ref, xn_ref = rest
    f = pl.program_id(1)

    @pl.when(f == 0)
    def _():
        x = x_ref[...]
        xn_ref[...] = _rms(x, g_ref[...]).astype(BF16)
        o_ref[...] = x

    a = _dot(xn_ref[...], w1_ref[...])
    a = jnp.square(jnp.maximum(a, 0.0)).astype(BF16)
    o_ref[...] += _dot(a, w2_ref[...])

    if final:
        @pl.when(f == nf - 1)
        def _():
            o_ref[...] = _rms(o_ref[...], fg_ref[...])


def _mlp(h, g, w1, w2, final_g=None):
    S, D = h.shape
    F = w1.shape[1]
    tm = min(1024, S)
    tf = 2048
    nf = F // tf
    final = final_g is not None
    in_specs = [
        pl.BlockSpec((tm, D), lambda i, f: (i, 0)),
        pl.BlockSpec((1, D), lambda i, f: (0, 0)),
        pl.BlockSpec((D, tf), lambda i, f: (0, f)),
        pl.BlockSpec((tf, D), lambda i, f: (f, 0)),
    ]
    args = [h, g.reshape(1, D), w1, w2]
    if final:
        in_specs.append(pl.BlockSpec((1, D), lambda i, f: (0, 0)))
        args.append(final_g.reshape(1, D))
    return pl.pallas_call(
        functools.partial(_mlp_kernel, nf=nf, final=final),
        grid=(S // tm, nf),
        in_specs=in_specs,
        out_specs=pl.BlockSpec((tm, D), lambda i, f: (i, 0)),
        out_shape=jax.ShapeDtypeStruct((S, D), F32),
        scratch_shapes=[pltpu.VMEM((tm, D), BF16)],
        compiler_params=_cparams(("parallel", "arbitrary")),
        name="mlp",
    )(*args)


def _proj_kernel(x_ref, g_ref, w_ref, *rest, key_major, rope_tiles):
    if rope_tiles:
        rc_ref, rs1_ref, rs2_ref, o_ref, xn_ref = rest
    else:
        o_ref, xn_ref = rest
    j = pl.program_id(1)

    @pl.when(j == 0)
    def _():
        xn_ref[...] = _rms(x_ref[...], g_ref[...]).astype(BF16)

    a = _dot_nt(w_ref[...], xn_ref[...]) if key_major else _dot(xn_ref[...], w_ref[...])
    if not rope_tiles:
        o_ref[...] = a.astype(o_ref.dtype)
        return
    axis = 0 if key_major else 1
    is_rope = functools.reduce(jnp.logical_or, [j == t for t in rope_tiles])

    @pl.when(is_rope)
    def _():
        rc, rs1, rs2 = rc_ref[...], rs1_ref[...], rs2_ref[...]
        blocks = [lax.slice_in_dim(a, b * LANES, (b + 1) * LANES, axis=axis) for b in range(a.shape[axis] // LANES)]
        o_ref[...] = jnp.concatenate([_rope(blk, rc, rs1, rs2, axis) for blk in blocks], axis=axis).astype(o_ref.dtype)

    @pl.when(jnp.logical_not(is_rope))
    def _():
        o_ref[...] = a.astype(o_ref.dtype)


def _proj(h, g, w, name, *, key_major, tn, rope=None, rope_tiles=()):
    S, D = h.shape
    tm = min(1024, S)
    if key_major:
        N = w.shape[0]
        w_spec = pl.BlockSpec((tn, D), lambda i, j: (j, 0))
        rope_spec = pl.BlockSpec((LANES, tm), lambda i, j: (0, i))
        out_spec = pl.BlockSpec((tn, tm), lambda i, j: (j, i))
        out_shape = jax.ShapeDtypeStruct((N, S), BF16)
    else:
        N = w.shape[1]
        w_spec = pl.BlockSpec((D, tn), lambda i, j: (0, j))
        rope_spec = pl.BlockSpec((tm, LANES), lambda i, j: (i, 0))
        out_spec = pl.BlockSpec((tm, tn), lambda i, j: (i, j))
        out_shape = jax.ShapeDtypeStruct((S, N), BF16)
    in_specs = [pl.BlockSpec((tm, D), lambda i, j: (i, 0)), pl.BlockSpec((1, D), lambda i, j: (0, 0)), w_spec]
    args = [h, g.reshape(1, D), w]
    if rope_tiles:
        in_specs += [rope_spec] * 3
        args += list(rope)
    return pl.pallas_call(
        functools.partial(_proj_kernel, key_major=key_major, rope_tiles=tuple(rope_tiles)),
        grid=(S // tm, N // tn),
        in_specs=in_specs,
        out_specs=out_spec,
        out_shape=out_shape,
        scratch_shapes=[pltpu.VMEM((tm, D), BF16)],
        compiler_params=_cparams(("parallel", "arbitrary")),
        name=name,
    )(*args)


def _matmul_res_kernel(at_ref, w_ref, r_ref, o_ref):
    o_ref[...] = r_ref[...] + _dot_tn(at_ref[...], w_ref[...])


def _matmul_res(at, w, res, name):
    K, S = at.shape
    N = w.shape[1]
    tm = min(1024, S)
    return pl.pallas_call(
        _matmul_res_kernel,
        grid=(S // tm,),
        in_specs=[
            pl.BlockSpec((K, tm), lambda i: (0, i)),
            pl.BlockSpec((K, N), lambda i: (0, 0)),
            pl.BlockSpec((tm, N), lambda i: (i, 0)),
        ],
        out_specs=pl.BlockSpec((tm, N), lambda i: (i, 0)),
        out_shape=jax.ShapeDtypeStruct((S, N), F32),
        compiler_params=_cparams(("parallel",)),
        name=name,
    )(at, w, res)


def _fox_gate_kernel(x_ref, g_ref, wf_ref, bf_ref, place_ref, c_ref, carry_ref, *, tm):
    @pl.when(pl.program_id(0) == 0)
    def _():
        carry_ref[...] = jnp.zeros_like(carry_ref)

    xn = _rms(x_ref[...], g_ref[...]).astype(BF16)
    z = _dot(xn, wf_ref[...]) + bf_ref[...]
    logf = jnp.minimum(z, 0.0) - jnp.log(1.0 + jnp.exp(-jnp.abs(z)))
    row = lax.broadcasted_iota(jnp.int32, (tm, tm), 0)
    col = lax.broadcasted_iota(jnp.int32, (tm, tm), 1)
    tri = jnp.where(row >= col, 1.0, 0.0).astype(BF16)
    hi, mid, lo = _split3(logf)
    c = _dot(tri, hi) + _dot(tri, mid) + _dot(tri, lo) + carry_ref[...]
    carry_ref[...] = c[tm - 1:tm, :]
    terms = jnp.concatenate(_split3(c * -LOG2E), axis=1)
    c_ref[...] = _dot(terms, place_ref[...]).astype(c_ref.dtype)


def _fox_gate(h, g, wf, bf, place):
    S, D = h.shape
    tm = min(512, S)
    return pl.pallas_call(
        functools.partial(_fox_gate_kernel, tm=tm),
        grid=(S // tm,),
        in_specs=[
            pl.BlockSpec((tm, D), lambda i: (i, 0)),
            pl.BlockSpec((1, D), lambda i: (0, 0)),
            pl.BlockSpec((D, LANES), lambda i: (0, 0)),
            pl.BlockSpec((1, LANES), lambda i: (0, 0)),
            pl.BlockSpec((FOX_BIAS_TERMS * LANES, D_MODEL), lambda i: (0, 0)),
        ],
        out_specs=pl.BlockSpec((tm, D_MODEL), lambda i: (i, 0)),
        out_shape=jax.ShapeDtypeStruct((S, D_MODEL), BF16),
        scratch_shapes=[pltpu.VMEM((1, LANES), F32)],
        compiler_params=_cparams(("arbitrary",)),
        name="fox_gate",
    )(h, g.reshape(1, D), wf, bf, place)


def _fox_attn_kernel(qt_ref, k_ref, c_ref, vt_ref, ot_ref, s0_ref, s1_ref, m_ref, acc_ref, *, t):
    i = pl.program_id(1)
    row = lax.broadcasted_iota(jnp.int32, (LANES, t), 0)
    q_heads = _head_rows(qt_ref[...])
    qa = []
    for h in range(2):
        ones_rows = jnp.logical_and(row >= FOX_BIAS_TERMS * h, row < FOX_BIAS_TERMS * (h + 1))
        qa.append(jnp.concatenate([q_heads[h], jnp.where(ones_rows, 1.0, 0.0).astype(BF16)], axis=0))
    m_ref[...] = jnp.full(m_ref.shape, NEG, F32)
    acc_ref[...] = jnp.zeros(acc_ref.shape, F32)
    s_bufs = (s0_ref, s1_ref)

    def scores(j, buf):
        rows = pl.ds(pl.multiple_of(j * t, t), t)
        lhs = jnp.concatenate([k_ref[rows, :], c_ref[rows, :]], axis=1)
        for h in range(2):
            s_bufs[buf][h] = _dot(lhs, qa[h])

    def consume(j, buf, diag):
        vt = _with_ones_rows(vt_ref[:, pl.ds(pl.multiple_of(j * t, t), t)])
        for h in range(2):
            _online_update(s_bufs[buf], vt, m_ref, acc_ref, h, diag)

    _pipelined_tiles(i, scores, consume, pairs_per_step=2)
    ot = jnp.where(row < HEAD_DIM, _online_result(acc_ref, 0), _online_result(acc_ref, 1))
    ot_ref[...] = ot.astype(ot_ref.dtype)


def _fox_attn(qvt, k, c_terms):
    S = k.shape[0]
    t = min(FOX_TILE, S)
    n_pairs = N_HEADS // 2
    return pl.pallas_call(
        functools.partial(_fox_attn_kernel, t=t),
        grid=(n_pairs, S // t),
        in_specs=[
            pl.BlockSpec((LANES, t), lambda hp, i: (hp, i)),
            pl.BlockSpec((S, LANES), lambda hp, i: (0, hp)),
            pl.BlockSpec((S, LANES), lambda hp, i: (0, hp)),
            pl.BlockSpec((LANES, S), lambda hp, i: (n_pairs + hp, 0)),
        ],
        out_specs=pl.BlockSpec((LANES, t), lambda hp, i: (hp, i)),
        out_shape=jax.ShapeDtypeStruct((D_MODEL, S), BF16),
        scratch_shapes=([pltpu.VMEM((2, t, t), F32)] * 2 + [pltpu.VMEM((2, 1, t), F32)]
                        + [pltpu.VMEM((2, LANES + SUM_ROWS, t), F32)]),
        compiler_params=_cparams(("parallel", "arbitrary")),
        name="fox_attn",
    )(qvt, k, c_terms, qvt)


def _fox_bias_placement():
    place = np.zeros((FOX_BIAS_TERMS * LANES, D_MODEL), np.float32)
    for head in range(N_HEADS):
        for n in range(FOX_BIAS_TERMS):
            place[n * LANES + head, (head // 2) * LANES + FOX_BIAS_TERMS * (head % 2) + n] = 1.0
    return jnp.asarray(place, BF16)


def _fox_layer(h, g, w_qkv, w_f, b_f, w_o):
    wf = jnp.pad(w_f, ((0, 0), (0, LANES - N_HEADS))).astype(BF16)
    bf = jnp.pad(b_f, (0, LANES - N_HEADS)).reshape(1, LANES)
    c_terms = _fox_gate(h, g, wf, bf, _fox_bias_placement())
    w_qv = jnp.concatenate([w_qkv[:, :D_MODEL] * QK_SCALE, w_qkv[:, 2 * D_MODEL:]], axis=1).T.astype(BF16)
    qvt = _proj(h, g, w_qv, "fox_qv", key_major=True, tn=1024)
    k = _proj(h, g, w_qkv[:, D_MODEL:2 * D_MODEL].astype(BF16), "fox_k", key_major=False, tn=1024)
    ot = _fox_attn(qvt, k, c_terms)
    return _matmul_res(ot, w_o.astype(BF16), h, "fox_out")


def _pool_kernel(x_ref, halo_ref, g_ref, w_ref, sc_ref, o_ref, *, tm):
    i = pl.program_id(0)
    x = x_ref[...]
    g = g_ref[...]
    xn = _rms(x, g)
    hn = jnp.where(i > 0, _rms(halo_ref[...], g), 0.0)
    xe = jnp.concatenate([hn, xn], axis=0)
    tpos = i * tm + lax.broadcasted_iota(jnp.int32, (tm, 1), 0)
    for gi, w in enumerate(POOL_WINDOWS):
        sl = slice(gi * POOL_GROUP, (gi + 1) * POOL_GROUP)
        s = xe[:, sl]
        k = 1
        while k < w:
            s = s + pltpu.roll(s, k, 0)
            k *= 2
        cnt = jnp.minimum(tpos + 1, w).astype(F32)
        d = (s[POOL_HALO:, :] / cnt - xn[:, sl]).astype(BF16)
        o_ref[:, sl] = x[:, sl] + _dot(d, w_ref[gi]) * sc_ref[:, sl]


def _pool_layer(h, g, w_pool, pool_scale):
    S, D = h.shape
    tm = min(1024, S)
    return pl.pallas_call(
        functools.partial(_pool_kernel, tm=tm),
        grid=(S // tm,),
        in_specs=[
            pl.BlockSpec((tm, D), lambda i: (i, 0)),
            pl.BlockSpec((POOL_HALO, D), lambda i: (jnp.maximum(i * (tm // POOL_HALO) - 1, 0), 0)),
            pl.BlockSpec((1, D), lambda i: (0, 0)),
            pl.BlockSpec((len(POOL_WINDOWS), POOL_GROUP, POOL_GROUP), lambda i: (0, 0, 0)),
            pl.BlockSpec((1, D), lambda i: (0, 0)),
        ],
        out_specs=pl.BlockSpec((tm, D), lambda i: (i, 0)),
        out_shape=jax.ShapeDtypeStruct((S, D), F32),
        compiler_params=_cparams(("parallel",)),
        name="pool",
    )(h, h, g.reshape(1, D), w_pool.astype(BF16), pool_scale.reshape(1, D))


def _conv_in_kernel(x_ref, g_ref, wb_ref, wc_ref, wu_ref, b_ref, z_ref, xn_ref):
    @pl.when(pl.program_id(1) == 0)
    def _():
        xn_ref[...] = _rms(x_ref[...], g_ref[...]).astype(BF16)

    xn = xn_ref[...]
    b_ref[...] = _dot(xn, wb_ref[...])
    z_ref[...] = _dot(xn, wc_ref[...]) * _dot(xn, wu_ref[...])


def _conv_out_kernel(b_ref, z_ref, zh_ref, cw_ref, w_ref, r_ref, o_ref, *, tm):
    i = pl.program_id(0)
    z = z_ref[...]
    zh = jnp.where(i > 0, zh_ref[...], 0.0)
    row = lax.broadcasted_iota(jnp.int32, (tm, 1), 0)
    prev1 = zh[CONV_HALO - 1:CONV_HALO, :]
    prev2 = zh[CONV_HALO - 2:CONV_HALO - 1, :]
    z1 = jnp.where(row == 0, prev1, pltpu.roll(z, 1, 0))
    z2 = jnp.where(row == 0, prev2, jnp.where(row == 1, prev1, pltpu.roll(z, 2, 0)))
    cw = cw_ref[...]
    conv = cw[0:1, :] * z2 + cw[1:2, :] * z1 + cw[2:3, :] * z
    y = (b_ref[...] * conv).astype(BF16)
    o_ref[...] = r_ref[...] + _dot(y, w_ref[...])


def _conv_layer(h, g, w_in, conv_w, w_out):
    S, D = h.shape
    tm = min(1024, S)
    tn = 512
    nj = D // tn
    w_in = w_in.astype(BF16)
    b, z = pl.pallas_call(
        _conv_in_kernel,
        grid=(S // tm, nj),
        in_specs=[
            pl.BlockSpec((tm, D), lambda i, j: (i, 0)),
            pl.BlockSpec((1, D), lambda i, j: (0, 0)),
            pl.BlockSpec((D, tn), lambda i, j: (0, j)),
            pl.BlockSpec((D, tn), lambda i, j: (0, nj + j)),
            pl.BlockSpec((D, tn), lambda i, j: (0, 2 * nj + j)),
        ],
        out_specs=[pl.BlockSpec((tm, tn), lambda i, j: (i, j)),
                   pl.BlockSpec((tm, tn), lambda i, j: (i, j))],
        out_shape=[jax.ShapeDtypeStruct((S, D), F32), jax.ShapeDtypeStruct((S, D), F32)],
        scratch_shapes=[pltpu.VMEM((tm, D), BF16)],
        compiler_params=_cparams(("parallel", "arbitrary")),
        name="conv_in",
    )(h, g.reshape(1, D), w_in, w_in, w_in)
    cw = jnp.pad(conv_w, ((0, 8 - CONV_WIDTH), (0, 0)))
    return pl.pallas_call(
        functools.partial(_conv_out_kernel, tm=tm),
        grid=(S // tm,),
        in_specs=[
            pl.BlockSpec((tm, D), lambda i: (i, 0)),
            pl.BlockSpec((tm, D), lambda i: (i, 0)),
            pl.BlockSpec((CONV_HALO, D), lambda i: (jnp.maximum(i * (tm // CONV_HALO) - 1, 0), 0)),
            pl.BlockSpec((8, D), lambda i: (0, 0)),
            pl.BlockSpec((D, D), lambda i: (0, 0)),
            pl.BlockSpec((tm, D), lambda i: (i, 0)),
        ],
        out_specs=pl.BlockSpec((tm, D), lambda i: (i, 0)),
        out_shape=jax.ShapeDtypeStruct((S, D), F32),
        compiler_params=_cparams(("parallel",)),
        name="conv_out",
    )(b, z, z, cw, w_out.astype(BF16), h)


def _nsa_cmp_kernel(r_ref, w1_ref, pe_ref, w2_ref, rc_ref, rs1_ref, rs2_ref, o_ref, acc_ref, *, nc):
    kv = pl.program_id(0)
    hd = pl.program_id(1)
    half = CMP_STRIDE * HEAD_DIM
    r = r_ref[0, 0]
    w1 = w1_ref[0]
    first = _dot(r, w1[:half, :])
    second = _dot(r, w1[half:, :])
    pe_term = _dot(pe_ref[0], w1)[0:1, :]
    pre = first + pltpu.roll(second, nc - 1, 0) + pe_term
    ge = 0.5 * pre * (1.0 + jnp.tanh(0.7978845608028654 * (pre + 0.044715 * pre * pre * pre)))
    y = _dot(ge.astype(BF16), w2_ref[0, 0])

    @pl.when(hd == 0)
    def _():
        acc_ref[...] = y

    @pl.when(hd > 0)
    def _():
        acc_ref[...] += y

    @pl.when(hd == NSA_KV_HEADS - 1)
    def _():
        acc = acc_ref[...]

        @pl.when(kv == 0)
        def _():
            rc, rs1, rs2 = rc_ref[...], rs1_ref[...], rs2_ref[...]
            o_ref[0] = jnp.concatenate(
                [_rope(acc[:, :LANES], rc, rs1, rs2, 1), _rope(acc[:, LANES:], rc, rs1, rs2, 1)],
                axis=1).astype(o_ref.dtype)

        @pl.when(kv == 1)
        def _():
            o_ref[0] = acc.astype(o_ref.dtype)


def _nsa_cmp(r, w1, pe, w2p, rc, rs1, rs2):
    nc = r.shape[2]
    kvd = NSA_KV_HEADS * HEAD_DIM
    return pl.pallas_call(
        functools.partial(_nsa_cmp_kernel, nc=nc),
        grid=(2, NSA_KV_HEADS),
        in_specs=[
            pl.BlockSpec((1, 1, nc, CMP_STRIDE * HEAD_DIM), lambda a, b: (a, b, 0, 0)),
            pl.BlockSpec((1, CMP_BLOCK * HEAD_DIM, CMP_HIDDEN), lambda a, b: (a, 0, 0)),
            pl.BlockSpec((1, 8, CMP_BLOCK * HEAD_DIM), lambda a, b: (a, 0, 0)),
            pl.BlockSpec((1, 1, CMP_HIDDEN, kvd), lambda a, b: (a, b, 0, 0)),
            pl.BlockSpec((nc, LANES), lambda a, b: (0, 0)),
            pl.BlockSpec((nc, LANES), lambda a, b: (0, 0)),
            pl.BlockSpec((nc, LANES), lambda a, b: (0, 0)),
        ],
        out_specs=pl.BlockSpec((1, nc, kvd), lambda a, b: (a, 0, 0)),
        out_shape=jax.ShapeDtypeStruct((2, nc, kvd), BF16),
        scratch_shapes=[pltpu.VMEM((nc, kvd), F32)],
        compiler_params=_cparams(("parallel", "arbitrary")),
        name="nsa_cmp",
    )(r, w1, pe, w2p, rc, rs1, rs2)


def _nsa_cmp_attn_body(i, qt_ref, kc_ref, vct_ref, ovt_ref, oct_ref, selt_ref, *, t, nc, nsp):
    qpos = i * t + lax.broadcasted_iota(jnp.int32, (1, t), 1)
    cmp_end = CMP_STRIDE * lax.broadcasted_iota(jnp.int32, (nc, t), 0) + (CMP_BLOCK - 1)
    cmask = cmp_end <= qpos
    any_valid = qpos >= CMP_BLOCK - 1
    low = lax.broadcasted_iota(jnp.int32, (LANES, t), 0) < HEAD_DIM
    ovt = ovt_ref[:nsp, :nc]
    blk = lax.broadcasted_iota(jnp.int32, (nsp, t), 0)
    blk_f = blk.astype(F32)
    cur = jnp.right_shift(qpos, SLC_SHIFT)
    forced = jnp.logical_or(blk == 0, jnp.logical_or(blk == cur, blk == cur - 1))
    valid = blk * SLC_BLOCK <= qpos
    vals = []
    for pair in range(NSA_KV_HEADS // 2):
        kc = kc_ref[0, :nc, pair * LANES:(pair + 1) * LANES]
        vct = vct_ref[pair * LANES:(pair + 1) * LANES, :nc]
        imp = [jnp.zeros((nc, t), F32), jnp.zeros((nc, t), F32)]
        for g in range(NSA_GROUP):
            rows = slice((pair * NSA_GROUP + g) * LANES, (pair * NSA_GROUP + g + 1) * LANES)
            qh = _head_rows(qt_ref[rows, :])
            oc = []
            for h in range(2):
                s = jnp.where(cmask, _dot(kc, qh[h]), NEG)
                p = jnp.exp2(s - jnp.max(s, axis=0, keepdims=True))
                l = jnp.sum(p, axis=0, keepdims=True)
                pn = p * jnp.where(any_valid, 1.0 / l, 0.0)
                imp[h] = imp[h] + pn
                oc.append(_dot(vct, pn.astype(BF16)))
            oct_ref[rows, :] = jnp.where(low, oc[0], oc[1]).astype(oct_ref.dtype)
        for h in range(2):
            hi, mid, lo = _split3(imp[h])
            score = _dot(ovt, hi) + _dot(ovt, mid) + _dot(ovt, lo)
            vals.append(jnp.where(valid, jnp.where(forced, FORCE_SCORE, score), NEG))

    def pick_one(_, vals):
        out = []
        for v in vals:
            mx = jnp.max(v, axis=0, keepdims=True)
            first = jnp.min(jnp.where(v == mx, blk_f, float(nsp)), axis=0, keepdims=True)
            out.append(jnp.where(blk_f == first, -jnp.inf, v))
        return tuple(out)

    vals = lax.fori_loop(0, SLC_TOPK, pick_one, tuple(vals))
    for kvh in range(NSA_KV_HEADS):
        picked = jnp.logical_and(valid, vals[kvh] == -jnp.inf)
        selt_ref[kvh, :nsp, :] = jnp.where(picked, 1.0, 0.0).astype(selt_ref.dtype)
        if nsp < selt_ref.shape[1]:
            selt_ref[kvh, nsp:, :] = jnp.zeros((selt_ref.shape[1] - nsp, t), selt_ref.dtype)


def _nsa_cmp_attn_kernel(qt_ref, kc_ref, vct_ref, ovt_ref, oct_ref, selt_ref, *, t, n_steps, n_spans):
    i = pl.program_id(0)
    nsp, nc = ovt_ref.shape
    for q in range(n_spans):
        @pl.when(jnp.logical_and(i >= q * n_steps // n_spans, i < (q + 1) * n_steps // n_spans))
        def _(q=q):
            _nsa_cmp_attn_body(i, qt_ref, kc_ref, vct_ref, ovt_ref, oct_ref, selt_ref, t=t,
                               nc=(q + 1) * nc // n_spans, nsp=(q + 1) * nsp // n_spans)


def _nsa_cmp_attn(proj_t, kvc, vct, ovt):
    S = proj_t.shape[1]
    t = NSA_TILE
    nsp, nc = ovt.shape
    kvd = NSA_KV_HEADS * HEAD_DIM
    n_spans = max(1, min(CMP_SPANS, nc // NSA_PROJ_TILE))
    return pl.pallas_call(
        functools.partial(_nsa_cmp_attn_kernel, t=t, n_steps=S // t, n_spans=n_spans),
        grid=(S // t,),
        in_specs=[
            pl.BlockSpec((D_MODEL, t), lambda i: (0, i)),
            pl.BlockSpec((1, nc, kvd), lambda i: (0, 0, 0)),
            pl.BlockSpec((kvd, nc), lambda i: (0, 0)),
            pl.BlockSpec((nsp, nc), lambda i: (0, 0)),
        ],
        out_specs=[pl.BlockSpec((D_MODEL, t), lambda i: (0, i)),
                   pl.BlockSpec((NSA_KV_HEADS, nsp, t), lambda i: (0, 0, i))],
        out_shape=[jax.ShapeDtypeStruct((D_MODEL, S), BF16),
                   jax.ShapeDtypeStruct((NSA_KV_HEADS, nsp, S), BF16)],
        compiler_params=_cparams(("parallel",)),
        name="nsa_cmp_attn",
    )(proj_t, kvc, vct, ovt)


def _nsa_sel_kernel(qt_ref, ks_ref, vst_ref, selt_ref, kw0_ref, kw1_ref, kw2_ref, vwt0_ref, vwt1_ref, vwt2_ref,
                    gzt_ref, et_ref, oct_ref, ot_ref, s0_ref, s1_ref, bias_ref, m_ref, acc_ref, *, t):
    i = pl.program_id(1)
    n_heads = 2 * NSA_GROUP
    low = lax.broadcasted_iota(jnp.int32, (LANES, t), 0) < HEAD_DIM
    qs = []
    for g in range(NSA_GROUP):
        qs.extend(_head_rows(qt_ref[g * LANES:(g + 1) * LANES, :]))
    bias_ref[...] = ((selt_ref[...].astype(F32) - 1.0) * -NEG).astype(BF16)
    m_ref[...] = jnp.full(m_ref.shape, NEG, F32)
    acc_ref[...] = jnp.zeros(acc_ref.shape, F32)
    tok_blk = jnp.right_shift(lax.broadcasted_iota(jnp.int32, (t, LANES), 0), SLC_SHIFT)
    lane_col = lax.broadcasted_iota(jnp.int32, (t, LANES), 1)
    zero_rows = jnp.zeros((LANES - SEL_GROUP, t), BF16)
    tiles_per_group = SEL_GROUP * SLC_BLOCK // t
    s_bufs = (s0_ref, s1_ref)

    def scores(j, buf):
        k = ks_ref[pl.ds(pl.multiple_of(j * t, t), t), :]
        grp = j // tiles_per_group
        first_blk = (j % tiles_per_group) * (t // SLC_BLOCK)
        expand = jnp.where(lane_col == first_blk + tok_blk, 1.0, 0.0).astype(BF16)
        lhs = jnp.concatenate([k, expand], axis=1)
        for h in range(2):
            blk_bias = bias_ref[h, grp]
            for g in range(NSA_GROUP):
                rhs = jnp.concatenate([qs[2 * g + h], blk_bias, zero_rows], axis=0)
                s_bufs[buf][2 * g + h] = _dot(lhs, rhs)

    def consume(j, buf, diag):
        vt = _with_ones_rows(vst_ref[:, pl.ds(pl.multiple_of(j * t, t), t)])
        for idx in range(n_heads):
            _online_update(s_bufs[buf], vt, m_ref, acc_ref, idx, diag)

    _pipelined_tiles(i, scores, consume, pairs_per_step=2)

    n_win = WIN // t + 1
    k_win = jnp.concatenate([kw0_ref[...], kw1_ref[...], kw2_ref[...]], axis=0)
    vt_win = jnp.concatenate([vwt0_ref[...], vwt1_ref[...], vwt2_ref[...]], axis=1)
    kpos = (i - (n_win - 1)) * t + lax.broadcasted_iota(jnp.int32, (n_win * t, t), 0)
    qpos = i * t + lax.broadcasted_iota(jnp.int32, (n_win * t, t), 1)
    wmask = jnp.logical_and(jnp.logical_and(kpos <= qpos, kpos > qpos - WIN), kpos >= 0)
    ow = []
    for idx in range(n_heads):
        s = jnp.where(wmask, _dot(k_win, qs[idx]), NEG)
        p = jnp.exp2(s - jnp.max(s, axis=0, keepdims=True))
        l = jnp.sum(p, axis=0, keepdims=True)
        ow.append(_dot(vt_win, p.astype(BF16)) / l)

    gates = jax.nn.sigmoid(_dot(et_ref[0], gzt_ref[...]))
    gw = NSA_GROUP * LANES
    for g in range(NSA_GROUP):
        rows = slice(g * LANES, (g + 1) * LANES)
        o_cmp = oct_ref[rows, :].astype(F32)
        o_slc = jnp.where(low, _online_result(acc_ref, 2 * g), _online_result(acc_ref, 2 * g + 1))
        o_win = jnp.where(low, ow[2 * g], ow[2 * g + 1])
        out = (gates[g * LANES:(g + 1) * LANES, :] * o_cmp
               + gates[gw + g * LANES:gw + (g + 1) * LANES, :] * o_slc
               + gates[2 * gw + g * LANES:2 * gw + (g + 1) * LANES, :] * o_win)
        ot_ref[rows, :] = out.astype(ot_ref.dtype)


def _nsa_sel(nat, proj_t, selt, oct, gate_expand_t):
    S = nat.shape[0]
    t = NSA_TILE
    n_grp = selt.shape[1]
    n_heads = 2 * NSA_GROUP
    gl = NSA_GROUP * LANES

    def kwin_spec(back):
        return pl.BlockSpec((t, LANES), lambda p, i: (jnp.maximum(i - back, 0), 6 + p))

    def vwin_spec(back):
        return pl.BlockSpec((LANES, t), lambda p, i: (10 + p, jnp.maximum(i - back, 0)))

    return pl.pallas_call(
        functools.partial(_nsa_sel_kernel, t=t),
        grid=(NSA_KV_HEADS // 2, S // t),
        in_specs=[
            pl.BlockSpec((gl, t), lambda p, i: (p, i)),
            pl.BlockSpec((S, LANES), lambda p, i: (0, 4 + p)),
            pl.BlockSpec((LANES, S), lambda p, i: (8 + p, 0)),
            pl.BlockSpec((2, n_grp, SEL_GROUP, t), lambda p, i: (p, 0, 0, i)),
            kwin_spec(2), kwin_spec(1), kwin_spec(0),
            vwin_spec(2), vwin_spec(1), vwin_spec(0),
            pl.BlockSpec((NSA_PROJ_TILE, t), lambda p, i: (6, i)),
            pl.BlockSpec((1, 3 * gl, NSA_PROJ_TILE), lambda p, i: (p, 0, 0)),
            pl.BlockSpec((gl, t), lambda p, i: (p, i)),
        ],
        out_specs=pl.BlockSpec((gl, t), lambda p, i: (p, i)),
        out_shape=jax.ShapeDtypeStruct((D_MODEL, S), BF16),
        scratch_shapes=([pltpu.VMEM((n_heads, t, t), F32)] * 2 + [pltpu.VMEM((2, n_grp, SEL_GROUP, t), BF16)]
                        + [pltpu.VMEM((n_heads, 1, t), F32)] + [pltpu.VMEM((n_heads, LANES + SUM_ROWS, t), F32)]),
        compiler_params=_cparams(("parallel", "arbitrary")),
        name="nsa_sel",
    )(proj_t, nat, proj_t, selt, nat, nat, nat, proj_t, proj_t, proj_t, proj_t, gate_expand_t, oct)


def _rope_tables(positions):
    half = ROT_DIM // 2
    inv = ROPE_THETA ** (-jnp.arange(0, ROT_DIM, 2, dtype=F32) / ROT_DIM)
    ang = positions.astype(F32)[:, None] * inv[None, :]
    cos, sin = jnp.cos(ang), jnp.sin(ang)
    S = positions.shape[0]
    ones = jnp.ones((S, HEAD_DIM - ROT_DIM), F32)
    zeros_h = jnp.zeros((S, half), F32)
    zeros_r = jnp.zeros((S, HEAD_DIM - ROT_DIM), F32)
    rc = jnp.concatenate([cos, cos, ones], axis=1)
    rs1 = jnp.concatenate([-sin, zeros_h, zeros_r], axis=1)
    rs2 = jnp.concatenate([zeros_h, sin, zeros_r], axis=1)
    reps = LANES // HEAD_DIM
    return jnp.tile(rc, (1, reps)), jnp.tile(rs1, (1, reps)), jnp.tile(rs2, (1, reps))


def _pair_heads(w, axis):
    shape = w.shape
    w = w.reshape(shape[:axis] + (NSA_KV_HEADS // 2, 2, NSA_GROUP, HEAD_DIM) + shape[axis + 1:])
    w = jnp.swapaxes(w, axis + 1, axis + 2)
    return w.reshape(shape)


def _nsa_constants(S):
    nc = S // CMP_STRIDE
    n_cmp = (S - CMP_BLOCK) // CMP_STRIDE + 1
    ns = S // SLC_BLOCK
    nsp = -(-ns // LANES) * LANES
    ci = np.arange(nc)[None, :] * CMP_STRIDE
    st = np.arange(nsp)[:, None] * SLC_BLOCK
    ovt = (ci < st + SLC_BLOCK) & (ci + CMP_BLOCK > st) & (np.arange(nc)[None, :] < n_cmp) & (np.arange(nsp)[:, None] < ns)
    e = np.zeros((NSA_KV_HEADS // 2, 3 * NSA_GROUP * LANES, NSA_PROJ_TILE), np.float32)
    for p in range(NSA_KV_HEADS // 2):
        for br in range(3):
            for g in range(NSA_GROUP):
                for hf in range(2):
                    src = br * N_HEADS + (2 * p + hf) * NSA_GROUP + g
                    r0 = br * NSA_GROUP * LANES + g * LANES + hf * HEAD_DIM
                    e[p, r0:r0 + HEAD_DIM, src] = 1.0
    cmp_end = np.minimum(np.arange(nc) * CMP_STRIDE + CMP_BLOCK - 1, S - 1)
    return nc, jnp.asarray(ovt.astype(np.float32), BF16), jnp.asarray(e, BF16), cmp_end


def _nsa_layer(h, g, positions, w_in, pe_k, w1_k, w2_k, pe_v, w1_v, w2_v, w_o):
    S, D = h.shape
    qd = N_HEADS * HEAD_DIM
    kvd = NSA_KV_HEADS * HEAD_DIM
    nc, ovt, gate_expand_t, cmp_end = _nsa_constants(S)
    rc, rs1, rs2 = _rope_tables(positions)

    def kv_piece(n):
        return w_in[:, qd + n * kvd:qd + (n + 1) * kvd]

    w_nat = jnp.concatenate([kv_piece(0), kv_piece(1), kv_piece(2), kv_piece(4)], axis=1).astype(BF16)
    wg = jnp.pad(w_in[:, qd + 6 * kvd:], ((0, 0), (0, NSA_PROJ_TILE - 3 * N_HEADS)))
    w_t = jnp.concatenate([_pair_heads(w_in[:, :qd], 1) * QK_SCALE, kv_piece(3), kv_piece(5), wg], axis=1).T.astype(BF16)
    nat = _proj(h, g, w_nat, "nsa_proj", key_major=False, tn=2 * NSA_PROJ_TILE,
                rope=(rc, rs1, rs2), rope_tiles=NSA_NAT_ROPE_TILES)
    proj_t = _proj(h, g, w_t, "nsa_proj_t", key_major=True, tn=NSA_PROJ_TILE,
                   rope=(rc.T, rs1.T, rs2.T), rope_tiles=NSA_T_ROPE_TILES)

    raw = nat[:, :2 * kvd].reshape(nc, CMP_STRIDE, 2, NSA_KV_HEADS, HEAD_DIM)
    raw = raw.transpose(2, 3, 0, 1, 4).reshape(2, NSA_KV_HEADS, nc, CMP_STRIDE * HEAD_DIM)
    w1 = jnp.stack([w1_k, w1_v]).astype(BF16)
    pe = jnp.stack([pe_k.reshape(1, -1), pe_v.reshape(1, -1)])
    pe = jnp.pad(pe, ((0, 0), (0, 7), (0, 0))).astype(BF16)
    w2 = jnp.stack([w2_k, w2_v])
    eye = jnp.eye(NSA_KV_HEADS, dtype=F32)
    w2p = (w2[:, None, :, None, :] * eye[None, :, None, :, None]).reshape(2, NSA_KV_HEADS, CMP_HIDDEN, kvd).astype(BF16)
    kvc = _nsa_cmp(raw, w1, pe, w2p, rc[cmp_end], rs1[cmp_end], rs2[cmp_end])

    oct, selt = _nsa_cmp_attn(proj_t, kvc, kvc[1].T, ovt)
    selt = selt.reshape(NSA_KV_HEADS, -1, SEL_GROUP, S)
    ot = _nsa_sel(nat, proj_t, selt, oct, gate_expand_t)
    return _matmul_res(ot, _pair_heads(w_o, 0).astype(BF16), h, "nsa_out")


def _trunk(x2, positions, p):
    h = _fox_layer(x2, p["l0_norm_mix"], p["l0_fox_w_qkv"], p["l0_fox_w_f"], p["l0_fox_b_f"], p["l0_fox_w_o"])
    h = _mlp(h, p["l0_norm_mlp"], p["l0_mlp_w1"].astype(BF16), p["l0_mlp_w2"].astype(BF16))
    h = _pool_layer(h, p["l1_norm_mix"], p["l1_pool_w"], p["l1_pool_scale"])
    h = _mlp(h, p["l1_norm_mlp"], p["l1_mlp_w1"].astype(BF16), p["l1_mlp_w2"].astype(BF16))
    h = _conv_layer(h, p["l2_norm_mix"], p["l2_conv_w_in"], p["l2_conv_w"], p["l2_conv_w_out"])
    h = _mlp(h, p["l2_norm_mlp"], p["l2_mlp_w1"].astype(BF16), p["l2_mlp_w2"].astype(BF16))
    h = _nsa_layer(h, p["l3_norm_mix"], positions, p["l3_nsa_w_in"], p["l3_nsa_cmp_pe_k"], p["l3_nsa_cmp_w1_k"],
                   p["l3_nsa_cmp_w2_k"], p["l3_nsa_cmp_pe_v"], p["l3_nsa_cmp_w1_v"], p["l3_nsa_cmp_w2_v"],
                   p["l3_nsa_w_o"])
    return _mlp(h, p["l3_norm_mlp"], p["l3_mlp_w1"].astype(BF16), p["l3_mlp_w2"].astype(BF16), p["final_norm"])


def kernel(x, positions, l0_norm_mix, l0_fox_w_qkv, l0_fox_w_f, l0_fox_b_f, l0_fox_w_o, l0_norm_mlp, l0_mlp_w1, l0_mlp_w2, l1_norm_mix, l1_pool_w, l1_pool_scale, l1_norm_mlp, l1_mlp_w1, l1_mlp_w2, l2_norm_mix, l2_conv_w_in, l2_conv_w, l2_conv_w_out, l2_norm_mlp, l2_mlp_w1, l2_mlp_w2, l3_norm_mix, l3_nsa_w_in, l3_nsa_cmp_pe_k, l3_nsa_cmp_w1_k, l3_nsa_cmp_w2_k, l3_nsa_cmp_pe_v, l3_nsa_cmp_w1_v, l3_nsa_cmp_w2_v, l3_nsa_w_o, l3_norm_mlp, l3_mlp_w1, l3_mlp_w2, final_norm):
    params = dict(locals())
    B, S, D = x.shape
    outs = [_trunk(x[b], positions, params) for b in range(B)]
    return jnp.stack(outs, axis=0)
```

```python
import functools

import numpy as np
import jax
import jax.numpy as jnp
from jax import lax
from jax.experimental import pallas as pl
from jax.experimental.pallas import tpu as pltpu

F32 = jnp.float32
BF16 = jnp.bfloat16

D_MODEL = 1024
HEAD_DIM = 64
N_HEADS = D_MODEL // HEAD_DIM
D_FF = 4 * D_MODEL
ROPE_THETA = 500000.0
ROT_DIM = HEAD_DIM // 4
RMS_EPS = 1e-6
POOL_WINDOWS = (2, 4, 8, 16)
POOL_GROUP = D_MODEL // len(POOL_WINDOWS)
POOL_HALO = 16
CONV_WIDTH = 3
CONV_HALO = 8
NSA_KV_HEADS = 4
NSA_GROUP = N_HEADS // NSA_KV_HEADS
CMP_BLOCK = 32
CMP_STRIDE = 16
CMP_HIDDEN = 256
SLC_BLOCK = 64
SLC_SHIFT = 6
SLC_TOPK = 16
WIN = 512
FORCE_SCORE = 1e9
NEG = -1e30
LOG2E = 1.4426950408889634
QK_SCALE = HEAD_DIM ** -0.5 * LOG2E

LANES = 128
V7X_VMEM_LIMIT = 56 * 1024 * 1024
FOX_TILE = 512
FOX_BIAS_TERMS = 3
NSA_TILE = WIN // 2
SEL_GROUP = 16
SUM_ROWS = 16
CMP_TILE = 256
IMP_PAD = 8
NSA_PROJ_TILE = 2 * LANES
NSA_NAT_WIDTH = 8 * LANES
NSA_NAT_ROPE_TILES = (1,)
NSA_T_ROWS = 14 * LANES
NSA_T_ROPE_TILES = (0, 1, 2, 3)


def _cparams(semantics):
    return pltpu.CompilerParams(dimension_semantics=semantics, vmem_limit_bytes=V7X_VMEM_LIMIT)


def _rms(x, g):
    return x * lax.rsqrt(jnp.mean(x * x, axis=-1, keepdims=True) + RMS_EPS) * g


def _dot(a, b):
    return jnp.dot(a, b, preferred_element_type=F32)


def _dot_nt(a, b):
    return lax.dot_general(a, b, (((1,), (1,)), ((), ())), preferred_element_type=F32)


def _dot_tn(a, b):
    return lax.dot_general(a, b, (((0,), (0,)), ((), ())), preferred_element_type=F32)


def _split3(x):
    hi = x.astype(BF16)
    r = x - hi.astype(F32)
    mid = r.astype(BF16)
    lo = (r - mid.astype(F32)).astype(BF16)
    return hi, mid, lo


def _rope(a, rc, rs1, rs2, axis):
    half = ROT_DIM // 2
    return a * rc + pltpu.roll(a, LANES - half, axis) * rs1 + pltpu.roll(a, half, axis) * rs2


def _head_rows(qt_blk):
    low = lax.broadcasted_iota(jnp.int32, qt_blk.shape, 0) < HEAD_DIM
    qf = qt_blk.astype(F32)
    return jnp.where(low, qf, 0.0).astype(BF16), jnp.where(low, 0.0, qf).astype(BF16)


def _with_ones_rows(vt):
    return jnp.concatenate([vt, jnp.ones((SUM_ROWS, vt.shape[1]), vt.dtype)], axis=0)


def _online_update(s_ref, vt_ones, m_ref, acc_ref, idx, causal):
    tk, t = s_ref.shape[1:]
    s = s_ref[idx]
    if causal:
        s = jnp.where(lax.broadcasted_iota(jnp.int32, (tk, t), 0) <= lax.broadcasted_iota(jnp.int32, (tk, t), 1),
                      s, NEG)
    m_old = m_ref[idx]
    m_new = jnp.maximum(m_old, jnp.max(s, axis=0, keepdims=True))
    p = jnp.exp2(s - m_new).astype(BF16)
    m_ref[idx] = m_new
    acc_ref[idx] = jnp.exp2(m_old - m_new) * acc_ref[idx] + _dot(vt_ones, p)


def _online_result(acc_ref, idx):
    acc = acc_ref[idx]
    return acc[:LANES, :] / acc[LANES:LANES + 1, :]


def _pipelined_tiles(i, scores, consume, pairs_per_step=1):
    scores(0, 0)

    def pair(j):
        scores(j + 1, 1)
        consume(j, 0, False)
        scores(j + 2, 0)
        consume(j + 1, 1, False)

    def body(jj, carry):
        for u in range(pairs_per_step):
            pair(2 * (pairs_per_step * jj + u))
        return carry

    n_pairs = i // 2
    n_steps = n_pairs // pairs_per_step
    lax.fori_loop(0, n_steps, body, 0)
    if pairs_per_step > 1:
        def tail(jj, carry):
            pair(2 * jj)
            return carry

        lax.fori_loop(n_steps * pairs_per_step, n_pairs, tail, 0)

    @pl.when(i % 2 == 0)
    def _():
        consume(i, 0, True)

    @pl.when(i % 2 == 1)
    def _():
        scores(i, 1)
        consume(i - 1, 0, False)
        consume(i, 1, True)


def _mlp_kernel(x_ref, g_ref, w1_ref, w2_ref, *rest, nf, final):
    if final:
        fg_ref, o_ref, xn_ref = rest
    else:
        o_ref, xn_ref = rest
    f = pl.program_id(1)

    @pl.when(f == 0)
    def _():
        x = x_ref[...]
        xn_ref[...] = _rms(x, g_ref[...]).astype(BF16)
        o_ref[...] = x

    a = _dot(xn_ref[...], w1_ref[...])
    a = jnp.square(jnp.maximum(a, 0.0)).astype(BF16)
    o_ref[...] += _dot(a, w2_ref[...])

    if final:
        @pl.when(f == nf - 1)
        def _():
            o_ref[...] = _rms(o_ref[...], fg_ref[...])


def _mlp(h, g, w1, w2, final_g=None):
    S, D = h.shape
    F = w1.shape[1]
    tm = min(1024, S)
    tf = 2048
    nf = F // tf
    final = final_g is not None
    in_specs = [
        pl.BlockSpec((tm, D), lambda i, f: (i, 0)),
        pl.BlockSpec((1, D), lambda i, f: (0, 0)),
        pl.BlockSpec((D, tf), lambda i, f: (0, f)),
        pl.BlockSpec((tf, D), lambda i, f: (f, 0)),
    ]
    args = [h, g.reshape(1, D), w1, w2]
    if final:
        in_specs.append(pl.BlockSpec((1, D), lambda i, f: (0, 0)))
        args.append(final_g.reshape(1, D))
    return pl.pallas_call(
        functools.partial(_mlp_kernel, nf=nf, final=final),
        grid=(S // tm, nf),
        in_specs=in_specs,
        out_specs=pl.BlockSpec((tm, D), lambda i, f: (i, 0)),
        out_shape=jax.ShapeDtypeStruct((S, D), F32),
        scratch_shapes=[pltpu.VMEM((tm, D), BF16)],
        compiler_params=_cparams(("parallel", "arbitrary")),
        name="mlp",
    )(*args)


def _proj_kernel(x_ref, g_ref, w_ref, *rest, key_major, rope_tiles):
    if rope_tiles:
        rc_ref, rs1_ref, rs2_ref, o_ref, xn_ref = rest
    else:
        o_ref, xn_ref = rest
    j = pl.program_id(1)

    @pl.when(j == 0)
    def _():
        xn_ref[...] = _rms(x_ref[...], g_ref[...]).astype(BF16)

    a = _dot_nt(w_ref[...], xn_ref[...]) if key_major else _dot(xn_ref[...], w_ref[...])
    if not rope_tiles:
        o_ref[...] = a.astype(o_ref.dtype)
        return
    axis = 0 if key_major else 1
    is_rope = functools.reduce(jnp.logical_or, [j == t for t in rope_tiles])

    @pl.when(is_rope)
    def _():
        rc, rs1, rs2 = rc_ref[...], rs1_ref[...], rs2_ref[...]
        blocks = [lax.slice_in_dim(a, b * LANES, (b + 1) * LANES, axis=axis) for b in range(a.shape[axis] // LANES)]
        o_ref[...] = jnp.concatenate([_rope(blk, rc, rs1, rs2, axis) for blk in blocks], axis=axis).astype(o_ref.dtype)

    @pl.when(jnp.logical_not(is_rope))
    def _():
        o_ref[...] = a.astype(o_ref.dtype)


def _proj(h, g, w, name, *, key_major, tn, rope=None, rope_tiles=()):
    S, D = h.shape
    tm = min(1024, S)
    if key_major:
        N = w.shape[0]
        w_spec = pl.BlockSpec((tn, D), lambda i, j: (j, 0))
        rope_spec = pl.BlockSpec((LANES, tm), lambda i, j: (0, i))
        out_spec = pl.BlockSpec((tn, tm), lambda i, j: (j, i))
        out_shape = jax.ShapeDtypeStruct((N, S), BF16)
    else:
        N = w.shape[1]
        w_spec = pl.BlockSpec((D, tn), lambda i, j: (0, j))
        rope_spec = pl.BlockSpec((tm, LANES), lambda i, j: (i, 0))
        out_spec = pl.BlockSpec((tm, tn), lambda i, j: (i, j))
        out_shape = jax.ShapeDtypeStruct((S, N), BF16)
    in_specs = [pl.BlockSpec((tm, D), lambda i, j: (i, 0)), pl.BlockSpec((1, D), lambda i, j: (0, 0)), w_spec]
    args = [h, g.reshape(1, D), w]
    if rope_tiles:
        in_specs += [rope_spec] * 3
        args += list(rope)
    return pl.pallas_call(
        functools.partial(_proj_kernel, key_major=key_major, rope_tiles=tuple(rope_tiles)),
        grid=(S // tm, N // tn),
        in_specs=in_specs,
        out_specs=out_spec,
        out_shape=out_shape,
        scratch_shapes=[pltpu.VMEM((tm, D), BF16)],
        compiler_params=_cparams(("parallel", "arbitrary")),
        name=name,
    )(*args)


def _matmul_res_kernel(at_ref, w_ref, r_ref, o_ref):
    o_ref[...] = r_ref[...] + _dot_tn(at_ref[...], w_ref[...])


def _matmul_res(at, w, res, name):
    K, S = at.shape
    N = w.shape[1]
    tm = min(1024, S)
    return pl.pallas_call(
        _matmul_res_kernel,
        grid=(S // tm,),
        in_specs=[
            pl.BlockSpec((K, tm), lambda i: (0, i)),
            pl.BlockSpec((K, N), lambda i: (0, 0)),
            pl.BlockSpec((tm, N), lambda i: (i, 0)),
        ],
        out_specs=pl.BlockSpec((tm, N), lambda i: (i, 0)),
        out_shape=jax.ShapeDtypeStruct((S, N), F32),
        compiler_params=_cparams(("parallel",)),
        name=name,
    )(at, w, res)


def _fox_gate_kernel(x_ref, g_ref, wf_ref, bf_ref, place_ref, c_ref, carry_ref, *, tm):
    @pl.when(pl.program_id(0) == 0)
    def _():
        carry_ref[...] = jnp.zeros_like(carry_ref)

    xn = _rms(x_ref[...], g_ref[...]).astype(BF16)
    z = _dot(xn, wf_ref[...]) + bf_ref[...]
    logf = jnp.minimum(z, 0.0) - jnp.log(1.0 + jnp.exp(-jnp.abs(z)))
    row = lax.broadcasted_iota(jnp.int32, (tm, tm), 0)
    col = lax.broadcasted_iota(jnp.int32, (tm, tm), 1)
    tri = jnp.where(row >= col, 1.0, 0.0).astype(BF16)
    hi, mid, lo = _split3(logf)
    c = _dot(tri, hi) + _dot(tri, mid) + _dot(tri, lo) + carry_ref[...]
    carry_ref[...] = c[tm - 1:tm, :]
    terms = jnp.concatenate(_split3(c * -LOG2E), axis=1)
    c_ref[...] = _dot(terms, place_ref[...]).astype(c_ref.dtype)


def _fox_gate(h, g, wf, bf, place):
    S, D = h.shape
    tm = min(512, S)
    return pl.pallas_call(
        functools.partial(_fox_gate_kernel, tm=tm),
        grid=(S // tm,),
        in_specs=[
            pl.BlockSpec((tm, D), lambda i: (i, 0)),
            pl.BlockSpec((1, D), lambda i: (0, 0)),
            pl.BlockSpec((D, LANES), lambda i: (0, 0)),
            pl.BlockSpec((1, LANES), lambda i: (0, 0)),
            pl.BlockSpec((FOX_BIAS_TERMS * LANES, D_MODEL), lambda i: (0, 0)),
        ],
        out_specs=pl.BlockSpec((tm, D_MODEL), lambda i: (i, 0)),
        out_shape=jax.ShapeDtypeStruct((S, D_MODEL), BF16),
        scratch_shapes=[pltpu.VMEM((1, LANES), F32)],
        compiler_params=_cparams(("arbitrary",)),
        name="fox_gate",
    )(h, g.reshape(1, D), wf, bf, place)


def _fox_attn_kernel(qt_ref, k_ref, c_ref, vt_ref, ot_ref, s0_ref, s1_ref, m_ref, acc_ref, *, t):
    i = pl.program_id(1)
    row = lax.broadcasted_iota(jnp.int32, (LANES, t), 0)
    q_heads = _head_rows(qt_ref[...])
    qa = []
    for h in range(2):
        ones_rows = jnp.logical_and(row >= FOX_BIAS_TERMS * h, row < FOX_BIAS_TERMS * (h + 1))
        qa.append(jnp.concatenate([q_heads[h], jnp.where(ones_rows, 1.0, 0.0).astype(BF16)], axis=0))
    m_ref[...] = jnp.full(m_ref.shape, NEG, F32)
    acc_ref[...] = jnp.zeros(acc_ref.shape, F32)
    s_bufs = (s0_ref, s1_ref)

    def scores(j, buf):
        rows = pl.ds(pl.multiple_of(j * t, t), t)
        lhs = jnp.concatenate([k_ref[rows, :], c_ref[rows, :]], axis=1)
        for h in range(2):
            s_bufs[buf][h] = _dot(lhs, qa[h])

    def consume(j, buf, diag):
        vt = _with_ones_rows(vt_ref[:, pl.ds(pl.multiple_of(j * t, t), t)])
        for h in range(2):
            _online_update(s_bufs[buf], vt, m_ref, acc_ref, h, diag)

    _pipelined_tiles(i, scores, consume, pairs_per_step=2)
    ot = jnp.where(row < HEAD_DIM, _online_result(acc_ref, 0), _online_result(acc_ref, 1))
    ot_ref[...] = ot.astype(ot_ref.dtype)


def _fox_attn(qvt, k, c_terms):
    S = k.shape[0]
    t = min(FOX_TILE, S)
    n_pairs = N_HEADS // 2
    return pl.pallas_call(
        functools.partial(_fox_attn_kernel, t=t),
        grid=(n_pairs, S // t),
        in_specs=[
            pl.BlockSpec((LANES, t), lambda hp, i: (hp, i)),
            pl.BlockSpec((S, LANES), lambda hp, i: (0, hp)),
            pl.BlockSpec((S, LANES), lambda hp, i: (0, hp)),
            pl.BlockSpec((LANES, S), lambda hp, i: (n_pairs + hp, 0)),
        ],
        out_specs=pl.BlockSpec((LANES, t), lambda hp, i: (hp, i)),
        out_shape=jax.ShapeDtypeStruct((D_MODEL, S), BF16),
        scratch_shapes=([pltpu.VMEM((2, t, t), F32)] * 2 + [pltpu.VMEM((2, 1, t), F32)]
                        + [pltpu.VMEM((2, LANES + SUM_ROWS, t), F32)]),
        compiler_params=_cparams(("parallel", "arbitrary")),
        name="fox_attn",
    )(qvt, k, c_terms, qvt)


def _fox_bias_placement():
    place = np.zeros((FOX_BIAS_TERMS * LANES, D_MODEL), np.float32)
    for head in range(N_HEADS):
        for n in range(FOX_BIAS_TERMS):
            place[n * LANES + head, (head // 2) * LANES + FOX_BIAS_TERMS * (head % 2) + n] = 1.0
    return jnp.asarray(place, BF16)


def _fox_layer(h, g, w_qkv, w_f, b_f, w_o):
    wf = jnp.pad(w_f, ((0, 0), (0, LANES - N_HEADS))).astype(BF16)
    bf = jnp.pad(b_f, (0, LANES - N_HEADS)).reshape(1, LANES)
    c_terms = _fox_gate(h, g, wf, bf, _fox_bias_placement())
    w_qv = jnp.concatenate([w_qkv[:, :D_MODEL] * QK_SCALE, w_qkv[:, 2 * D_MODEL:]], axis=1).T.astype(BF16)
    qvt = _proj(h, g, w_qv, "fox_qv", key_major=True, tn=1024)
    k = _proj(h, g, w_qkv[:, D_MODEL:2 * D_MODEL].astype(BF16), "fox_k", key_major=False, tn=1024)
    ot = _fox_attn(qvt, k, c_terms)
    return _matmul_res(ot, w_o.astype(BF16), h, "fox_out")


def _pool_kernel(x_ref, halo_ref, g_ref, w_ref, sc_ref, o_ref, *, tm):
    i = pl.program_id(0)
    x = x_ref[...]
    g = g_ref[...]
    xn = _rms(x, g)
    hn = jnp.where(i > 0, _rms(halo_ref[...], g), 0.0)
    xe = jnp.concatenate([hn, xn], axis=0)
    tpos = i * tm + lax.broadcasted_iota(jnp.int32, (tm, 1), 0)
    for gi, w in enumerate(POOL_WINDOWS):
        sl = slice(gi * POOL_GROUP, (gi + 1) * POOL_GROUP)
        s = xe[:, sl]
        k = 1
        while k < w:
            s = s + pltpu.roll(s, k, 0)
            k *= 2
        cnt = jnp.minimum(tpos + 1, w).astype(F32)
        d = (s[POOL_HALO:, :] / cnt - xn[:, sl]).astype(BF16)
        o_ref[:, sl] = x[:, sl] + _dot(d, w_ref[gi]) * sc_ref[:, sl]


def _pool_layer(h, g, w_pool, pool_scale):
    S, D = h.shape
    tm = min(1024, S)
    return pl.pallas_call(
        functools.partial(_pool_kernel, tm=tm),
        grid=(S // tm,),
        in_specs=[
            pl.BlockSpec((tm, D), lambda i: (i, 0)),
            pl.BlockSpec((POOL_HALO, D), lambda i: (jnp.maximum(i * (tm // POOL_HALO) - 1, 0), 0)),
            pl.BlockSpec((1, D), lambda i: (0, 0)),
            pl.BlockSpec((len(POOL_WINDOWS), POOL_GROUP, POOL_GROUP), lambda i: (0, 0, 0)),
            pl.BlockSpec((1, D), lambda i: (0, 0)),
        ],
        out_specs=pl.BlockSpec((tm, D), lambda i: (i, 0)),
        out_shape=jax.ShapeDtypeStruct((S, D), F32),
        compiler_params=_cparams(("parallel",)),
        name="pool",
    )(h, h, g.reshape(1, D), w_pool.astype(BF16), pool_scale.reshape(1, D))


def _conv_in_kernel(x_ref, g_ref, wb_ref, wc_ref, wu_ref, b_ref, z_ref, xn_ref):
    @pl.when(pl.program_id(1) == 0)
    def _():
        xn_ref[...] = _rms(x_ref[...], g_ref[...]).astype(BF16)

    xn = xn_ref[...]
    b_ref[...] = _dot(xn, wb_ref[...])
    z_ref[...] = _dot(xn, wc_ref[...]) * _dot(xn, wu_ref[...])


def _conv_out_kernel(b_ref, z_ref, zh_ref, cw_ref, w_ref, r_ref, o_ref, *, tm):
    i = pl.program_id(0)
    z = z_ref[...]
    zh = jnp.where(i > 0, zh_ref[...], 0.0)
    row = lax.broadcasted_iota(jnp.int32, (tm, 1), 0)
    prev1 = zh[CONV_HALO - 1:CONV_HALO, :]
    prev2 = zh[CONV_HALO - 2:CONV_HALO - 1, :]
    z1 = jnp.where(row == 0, prev1, pltpu.roll(z, 1, 0))
    z2 = jnp.where(row == 0, prev2, jnp.where(row == 1, prev1, pltpu.roll(z, 2, 0)))
    cw = cw_ref[...]
    conv = cw[0:1, :] * z2 + cw[1:2, :] * z1 + cw[2:3, :] * z
    y = (b_ref[...] * conv).astype(BF16)
    o_ref[...] = r_ref[...] + _dot(y, w_ref[...])


def _conv_layer(h, g, w_in, conv_w, w_out):
    S, D = h.shape
    tm = min(1024, S)
    tn = 512
    nj = D // tn
    w_in = w_in.astype(BF16)
    b, z = pl.pallas_call(
        _conv_in_kernel,
        grid=(S // tm, nj),
        in_specs=[
            pl.BlockSpec((tm, D), lambda i, j: (i, 0)),
            pl.BlockSpec((1, D), lambda i, j: (0, 0)),
            pl.BlockSpec((D, tn), lambda i, j: (0, j)),
            pl.BlockSpec((D, tn), lambda i, j: (0, nj + j)),
            pl.BlockSpec((D, tn), lambda i, j: (0, 2 * nj + j)),
        ],
        out_specs=[pl.BlockSpec((tm, tn), lambda i, j: (i, j)),
                   pl.BlockSpec((tm, tn), lambda i, j: (i, j))],
        out_shape=[jax.ShapeDtypeStruct((S, D), F32), jax.ShapeDtypeStruct((S, D), F32)],
        scratch_shapes=[pltpu.VMEM((tm, D), BF16)],
        compiler_params=_cparams(("parallel", "arbitrary")),
        name="conv_in",
    )(h, g.reshape(1, D), w_in, w_in, w_in)
    cw = jnp.pad(conv_w, ((0, 8 - CONV_WIDTH), (0, 0)))
    return pl.pallas_call(
        functools.partial(_conv_out_kernel, tm=tm),
        grid=(S // tm,),
        in_specs=[
            pl.BlockSpec((tm, D), lambda i: (i, 0)),
            pl.BlockSpec((tm, D), lambda i: (i, 0)),
            pl.BlockSpec((CONV_HALO, D), lambda i: (jnp.maximum(i * (tm // CONV_HALO) - 1, 0), 0)),
            pl.BlockSpec((8, D), lambda i: (0, 0)),
            pl.BlockSpec((D, D), lambda i: (0, 0)),
            pl.BlockSpec((tm, D), lambda i: (i, 0)),
        ],
        out_specs=pl.BlockSpec((tm, D), lambda i: (i, 0)),
        out_shape=jax.ShapeDtypeStruct((S, D), F32),
        compiler_params=_cparams(("parallel",)),
        name="conv_out",
    )(b, z, z, cw, w_out.astype(BF16), h)


def _nsa_cmp_kernel(r_ref, w1_ref, pe_ref, w2_ref, rc_ref, rs1_ref, rs2_ref, o_ref, acc_ref, *, nc):
    kv = pl.program_id(0)
    hd = pl.program_id(1)
    half = CMP_STRIDE * HEAD_DIM
    r = r_ref[0, 0]
    w1 = w1_ref[0]
    first = _dot(r, w1[:half, :])
    second = _dot(r, w1[half:, :])
    pe_term = _dot(pe_ref[0], w1)[0:1, :]
    pre = first + pltpu.roll(second, nc - 1, 0) + pe_term
    ge = 0.5 * pre * (1.0 + jnp.tanh(0.7978845608028654 * (pre + 0.044715 * pre * pre * pre)))
    y = _dot(ge.astype(BF16), w2_ref[0, 0])

    @pl.when(hd == 0)
    def _():
        acc_ref[...] = y

    @pl.when(hd > 0)
    def _():
        acc_ref[...] += y

    @pl.when(hd == NSA_KV_HEADS - 1)
    def _():
        acc = acc_ref[...]

        @pl.when(kv == 0)
        def _():
            rc, rs1, rs2 = rc_ref[...], rs1_ref[...], rs2_ref[...]
            o_ref[0] = jnp.concatenate(
                [_rope(acc[:, :LANES], rc, rs1, rs2, 1), _rope(acc[:, LANES:], rc, rs1, rs2, 1)],
                axis=1).astype(o_ref.dtype)

        @pl.when(kv == 1)
        def _():
            o_ref[0] = acc.astype(o_ref.dtype)


def _nsa_cmp(r, w1, pe, w2p, rc, rs1, rs2):
    nc = r.shape[2]
    kvd = NSA_KV_HEADS * HEAD_DIM
    return pl.pallas_call(
        functools.partial(_nsa_cmp_kernel, nc=nc),
        grid=(2, NSA_KV_HEADS),
        in_specs=[
            pl.BlockSpec((1, 1, nc, CMP_STRIDE * HEAD_DIM), lambda a, b: (a, b, 0, 0)),
            pl.BlockSpec((1, CMP_BLOCK * HEAD_DIM, CMP_HIDDEN), lambda a, b: (a, 0, 0)),
            pl.BlockSpec((1, 8, CMP_BLOCK * HEAD_DIM), lambda a, b: (a, 0, 0)),
            pl.BlockSpec((1, 1, CMP_HIDDEN, kvd), lambda a, b: (a, b, 0, 0)),
            pl.BlockSpec((nc, LANES), lambda a, b: (0, 0)),
            pl.BlockSpec((nc, LANES), lambda a, b: (0, 0)),
            pl.BlockSpec((nc, LANES), lambda a, b: (0, 0)),
        ],
        out_specs=pl.BlockSpec((1, nc, kvd), lambda a, b: (a, 0, 0)),
        out_shape=jax.ShapeDtypeStruct((2, nc, kvd), BF16),
        scratch_shapes=[pltpu.VMEM((nc, kvd), F32)],
        compiler_params=_cparams(("parallel", "arbitrary")),
        name="nsa_cmp",
    )(r, w1, pe, w2p, rc, rs1, rs2)


def _nsa_cmp_attn_body(i, qt_ref, kc_ref, vct_ref, w_ref, oct_ref, selt_ref, *, t, n_tiles, ct, nsp):
    qpos = i * t + lax.broadcasted_iota(jnp.int32, (1, t), 1)
    any_valid = qpos >= CMP_BLOCK - 1
    low = lax.broadcasted_iota(jnp.int32, (LANES, t), 0) < HEAD_DIM
    per_tile = ct * CMP_STRIDE // SLC_BLOCK
    stat_rows = jnp.concatenate([jnp.ones((SUM_ROWS, ct), BF16), w_ref[...]], axis=0)
    cmasks = [CMP_STRIDE * (c * ct + lax.broadcasted_iota(jnp.int32, (ct, t), 0)) + (CMP_BLOCK - 1) <= qpos
              for c in range(n_tiles)]
    blk = lax.broadcasted_iota(jnp.int32, (nsp, t), 0)
    blk_f = blk.astype(F32)
    cur = jnp.right_shift(qpos, SLC_SHIFT)
    forced = jnp.logical_or(blk == 0, jnp.logical_or(blk == cur, blk == cur - 1))
    valid = blk * SLC_BLOCK <= qpos
    vals = []
    for pair in range(NSA_KV_HEADS // 2):
        lanes = slice(pair * LANES, (pair + 1) * LANES)
        lhs = [jnp.concatenate([vct_ref[lanes, c * ct:(c + 1) * ct], stat_rows], axis=0) for c in range(n_tiles)]
        imp = [[None] * n_tiles, [None] * n_tiles]
        for g in range(NSA_GROUP):
            rows = slice((pair * NSA_GROUP + g) * LANES, (pair * NSA_GROUP + g + 1) * LANES)
            qh = _head_rows(qt_ref[rows, :])
            oc = []
            for h in range(2):
                m = jnp.full((1, t), NEG, F32)
                acc = jnp.zeros((LANES + SUM_ROWS, t), F32)
                parts, maxes = [], []
                for c in range(n_tiles):
                    s = jnp.where(cmasks[c], _dot(kc_ref[0, c * ct:(c + 1) * ct, lanes], qh[h]), NEG)
                    m_new = jnp.maximum(m, jnp.max(s, axis=0, keepdims=True))
                    p = jnp.exp2(s - m_new)
                    p_hi = p.astype(BF16)
                    p_lo = (p - p_hi.astype(F32)).astype(BF16)
                    r_hi = _dot(lhs[c], p_hi)
                    r_lo = _dot(stat_rows, p_lo)
                    stats = r_hi[LANES:, :] + r_lo
                    acc = jnp.exp2(m - m_new) * acc + jnp.concatenate([r_hi[:LANES, :], stats[:SUM_ROWS, :]], axis=0)
                    parts.append(stats[SUM_ROWS:, :])
                    maxes.append(m_new)
                    m = m_new
                inv = jnp.where(any_valid, 1.0 / acc[LANES:LANES + 1, :], 0.0)
                oc.append(acc[:LANES, :] * inv)
                for c in range(n_tiles):
                    piece = parts[c] * (jnp.exp2(maxes[c] - m) * inv)
                    imp[h][c] = piece if imp[h][c] is None else imp[h][c] + piece
            oct_ref[rows, :] = jnp.where(low, oc[0], oc[1]).astype(oct_ref.dtype)
        for h in range(2):
            segs = []
            for c in range(n_tiles):
                seg = imp[h][c][:per_tile, :]
                if c > 0:
                    spill = imp[h][c - 1][per_tile:per_tile + IMP_PAD, :]
                    seg = jnp.concatenate([seg[:IMP_PAD, :] + spill, seg[IMP_PAD:, :]], axis=0)
                segs.append(seg)
            score = jnp.concatenate(segs, axis=0)[:nsp, :]
            vals.append(jnp.where(valid, jnp.where(forced, FORCE_SCORE, score), NEG))

    def pick_one(_, vals):
        out = []
        for v in vals:
            mx = jnp.max(v, axis=0, keepdims=True)
            first = jnp.min(jnp.where(v == mx, blk_f, float(nsp)), axis=0, keepdims=True)
            out.append(jnp.where(blk_f == first, -jnp.inf, v))
        return tuple(out)

    vals = lax.fori_loop(0, SLC_TOPK, pick_one, tuple(vals))
    for kvh in range(NSA_KV_HEADS):
        picked = jnp.logical_and(valid, vals[kvh] == -jnp.inf)
        selt_ref[kvh, :nsp, :] = jnp.where(picked, 1.0, 0.0).astype(selt_ref.dtype)
        if nsp < selt_ref.shape[1]:
            selt_ref[kvh, nsp:, :] = jnp.zeros((selt_ref.shape[1] - nsp, t), selt_ref.dtype)


def _nsa_cmp_attn_kernel(qt_ref, kc_ref, vct_ref, w_ref, oct_ref, selt_ref, *, t, n_steps, n_spans):
    i = pl.program_id(0)
    ct = w_ref.shape[1]
    for q in range(n_spans):
        @pl.when(jnp.logical_and(i >= q * n_steps // n_spans, i < (q + 1) * n_steps // n_spans))
        def _(q=q):
            _nsa_cmp_attn_body(i, qt_ref, kc_ref, vct_ref, w_ref, oct_ref, selt_ref, t=t, n_tiles=q + 1, ct=ct,
                               nsp=(q + 1) * ct * CMP_STRIDE // SLC_BLOCK)


def _nsa_cmp_attn(proj_t, kvc, vct, w_imp, nsp):
    S = proj_t.shape[1]
    t = NSA_TILE
    nc = kvc.shape[1]
    imp_rows, ct = w_imp.shape
    kvd = NSA_KV_HEADS * HEAD_DIM
    return pl.pallas_call(
        functools.partial(_nsa_cmp_attn_kernel, t=t, n_steps=S // t, n_spans=nc // ct),
        grid=(S // t,),
        in_specs=[
            pl.BlockSpec((D_MODEL, t), lambda i: (0, i)),
            pl.BlockSpec((1, nc, kvd), lambda i: (0, 0, 0)),
            pl.BlockSpec((kvd, nc), lambda i: (0, 0)),
            pl.BlockSpec((imp_rows, ct), lambda i: (0, 0)),
        ],
        out_specs=[pl.BlockSpec((D_MODEL, t), lambda i: (0, i)),
                   pl.BlockSpec((NSA_KV_HEADS, nsp, t), lambda i: (0, 0, i))],
        out_shape=[jax.ShapeDtypeStruct((D_MODEL, S), BF16),
                   jax.ShapeDtypeStruct((NSA_KV_HEADS, nsp, S), BF16)],
        compiler_params=_cparams(("parallel",)),
        name="nsa_cmp_attn",
    )(proj_t, kvc, vct, w_imp)


def _nsa_sel_kernel(qt_ref, ks_ref, vst_ref, selt_ref, kw0_ref, kw1_ref, kw2_ref, vwt0_ref, vwt1_ref, vwt2_ref,
                    gzt_ref, et_ref, oct_ref, ot_ref, s0_ref, s1_ref, bias_ref, m_ref, acc_ref, *, t):
    i = pl.program_id(1)
    n_heads = 2 * NSA_GROUP
    low = lax.broadcasted_iota(jnp.int32, (LANES, t), 0) < HEAD_DIM
    qs = []
    for g in range(NSA_GROUP):
        qs.extend(_head_rows(qt_ref[g * LANES:(g + 1) * LANES, :]))
    bias_ref[...] = ((selt_ref[...].astype(F32) - 1.0) * -NEG).astype(BF16)
    m_ref[...] = jnp.full(m_ref.shape, NEG, F32)
    acc_ref[...] = jnp.zeros(acc_ref.shape, F32)
    tok_blk = jnp.right_shift(lax.broadcasted_iota(jnp.int32, (t, LANES), 0), SLC_SHIFT)
    lane_col = lax.broadcasted_iota(jnp.int32, (t, LANES), 1)
    zero_rows = jnp.zeros((LANES - SEL_GROUP, t), BF16)
    tiles_per_group = SEL_GROUP * SLC_BLOCK // t
    s_bufs = (s0_ref, s1_ref)

    def scores(j, buf):
        k = ks_ref[pl.ds(pl.multiple_of(j * t, t), t), :]
        grp = j // tiles_per_group
        first_blk = (j % tiles_per_group) * (t // SLC_BLOCK)
        expand = jnp.where(lane_col == first_blk + tok_blk, 1.0, 0.0).astype(BF16)
        lhs = jnp.concatenate([k, expand], axis=1)
        for h in range(2):
            blk_bias = bias_ref[h, grp]
            for g in range(NSA_GROUP):
                rhs = jnp.concatenate([qs[2 * g + h], blk_bias, zero_rows], axis=0)
                s_bufs[buf][2 * g + h] = _dot(lhs, rhs)

    def consume(j, buf, diag):
        vt = _with_ones_rows(vst_ref[:, pl.ds(pl.multiple_of(j * t, t), t)])
        for idx in range(n_heads):
            _online_update(s_bufs[buf], vt, m_ref, acc_ref, idx, diag)

    _pipelined_tiles(i, scores, consume, pairs_per_step=2)

    n_win = WIN // t + 1
    k_win = jnp.concatenate([kw0_ref[...], kw1_ref[...], kw2_ref[...]], axis=0)
    vt_win = jnp.concatenate([vwt0_ref[...], vwt1_ref[...], vwt2_ref[...]], axis=1)
    kpos = (i - (n_win - 1)) * t + lax.broadcasted_iota(jnp.int32, (n_win * t, t), 0)
    qpos = i * t + lax.broadcasted_iota(jnp.int32, (n_win * t, t), 1)
    wmask = jnp.logical_and(jnp.logical_and(kpos <= qpos, kpos > qpos - WIN), kpos >= 0)
    ow = []
    for idx in range(n_heads):
        s = jnp.where(wmask, _dot(k_win, qs[idx]), NEG)
        p = jnp.exp2(s - jnp.max(s, axis=0, keepdims=True))
        l = jnp.sum(p, axis=0, keepdims=True)
        ow.append(_dot(vt_win, p.astype(BF16)) / l)

    gates = jax.nn.sigmoid(_dot(et_ref[0], gzt_ref[...]))
    gw = NSA_GROUP * LANES
    for g in range(NSA_GROUP):
        rows = slice(g * LANES, (g + 1) * LANES)
        o_cmp = oct_ref[rows, :].astype(F32)
        o_slc = jnp.where(low, _online_result(acc_ref, 2 * g), _online_result(acc_ref, 2 * g + 1))
        o_win = jnp.where(low, ow[2 * g], ow[2 * g + 1])
        out = (gates[g * LANES:(g + 1) * LANES, :] * o_cmp
               + gates[gw + g * LANES:gw + (g + 1) * LANES, :] * o_slc
               + gates[2 * gw + g * LANES:2 * gw + (g + 1) * LANES, :] * o_win)
        ot_ref[rows, :] = out.astype(ot_ref.dtype)


def _nsa_sel(nat, proj_t, selt, oct, gate_expand_t):
    S = nat.shape[0]
    t = NSA_TILE
    n_grp = selt.shape[1]
    n_heads = 2 * NSA_GROUP
    gl = NSA_GROUP * LANES

    def kwin_spec(back):
        return pl.BlockSpec((t, LANES), lambda p, i: (jnp.maximum(i - back, 0), 6 + p))

    def vwin_spec(back):
        return pl.BlockSpec((LANES, t), lambda p, i: (10 + p, jnp.maximum(i - back, 0)))

    return pl.pallas_call(
        functools.partial(_nsa_sel_kernel, t=t),
        grid=(NSA_KV_HEADS // 2, S // t),
        in_specs=[
            pl.BlockSpec((gl, t), lambda p, i: (p, i)),
            pl.BlockSpec((S, LANES), lambda p, i: (0, 4 + p)),
            pl.BlockSpec((LANES, S), lambda p, i: (8 + p, 0)),
            pl.BlockSpec((2, n_grp, SEL_GROUP, t), lambda p, i: (p, 0, 0, i)),
            kwin_spec(2), kwin_spec(1), kwin_spec(0),
            vwin_spec(2), vwin_spec(1), vwin_spec(0),
            pl.BlockSpec((NSA_PROJ_TILE, t), lambda p, i: (6, i)),
            pl.BlockSpec((1, 3 * gl, NSA_PROJ_TILE), lambda p, i: (p, 0, 0)),
            pl.BlockSpec((gl, t), lambda p, i: (p, i)),
        ],
        out_specs=pl.BlockSpec((gl, t), lambda p, i: (p, i)),
        out_shape=jax.ShapeDtypeStruct((D_MODEL, S), BF16),
        scratch_shapes=([pltpu.VMEM((n_heads, t, t), F32)] * 2 + [pltpu.VMEM((2, n_grp, SEL_GROUP, t), BF16)]
                        + [pltpu.VMEM((n_heads, 1, t), F32)] + [pltpu.VMEM((n_heads, LANES + SUM_ROWS, t), F32)]),
        compiler_params=_cparams(("parallel", "arbitrary")),
        name="nsa_sel",
    )(proj_t, nat, proj_t, selt, nat, nat, nat, proj_t, proj_t, proj_t, proj_t, gate_expand_t, oct)


def _rope_tables(positions):
    half = ROT_DIM // 2
    inv = ROPE_THETA ** (-jnp.arange(0, ROT_DIM, 2, dtype=F32) / ROT_DIM)
    ang = positions.astype(F32)[:, None] * inv[None, :]
    cos, sin = jnp.cos(ang), jnp.sin(ang)
    S = positions.shape[0]
    ones = jnp.ones((S, HEAD_DIM - ROT_DIM), F32)
    zeros_h = jnp.zeros((S, half), F32)
    zeros_r = jnp.zeros((S, HEAD_DIM - ROT_DIM), F32)
    rc = jnp.concatenate([cos, cos, ones], axis=1)
    rs1 = jnp.concatenate([-sin, zeros_h, zeros_r], axis=1)
    rs2 = jnp.concatenate([zeros_h, sin, zeros_r], axis=1)
    reps = LANES // HEAD_DIM
    return jnp.tile(rc, (1, reps)), jnp.tile(rs1, (1, reps)), jnp.tile(rs2, (1, reps))


def _pair_heads(w, axis):
    shape = w.shape
    w = w.reshape(shape[:axis] + (NSA_KV_HEADS // 2, 2, NSA_GROUP, HEAD_DIM) + shape[axis + 1:])
    w = jnp.swapaxes(w, axis + 1, axis + 2)
    return w.reshape(shape)


def _nsa_constants(S):
    nc = S // CMP_STRIDE
    ns = S // SLC_BLOCK
    nsp = -(-ns // LANES) * LANES
    ct = min(CMP_TILE, nc)
    imp_rows = -(-(ct * CMP_STRIDE // SLC_BLOCK + IMP_PAD) // SUM_ROWS) * SUM_ROWS
    ci = np.arange(ct)[None, :] * CMP_STRIDE
    st = np.arange(imp_rows)[:, None] * SLC_BLOCK
    w_imp = (ci < st + SLC_BLOCK) & (ci + CMP_BLOCK > st)
    e = np.zeros((NSA_KV_HEADS // 2, 3 * NSA_GROUP * LANES, NSA_PROJ_TILE), np.float32)
    for p in range(NSA_KV_HEADS // 2):
        for br in range(3):
            for g in range(NSA_GROUP):
                for hf in range(2):
                    src = br * N_HEADS + (2 * p + hf) * NSA_GROUP + g
                    r0 = br * NSA_GROUP * LANES + g * LANES + hf * HEAD_DIM
                    e[p, r0:r0 + HEAD_DIM, src] = 1.0
    cmp_end = np.minimum(np.arange(nc) * CMP_STRIDE + CMP_BLOCK - 1, S - 1)
    return nc, nsp, jnp.asarray(w_imp.astype(np.float32), BF16), jnp.asarray(e, BF16), cmp_end


def _nsa_layer(h, g, positions, w_in, pe_k, w1_k, w2_k, pe_v, w1_v, w2_v, w_o):
    S, D = h.shape
    qd = N_HEADS * HEAD_DIM
    kvd = NSA_KV_HEADS * HEAD_DIM
    nc, nsp, w_imp, gate_expand_t, cmp_end = _nsa_constants(S)
    rc, rs1, rs2 = _rope_tables(positions)

    def kv_piece(n):
        return w_in[:, qd + n * kvd:qd + (n + 1) * kvd]

    w_nat = jnp.concatenate([kv_piece(0), kv_piece(1), kv_piece(2), kv_piece(4)], axis=1).astype(BF16)
    wg = jnp.pad(w_in[:, qd + 6 * kvd:], ((0, 0), (0, NSA_PROJ_TILE - 3 * N_HEADS)))
    w_t = jnp.concatenate([_pair_heads(w_in[:, :qd], 1) * QK_SCALE, kv_piece(3), kv_piece(5), wg], axis=1).T.astype(BF16)
    nat = _proj(h, g, w_nat, "nsa_proj", key_major=False, tn=2 * NSA_PROJ_TILE,
                rope=(rc, rs1, rs2), rope_tiles=NSA_NAT_ROPE_TILES)
    proj_t = _proj(h, g, w_t, "nsa_proj_t", key_major=True, tn=NSA_PROJ_TILE,
                   rope=(rc.T, rs1.T, rs2.T), rope_tiles=NSA_T_ROPE_TILES)

    raw = nat[:, :2 * kvd].reshape(nc, CMP_STRIDE, 2, NSA_KV_HEADS, HEAD_DIM)
    raw = raw.transpose(2, 3, 0, 1, 4).reshape(2, NSA_KV_HEADS, nc, CMP_STRIDE * HEAD_DIM)
    w1 = jnp.stack([w1_k, w1_v]).astype(BF16)
    pe = jnp.stack([pe_k.reshape(1, -1), pe_v.reshape(1, -1)])
    pe = jnp.pad(pe, ((0, 0), (0, 7), (0, 0))).astype(BF16)
    w2 = jnp.stack([w2_k, w2_v])
    eye = jnp.eye(NSA_KV_HEADS, dtype=F32)
    w2p = (w2[:, None, :, None, :] * eye[None, :, None, :, None]).reshape(2, NSA_KV_HEADS, CMP_HIDDEN, kvd).astype(BF16)
    kvc = _nsa_cmp(raw, w1, pe, w2p, rc[cmp_end], rs1[cmp_end], rs2[cmp_end])

    oct, selt = _nsa_cmp_attn(proj_t, kvc, kvc[1].T, w_imp, nsp)
    selt = selt.reshape(NSA_KV_HEADS, -1, SEL_GROUP, S)
    ot = _nsa_sel(nat, proj_t, selt, oct, gate_expand_t)
    return _matmul_res(ot, _pair_heads(w_o, 0).astype(BF16), h, "nsa_out")


def _trunk(x2, positions, p):
    h = _fox_layer(x2, p["l0_norm_mix"], p["l0_fox_w_qkv"], p["l0_fox_w_f"], p["l0_fox_b_f"], p["l0_fox_w_o"])
    h = _mlp(h, p["l0_norm_mlp"], p["l0_mlp_w1"].astype(BF16), p["l0_mlp_w2"].astype(BF16))
    h = _pool_layer(h, p["l1_norm_mix"], p["l1_pool_w"], p["l1_pool_scale"])
    h = _mlp(h, p["l1_norm_mlp"], p["l1_mlp_w1"].astype(BF16), p["l1_mlp_w2"].astype(BF16))
    h = _conv_layer(h, p["l2_norm_mix"], p["l2_conv_w_in"], p["l2_conv_w"], p["l2_conv_w_out"])
    h = _mlp(h, p["l2_norm_mlp"], p["l2_mlp_w1"].astype(BF16), p["l2_mlp_w2"].astype(BF16))
    h = _nsa_layer(h, p["l3_norm_mix"], positions, p["l3_nsa_w_in"], p["l3_nsa_cmp_pe_k"], p["l3_nsa_cmp_w1_k"],
                   p["l3_nsa_cmp_w2_k"], p["l3_nsa_cmp_pe_v"], p["l3_nsa_cmp_w1_v"], p["l3_nsa_cmp_w2_v"],
                   p["l3_nsa_w_o"])
    return _mlp(h, p["l3_norm_mlp"], p["l3_mlp_w1"].astype(BF16), p["l3_mlp_w2"].astype(BF16), p["final_norm"])


def kernel(x, positions, l0_norm_mix, l0_fox_w_qkv, l0_fox_w_f, l0_fox_b_f, l0_fox_w_o, l0_norm_mlp, l0_mlp_w1, l0_mlp_w2, l1_norm_mix, l1_pool_w, l1_pool_scale, l1_norm_mlp, l1_mlp_w1, l1_mlp_w2, l2_norm_mix, l2_conv_w_in, l2_conv_w, l2_conv_w_out, l2_norm_mlp, l2_mlp_w1, l2_mlp_w2, l3_norm_mix, l3_nsa_w_in, l3_nsa_cmp_pe_k, l3_nsa_cmp_w1_k, l3_nsa_cmp_w2_k, l3_nsa_cmp_pe_v, l3_nsa_cmp_w1_v, l3_nsa_cmp_w2_v, l3_nsa_w_o, l3_norm_mlp, l3_mlp_w1, l3_mlp_w2, final_norm):
    params = dict(locals())
    B, S, D = x.shape
    outs = [_trunk(x[b], positions, params) for b in range(B)]
    return jnp.stack(outs, axis=0)
```

```python
import functools

import numpy as np
import jax
import jax.numpy as jnp
from jax import lax
from jax.experimental import pallas as pl
from jax.experimental.pallas import tpu as pltpu

F32 = jnp.float32
BF16 = jnp.bfloat16

D_MODEL = 1024
HEAD_DIM = 64
N_HEADS = D_MODEL // HEAD_DIM
D_FF = 4 * D_MODEL
ROPE_THETA = 500000.0
ROT_DIM = HEAD_DIM // 4
RMS_EPS = 1e-6
POOL_WINDOWS = (2, 4, 8, 16)
POOL_GROUP = D_MODEL // len(POOL_WINDOWS)
POOL_HALO = 16
CONV_WIDTH = 3
CONV_HALO = 8
NSA_KV_HEADS = 4
NSA_GROUP = N_HEADS // NSA_KV_HEADS
CMP_BLOCK = 32
CMP_STRIDE = 16
CMP_HIDDEN = 256
SLC_BLOCK = 64
SLC_SHIFT = 6
SLC_TOPK = 16
WIN = 512
FORCE_SCORE = 1e9
NEG = -1e30
LOG2E = 1.4426950408889634
QK_SCALE = HEAD_DIM ** -0.5 * LOG2E

LANES = 128
V7X_VMEM_LIMIT = 56 * 1024 * 1024
FOX_TILE = 512
FOX_BIAS_TERMS = 3
NSA_TILE = WIN // 2
SEL_GROUP = 16
SUM_ROWS = 16
CMP_TILE = 256
IMP_PAD = 8
NSA_PROJ_TILE = 2 * LANES
NSA_NAT_WIDTH = 8 * LANES
NAT_K_SLC, NAT_K_WIN = 4, 6
NSA_NAT_ROPE_TILES = (1,)
NSA_T_ROWS = 14 * LANES
T_V_SLC, T_V_WIN, T_GATE = 8, 10, 12
NSA_T_ROPE_TILES = (0, 1, 2, 3)


def _cparams(semantics):
    return pltpu.CompilerParams(dimension_semantics=semantics, vmem_limit_bytes=V7X_VMEM_LIMIT)


def _rms(x, g):
    return x * lax.rsqrt(jnp.mean(x * x, axis=-1, keepdims=True) + RMS_EPS) * g


def _dot(a, b):
    return jnp.dot(a, b, preferred_element_type=F32)


def _dot_nt(a, b):
    return lax.dot_general(a, b, (((1,), (1,)), ((), ())), preferred_element_type=F32)


def _dot_tn(a, b):
    return lax.dot_general(a, b, (((0,), (0,)), ((), ())), preferred_element_type=F32)


def _split3(x):
    hi = x.astype(BF16)
    r = x - hi.astype(F32)
    mid = r.astype(BF16)
    lo = (r - mid.astype(F32)).astype(BF16)
    return hi, mid, lo


def _rope(a, rc, rs1, rs2, axis):
    half = ROT_DIM // 2
    return a * rc + pltpu.roll(a, LANES - half, axis) * rs1 + pltpu.roll(a, half, axis) * rs2


def _head_rows(qt_blk):
    low = lax.broadcasted_iota(jnp.int32, qt_blk.shape, 0) < HEAD_DIM
    qf = qt_blk.astype(F32)
    return jnp.where(low, qf, 0.0).astype(BF16), jnp.where(low, 0.0, qf).astype(BF16)


def _with_ones_rows(vt):
    return jnp.concatenate([vt, jnp.ones((SUM_ROWS, vt.shape[1]), vt.dtype)], axis=0)


def _online_update(s_ref, vt_ones, m_ref, acc_ref, idx, causal):
    tk, t = s_ref.shape[1:]
    s = s_ref[idx]
    if causal:
        s = jnp.where(lax.broadcasted_iota(jnp.int32, (tk, t), 0) <= lax.broadcasted_iota(jnp.int32, (tk, t), 1),
                      s, NEG)
    m_old = m_ref[idx]
    m_new = jnp.maximum(m_old, jnp.max(s, axis=0, keepdims=True))
    p = jnp.exp2(s - m_new).astype(BF16)
    m_ref[idx] = m_new
    acc_ref[idx] = jnp.exp2(m_old - m_new) * acc_ref[idx] + _dot(vt_ones, p)


def _online_result(acc_ref, idx):
    acc = acc_ref[idx]
    return acc[:LANES, :] / acc[LANES:LANES + 1, :]


def _pipelined_tiles(i, scores, consume, pairs_per_step=1):
    scores(0, 0)

    def pair(j):
        scores(j + 1, 1)
        consume(j, 0, False)
        scores(j + 2, 0)
        consume(j + 1, 1, False)

    def body(jj, carry):
        for u in range(pairs_per_step):
            pair(2 * (pairs_per_step * jj + u))
        return carry

    n_pairs = i // 2
    n_steps = n_pairs // pairs_per_step
    lax.fori_loop(0, n_steps, body, 0)
    if pairs_per_step > 1:
        def tail(jj, carry):
            pair(2 * jj)
            return carry

        lax.fori_loop(n_steps * pairs_per_step, n_pairs, tail, 0)

    @pl.when(i % 2 == 0)
    def _():
        consume(i, 0, True)

    @pl.when(i % 2 == 1)
    def _():
        scores(i, 1)
        consume(i - 1, 0, False)
        consume(i, 1, True)


def _mlp_kernel(x_ref, g_ref, w1_ref, w2_ref, *rest, nf, final):
    if final:
        fg_ref, o_ref, xn_ref = rest
    else:
        o_ref, xn_ref = rest
    f = pl.program_id(1)

    @pl.when(f == 0)
    def _():
        x = x_ref[...]
        xn_ref[...] = _rms(x, g_ref[...]).astype(BF16)
        o_ref[...] = x

    a = _dot(xn_ref[...], w1_ref[...])
    a = jnp.square(jnp.maximum(a, 0.0)).astype(BF16)
    o_ref[...] += _dot(a, w2_ref[...])

    if final:
        @pl.when(f == nf - 1)
        def _():
            o_ref[...] = _rms(o_ref[...], fg_ref[...])


def _mlp(h, g, w1, w2, final_g=None):
    S, D = h.shape
    F = w1.shape[1]
    tm = min(1024, S)
    tf = 2048
    nf = F // tf
    final = final_g is not None
    in_specs = [
        pl.BlockSpec((tm, D), lambda i, f: (i, 0)),
        pl.BlockSpec((1, D), lambda i, f: (0, 0)),
        pl.BlockSpec((D, tf), lambda i, f: (0, f)),
        pl.BlockSpec((tf, D), lambda i, f: (f, 0)),
    ]
    args = [h, g.reshape(1, D), w1, w2]
    if final:
        in_specs.append(pl.BlockSpec((1, D), lambda i, f: (0, 0)))
        args.append(final_g.reshape(1, D))
    return pl.pallas_call(
        functools.partial(_mlp_kernel, nf=nf, final=final),
        grid=(S // tm, nf),
        in_specs=in_specs,
        out_specs=pl.BlockSpec((tm, D), lambda i, f: (i, 0)),
        out_shape=jax.ShapeDtypeStruct((S, D), F32),
        scratch_shapes=[pltpu.VMEM((tm, D), BF16)],
        compiler_params=_cparams(("parallel", "arbitrary")),
        name="mlp",
    )(*args)


def _proj_kernel(x_ref, g_ref, w_ref, *rest, key_major, rope_tiles):
    if rope_tiles:
        rc_ref, rs1_ref, rs2_ref, o_ref, xn_ref = rest
    else:
        o_ref, xn_ref = rest
    j = pl.program_id(1)

    @pl.when(j == 0)
    def _():
        xn_ref[...] = _rms(x_ref[...], g_ref[...]).astype(BF16)

    a = _dot_nt(w_ref[...], xn_ref[...]) if key_major else _dot(xn_ref[...], w_ref[...])
    if not rope_tiles:
        o_ref[...] = a.astype(o_ref.dtype)
        return
    axis = 0 if key_major else 1
    is_rope = functools.reduce(jnp.logical_or, [j == t for t in rope_tiles])

    @pl.when(is_rope)
    def _():
        rc, rs1, rs2 = rc_ref[...], rs1_ref[...], rs2_ref[...]
        blocks = [lax.slice_in_dim(a, b * LANES, (b + 1) * LANES, axis=axis) for b in range(a.shape[axis] // LANES)]
        o_ref[...] = jnp.concatenate([_rope(blk, rc, rs1, rs2, axis) for blk in blocks], axis=axis).astype(o_ref.dtype)

    @pl.when(jnp.logical_not(is_rope))
    def _():
        o_ref[...] = a.astype(o_ref.dtype)


def _proj(h, g, w, name, *, key_major, tn, rope=None, rope_tiles=()):
    S, D = h.shape
    tm = min(1024, S)
    if key_major:
        N = w.shape[0]
        w_spec = pl.BlockSpec((tn, D), lambda i, j: (j, 0))
        rope_spec = pl.BlockSpec((LANES, tm), lambda i, j: (0, i))
        out_spec = pl.BlockSpec((tn, tm), lambda i, j: (j, i))
        out_shape = jax.ShapeDtypeStruct((N, S), BF16)
    else:
        N = w.shape[1]
        w_spec = pl.BlockSpec((D, tn), lambda i, j: (0, j))
        rope_spec = pl.BlockSpec((tm, LANES), lambda i, j: (i, 0))
        out_spec = pl.BlockSpec((tm, tn), lambda i, j: (i, j))
        out_shape = jax.ShapeDtypeStruct((S, N), BF16)
    in_specs = [pl.BlockSpec((tm, D), lambda i, j: (i, 0)), pl.BlockSpec((1, D), lambda i, j: (0, 0)), w_spec]
    args = [h, g.reshape(1, D), w]
    if rope_tiles:
        in_specs += [rope_spec] * 3
        args += list(rope)
    return pl.pallas_call(
        functools.partial(_proj_kernel, key_major=key_major, rope_tiles=tuple(rope_tiles)),
        grid=(S // tm, N // tn),
        in_specs=in_specs,
        out_specs=out_spec,
        out_shape=out_shape,
        scratch_shapes=[pltpu.VMEM((tm, D), BF16)],
        compiler_params=_cparams(("parallel", "arbitrary")),
        name=name,
    )(*args)


def _matmul_res_kernel(at_ref, w_ref, r_ref, o_ref):
    o_ref[...] = r_ref[...] + _dot_tn(at_ref[...], w_ref[...])


def _matmul_res(at, w, res, name):
    K, S = at.shape
    N = w.shape[1]
    tm = min(1024, S)
    return pl.pallas_call(
        _matmul_res_kernel,
        grid=(S // tm,),
        in_specs=[
            pl.BlockSpec((K, tm), lambda i: (0, i)),
            pl.BlockSpec((K, N), lambda i: (0, 0)),
            pl.BlockSpec((tm, N), lambda i: (i, 0)),
        ],
        out_specs=pl.BlockSpec((tm, N), lambda i: (i, 0)),
        out_shape=jax.ShapeDtypeStruct((S, N), F32),
        compiler_params=_cparams(("parallel",)),
        name=name,
    )(at, w, res)


def _fox_gate_kernel(x_ref, g_ref, wf_ref, bf_ref, place_ref, c_ref, carry_ref, *, tm):
    @pl.when(pl.program_id(0) == 0)
    def _():
        carry_ref[...] = jnp.zeros_like(carry_ref)

    xn = _rms(x_ref[...], g_ref[...]).astype(BF16)
    z = _dot(xn, wf_ref[...]) + bf_ref[...]
    logf = jnp.minimum(z, 0.0) - jnp.log(1.0 + jnp.exp(-jnp.abs(z)))
    row = lax.broadcasted_iota(jnp.int32, (tm, tm), 0)
    col = lax.broadcasted_iota(jnp.int32, (tm, tm), 1)
    tri = jnp.where(row >= col, 1.0, 0.0).astype(BF16)
    hi, mid, lo = _split3(logf)
    c = _dot(tri, hi) + _dot(tri, mid) + _dot(tri, lo) + carry_ref[...]
    carry_ref[...] = c[tm - 1:tm, :]
    terms = jnp.concatenate(_split3(c * -LOG2E), axis=1)
    c_ref[...] = _dot(terms, place_ref[...]).astype(c_ref.dtype)


def _fox_gate(h, g, wf, bf, place):
    S, D = h.shape
    tm = min(512, S)
    return pl.pallas_call(
        functools.partial(_fox_gate_kernel, tm=tm),
        grid=(S // tm,),
        in_specs=[
            pl.BlockSpec((tm, D), lambda i: (i, 0)),
            pl.BlockSpec((1, D), lambda i: (0, 0)),
            pl.BlockSpec((D, LANES), lambda i: (0, 0)),
            pl.BlockSpec((1, LANES), lambda i: (0, 0)),
            pl.BlockSpec((FOX_BIAS_TERMS * LANES, D_MODEL), lambda i: (0, 0)),
        ],
        out_specs=pl.BlockSpec((tm, D_MODEL), lambda i: (i, 0)),
        out_shape=jax.ShapeDtypeStruct((S, D_MODEL), BF16),
        scratch_shapes=[pltpu.VMEM((1, LANES), F32)],
        compiler_params=_cparams(("arbitrary",)),
        name="fox_gate",
    )(h, g.reshape(1, D), wf, bf, place)


def _fox_attn_kernel(qt_ref, k_ref, c_ref, vt_ref, ot_ref, s0_ref, s1_ref, m_ref, acc_ref, *, t):
    i = pl.program_id(1)
    row = lax.broadcasted_iota(jnp.int32, (LANES, t), 0)
    q_heads = _head_rows(qt_ref[...])
    qa = []
    for h in range(2):
        ones_rows = jnp.logical_and(row >= FOX_BIAS_TERMS * h, row < FOX_BIAS_TERMS * (h + 1))
        qa.append(jnp.concatenate([q_heads[h], jnp.where(ones_rows, 1.0, 0.0).astype(BF16)], axis=0))
    m_ref[...] = jnp.full(m_ref.shape, NEG, F32)
    acc_ref[...] = jnp.zeros(acc_ref.shape, F32)
    s_bufs = (s0_ref, s1_ref)

    def scores(j, buf):
        rows = pl.ds(pl.multiple_of(j * t, t), t)
        lhs = jnp.concatenate([k_ref[rows, :], c_ref[rows, :]], axis=1)
        for h in range(2):
            s_bufs[buf][h] = _dot(lhs, qa[h])

    def consume(j, buf, diag):
        vt = _with_ones_rows(vt_ref[:, pl.ds(pl.multiple_of(j * t, t), t)])
        for h in range(2):
            _online_update(s_bufs[buf], vt, m_ref, acc_ref, h, diag)

    _pipelined_tiles(i, scores, consume, pairs_per_step=2)
    ot = jnp.where(row < HEAD_DIM, _online_result(acc_ref, 0), _online_result(acc_ref, 1))
    ot_ref[...] = ot.astype(ot_ref.dtype)


def _fox_attn(qvt, k, c_terms):
    S = k.shape[0]
    t = min(FOX_TILE, S)
    n_pairs = N_HEADS // 2
    return pl.pallas_call(
        functools.partial(_fox_attn_kernel, t=t),
        grid=(n_pairs, S // t),
        in_specs=[
            pl.BlockSpec((LANES, t), lambda hp, i: (hp, i)),
            pl.BlockSpec((S, LANES), lambda hp, i: (0, hp)),
            pl.BlockSpec((S, LANES), lambda hp, i: (0, hp)),
            pl.BlockSpec((LANES, S), lambda hp, i: (n_pairs + hp, 0)),
        ],
        out_specs=pl.BlockSpec((LANES, t), lambda hp, i: (hp, i)),
        out_shape=jax.ShapeDtypeStruct((D_MODEL, S), BF16),
        scratch_shapes=([pltpu.VMEM((2, t, t), F32)] * 2 + [pltpu.VMEM((2, 1, t), F32)]
                        + [pltpu.VMEM((2, LANES + SUM_ROWS, t), F32)]),
        compiler_params=_cparams(("parallel", "arbitrary")),
        name="fox_attn",
    )(qvt, k, c_terms, qvt)


def _fox_bias_placement():
    place = np.zeros((FOX_BIAS_TERMS * LANES, D_MODEL), np.float32)
    for head in range(N_HEADS):
        for n in range(FOX_BIAS_TERMS):
            place[n * LANES + head, (head // 2) * LANES + FOX_BIAS_TERMS * (head % 2) + n] = 1.0
    return jnp.asarray(place, BF16)


def _fox_layer(h, g, w_qkv, w_f, b_f, w_o):
    wf = jnp.pad(w_f, ((0, 0), (0, LANES - N_HEADS))).astype(BF16)
    bf = jnp.pad(b_f, (0, LANES - N_HEADS)).reshape(1, LANES)
    c_terms = _fox_gate(h, g, wf, bf, _fox_bias_placement())
    w_qv = jnp.concatenate([w_qkv[:, :D_MODEL] * QK_SCALE, w_qkv[:, 2 * D_MODEL:]], axis=1).T.astype(BF16)
    qvt = _proj(h, g, w_qv, "fox_qv", key_major=True, tn=1024)
    k = _proj(h, g, w_qkv[:, D_MODEL:2 * D_MODEL].astype(BF16), "fox_k", key_major=False, tn=1024)
    ot = _fox_attn(qvt, k, c_terms)
    return _matmul_res(ot, w_o.astype(BF16), h, "fox_out")


def _pool_kernel(x_ref, halo_ref, g_ref, w_ref, sc_ref, o_ref, *, tm):
    i = pl.program_id(0)
    x = x_ref[...]
    g = g_ref[...]
    xn = _rms(x, g)
    hn = jnp.where(i > 0, _rms(halo_ref[...], g), 0.0)
    xe = jnp.concatenate([hn, xn], axis=0)
    tpos = i * tm + lax.broadcasted_iota(jnp.int32, (tm, 1), 0)
    for gi, w in enumerate(POOL_WINDOWS):
        sl = slice(gi * POOL_GROUP, (gi + 1) * POOL_GROUP)
        s = xe[:, sl]
        k = 1
        while k < w:
            s = s + pltpu.roll(s, k, 0)
            k *= 2
        cnt = jnp.minimum(tpos + 1, w).astype(F32)
        d = (s[POOL_HALO:, :] / cnt - xn[:, sl]).astype(BF16)
        o_ref[:, sl] = x[:, sl] + _dot(d, w_ref[gi]) * sc_ref[:, sl]


def _pool_layer(h, g, w_pool, pool_scale):
    S, D = h.shape
    tm = min(1024, S)
    return pl.pallas_call(
        functools.partial(_pool_kernel, tm=tm),
        grid=(S // tm,),
        in_specs=[
            pl.BlockSpec((tm, D), lambda i: (i, 0)),
            pl.BlockSpec((POOL_HALO, D), lambda i: (jnp.maximum(i * (tm // POOL_HALO) - 1, 0), 0)),
            pl.BlockSpec((1, D), lambda i: (0, 0)),
            pl.BlockSpec((len(POOL_WINDOWS), POOL_GROUP, POOL_GROUP), lambda i: (0, 0, 0)),
            pl.BlockSpec((1, D), lambda i: (0, 0)),
        ],
        out_specs=pl.BlockSpec((tm, D), lambda i: (i, 0)),
        out_shape=jax.ShapeDtypeStruct((S, D), F32),
        compiler_params=_cparams(("parallel",)),
        name="pool",
    )(h, h, g.reshape(1, D), w_pool.astype(BF16), pool_scale.reshape(1, D))


def _conv_in_kernel(x_ref, g_ref, wb_ref, wc_ref, wu_ref, b_ref, z_ref, xn_ref):
    @pl.when(pl.program_id(1) == 0)
    def _():
        xn_ref[...] = _rms(x_ref[...], g_ref[...]).astype(BF16)

    xn = xn_ref[...]
    b_ref[...] = _dot(xn, wb_ref[...])
    z_ref[...] = _dot(xn, wc_ref[...]) * _dot(xn, wu_ref[...])


def _conv_out_kernel(b_ref, z_ref, zh_ref, cw_ref, w_ref, r_ref, o_ref, *, tm):
    i = pl.program_id(0)
    z = z_ref[...]
    zh = jnp.where(i > 0, zh_ref[...], 0.0)
    row = lax.broadcasted_iota(jnp.int32, (tm, 1), 0)
    prev1 = zh[CONV_HALO - 1:CONV_HALO, :]
    prev2 = zh[CONV_HALO - 2:CONV_HALO - 1, :]
    z1 = jnp.where(row == 0, prev1, pltpu.roll(z, 1, 0))
    z2 = jnp.where(row == 0, prev2, jnp.where(row == 1, prev1, pltpu.roll(z, 2, 0)))
    cw = cw_ref[...]
    conv = cw[0:1, :] * z2 + cw[1:2, :] * z1 + cw[2:3, :] * z
    y = (b_ref[...] * conv).astype(BF16)
    o_ref[...] = r_ref[...] + _dot(y, w_ref[...])


def _conv_layer(h, g, w_in, conv_w, w_out):
    S, D = h.shape
    tm = min(1024, S)
    tn = 512
    nj = D // tn
    w_in = w_in.astype(BF16)
    b, z = pl.pallas_call(
        _conv_in_kernel,
        grid=(S // tm, nj),
        in_specs=[
            pl.BlockSpec((tm, D), lambda i, j: (i, 0)),
            pl.BlockSpec((1, D), lambda i, j: (0, 0)),
            pl.BlockSpec((D, tn), lambda i, j: (0, j)),
            pl.BlockSpec((D, tn), lambda i, j: (0, nj + j)),
            pl.BlockSpec((D, tn), lambda i, j: (0, 2 * nj + j)),
        ],
        out_specs=[pl.BlockSpec((tm, tn), lambda i, j: (i, j)),
                   pl.BlockSpec((tm, tn), lambda i, j: (i, j))],
        out_shape=[jax.ShapeDtypeStruct((S, D), F32), jax.ShapeDtypeStruct((S, D), F32)],
        scratch_shapes=[pltpu.VMEM((tm, D), BF16)],
        compiler_params=_cparams(("parallel", "arbitrary")),
        name="conv_in",
    )(h, g.reshape(1, D), w_in, w_in, w_in)
    cw = jnp.pad(conv_w, ((0, 8 - CONV_WIDTH), (0, 0)))
    return pl.pallas_call(
        functools.partial(_conv_out_kernel, tm=tm),
        grid=(S // tm,),
        in_specs=[
            pl.BlockSpec((tm, D), lambda i: (i, 0)),
            pl.BlockSpec((tm, D), lambda i: (i, 0)),
            pl.BlockSpec((CONV_HALO, D), lambda i: (jnp.maximum(i * (tm // CONV_HALO) - 1, 0), 0)),
            pl.BlockSpec((8, D), lambda i: (0, 0)),
            pl.BlockSpec((D, D), lambda i: (0, 0)),
            pl.BlockSpec((tm, D), lambda i: (i, 0)),
        ],
        out_specs=pl.BlockSpec((tm, D), lambda i: (i, 0)),
        out_shape=jax.ShapeDtypeStruct((S, D), F32),
        compiler_params=_cparams(("parallel",)),
        name="conv_out",
    )(b, z, z, cw, w_out.astype(BF16), h)


def _nsa_cmp_kernel(r_ref, w1_ref, pe_ref, w2_ref, rc_ref, rs1_ref, rs2_ref, o_ref, acc_ref, *, nc):
    kv = pl.program_id(0)
    hd = pl.program_id(1)
    half = CMP_STRIDE * HEAD_DIM
    r = r_ref[0, 0]
    w1 = w1_ref[0]
    first = _dot(r, w1[:half, :])
    second = _dot(r, w1[half:, :])
    pe_term = _dot(pe_ref[0], w1)[0:1, :]
    pre = first + pltpu.roll(second, nc - 1, 0) + pe_term
    ge = 0.5 * pre * (1.0 + jnp.tanh(0.7978845608028654 * (pre + 0.044715 * pre * pre * pre)))
    y = _dot(ge.astype(BF16), w2_ref[0, 0])

    @pl.when(hd == 0)
    def _():
        acc_ref[...] = y

    @pl.when(hd > 0)
    def _():
        acc_ref[...] += y

    @pl.when(hd == NSA_KV_HEADS - 1)
    def _():
        acc = acc_ref[...]

        @pl.when(kv == 0)
        def _():
            rc, rs1, rs2 = rc_ref[...], rs1_ref[...], rs2_ref[...]
            o_ref[0] = jnp.concatenate(
                [_rope(acc[:, :LANES], rc, rs1, rs2, 1), _rope(acc[:, LANES:], rc, rs1, rs2, 1)],
                axis=1).astype(o_ref.dtype)

        @pl.when(kv == 1)
        def _():
            o_ref[0] = acc.astype(o_ref.dtype)


def _nsa_cmp(r, w1, pe, w2p, rc, rs1, rs2):
    nc = r.shape[2]
    kvd = NSA_KV_HEADS * HEAD_DIM
    return pl.pallas_call(
        functools.partial(_nsa_cmp_kernel, nc=nc),
        grid=(2, NSA_KV_HEADS),
        in_specs=[
            pl.BlockSpec((1, 1, nc, CMP_STRIDE * HEAD_DIM), lambda a, b: (a, b, 0, 0)),
            pl.BlockSpec((1, CMP_BLOCK * HEAD_DIM, CMP_HIDDEN), lambda a, b: (a, 0, 0)),
            pl.BlockSpec((1, 8, CMP_BLOCK * HEAD_DIM), lambda a, b: (a, 0, 0)),
            pl.BlockSpec((1, 1, CMP_HIDDEN, kvd), lambda a, b: (a, b, 0, 0)),
            pl.BlockSpec((nc, LANES), lambda a, b: (0, 0)),
            pl.BlockSpec((nc, LANES), lambda a, b: (0, 0)),
            pl.BlockSpec((nc, LANES), lambda a, b: (0, 0)),
        ],
        out_specs=pl.BlockSpec((1, nc, kvd), lambda a, b: (a, 0, 0)),
        out_shape=jax.ShapeDtypeStruct((2, nc, kvd), BF16),
        scratch_shapes=[pltpu.VMEM((nc, kvd), F32)],
        compiler_params=_cparams(("parallel", "arbitrary")),
        name="nsa_cmp",
    )(r, w1, pe, w2p, rc, rs1, rs2)


def _nsa_cmp_attn_body(i, qt_ref, kc_ref, vct_ref, w_ref, oct_ref, selt_ref, *, t, n_tiles, ct, nsp):
    qpos = i * t + lax.broadcasted_iota(jnp.int32, (1, t), 1)
    any_valid = qpos >= CMP_BLOCK - 1
    low = lax.broadcasted_iota(jnp.int32, (LANES, t), 0) < HEAD_DIM
    per_tile = ct * CMP_STRIDE // SLC_BLOCK
    stat_rows = jnp.concatenate([jnp.ones((SUM_ROWS, ct), BF16), w_ref[...]], axis=0)
    cmasks = [CMP_STRIDE * (c * ct + lax.broadcasted_iota(jnp.int32, (ct, t), 0)) + (CMP_BLOCK - 1) <= qpos
              for c in range(n_tiles)]
    blk = lax.broadcasted_iota(jnp.int32, (nsp, t), 0)
    blk_f = blk.astype(F32)
    cur = jnp.right_shift(qpos, SLC_SHIFT)
    forced = jnp.logical_or(blk == 0, jnp.logical_or(blk == cur, blk == cur - 1))
    valid = blk * SLC_BLOCK <= qpos
    vals = []
    for pair in range(NSA_KV_HEADS // 2):
        lanes = slice(pair * LANES, (pair + 1) * LANES)
        lhs = [jnp.concatenate([vct_ref[lanes, c * ct:(c + 1) * ct], stat_rows], axis=0) for c in range(n_tiles)]
        imp = [[None] * n_tiles, [None] * n_tiles]
        for g in range(NSA_GROUP):
            rows = slice((pair * NSA_GROUP + g) * LANES, (pair * NSA_GROUP + g + 1) * LANES)
            qh = _head_rows(qt_ref[rows, :])
            oc = []
            for h in range(2):
                m = jnp.full((1, t), NEG, F32)
                acc = jnp.zeros((LANES + SUM_ROWS, t), F32)
                parts, maxes = [], []
                for c in range(n_tiles):
                    s = jnp.where(cmasks[c], _dot(kc_ref[0, c * ct:(c + 1) * ct, lanes], qh[h]), NEG)
                    m_new = jnp.maximum(m, jnp.max(s, axis=0, keepdims=True))
                    p = jnp.exp2(s - m_new)
                    p_hi = p.astype(BF16)
                    p_lo = (p - p_hi.astype(F32)).astype(BF16)
                    r_hi = _dot(lhs[c], p_hi)
                    r_lo = _dot(stat_rows, p_lo)
                    stats = r_hi[LANES:, :] + r_lo
                    acc = jnp.exp2(m - m_new) * acc + jnp.concatenate([r_hi[:LANES, :], stats[:SUM_ROWS, :]], axis=0)
                    parts.append(stats[SUM_ROWS:, :])
                    maxes.append(m_new)
                    m = m_new
                inv = jnp.where(any_valid, 1.0 / acc[LANES:LANES + 1, :], 0.0)
                oc.append(acc[:LANES, :] * inv)
                for c in range(n_tiles):
                    piece = parts[c] * (jnp.exp2(maxes[c] - m) * inv)
                    imp[h][c] = piece if imp[h][c] is None else imp[h][c] + piece
            oct_ref[rows, :] = jnp.where(low, oc[0], oc[1]).astype(oct_ref.dtype)
        for h in range(2):
            segs = []
            for c in range(n_tiles):
                seg = imp[h][c][:per_tile, :]
                if c > 0:
                    spill = imp[h][c - 1][per_tile:per_tile + IMP_PAD, :]
                    seg = jnp.concatenate([seg[:IMP_PAD, :] + spill, seg[IMP_PAD:, :]], axis=0)
                segs.append(seg)
            score = jnp.concatenate(segs, axis=0)[:nsp, :]
            vals.append(jnp.where(valid, jnp.where(forced, FORCE_SCORE, score), NEG))

    def pick_one(_, vals):
        out = []
        for v in vals:
            mx = jnp.max(v, axis=0, keepdims=True)
            first = jnp.min(jnp.where(v == mx, blk_f, float(nsp)), axis=0, keepdims=True)
            out.append(jnp.where(blk_f == first, -jnp.inf, v))
        return tuple(out)

    vals = lax.fori_loop(0, SLC_TOPK, pick_one, tuple(vals))
    for kvh in range(NSA_KV_HEADS):
        picked = jnp.logical_and(valid, vals[kvh] == -jnp.inf)
        selt_ref[kvh, :nsp, :] = jnp.where(picked, 1.0, 0.0).astype(selt_ref.dtype)
        if nsp < selt_ref.shape[1]:
            selt_ref[kvh, nsp:, :] = jnp.zeros((selt_ref.shape[1] - nsp, t), selt_ref.dtype)


def _nsa_cmp_attn_kernel(qt_ref, kc_ref, vct_ref, w_ref, oct_ref, selt_ref, *, t, n_steps, n_spans):
    i = pl.program_id(0)
    ct = w_ref.shape[1]
    for q in range(n_spans):
        @pl.when(jnp.logical_and(i >= q * n_steps // n_spans, i < (q + 1) * n_steps // n_spans))
        def _(q=q):
            _nsa_cmp_attn_body(i, qt_ref, kc_ref, vct_ref, w_ref, oct_ref, selt_ref, t=t, n_tiles=q + 1, ct=ct,
                               nsp=(q + 1) * ct * CMP_STRIDE // SLC_BLOCK)


def _nsa_cmp_attn(proj_t, kvc, vct, w_imp, nsp):
    S = proj_t.shape[1]
    t = NSA_TILE
    nc = kvc.shape[1]
    imp_rows, ct = w_imp.shape
    kvd = NSA_KV_HEADS * HEAD_DIM
    return pl.pallas_call(
        functools.partial(_nsa_cmp_attn_kernel, t=t, n_steps=S // t, n_spans=nc // ct),
        grid=(S // t,),
        in_specs=[
            pl.BlockSpec((D_MODEL, t), lambda i: (0, i)),
            pl.BlockSpec((1, nc, kvd), lambda i: (0, 0, 0)),
            pl.BlockSpec((kvd, nc), lambda i: (0, 0)),
            pl.BlockSpec((imp_rows, ct), lambda i: (0, 0)),
        ],
        out_specs=[pl.BlockSpec((D_MODEL, t), lambda i: (0, i)),
                   pl.BlockSpec((NSA_KV_HEADS, nsp, t), lambda i: (0, 0, i))],
        out_shape=[jax.ShapeDtypeStruct((D_MODEL, S), BF16),
                   jax.ShapeDtypeStruct((NSA_KV_HEADS, nsp, S), BF16)],
        compiler_params=_cparams(("parallel",)),
        name="nsa_cmp_attn",
    )(proj_t, kvc, vct, w_imp)


def _nsa_sel_kernel(qt_ref, ks_ref, vst_ref, selt_ref, kw0_ref, kw1_ref, kw2_ref, vwt0_ref, vwt1_ref, vwt2_ref,
                    gzt_ref, et_ref, oct_ref, ot_ref, s0_ref, s1_ref, bias_ref, m_ref, acc_ref, *, t):
    i = pl.program_id(1)
    n_heads = 2 * NSA_GROUP
    low = lax.broadcasted_iota(jnp.int32, (LANES, t), 0) < HEAD_DIM
    qs = []
    for g in range(NSA_GROUP):
        qs.extend(_head_rows(qt_ref[g * LANES:(g + 1) * LANES, :]))
    bias_ref[...] = ((selt_ref[...].astype(F32) - 1.0) * -NEG).astype(BF16)
    m_ref[...] = jnp.full(m_ref.shape, NEG, F32)
    acc_ref[...] = jnp.zeros(acc_ref.shape, F32)
    tok_blk = jnp.right_shift(lax.broadcasted_iota(jnp.int32, (t, LANES), 0), SLC_SHIFT)
    lane_col = lax.broadcasted_iota(jnp.int32, (t, LANES), 1)
    zero_rows = jnp.zeros((LANES - SEL_GROUP, t), BF16)
    tiles_per_group = SEL_GROUP * SLC_BLOCK // t
    s_bufs = (s0_ref, s1_ref)

    def scores(j, buf):
        k = ks_ref[pl.ds(pl.multiple_of(j * t, t), t), :]
        grp = j // tiles_per_group
        first_blk = (j % tiles_per_group) * (t // SLC_BLOCK)
        expand = jnp.where(lane_col == first_blk + tok_blk, 1.0, 0.0).astype(BF16)
        lhs = jnp.concatenate([k, expand], axis=1)
        for h in range(2):
            blk_bias = bias_ref[h, grp]
            for g in range(NSA_GROUP):
                rhs = jnp.concatenate([qs[2 * g + h], blk_bias, zero_rows], axis=0)
                s_bufs[buf][2 * g + h] = _dot(lhs, rhs)

    def consume(j, buf, diag):
        vt = _with_ones_rows(vst_ref[:, pl.ds(pl.multiple_of(j * t, t), t)])
        for idx in range(n_heads):
            _online_update(s_bufs[buf], vt, m_ref, acc_ref, idx, diag)

    _pipelined_tiles(i, scores, consume, pairs_per_step=2)

    n_win = WIN // t + 1
    k_win = jnp.concatenate([kw0_ref[...], kw1_ref[...], kw2_ref[...]], axis=0)
    vt_win = jnp.concatenate([vwt0_ref[...], vwt1_ref[...], vwt2_ref[...]], axis=1)
    kpos = (i - (n_win - 1)) * t + lax.broadcasted_iota(jnp.int32, (n_win * t, t), 0)
    qpos = i * t + lax.broadcasted_iota(jnp.int32, (n_win * t, t), 1)
    wmask = jnp.logical_and(jnp.logical_and(kpos <= qpos, kpos > qpos - WIN), kpos >= 0)
    ow = []
    for idx in range(n_heads):
        s = jnp.where(wmask, _dot(k_win, qs[idx]), NEG)
        p = jnp.exp2(s - jnp.max(s, axis=0, keepdims=True))
        l = jnp.sum(p, axis=0, keepdims=True)
        ow.append(_dot(vt_win, p.astype(BF16)) / l)

    gates = jax.nn.sigmoid(_dot(et_ref[0], gzt_ref[...]))
    gw = NSA_GROUP * LANES
    for g in range(NSA_GROUP):
        rows = slice(g * LANES, (g + 1) * LANES)
        o_cmp = oct_ref[rows, :].astype(F32)
        o_slc = jnp.where(low, _online_result(acc_ref, 2 * g), _online_result(acc_ref, 2 * g + 1))
        o_win = jnp.where(low, ow[2 * g], ow[2 * g + 1])
        out = (gates[g * LANES:(g + 1) * LANES, :] * o_cmp
               + gates[gw + g * LANES:gw + (g + 1) * LANES, :] * o_slc
               + gates[2 * gw + g * LANES:2 * gw + (g + 1) * LANES, :] * o_win)
        ot_ref[rows, :] = out.astype(ot_ref.dtype)


def _nsa_sel(nat, proj_t, selt, oct, gate_expand_t):
    S = nat.shape[0]
    t = NSA_TILE
    n_grp = selt.shape[1]
    n_heads = 2 * NSA_GROUP
    gl = NSA_GROUP * LANES

    def kwin_spec(back):
        return pl.BlockSpec((t, LANES), lambda p, i: (jnp.maximum(i - back, 0), NAT_K_WIN + p))

    def vwin_spec(back):
        return pl.BlockSpec((LANES, t), lambda p, i: (T_V_WIN + p, jnp.maximum(i - back, 0)))

    return pl.pallas_call(
        functools.partial(_nsa_sel_kernel, t=t),
        grid=(NSA_KV_HEADS // 2, S // t),
        in_specs=[
            pl.BlockSpec((gl, t), lambda p, i: (p, i)),
            pl.BlockSpec((S, LANES), lambda p, i: (0, NAT_K_SLC + p)),
            pl.BlockSpec((LANES, S), lambda p, i: (T_V_SLC + p, 0)),
            pl.BlockSpec((2, n_grp, SEL_GROUP, t), lambda p, i: (p, 0, 0, i)),
            kwin_spec(2), kwin_spec(1), kwin_spec(0),
            vwin_spec(2), vwin_spec(1), vwin_spec(0),
            pl.BlockSpec((NSA_PROJ_TILE, t), lambda p, i: (T_GATE * LANES // NSA_PROJ_TILE, i)),
            pl.BlockSpec((1, 3 * gl, NSA_PROJ_TILE), lambda p, i: (p, 0, 0)),
            pl.BlockSpec((gl, t), lambda p, i: (p, i)),
        ],
        out_specs=pl.BlockSpec((gl, t), lambda p, i: (p, i)),
        out_shape=jax.ShapeDtypeStruct((D_MODEL, S), BF16),
        scratch_shapes=([pltpu.VMEM((n_heads, t, t), F32)] * 2 + [pltpu.VMEM((2, n_grp, SEL_GROUP, t), BF16)]
                        + [pltpu.VMEM((n_heads, 1, t), F32)] + [pltpu.VMEM((n_heads, LANES + SUM_ROWS, t), F32)]),
        compiler_params=_cparams(("parallel", "arbitrary")),
        name="nsa_sel",
    )(proj_t, nat, proj_t, selt, nat, nat, nat, proj_t, proj_t, proj_t, proj_t, gate_expand_t, oct)


def _rope_tables(positions, key_major=False):
    half = ROT_DIM // 2
    inv = ROPE_THETA ** (-jnp.arange(0, ROT_DIM, 2, dtype=F32) / ROT_DIM)
    pos = positions.astype(F32)
    n = positions.shape[0]
    axis = 0 if key_major else 1
    ang = inv[:, None] * pos[None, :] if key_major else pos[:, None] * inv[None, :]
    cos, sin = jnp.cos(ang), jnp.sin(ang)

    def const(width, value):
        return jnp.full((width, n) if key_major else (n, width), value, F32)

    rest = HEAD_DIM - ROT_DIM
    rc = jnp.concatenate([cos, cos, const(rest, 1.0)], axis=axis)
    rs1 = jnp.concatenate([-sin, const(half, 0.0), const(rest, 0.0)], axis=axis)
    rs2 = jnp.concatenate([const(half, 0.0), sin, const(rest, 0.0)], axis=axis)
    reps = (LANES // HEAD_DIM, 1) if key_major else (1, LANES // HEAD_DIM)
    return jnp.tile(rc, reps), jnp.tile(rs1, reps), jnp.tile(rs2, reps)


def _pair_heads(w, axis):
    shape = w.shape
    w = w.reshape(shape[:axis] + (NSA_KV_HEADS // 2, 2, NSA_GROUP, HEAD_DIM) + shape[axis + 1:])
    w = jnp.swapaxes(w, axis + 1, axis + 2)
    return w.reshape(shape)


def _nsa_constants(S):
    nc = S // CMP_STRIDE
    ns = S // SLC_BLOCK
    nsp = -(-ns // LANES) * LANES
    ct = min(CMP_TILE, nc)
    imp_rows = -(-(ct * CMP_STRIDE // SLC_BLOCK + IMP_PAD) // SUM_ROWS) * SUM_ROWS
    ci = np.arange(ct)[None, :] * CMP_STRIDE
    st = np.arange(imp_rows)[:, None] * SLC_BLOCK
    w_imp = (ci < st + SLC_BLOCK) & (ci + CMP_BLOCK > st)
    e = np.zeros((NSA_KV_HEADS // 2, 3 * NSA_GROUP * LANES, NSA_PROJ_TILE), np.float32)
    for p in range(NSA_KV_HEADS // 2):
        for br in range(3):
            for g in range(NSA_GROUP):
                for hf in range(2):
                    src = br * N_HEADS + (2 * p + hf) * NSA_GROUP + g
                    r0 = br * NSA_GROUP * LANES + g * LANES + hf * HEAD_DIM
                    e[p, r0:r0 + HEAD_DIM, src] = 1.0
    cmp_end = np.minimum(np.arange(nc) * CMP_STRIDE + CMP_BLOCK - 1, S - 1)
    return nc, nsp, jnp.asarray(w_imp.astype(np.float32), BF16), jnp.asarray(e, BF16), cmp_end


def _nsa_layer(h, g, positions, w_in, pe_k, w1_k, w2_k, pe_v, w1_v, w2_v, w_o):
    S, D = h.shape
    qd = N_HEADS * HEAD_DIM
    kvd = NSA_KV_HEADS * HEAD_DIM
    nc, nsp, w_imp, gate_expand_t, cmp_end = _nsa_constants(S)
    rc, rs1, rs2 = _rope_tables(positions)

    def kv_piece(n):
        return w_in[:, qd + n * kvd:qd + (n + 1) * kvd]

    w_nat = jnp.concatenate([kv_piece(0), kv_piece(1), kv_piece(2), kv_piece(4)], axis=1).astype(BF16)
    wg = jnp.pad(w_in[:, qd + 6 * kvd:], ((0, 0), (0, NSA_PROJ_TILE - 3 * N_HEADS)))
    w_t = jnp.concatenate([_pair_heads(w_in[:, :qd], 1) * QK_SCALE, kv_piece(3), kv_piece(5), wg], axis=1).T.astype(BF16)
    nat = _proj(h, g, w_nat, "nsa_proj", key_major=False, tn=2 * NSA_PROJ_TILE,
                rope=(rc, rs1, rs2), rope_tiles=NSA_NAT_ROPE_TILES)
    proj_t = _proj(h, g, w_t, "nsa_proj_t", key_major=True, tn=NSA_PROJ_TILE,
                   rope=_rope_tables(positions, key_major=True), rope_tiles=NSA_T_ROPE_TILES)

    raw = nat[:, :2 * kvd].reshape(nc, CMP_STRIDE, 2, NSA_KV_HEADS, HEAD_DIM)
    raw = raw.transpose(2, 3, 0, 1, 4).reshape(2, NSA_KV_HEADS, nc, CMP_STRIDE * HEAD_DIM)
    w1 = jnp.stack([w1_k, w1_v]).astype(BF16)
    pe = jnp.stack([pe_k.reshape(1, -1), pe_v.reshape(1, -1)])
    pe = jnp.pad(pe, ((0, 0), (0, 7), (0, 0))).astype(BF16)
    w2 = jnp.stack([w2_k, w2_v])
    eye = jnp.eye(NSA_KV_HEADS, dtype=F32)
    w2p = (w2[:, None, :, None, :] * eye[None, :, None, :, None]).reshape(2, NSA_KV_HEADS, CMP_HIDDEN, kvd).astype(BF16)
    kvc = _nsa_cmp(raw, w1, pe, w2p, *_rope_tables(positions[cmp_end]))

    oct, selt = _nsa_cmp_attn(proj_t, kvc, kvc[1].T, w_imp, nsp)
    selt = selt.reshape(NSA_KV_HEADS, -1, SEL_GROUP, S)
    ot = _nsa_sel(nat, proj_t, selt, oct, gate_expand_t)
    return _matmul_res(ot, _pair_heads(w_o, 0).astype(BF16), h, "nsa_out")


def _trunk(x2, positions, p):
    h = _fox_layer(x2, p["l0_norm_mix"], p["l0_fox_w_qkv"], p["l0_fox_w_f"], p["l0_fox_b_f"], p["l0_fox_w_o"])
    h = _mlp(h, p["l0_norm_mlp"], p["l0_mlp_w1"].astype(BF16), p["l0_mlp_w2"].astype(BF16))
    h = _pool_layer(h, p["l1_norm_mix"], p["l1_pool_w"], p["l1_pool_scale"])
    h = _mlp(h, p["l1_norm_mlp"], p["l1_mlp_w1"].astype(BF16), p["l1_mlp_w2"].astype(BF16))
    h = _conv_layer(h, p["l2_norm_mix"], p["l2_conv_w_in"], p["l2_conv_w"], p["l2_conv_w_out"])
    h = _mlp(h, p["l2_norm_mlp"], p["l2_mlp_w1"].astype(BF16), p["l2_mlp_w2"].astype(BF16))
    h = _nsa_layer(h, p["l3_norm_mix"], positions, p["l3_nsa_w_in"], p["l3_nsa_cmp_pe_k"], p["l3_nsa_cmp_w1_k"],
                   p["l3_nsa_cmp_w2_k"], p["l3_nsa_cmp_pe_v"], p["l3_nsa_cmp_w1_v"], p["l3_nsa_cmp_w2_v"],
                   p["l3_nsa_w_o"])
    return _mlp(h, p["l3_norm_mlp"], p["l3_mlp_w1"].astype(BF16), p["l3_mlp_w2"].astype(BF16), p["final_norm"])


def kernel(x, positions, l0_norm_mix, l0_fox_w_qkv, l0_fox_w_f, l0_fox_b_f, l0_fox_w_o, l0_norm_mlp, l0_mlp_w1, l0_mlp_w2, l1_norm_mix, l1_pool_w, l1_pool_scale, l1_norm_mlp, l1_mlp_w1, l1_mlp_w2, l2_norm_mix, l2_conv_w_in, l2_conv_w, l2_conv_w_out, l2_norm_mlp, l2_mlp_w1, l2_mlp_w2, l3_norm_mix, l3_nsa_w_in, l3_nsa_cmp_pe_k, l3_nsa_cmp_w1_k, l3_nsa_cmp_w2_k, l3_nsa_cmp_pe_v, l3_nsa_cmp_w1_v, l3_nsa_cmp_w2_v, l3_nsa_w_o, l3_norm_mlp, l3_mlp_w1, l3_mlp_w2, final_norm):
    params = dict(locals())
    B, S, D = x.shape
    outs = [_trunk(x[b], positions, params) for b in range(B)]
    return jnp.stack(outs, axis=0)
```

```python
import functools

import numpy as np
import jax
import jax.numpy as jnp
from jax import lax
from jax.experimental import pallas as pl
from jax.experimental.pallas import tpu as pltpu

F32 = jnp.float32
BF16 = jnp.bfloat16

D_MODEL = 1024
HEAD_DIM = 64
N_HEADS = D_MODEL // HEAD_DIM
D_FF = 4 * D_MODEL
ROPE_THETA = 500000.0
ROT_DIM = HEAD_DIM // 4
RMS_EPS = 1e-6
POOL_WINDOWS = (2, 4, 8, 16)
POOL_GROUP = D_MODEL // len(POOL_WINDOWS)
POOL_HALO = 16
CONV_WIDTH = 3
CONV_HALO = 8
NSA_KV_HEADS = 4
NSA_GROUP = N_HEADS // NSA_KV_HEADS
CMP_BLOCK = 32
CMP_STRIDE = 16
CMP_HIDDEN = 256
SLC_BLOCK = 64
SLC_SHIFT = 6
SLC_TOPK = 16
WIN = 512
FORCE_SCORE = 1e9
NEG = -1e30
LOG2E = 1.4426950408889634
QK_SCALE = HEAD_DIM ** -0.5 * LOG2E

LANES = 128
V7X_VMEM_LIMIT = 56 * 1024 * 1024
FOX_TILE = 512
FOX_BIAS_TERMS = 3
NSA_TILE = WIN // 2
SEL_GROUP = 16
SUM_ROWS = 16
CMP_TILE = 256
IMP_PAD = 8
NSA_PROJ_TILE = 2 * LANES
NSA_NAT_WIDTH = 8 * LANES
NAT_K_SLC, NAT_K_WIN = 4, 6
NSA_NAT_ROPE_TILES = (1,)
NSA_T_ROWS = 16 * LANES
T_V_SLC, T_V_WIN, T_GATE = 8, 10, 12
NSA_T_ROPE_TILES = (0, 1)


def _cparams(semantics):
    return pltpu.CompilerParams(dimension_semantics=semantics, vmem_limit_bytes=V7X_VMEM_LIMIT)


def _rms(x, g):
    return x * lax.rsqrt(jnp.mean(x * x, axis=-1, keepdims=True) + RMS_EPS) * g


def _dot(a, b):
    return jnp.dot(a, b, preferred_element_type=F32)


def _dot_nt(a, b):
    return lax.dot_general(a, b, (((1,), (1,)), ((), ())), preferred_element_type=F32)


def _dot_tn(a, b):
    return lax.dot_general(a, b, (((0,), (0,)), ((), ())), preferred_element_type=F32)


def _split3(x):
    hi = x.astype(BF16)
    r = x - hi.astype(F32)
    mid = r.astype(BF16)
    lo = (r - mid.astype(F32)).astype(BF16)
    return hi, mid, lo


def _rope(a, rc, rs1, rs2, axis):
    half = ROT_DIM // 2
    return a * rc + pltpu.roll(a, LANES - half, axis) * rs1 + pltpu.roll(a, half, axis) * rs2


def _head_rows(qt_blk):
    low = lax.broadcasted_iota(jnp.int32, qt_blk.shape, 0) < HEAD_DIM
    qf = qt_blk.astype(F32)
    return jnp.where(low, qf, 0.0).astype(BF16), jnp.where(low, 0.0, qf).astype(BF16)


def _with_ones_rows(vt):
    return jnp.concatenate([vt, jnp.ones((SUM_ROWS, vt.shape[1]), vt.dtype)], axis=0)


def _online_update(s_ref, vt_ones, m_ref, acc_ref, idx, causal):
    tk, t = s_ref.shape[1:]
    s = s_ref[idx]
    if causal:
        s = jnp.where(lax.broadcasted_iota(jnp.int32, (tk, t), 0) <= lax.broadcasted_iota(jnp.int32, (tk, t), 1),
                      s, NEG)
    m_old = m_ref[idx]
    m_new = jnp.maximum(m_old, jnp.max(s, axis=0, keepdims=True))
    p = jnp.exp2(s - m_new).astype(BF16)
    m_ref[idx] = m_new
    acc_ref[idx] = jnp.exp2(m_old - m_new) * acc_ref[idx] + _dot(vt_ones, p)


def _online_result(acc_ref, idx):
    acc = acc_ref[idx]
    return acc[:LANES, :] / acc[LANES:LANES + 1, :]


def _pipelined_tiles(i, scores, consume, pairs_per_step=1):
    scores(0, 0)

    def pair(j):
        scores(j + 1, 1)
        consume(j, 0, False)
        scores(j + 2, 0)
        consume(j + 1, 1, False)

    def body(jj, carry):
        for u in range(pairs_per_step):
            pair(2 * (pairs_per_step * jj + u))
        return carry

    n_pairs = i // 2
    n_steps = n_pairs // pairs_per_step
    lax.fori_loop(0, n_steps, body, 0)
    if pairs_per_step > 1:
        def tail(jj, carry):
            pair(2 * jj)
            return carry

        lax.fori_loop(n_steps * pairs_per_step, n_pairs, tail, 0)

    @pl.when(i % 2 == 0)
    def _():
        consume(i, 0, True)

    @pl.when(i % 2 == 1)
    def _():
        scores(i, 1)
        consume(i - 1, 0, False)
        consume(i, 1, True)


def _mlp_kernel(x_ref, g_ref, w1_ref, w2_ref, *rest, nf, final):
    if final:
        fg_ref, o_ref, xn_ref = rest
    else:
        o_ref, xn_ref = rest
    f = pl.program_id(1)

    @pl.when(f == 0)
    def _():
        x = x_ref[...]
        xn_ref[...] = _rms(x, g_ref[...]).astype(BF16)
        o_ref[...] = x

    a = _dot(xn_ref[...], w1_ref[...])
    a = jnp.square(jnp.maximum(a, 0.0)).astype(BF16)
    o_ref[...] += _dot(a, w2_ref[...])

    if final:
        @pl.when(f == nf - 1)
        def _():
            o_ref[...] = _rms(o_ref[...], fg_ref[...])


def _mlp(h, g, w1, w2, final_g=None):
    S, D = h.shape
    F = w1.shape[1]
    tm = min(1024, S)
    tf = 2048
    nf = F // tf
    final = final_g is not None
    in_specs = [
        pl.BlockSpec((tm, D), lambda i, f: (i, 0)),
        pl.BlockSpec((1, D), lambda i, f: (0, 0)),
        pl.BlockSpec((D, tf), lambda i, f: (0, f)),
        pl.BlockSpec((tf, D), lambda i, f: (f, 0)),
    ]
    args = [h, g.reshape(1, D), w1, w2]
    if final:
        in_specs.append(pl.BlockSpec((1, D), lambda i, f: (0, 0)))
        args.append(final_g.reshape(1, D))
    return pl.pallas_call(
        functools.partial(_mlp_kernel, nf=nf, final=final),
        grid=(S // tm, nf),
        in_specs=in_specs,
        out_specs=pl.BlockSpec((tm, D), lambda i, f: (i, 0)),
        out_shape=jax.ShapeDtypeStruct((S, D), F32),
        scratch_shapes=[pltpu.VMEM((tm, D), BF16)],
        compiler_params=_cparams(("parallel", "arbitrary")),
        name="mlp",
    )(*args)


def _proj_kernel(x_ref, g_ref, w_ref, *rest, key_major, rope_tiles):
    if rope_tiles:
        rc_ref, rs1_ref, rs2_ref, o_ref, xn_ref = rest
    else:
        o_ref, xn_ref = rest
    j = pl.program_id(1)

    @pl.when(j == 0)
    def _():
        xn_ref[...] = _rms(x_ref[...], g_ref[...]).astype(BF16)

    a = _dot_nt(w_ref[...], xn_ref[...]) if key_major else _dot(xn_ref[...], w_ref[...])
    if not rope_tiles:
        o_ref[...] = a.astype(o_ref.dtype)
        return
    axis = 0 if key_major else 1
    is_rope = functools.reduce(jnp.logical_or, [j == t for t in rope_tiles])

    @pl.when(is_rope)
    def _():
        rc, rs1, rs2 = rc_ref[...], rs1_ref[...], rs2_ref[...]
        blocks = [lax.slice_in_dim(a, b * LANES, (b + 1) * LANES, axis=axis) for b in range(a.shape[axis] // LANES)]
        o_ref[...] = jnp.concatenate([_rope(blk, rc, rs1, rs2, axis) for blk in blocks], axis=axis).astype(o_ref.dtype)

    @pl.when(jnp.logical_not(is_rope))
    def _():
        o_ref[...] = a.astype(o_ref.dtype)


def _proj(h, g, w, name, *, key_major, tn, rope=None, rope_tiles=()):
    S, D = h.shape
    tm = min(1024, S)
    if key_major:
        N = w.shape[0]
        w_spec = pl.BlockSpec((tn, D), lambda i, j: (j, 0))
        rope_spec = pl.BlockSpec((LANES, tm), lambda i, j: (0, i))
        out_spec = pl.BlockSpec((tn, tm), lambda i, j: (j, i))
        out_shape = jax.ShapeDtypeStruct((N, S), BF16)
    else:
        N = w.shape[1]
        w_spec = pl.BlockSpec((D, tn), lambda i, j: (0, j))
        rope_spec = pl.BlockSpec((tm, LANES), lambda i, j: (i, 0))
        out_spec = pl.BlockSpec((tm, tn), lambda i, j: (i, j))
        out_shape = jax.ShapeDtypeStruct((S, N), BF16)
    in_specs = [pl.BlockSpec((tm, D), lambda i, j: (i, 0)), pl.BlockSpec((1, D), lambda i, j: (0, 0)), w_spec]
    args = [h, g.reshape(1, D), w]
    if rope_tiles:
        in_specs += [rope_spec] * 3
        args += list(rope)
    return pl.pallas_call(
        functools.partial(_proj_kernel, key_major=key_major, rope_tiles=tuple(rope_tiles)),
        grid=(S // tm, N // tn),
        in_specs=in_specs,
        out_specs=out_spec,
        out_shape=out_shape,
        scratch_shapes=[pltpu.VMEM((tm, D), BF16)],
        compiler_params=_cparams(("parallel", "arbitrary")),
        name=name,
    )(*args)


def _matmul_res_kernel(at_ref, w_ref, r_ref, o_ref):
    o_ref[...] = r_ref[...] + _dot_tn(at_ref[...], w_ref[...])


def _matmul_res(at, w, res, name):
    K, S = at.shape
    N = w.shape[1]
    tm = min(1024, S)
    return pl.pallas_call(
        _matmul_res_kernel,
        grid=(S // tm,),
        in_specs=[
            pl.BlockSpec((K, tm), lambda i: (0, i)),
            pl.BlockSpec((K, N), lambda i: (0, 0)),
            pl.BlockSpec((tm, N), lambda i: (i, 0)),
        ],
        out_specs=pl.BlockSpec((tm, N), lambda i: (i, 0)),
        out_shape=jax.ShapeDtypeStruct((S, N), F32),
        compiler_params=_cparams(("parallel",)),
        name=name,
    )(at, w, res)


def _fox_gate_kernel(x_ref, g_ref, wf_ref, bf_ref, place_ref, c_ref, carry_ref, *, tm):
    @pl.when(pl.program_id(0) == 0)
    def _():
        carry_ref[...] = jnp.zeros_like(carry_ref)

    xn = _rms(x_ref[...], g_ref[...]).astype(BF16)
    z = _dot(xn, wf_ref[...]) + bf_ref[...]
    logf = jnp.minimum(z, 0.0) - jnp.log(1.0 + jnp.exp(-jnp.abs(z)))
    row = lax.broadcasted_iota(jnp.int32, (tm, tm), 0)
    col = lax.broadcasted_iota(jnp.int32, (tm, tm), 1)
    tri = jnp.where(row >= col, 1.0, 0.0).astype(BF16)
    hi, mid, lo = _split3(logf)
    c = _dot(tri, hi) + _dot(tri, mid) + _dot(tri, lo) + carry_ref[...]
    carry_ref[...] = c[tm - 1:tm, :]
    terms = jnp.concatenate(_split3(c * -LOG2E), axis=1)
    c_ref[...] = _dot(terms, place_ref[...]).astype(c_ref.dtype)


def _fox_gate(h, g, wf, bf, place):
    S, D = h.shape
    tm = min(512, S)
    return pl.pallas_call(
        functools.partial(_fox_gate_kernel, tm=tm),
        grid=(S // tm,),
        in_specs=[
            pl.BlockSpec((tm, D), lambda i: (i, 0)),
            pl.BlockSpec((1, D), lambda i: (0, 0)),
            pl.BlockSpec((D, LANES), lambda i: (0, 0)),
            pl.BlockSpec((1, LANES), lambda i: (0, 0)),
            pl.BlockSpec((FOX_BIAS_TERMS * LANES, LANES), lambda i: (0, 0)),
        ],
        out_specs=pl.BlockSpec((tm, LANES), lambda i: (i, 0)),
        out_shape=jax.ShapeDtypeStruct((S, LANES), BF16),
        scratch_shapes=[pltpu.VMEM((1, LANES), F32)],
        compiler_params=_cparams(("arbitrary",)),
        name="fox_gate",
    )(h, g.reshape(1, D), wf, bf, place)


def _fox_attn_kernel(qt_ref, k_ref, c_ref, vt_ref, ot_ref, s0_ref, s1_ref, m_ref, acc_ref, *, t):
    hp = pl.program_id(0)
    i = pl.program_id(1)
    row = lax.broadcasted_iota(jnp.int32, (LANES, t), 0)
    q_heads = _head_rows(qt_ref[...])
    qa = []
    for h in range(2):
        first = FOX_BIAS_TERMS * (2 * hp + h)
        ones_rows = jnp.logical_and(row >= first, row < first + FOX_BIAS_TERMS)
        qa.append(jnp.concatenate([q_heads[h], jnp.where(ones_rows, 1.0, 0.0).astype(BF16)], axis=0))
    m_ref[...] = jnp.full(m_ref.shape, NEG, F32)
    acc_ref[...] = jnp.zeros(acc_ref.shape, F32)
    s_bufs = (s0_ref, s1_ref)

    def scores(j, buf):
        rows = pl.ds(pl.multiple_of(j * t, t), t)
        lhs = jnp.concatenate([k_ref[rows, :], c_ref[rows, :]], axis=1)
        for h in range(2):
            s_bufs[buf][h] = _dot(lhs, qa[h])

    def consume(j, buf, diag):
        vt = _with_ones_rows(vt_ref[:, pl.ds(pl.multiple_of(j * t, t), t)])
        for h in range(2):
            _online_update(s_bufs[buf], vt, m_ref, acc_ref, h, diag)

    _pipelined_tiles(i, scores, consume, pairs_per_step=2)
    ot = jnp.where(row < HEAD_DIM, _online_result(acc_ref, 0), _online_result(acc_ref, 1))
    ot_ref[...] = ot.astype(ot_ref.dtype)


def _fox_attn(qvt, k, c_terms):
    S = k.shape[0]
    t = min(FOX_TILE, S)
    n_pairs = N_HEADS // 2
    return pl.pallas_call(
        functools.partial(_fox_attn_kernel, t=t),
        grid=(n_pairs, S // t),
        in_specs=[
            pl.BlockSpec((LANES, t), lambda hp, i: (hp, i)),
            pl.BlockSpec((S, LANES), lambda hp, i: (0, hp)),
            pl.BlockSpec((S, LANES), lambda hp, i: (0, 0)),
            pl.BlockSpec((LANES, S), lambda hp, i: (n_pairs + hp, 0)),
        ],
        out_specs=pl.BlockSpec((LANES, t), lambda hp, i: (hp, i)),
        out_shape=jax.ShapeDtypeStruct((D_MODEL, S), BF16),
        scratch_shapes=([pltpu.VMEM((2, t, t), F32)] * 2 + [pltpu.VMEM((2, 1, t), F32)]
                        + [pltpu.VMEM((2, LANES + SUM_ROWS, t), F32)]),
        compiler_params=_cparams(("parallel", "arbitrary")),
        name="fox_attn",
    )(qvt, k, c_terms, qvt)


def _fox_bias_placement():
    place = np.zeros((FOX_BIAS_TERMS * LANES, LANES), np.float32)
    for head in range(N_HEADS):
        for n in range(FOX_BIAS_TERMS):
            place[n * LANES + head, FOX_BIAS_TERMS * head + n] = 1.0
    return jnp.asarray(place, BF16)


def _fox_layer(h, g, w_qkv, w_f, b_f, w_o):
    wf = jnp.pad(w_f, ((0, 0), (0, LANES - N_HEADS))).astype(BF16)
    bf = jnp.pad(b_f, (0, LANES - N_HEADS)).reshape(1, LANES)
    c_terms = _fox_gate(h, g, wf, bf, _fox_bias_placement())
    w_qv = jnp.concatenate([w_qkv[:, :D_MODEL] * QK_SCALE, w_qkv[:, 2 * D_MODEL:]], axis=1).T.astype(BF16)
    qvt = _proj(h, g, w_qv, "fox_qv", key_major=True, tn=1024)
    k = _proj(h, g, w_qkv[:, D_MODEL:2 * D_MODEL].astype(BF16), "fox_k", key_major=False, tn=1024)
    ot = _fox_attn(qvt, k, c_terms)
    return _matmul_res(ot, w_o.astype(BF16), h, "fox_out")


def _pool_kernel(x_ref, halo_ref, g_ref, w_ref, sc_ref, o_ref, *, tm):
    i = pl.program_id(0)
    x = x_ref[...]
    g = g_ref[...]
    xn = _rms(x, g)
    hn = jnp.where(i > 0, _rms(halo_ref[...], g), 0.0)
    xe = jnp.concatenate([hn, xn], axis=0)
    tpos = i * tm + lax.broadcasted_iota(jnp.int32, (tm, 1), 0)
    for gi, w in enumerate(POOL_WINDOWS):
        sl = slice(gi * POOL_GROUP, (gi + 1) * POOL_GROUP)
        s = xe[:, sl]
        k = 1
        while k < w:
            s = s + pltpu.roll(s, k, 0)
            k *= 2
        cnt = jnp.minimum(tpos + 1, w).astype(F32)
        d = (s[POOL_HALO:, :] / cnt - xn[:, sl]).astype(BF16)
        o_ref[:, sl] = x[:, sl] + _dot(d, w_ref[gi]) * sc_ref[:, sl]


def _pool_layer(h, g, w_pool, pool_scale):
    S, D = h.shape
    tm = min(1024, S)
    return pl.pallas_call(
        functools.partial(_pool_kernel, tm=tm),
        grid=(S // tm,),
        in_specs=[
            pl.BlockSpec((tm, D), lambda i: (i, 0)),
            pl.BlockSpec((POOL_HALO, D), lambda i: (jnp.maximum(i * (tm // POOL_HALO) - 1, 0), 0)),
            pl.BlockSpec((1, D), lambda i: (0, 0)),
            pl.BlockSpec((len(POOL_WINDOWS), POOL_GROUP, POOL_GROUP), lambda i: (0, 0, 0)),
            pl.BlockSpec((1, D), lambda i: (0, 0)),
        ],
        out_specs=pl.BlockSpec((tm, D), lambda i: (i, 0)),
        out_shape=jax.ShapeDtypeStruct((S, D), F32),
        compiler_params=_cparams(("parallel",)),
        name="pool",
    )(h, h, g.reshape(1, D), w_pool.astype(BF16), pool_scale.reshape(1, D))


def _conv_in_kernel(x_ref, g_ref, wb_ref, wc_ref, wu_ref, b_ref, z_ref, xn_ref):
    @pl.when(pl.program_id(1) == 0)
    def _():
        xn_ref[...] = _rms(x_ref[...], g_ref[...]).astype(BF16)

    xn = xn_ref[...]
    b_ref[...] = _dot(xn, wb_ref[...])
    z_ref[...] = _dot(xn, wc_ref[...]) * _dot(xn, wu_ref[...])


def _conv_out_kernel(b_ref, z_ref, zh_ref, cw_ref, w_ref, r_ref, o_ref, *, tm):
    i = pl.program_id(0)
    z = z_ref[...]
    zh = jnp.where(i > 0, zh_ref[...], 0.0)
    row = lax.broadcasted_iota(jnp.int32, (tm, 1), 0)
    prev1 = zh[CONV_HALO - 1:CONV_HALO, :]
    prev2 = zh[CONV_HALO - 2:CONV_HALO - 1, :]
    z1 = jnp.where(row == 0, prev1, pltpu.roll(z, 1, 0))
    z2 = jnp.where(row == 0, prev2, jnp.where(row == 1, prev1, pltpu.roll(z, 2, 0)))
    cw = cw_ref[...]
    conv = cw[0:1, :] * z2 + cw[1:2, :] * z1 + cw[2:3, :] * z
    y = (b_ref[...] * conv).astype(BF16)
    o_ref[...] = r_ref[...] + _dot(y, w_ref[...])


def _conv_layer(h, g, w_in, conv_w, w_out):
    S, D = h.shape
    tm = min(1024, S)
    tn = 1024
    nj = D // tn
    w_in = w_in.astype(BF16)
    b, z = pl.pallas_call(
        _conv_in_kernel,
        grid=(S // tm, nj),
        in_specs=[
            pl.BlockSpec((tm, D), lambda i, j: (i, 0)),
            pl.BlockSpec((1, D), lambda i, j: (0, 0)),
            pl.BlockSpec((D, tn), lambda i, j: (0, j)),
            pl.BlockSpec((D, tn), lambda i, j: (0, nj + j)),
            pl.BlockSpec((D, tn), lambda i, j: (0, 2 * nj + j)),
        ],
        out_specs=[pl.BlockSpec((tm, tn), lambda i, j: (i, j)),
                   pl.BlockSpec((tm, tn), lambda i, j: (i, j))],
        out_shape=[jax.ShapeDtypeStruct((S, D), F32), jax.ShapeDtypeStruct((S, D), F32)],
        scratch_shapes=[pltpu.VMEM((tm, D), BF16)],
        compiler_params=_cparams(("parallel", "arbitrary")),
        name="conv_in",
    )(h, g.reshape(1, D), w_in, w_in, w_in)
    cw = jnp.pad(conv_w, ((0, 8 - CONV_WIDTH), (0, 0)))
    return pl.pallas_call(
        functools.partial(_conv_out_kernel, tm=tm),
        grid=(S // tm,),
        in_specs=[
            pl.BlockSpec((tm, D), lambda i: (i, 0)),
            pl.BlockSpec((tm, D), lambda i: (i, 0)),
            pl.BlockSpec((CONV_HALO, D), lambda i: (jnp.maximum(i * (tm // CONV_HALO) - 1, 0), 0)),
            pl.BlockSpec((8, D), lambda i: (0, 0)),
            pl.BlockSpec((D, D), lambda i: (0, 0)),
            pl.BlockSpec((tm, D), lambda i: (i, 0)),
        ],
        out_specs=pl.BlockSpec((tm, D), lambda i: (i, 0)),
        out_shape=jax.ShapeDtypeStruct((S, D), F32),
        compiler_params=_cparams(("parallel",)),
        name="conv_out",
    )(b, z, z, cw, w_out.astype(BF16), h)


def _nsa_cmp_kernel(r_ref, w1_ref, pe_ref, w2_ref, rc_ref, rs1_ref, rs2_ref, o_ref, acc_ref, *, nc):
    kv = pl.program_id(0)
    hd = pl.program_id(1)
    half = CMP_STRIDE * HEAD_DIM
    r = r_ref[0, 0]
    w1 = w1_ref[0]
    first = _dot(r, w1[:half, :])
    second = _dot(r, w1[half:, :])
    pe_term = _dot(pe_ref[0], w1)[0:1, :]
    pre = first + pltpu.roll(second, nc - 1, 0) + pe_term
    ge = 0.5 * pre * (1.0 + jnp.tanh(0.7978845608028654 * (pre + 0.044715 * pre * pre * pre)))
    y = _dot(ge.astype(BF16), w2_ref[0, 0])

    @pl.when(hd == 0)
    def _():
        acc_ref[...] = y

    @pl.when(hd > 0)
    def _():
        acc_ref[...] += y

    @pl.when(hd == NSA_KV_HEADS - 1)
    def _():
        acc = acc_ref[...]

        @pl.when(kv == 0)
        def _():
            rc, rs1, rs2 = rc_ref[...], rs1_ref[...], rs2_ref[...]
            o_ref[0] = jnp.concatenate(
                [_rope(acc[:, :LANES], rc, rs1, rs2, 1), _rope(acc[:, LANES:], rc, rs1, rs2, 1)],
                axis=1).astype(o_ref.dtype)

        @pl.when(kv == 1)
        def _():
            o_ref[0] = acc.astype(o_ref.dtype)


def _nsa_cmp(r, w1, pe, w2p, rc, rs1, rs2):
    nc = r.shape[2]
    kvd = NSA_KV_HEADS * HEAD_DIM
    return pl.pallas_call(
        functools.partial(_nsa_cmp_kernel, nc=nc),
        grid=(2, NSA_KV_HEADS),
        in_specs=[
            pl.BlockSpec((1, 1, nc, CMP_STRIDE * HEAD_DIM), lambda a, b: (a, b, 0, 0)),
            pl.BlockSpec((1, CMP_BLOCK * HEAD_DIM, CMP_HIDDEN), lambda a, b: (a, 0, 0)),
            pl.BlockSpec((1, 8, CMP_BLOCK * HEAD_DIM), lambda a, b: (a, 0, 0)),
            pl.BlockSpec((1, 1, CMP_HIDDEN, kvd), lambda a, b: (a, b, 0, 0)),
            pl.BlockSpec((nc, LANES), lambda a, b: (0, 0)),
            pl.BlockSpec((nc, LANES), lambda a, b: (0, 0)),
            pl.BlockSpec((nc, LANES), lambda a, b: (0, 0)),
        ],
        out_specs=pl.BlockSpec((1, nc, kvd), lambda a, b: (a, 0, 0)),
        out_shape=jax.ShapeDtypeStruct((2, nc, kvd), BF16),
        scratch_shapes=[pltpu.VMEM((nc, kvd), F32)],
        compiler_params=_cparams(("parallel", "arbitrary")),
        name="nsa_cmp",
    )(r, w1, pe, w2p, rc, rs1, rs2)


def _nsa_cmp_attn_body(i, qt_ref, kc_ref, vct_ref, w_ref, oct_ref, selt_ref, *, t, n_tiles, ct, nsp):
    qpos = i * t + lax.broadcasted_iota(jnp.int32, (1, t), 1)
    any_valid = qpos >= CMP_BLOCK - 1
    low = lax.broadcasted_iota(jnp.int32, (LANES, t), 0) < HEAD_DIM
    per_tile = ct * CMP_STRIDE // SLC_BLOCK
    stat_rows = jnp.concatenate([jnp.ones((SUM_ROWS, ct), BF16), w_ref[...]], axis=0)
    cmasks = [CMP_STRIDE * (c * ct + lax.broadcasted_iota(jnp.int32, (ct, t), 0)) + (CMP_BLOCK - 1) <= qpos
              for c in range(n_tiles)]
    blk = lax.broadcasted_iota(jnp.int32, (nsp, t), 0)
    blk_f = blk.astype(F32)
    cur = jnp.right_shift(qpos, SLC_SHIFT)
    forced = jnp.logical_or(blk == 0, jnp.logical_or(blk == cur, blk == cur - 1))
    valid = blk * SLC_BLOCK <= qpos
    vals = []
    for pair in range(NSA_KV_HEADS // 2):
        lanes = slice(pair * LANES, (pair + 1) * LANES)
        lhs = [jnp.concatenate([vct_ref[lanes, c * ct:(c + 1) * ct], stat_rows], axis=0) for c in range(n_tiles)]
        imp = [[None] * n_tiles, [None] * n_tiles]
        for g in range(NSA_GROUP):
            rows = slice((pair * NSA_GROUP + g) * LANES, (pair * NSA_GROUP + g + 1) * LANES)
            qh = _head_rows(qt_ref[rows, :])
            oc = []
            for h in range(2):
                m = jnp.full((1, t), NEG, F32)
                acc = jnp.zeros((LANES + SUM_ROWS, t), F32)
                parts, maxes = [], []
                for c in range(n_tiles):
                    s = jnp.where(cmasks[c], _dot(kc_ref[0, c * ct:(c + 1) * ct, lanes], qh[h]), NEG)
                    m_new = jnp.maximum(m, jnp.max(s, axis=0, keepdims=True))
                    p = jnp.exp2(s - m_new)
                    p_hi = p.astype(BF16)
                    p_lo = (p - p_hi.astype(F32)).astype(BF16)
                    r_hi = _dot(lhs[c], p_hi)
                    r_lo = _dot(stat_rows, p_lo)
                    stats = r_hi[LANES:, :] + r_lo
                    acc = jnp.exp2(m - m_new) * acc + jnp.concatenate([r_hi[:LANES, :], stats[:SUM_ROWS, :]], axis=0)
                    parts.append(stats[SUM_ROWS:, :])
                    maxes.append(m_new)
                    m = m_new
                inv = jnp.where(any_valid, 1.0 / acc[LANES:LANES + 1, :], 0.0)
                oc.append(acc[:LANES, :] * inv)
                for c in range(n_tiles):
                    piece = parts[c] * (jnp.exp2(maxes[c] - m) * inv)
                    imp[h][c] = piece if imp[h][c] is None else imp[h][c] + piece
            oct_ref[rows, :] = jnp.where(low, oc[0], oc[1]).astype(oct_ref.dtype)
        for h in range(2):
            segs = []
            for c in range(n_tiles):
                seg = imp[h][c][:per_tile, :]
                if c > 0:
                    spill = imp[h][c - 1][per_tile:per_tile + IMP_PAD, :]
                    seg = jnp.concatenate([seg[:IMP_PAD, :] + spill, seg[IMP_PAD:, :]], axis=0)
                segs.append(seg)
            score = jnp.concatenate(segs, axis=0)[:nsp, :]
            vals.append(jnp.where(valid, jnp.where(forced, FORCE_SCORE, score), NEG))

    def pick_one(_, vals):
        out = []
        for v in vals:
            mx = jnp.max(v, axis=0, keepdims=True)
            first = jnp.min(jnp.where(v == mx, blk_f, float(nsp)), axis=0, keepdims=True)
            out.append(jnp.where(blk_f == first, -jnp.inf, v))
        return tuple(out)

    vals = lax.fori_loop(0, SLC_TOPK, pick_one, tuple(vals))
    for kvh in range(NSA_KV_HEADS):
        picked = jnp.logical_and(valid, vals[kvh] == -jnp.inf)
        selt_ref[kvh, :nsp, :] = jnp.where(picked, 1.0, 0.0).astype(selt_ref.dtype)
        if nsp < selt_ref.shape[1]:
            selt_ref[kvh, nsp:, :] = jnp.zeros((selt_ref.shape[1] - nsp, t), selt_ref.dtype)


def _nsa_cmp_attn_kernel(qt_ref, kc_ref, vct_ref, w_ref, oct_ref, selt_ref, *, t, n_steps, n_spans):
    i = pl.program_id(0)
    ct = w_ref.shape[1]
    for q in range(n_spans):
        @pl.when(jnp.logical_and(i >= q * n_steps // n_spans, i < (q + 1) * n_steps // n_spans))
        def _(q=q):
            _nsa_cmp_attn_body(i, qt_ref, kc_ref, vct_ref, w_ref, oct_ref, selt_ref, t=t, n_tiles=q + 1, ct=ct,
                               nsp=(q + 1) * ct * CMP_STRIDE // SLC_BLOCK)


def _nsa_cmp_attn(proj_t, kvc, vct, w_imp, nsp):
    S = proj_t.shape[1]
    t = NSA_TILE
    nc = kvc.shape[1]
    imp_rows, ct = w_imp.shape
    kvd = NSA_KV_HEADS * HEAD_DIM
    return pl.pallas_call(
        functools.partial(_nsa_cmp_attn_kernel, t=t, n_steps=S // t, n_spans=nc // ct),
        grid=(S // t,),
        in_specs=[
            pl.BlockSpec((D_MODEL, t), lambda i: (0, i)),
            pl.BlockSpec((1, nc, kvd), lambda i: (0, 0, 0)),
            pl.BlockSpec((kvd, nc), lambda i: (0, 0)),
            pl.BlockSpec((imp_rows, ct), lambda i: (0, 0)),
        ],
        out_specs=[pl.BlockSpec((D_MODEL, t), lambda i: (0, i)),
                   pl.BlockSpec((NSA_KV_HEADS, nsp, t), lambda i: (0, 0, i))],
        out_shape=[jax.ShapeDtypeStruct((D_MODEL, S), BF16),
                   jax.ShapeDtypeStruct((NSA_KV_HEADS, nsp, S), BF16)],
        compiler_params=_cparams(("parallel",)),
        name="nsa_cmp_attn",
    )(proj_t, kvc, vct, w_imp)


def _nsa_sel_kernel(qt_ref, ks_ref, vst_ref, selt_ref, kw0_ref, kw1_ref, kw2_ref, vwt0_ref, vwt1_ref, vwt2_ref,
                    gzt_ref, et_ref, oct_ref, ot_ref, s0_ref, s1_ref, bias_ref, m_ref, acc_ref, *, t):
    i = pl.program_id(1)
    n_heads = 2 * NSA_GROUP
    low = lax.broadcasted_iota(jnp.int32, (LANES, t), 0) < HEAD_DIM
    qs = []
    for g in range(NSA_GROUP):
        qs.extend(_head_rows(qt_ref[g * LANES:(g + 1) * LANES, :]))
    bias_ref[...] = ((selt_ref[...].astype(F32) - 1.0) * -NEG).astype(BF16)
    m_ref[...] = jnp.full(m_ref.shape, NEG, F32)
    acc_ref[...] = jnp.zeros(acc_ref.shape, F32)
    tok_blk = jnp.right_shift(lax.broadcasted_iota(jnp.int32, (t, LANES), 0), SLC_SHIFT)
    lane_col = lax.broadcasted_iota(jnp.int32, (t, LANES), 1)
    zero_rows = jnp.zeros((LANES - SEL_GROUP, t), BF16)
    tiles_per_group = SEL_GROUP * SLC_BLOCK // t
    s_bufs = (s0_ref, s1_ref)

    def scores(j, buf):
        k = ks_ref[pl.ds(pl.multiple_of(j * t, t), t), :]
        grp = j // tiles_per_group
        first_blk = (j % tiles_per_group) * (t // SLC_BLOCK)
        expand = jnp.where(lane_col == first_blk + tok_blk, 1.0, 0.0).astype(BF16)
        lhs = jnp.concatenate([k, expand], axis=1)
        for h in range(2):
            blk_bias = bias_ref[h, grp]
            for g in range(NSA_GROUP):
                rhs = jnp.concatenate([qs[2 * g + h], blk_bias, zero_rows], axis=0)
                s_bufs[buf][2 * g + h] = _dot(lhs, rhs)

    def consume(j, buf, diag):
        vt = _with_ones_rows(vst_ref[:, pl.ds(pl.multiple_of(j * t, t), t)])
        for idx in range(n_heads):
            _online_update(s_bufs[buf], vt, m_ref, acc_ref, idx, diag)

    _pipelined_tiles(i, scores, consume, pairs_per_step=2)

    n_win = WIN // t + 1
    k_win = jnp.concatenate([kw0_ref[...], kw1_ref[...], kw2_ref[...]], axis=0)
    vt_win = jnp.concatenate([vwt0_ref[...], vwt1_ref[...], vwt2_ref[...]], axis=1)
    kpos = (i - (n_win - 1)) * t + lax.broadcasted_iota(jnp.int32, (n_win * t, t), 0)
    qpos = i * t + lax.broadcasted_iota(jnp.int32, (n_win * t, t), 1)
    wmask = jnp.logical_and(jnp.logical_and(kpos <= qpos, kpos > qpos - WIN), kpos >= 0)
    ow = []
    for idx in range(n_heads):
        s = jnp.where(wmask, _dot(k_win, qs[idx]), NEG)
        p = jnp.exp2(s - jnp.max(s, axis=0, keepdims=True))
        l = jnp.sum(p, axis=0, keepdims=True)
        ow.append(_dot(vt_win, p.astype(BF16)) / l)

    gates = jax.nn.sigmoid(_dot(et_ref[0], gzt_ref[...]))
    gw = NSA_GROUP * LANES
    for g in range(NSA_GROUP):
        rows = slice(g * LANES, (g + 1) * LANES)
        o_cmp = oct_ref[rows, :].astype(F32)
        o_slc = jnp.where(low, _online_result(acc_ref, 2 * g), _online_result(acc_ref, 2 * g + 1))
        o_win = jnp.where(low, ow[2 * g], ow[2 * g + 1])
        out = (gates[g * LANES:(g + 1) * LANES, :] * o_cmp
               + gates[gw + g * LANES:gw + (g + 1) * LANES, :] * o_slc
               + gates[2 * gw + g * LANES:2 * gw + (g + 1) * LANES, :] * o_win)
        ot_ref[rows, :] = out.astype(ot_ref.dtype)


def _nsa_sel(nat, proj_t, selt, oct, gate_expand_t):
    S = nat.shape[0]
    t = NSA_TILE
    n_grp = selt.shape[1]
    n_heads = 2 * NSA_GROUP
    gl = NSA_GROUP * LANES

    def kwin_spec(back):
        return pl.BlockSpec((t, LANES), lambda p, i: (jnp.maximum(i - back, 0), NAT_K_WIN + p))

    def vwin_spec(back):
        return pl.BlockSpec((LANES, t), lambda p, i: (T_V_WIN + p, jnp.maximum(i - back, 0)))

    return pl.pallas_call(
        functools.partial(_nsa_sel_kernel, t=t),
        grid=(NSA_KV_HEADS // 2, S // t),
        in_specs=[
            pl.BlockSpec((gl, t), lambda p, i: (p, i)),
            pl.BlockSpec((S, LANES), lambda p, i: (0, NAT_K_SLC + p)),
            pl.BlockSpec((LANES, S), lambda p, i: (T_V_SLC + p, 0)),
            pl.BlockSpec((2, n_grp, SEL_GROUP, t), lambda p, i: (p, 0, 0, i)),
            kwin_spec(2), kwin_spec(1), kwin_spec(0),
            vwin_spec(2), vwin_spec(1), vwin_spec(0),
            pl.BlockSpec((NSA_PROJ_TILE, t), lambda p, i: (T_GATE * LANES // NSA_PROJ_TILE, i)),
            pl.BlockSpec((1, 3 * gl, NSA_PROJ_TILE), lambda p, i: (p, 0, 0)),
            pl.BlockSpec((gl, t), lambda p, i: (p, i)),
        ],
        out_specs=pl.BlockSpec((gl, t), lambda p, i: (p, i)),
        out_shape=jax.ShapeDtypeStruct((D_MODEL, S), BF16),
        scratch_shapes=([pltpu.VMEM((n_heads, t, t), F32)] * 2 + [pltpu.VMEM((2, n_grp, SEL_GROUP, t), BF16)]
                        + [pltpu.VMEM((n_heads, 1, t), F32)] + [pltpu.VMEM((n_heads, LANES + SUM_ROWS, t), F32)]),
        compiler_params=_cparams(("parallel", "arbitrary")),
        name="nsa_sel",
    )(proj_t, nat, proj_t, selt, nat, nat, nat, proj_t, proj_t, proj_t, proj_t, gate_expand_t, oct)


def _rope_tables(positions, key_major=False):
    half = ROT_DIM // 2
    inv = ROPE_THETA ** (-jnp.arange(0, ROT_DIM, 2, dtype=F32) / ROT_DIM)
    pos = positions.astype(F32)
    n = positions.shape[0]
    axis = 0 if key_major else 1
    ang = inv[:, None] * pos[None, :] if key_major else pos[:, None] * inv[None, :]
    cos, sin = jnp.cos(ang), jnp.sin(ang)

    def const(width, value):
        return jnp.full((width, n) if key_major else (n, width), value, F32)

    rest = HEAD_DIM - ROT_DIM
    rc = jnp.concatenate([cos, cos, const(rest, 1.0)], axis=axis)
    rs1 = jnp.concatenate([-sin, const(half, 0.0), const(rest, 0.0)], axis=axis)
    rs2 = jnp.concatenate([const(half, 0.0), sin, const(rest, 0.0)], axis=axis)
    reps = (LANES // HEAD_DIM, 1) if key_major else (1, LANES // HEAD_DIM)
    return jnp.tile(rc, reps), jnp.tile(rs1, reps), jnp.tile(rs2, reps)


def _pair_heads(w, axis):
    shape = w.shape
    w = w.reshape(shape[:axis] + (NSA_KV_HEADS // 2, 2, NSA_GROUP, HEAD_DIM) + shape[axis + 1:])
    w = jnp.swapaxes(w, axis + 1, axis + 2)
    return w.reshape(shape)


def _nsa_constants(S):
    nc = S // CMP_STRIDE
    ns = S // SLC_BLOCK
    nsp = -(-ns // LANES) * LANES
    ct = min(CMP_TILE, nc)
    imp_rows = -(-(ct * CMP_STRIDE // SLC_BLOCK + IMP_PAD) // SUM_ROWS) * SUM_ROWS
    ci = np.arange(ct)[None, :] * CMP_STRIDE
    st = np.arange(imp_rows)[:, None] * SLC_BLOCK
    w_imp = (ci < st + SLC_BLOCK) & (ci + CMP_BLOCK > st)
    e = np.zeros((NSA_KV_HEADS // 2, 3 * NSA_GROUP * LANES, NSA_PROJ_TILE), np.float32)
    for p in range(NSA_KV_HEADS // 2):
        for br in range(3):
            for g in range(NSA_GROUP):
                for hf in range(2):
                    src = br * N_HEADS + (2 * p + hf) * NSA_GROUP + g
                    r0 = br * NSA_GROUP * LANES + g * LANES + hf * HEAD_DIM
                    e[p, r0:r0 + HEAD_DIM, src] = 1.0
    cmp_end = np.minimum(np.arange(nc) * CMP_STRIDE + CMP_BLOCK - 1, S - 1)
    return nc, nsp, jnp.asarray(w_imp.astype(np.float32), BF16), jnp.asarray(e, BF16), cmp_end


def _nsa_layer(h, g, positions, w_in, pe_k, w1_k, w2_k, pe_v, w1_v, w2_v, w_o):
    S, D = h.shape
    qd = N_HEADS * HEAD_DIM
    kvd = NSA_KV_HEADS * HEAD_DIM
    nc, nsp, w_imp, gate_expand_t, cmp_end = _nsa_constants(S)
    rc, rs1, rs2 = _rope_tables(positions)

    def kv_piece(n):
        return w_in[:, qd + n * kvd:qd + (n + 1) * kvd]

    w_nat = jnp.concatenate([kv_piece(0), kv_piece(1), kv_piece(2), kv_piece(4)], axis=1).astype(BF16)
    wg = jnp.pad(w_in[:, qd + 6 * kvd:], ((0, 0), (0, NSA_T_ROWS - T_GATE * LANES - 3 * N_HEADS)))
    w_t = jnp.concatenate([_pair_heads(w_in[:, :qd], 1) * QK_SCALE, kv_piece(3), kv_piece(5), wg], axis=1).T.astype(BF16)
    nat = _proj(h, g, w_nat, "nsa_proj", key_major=False, tn=2 * NSA_PROJ_TILE,
                rope=(rc, rs1, rs2), rope_tiles=NSA_NAT_ROPE_TILES)
    proj_t = _proj(h, g, w_t, "nsa_proj_t", key_major=True, tn=2 * NSA_PROJ_TILE,
                   rope=_rope_tables(positions, key_major=True), rope_tiles=NSA_T_ROPE_TILES)

    raw = nat[:, :2 * kvd].reshape(nc, CMP_STRIDE, 2, NSA_KV_HEADS, HEAD_DIM)
    raw = raw.transpose(2, 3, 0, 1, 4).reshape(2, NSA_KV_HEADS, nc, CMP_STRIDE * HEAD_DIM)
    w1 = jnp.stack([w1_k, w1_v]).astype(BF16)
    pe = jnp.stack([pe_k.reshape(1, -1), pe_v.reshape(1, -1)])
    pe = jnp.pad(pe, ((0, 0), (0, 7), (0, 0))).astype(BF16)
    w2 = jnp.stack([w2_k, w2_v])
    eye = jnp.eye(NSA_KV_HEADS, dtype=F32)
    w2p = (w2[:, None, :, None, :] * eye[None, :, None, :, None]).reshape(2, NSA_KV_HEADS, CMP_HIDDEN, kvd).astype(BF16)
    kvc = _nsa_cmp(raw, w1, pe, w2p, *_rope_tables(positions[cmp_end]))

    oct, selt = _nsa_cmp_attn(proj_t, kvc, kvc[1].T, w_imp, nsp)
    selt = selt.reshape(NSA_KV_HEADS, -1, SEL_GROUP, S)
    ot = _nsa_sel(nat, proj_t, selt, oct, gate_expand_t)
    return _matmul_res(ot, _pair_heads(w_o, 0).astype(BF16), h, "nsa_out")


def _trunk(x2, positions, p):
    h = _fox_layer(x2, p["l0_norm_mix"], p["l0_fox_w_qkv"], p["l0_fox_w_f"], p["l0_fox_b_f"], p["l0_fox_w_o"])
    h = _mlp(h, p["l0_norm_mlp"], p["l0_mlp_w1"].astype(BF16), p["l0_mlp_w2"].astype(BF16))
    h = _pool_layer(h, p["l1_norm_mix"], p["l1_pool_w"], p["l1_pool_scale"])
    h = _mlp(h, p["l1_norm_mlp"], p["l1_mlp_w1"].astype(BF16), p["l1_mlp_w2"].astype(BF16))
    h = _conv_layer(h, p["l2_norm_mix"], p["l2_conv_w_in"], p["l2_conv_w"], p["l2_conv_w_out"])
    h = _mlp(h, p["l2_norm_mlp"], p["l2_mlp_w1"].astype(BF16), p["l2_mlp_w2"].astype(BF16))
    h = _nsa_layer(h, p["l3_norm_mix"], positions, p["l3_nsa_w_in"], p["l3_nsa_cmp_pe_k"], p["l3_nsa_cmp_w1_k"],
                   p["l3_nsa_cmp_w2_k"], p["l3_nsa_cmp_pe_v"], p["l3_nsa_cmp_w1_v"], p["l3_nsa_cmp_w2_v"],
                   p["l3_nsa_w_o"])
    return _mlp(h, p["l3_norm_mlp"], p["l3_mlp_w1"].astype(BF16), p["l3_mlp_w2"].astype(BF16), p["final_norm"])


def kernel(x, positions, l0_norm_mix, l0_fox_w_qkv, l0_fox_w_f, l0_fox_b_f, l0_fox_w_o, l0_norm_mlp, l0_mlp_w1, l0_mlp_w2, l1_norm_mix, l1_pool_w, l1_pool_scale, l1_norm_mlp, l1_mlp_w1, l1_mlp_w2, l2_norm_mix, l2_conv_w_in, l2_conv_w, l2_conv_w_out, l2_norm_mlp, l2_mlp_w1, l2_mlp_w2, l3_norm_mix, l3_nsa_w_in, l3_nsa_cmp_pe_k, l3_nsa_cmp_w1_k, l3_nsa_cmp_w2_k, l3_nsa_cmp_pe_v, l3_nsa_cmp_w1_v, l3_nsa_cmp_w2_v, l3_nsa_w_o, l3_norm_mlp, l3_mlp_w1, l3_mlp_w2, final_norm):
    params = dict(locals())
    B, S, D = x.shape
    outs = [_trunk(x[b], positions, params) for b in range(B)]
    return jnp.stack(outs, axis=0)
```

```python
import functools

import numpy as np
import jax
import jax.numpy as jnp
from jax import lax
from jax.experimental import pallas as pl
from jax.experimental.pallas import tpu as pltpu

F32 = jnp.float32
BF16 = jnp.bfloat16

D_MODEL = 1024
HEAD_DIM = 64
N_HEADS = D_MODEL // HEAD_DIM
D_FF = 4 * D_MODEL
ROPE_THETA = 500000.0
ROT_DIM = HEAD_DIM // 4
RMS_EPS = 1e-6
POOL_WINDOWS = (2, 4, 8, 16)
POOL_GROUP = D_MODEL // len(POOL_WINDOWS)
POOL_HALO = 16
CONV_WIDTH = 3
CONV_HALO = 8
NSA_KV_HEADS = 4
NSA_GROUP = N_HEADS // NSA_KV_HEADS
CMP_BLOCK = 32
CMP_STRIDE = 16
CMP_HIDDEN = 256
SLC_BLOCK = 64
SLC_SHIFT = 6
SLC_TOPK = 16
WIN = 512
FORCE_SCORE = 1e9
NEG = -1e30
LOG2E = 1.4426950408889634
QK_SCALE = HEAD_DIM ** -0.5 * LOG2E

LANES = 128
V7X_VMEM_LIMIT = 56 * 1024 * 1024
FOX_TILE = 512
FOX_BIAS_TERMS = 3
NSA_TILE = WIN // 2
SEL_GROUP = 16
SUM_ROWS = 16
CMP_TILE = 256
IMP_PAD = 8
NSA_PROJ_TILE = 2 * LANES
NSA_NAT_WIDTH = 8 * LANES
NAT_K_SLC, NAT_K_WIN = 4, 6
NSA_NAT_ROPE_TILES = (1,)
NSA_T_ROWS = 16 * LANES
T_V_SLC, T_V_WIN, T_GATE = 8, 10, 12
NSA_T_ROPE_TILES = (0, 1)


def _cparams(semantics):
    return pltpu.CompilerParams(dimension_semantics=semantics, vmem_limit_bytes=V7X_VMEM_LIMIT)


def _rms(x, g):
    return x * lax.rsqrt(jnp.mean(x * x, axis=-1, keepdims=True) + RMS_EPS) * g


def _dot(a, b):
    return jnp.dot(a, b, preferred_element_type=F32)


def _dot_nt(a, b):
    return lax.dot_general(a, b, (((1,), (1,)), ((), ())), preferred_element_type=F32)


def _dot_tn(a, b):
    return lax.dot_general(a, b, (((0,), (0,)), ((), ())), preferred_element_type=F32)


def _split3(x):
    hi = x.astype(BF16)
    r = x - hi.astype(F32)
    mid = r.astype(BF16)
    lo = (r - mid.astype(F32)).astype(BF16)
    return hi, mid, lo


def _rope(a, rc, rs1, rs2, axis):
    half = ROT_DIM // 2
    return a * rc + pltpu.roll(a, LANES - half, axis) * rs1 + pltpu.roll(a, half, axis) * rs2


def _head_rows(qt_blk):
    low = lax.broadcasted_iota(jnp.int32, qt_blk.shape, 0) < HEAD_DIM
    qf = qt_blk.astype(F32)
    return jnp.where(low, qf, 0.0).astype(BF16), jnp.where(low, 0.0, qf).astype(BF16)


def _with_ones_rows(vt):
    return jnp.concatenate([vt, jnp.ones((SUM_ROWS, vt.shape[1]), vt.dtype)], axis=0)


def _online_update(s_ref, vt_ones, m_ref, acc_ref, idx, causal):
    tk, t = s_ref.shape[1:]
    s = s_ref[idx]
    if causal:
        s = jnp.where(lax.broadcasted_iota(jnp.int32, (tk, t), 0) <= lax.broadcasted_iota(jnp.int32, (tk, t), 1),
                      s, NEG)
    m_old = m_ref[idx]
    m_new = jnp.maximum(m_old, jnp.max(s, axis=0, keepdims=True))
    p = jnp.exp2(s - m_new).astype(BF16)
    m_ref[idx] = m_new
    acc_ref[idx] = jnp.exp2(m_old - m_new) * acc_ref[idx] + _dot(vt_ones, p)


def _online_result(acc_ref, idx):
    acc = acc_ref[idx]
    return acc[:LANES, :] / acc[LANES:LANES + 1, :]


def _pipelined_tiles(i, scores, consume, pairs_per_step=1):
    scores(0, 0)

    def pair(j):
        scores(j + 1, 1)
        consume(j, 0, False)
        scores(j + 2, 0)
        consume(j + 1, 1, False)

    n_pairs = i // 2
    done = 0
    width = pairs_per_step
    while width >= 1:
        trips = (n_pairs - done) // width

        def body(jj, carry, width=width, done=done):
            for u in range(width):
                pair(2 * (done + width * jj + u))
            return carry

        lax.fori_loop(0, trips, body, 0)
        done = done + trips * width
        width //= 2

    @pl.when(i % 2 == 0)
    def _():
        consume(i, 0, True)

    @pl.when(i % 2 == 1)
    def _():
        scores(i, 1)
        consume(i - 1, 0, False)
        consume(i, 1, True)


def _mlp_kernel(x_ref, g_ref, w1_ref, w2_ref, *rest, nf, final):
    if final:
        fg_ref, o_ref, xn_ref = rest
    else:
        o_ref, xn_ref = rest
    f = pl.program_id(1)

    @pl.when(f == 0)
    def _():
        x = x_ref[...]
        xn_ref[...] = _rms(x, g_ref[...]).astype(BF16)
        o_ref[...] = x

    a = _dot(xn_ref[...], w1_ref[...])
    a = jnp.square(jnp.maximum(a, 0.0)).astype(BF16)
    o_ref[...] += _dot(a, w2_ref[...])

    if final:
        @pl.when(f == nf - 1)
        def _():
            o_ref[...] = _rms(o_ref[...], fg_ref[...])


def _mlp(h, g, w1, w2, final_g=None):
    S, D = h.shape
    F = w1.shape[1]
    tm = min(1024, S)
    tf = 2048
    nf = F // tf
    final = final_g is not None
    in_specs = [
        pl.BlockSpec((tm, D), lambda i, f: (i, 0)),
        pl.BlockSpec((1, D), lambda i, f: (0, 0)),
        pl.BlockSpec((D, tf), lambda i, f: (0, f)),
        pl.BlockSpec((tf, D), lambda i, f: (f, 0)),
    ]
    args = [h, g.reshape(1, D), w1, w2]
    if final:
        in_specs.append(pl.BlockSpec((1, D), lambda i, f: (0, 0)))
        args.append(final_g.reshape(1, D))
    return pl.pallas_call(
        functools.partial(_mlp_kernel, nf=nf, final=final),
        grid=(S // tm, nf),
        in_specs=in_specs,
        out_specs=pl.BlockSpec((tm, D), lambda i, f: (i, 0)),
        out_shape=jax.ShapeDtypeStruct((S, D), F32),
        scratch_shapes=[pltpu.VMEM((tm, D), BF16)],
        compiler_params=_cparams(("parallel", "arbitrary")),
        name="mlp",
    )(*args)


def _proj_kernel(x_ref, g_ref, w_ref, *rest, key_major, rope_tiles):
    if rope_tiles:
        rc_ref, rs1_ref, rs2_ref, o_ref, xn_ref = rest
    else:
        o_ref, xn_ref = rest
    j = pl.program_id(1)

    @pl.when(j == 0)
    def _():
        xn_ref[...] = _rms(x_ref[...], g_ref[...]).astype(BF16)

    a = _dot_nt(w_ref[...], xn_ref[...]) if key_major else _dot(xn_ref[...], w_ref[...])
    if not rope_tiles:
        o_ref[...] = a.astype(o_ref.dtype)
        return
    axis = 0 if key_major else 1
    is_rope = functools.reduce(jnp.logical_or, [j == t for t in rope_tiles])

    @pl.when(is_rope)
    def _():
        rc, rs1, rs2 = rc_ref[...], rs1_ref[...], rs2_ref[...]
        blocks = [lax.slice_in_dim(a, b * LANES, (b + 1) * LANES, axis=axis) for b in range(a.shape[axis] // LANES)]
        o_ref[...] = jnp.concatenate([_rope(blk, rc, rs1, rs2, axis) for blk in blocks], axis=axis).astype(o_ref.dtype)

    @pl.when(jnp.logical_not(is_rope))
    def _():
        o_ref[...] = a.astype(o_ref.dtype)


def _proj(h, g, w, name, *, key_major, tn, rope=None, rope_tiles=()):
    S, D = h.shape
    tm = min(1024, S)
    if key_major:
        N = w.shape[0]
        w_spec = pl.BlockSpec((tn, D), lambda i, j: (j, 0))
        rope_spec = pl.BlockSpec((LANES, tm), lambda i, j: (0, i))
        out_spec = pl.BlockSpec((tn, tm), lambda i, j: (j, i))
        out_shape = jax.ShapeDtypeStruct((N, S), BF16)
    else:
        N = w.shape[1]
        w_spec = pl.BlockSpec((D, tn), lambda i, j: (0, j))
        rope_spec = pl.BlockSpec((tm, LANES), lambda i, j: (i, 0))
        out_spec = pl.BlockSpec((tm, tn), lambda i, j: (i, j))
        out_shape = jax.ShapeDtypeStruct((S, N), BF16)
    in_specs = [pl.BlockSpec((tm, D), lambda i, j: (i, 0)), pl.BlockSpec((1, D), lambda i, j: (0, 0)), w_spec]
    args = [h, g.reshape(1, D), w]
    if rope_tiles:
        in_specs += [rope_spec] * 3
        args += list(rope)
    return pl.pallas_call(
        functools.partial(_proj_kernel, key_major=key_major, rope_tiles=tuple(rope_tiles)),
        grid=(S // tm, N // tn),
        in_specs=in_specs,
        out_specs=out_spec,
        out_shape=out_shape,
        scratch_shapes=[pltpu.VMEM((tm, D), BF16)],
        compiler_params=_cparams(("parallel", "arbitrary")),
        name=name,
    )(*args)


def _matmul_res_kernel(at_ref, w_ref, r_ref, o_ref):
    o_ref[...] = r_ref[...] + _dot_tn(at_ref[...], w_ref[...])


def _matmul_res(at, w, res, name):
    K, S = at.shape
    N = w.shape[1]
    tm = min(1024, S)
    return pl.pallas_call(
        _matmul_res_kernel,
        grid=(S // tm,),
        in_specs=[
            pl.BlockSpec((K, tm), lambda i: (0, i)),
            pl.BlockSpec((K, N), lambda i: (0, 0)),
            pl.BlockSpec((tm, N), lambda i: (i, 0)),
        ],
        out_specs=pl.BlockSpec((tm, N), lambda i: (i, 0)),
        out_shape=jax.ShapeDtypeStruct((S, N), F32),
        compiler_params=_cparams(("parallel",)),
        name=name,
    )(at, w, res)


def _fox_gate_kernel(x_ref, g_ref, wf_ref, bf_ref, place_ref, c_ref, carry_ref, *, tm):
    @pl.when(pl.program_id(0) == 0)
    def _():
        carry_ref[...] = jnp.zeros_like(carry_ref)

    xn = _rms(x_ref[...], g_ref[...]).astype(BF16)
    z = _dot(xn, wf_ref[...]) + bf_ref[...]
    logf = jnp.minimum(z, 0.0) - jnp.log(1.0 + jnp.exp(-jnp.abs(z)))
    row = lax.broadcasted_iota(jnp.int32, (tm, tm), 0)
    col = lax.broadcasted_iota(jnp.int32, (tm, tm), 1)
    tri = jnp.where(row >= col, 1.0, 0.0).astype(BF16)
    hi, mid, lo = _split3(logf)
    c = _dot(tri, hi) + _dot(tri, mid) + _dot(tri, lo) + carry_ref[...]
    carry_ref[...] = c[tm - 1:tm, :]
    terms = jnp.concatenate(_split3(c * -LOG2E), axis=1)
    c_ref[...] = _dot(terms, place_ref[...]).astype(c_ref.dtype)


def _fox_gate(h, g, wf, bf, place):
    S, D = h.shape
    tm = min(512, S)
    return pl.pallas_call(
        functools.partial(_fox_gate_kernel, tm=tm),
        grid=(S // tm,),
        in_specs=[
            pl.BlockSpec((tm, D), lambda i: (i, 0)),
            pl.BlockSpec((1, D), lambda i: (0, 0)),
            pl.BlockSpec((D, LANES), lambda i: (0, 0)),
            pl.BlockSpec((1, LANES), lambda i: (0, 0)),
            pl.BlockSpec((FOX_BIAS_TERMS * LANES, LANES), lambda i: (0, 0)),
        ],
        out_specs=pl.BlockSpec((tm, LANES), lambda i: (i, 0)),
        out_shape=jax.ShapeDtypeStruct((S, LANES), BF16),
        scratch_shapes=[pltpu.VMEM((1, LANES), F32)],
        compiler_params=_cparams(("arbitrary",)),
        name="fox_gate",
    )(h, g.reshape(1, D), wf, bf, place)


def _fox_attn_kernel(qt_ref, k_ref, c_ref, vt_ref, ot_ref, s0_ref, s1_ref, m_ref, acc_ref, *, t):
    hp = pl.program_id(0)
    i = pl.program_id(1)
    row = lax.broadcasted_iota(jnp.int32, (LANES, t), 0)
    q_heads = _head_rows(qt_ref[...])
    qa = []
    for h in range(2):
        first = FOX_BIAS_TERMS * (2 * hp + h)
        ones_rows = jnp.logical_and(row >= first, row < first + FOX_BIAS_TERMS)
        qa.append(jnp.concatenate([q_heads[h], jnp.where(ones_rows, 1.0, 0.0).astype(BF16)], axis=0))
    m_ref[...] = jnp.full(m_ref.shape, NEG, F32)
    acc_ref[...] = jnp.zeros(acc_ref.shape, F32)
    s_bufs = (s0_ref, s1_ref)

    def scores(j, buf):
        rows = pl.ds(pl.multiple_of(j * t, t), t)
        lhs = jnp.concatenate([k_ref[rows, :], c_ref[rows, :]], axis=1)
        for h in range(2):
            s_bufs[buf][h] = _dot(lhs, qa[h])

    def consume(j, buf, diag):
        vt = _with_ones_rows(vt_ref[:, pl.ds(pl.multiple_of(j * t, t), t)])
        for h in range(2):
            _online_update(s_bufs[buf], vt, m_ref, acc_ref, h, diag)

    _pipelined_tiles(i, scores, consume, pairs_per_step=4)
    ot = jnp.where(row < HEAD_DIM, _online_result(acc_ref, 0), _online_result(acc_ref, 1))
    ot_ref[...] = ot.astype(ot_ref.dtype)


def _fox_attn(qvt, k, c_terms):
    S = k.shape[0]
    t = min(FOX_TILE, S)
    n_pairs = N_HEADS // 2
    return pl.pallas_call(
        functools.partial(_fox_attn_kernel, t=t),
        grid=(n_pairs, S // t),
        in_specs=[
            pl.BlockSpec((LANES, t), lambda hp, i: (hp, i)),
            pl.BlockSpec((S, LANES), lambda hp, i: (0, hp)),
            pl.BlockSpec((S, LANES), lambda hp, i: (0, 0)),
            pl.BlockSpec((LANES, S), lambda hp, i: (n_pairs + hp, 0)),
        ],
        out_specs=pl.BlockSpec((LANES, t), lambda hp, i: (hp, i)),
        out_shape=jax.ShapeDtypeStruct((D_MODEL, S), BF16),
        scratch_shapes=([pltpu.VMEM((2, t, t), F32)] * 2 + [pltpu.VMEM((2, 1, t), F32)]
                        + [pltpu.VMEM((2, LANES + SUM_ROWS, t), F32)]),
        compiler_params=_cparams(("parallel", "arbitrary")),
        name="fox_attn",
    )(qvt, k, c_terms, qvt)


def _fox_bias_placement():
    place = np.zeros((FOX_BIAS_TERMS * LANES, LANES), np.float32)
    for head in range(N_HEADS):
        for n in range(FOX_BIAS_TERMS):
            place[n * LANES + head, FOX_BIAS_TERMS * head + n] = 1.0
    return jnp.asarray(place, BF16)


def _fox_layer(h, g, w_qkv, w_f, b_f, w_o):
    wf = jnp.pad(w_f, ((0, 0), (0, LANES - N_HEADS))).astype(BF16)
    bf = jnp.pad(b_f, (0, LANES - N_HEADS)).reshape(1, LANES)
    c_terms = _fox_gate(h, g, wf, bf, _fox_bias_placement())
    w_qv = jnp.concatenate([w_qkv[:, :D_MODEL] * QK_SCALE, w_qkv[:, 2 * D_MODEL:]], axis=1).T.astype(BF16)
    qvt = _proj(h, g, w_qv, "fox_qv", key_major=True, tn=1024)
    k = _proj(h, g, w_qkv[:, D_MODEL:2 * D_MODEL].astype(BF16), "fox_k", key_major=False, tn=1024)
    ot = _fox_attn(qvt, k, c_terms)
    return _matmul_res(ot, w_o.astype(BF16), h, "fox_out")


def _pool_kernel(x_ref, halo_ref, g_ref, w_ref, sc_ref, o_ref, *, tm):
    i = pl.program_id(0)
    x = x_ref[...]
    g = g_ref[...]
    xn = _rms(x, g)
    hn = jnp.where(i > 0, _rms(halo_ref[...], g), 0.0)
    xe = jnp.concatenate([hn, xn], axis=0)
    tpos = i * tm + lax.broadcasted_iota(jnp.int32, (tm, 1), 0)
    for gi, w in enumerate(POOL_WINDOWS):
        sl = slice(gi * POOL_GROUP, (gi + 1) * POOL_GROUP)
        s = xe[:, sl]
        k = 1
        while k < w:
            s = s + pltpu.roll(s, k, 0)
            k *= 2
        cnt = jnp.minimum(tpos + 1, w).astype(F32)
        d = (s[POOL_HALO:, :] / cnt - xn[:, sl]).astype(BF16)
        o_ref[:, sl] = x[:, sl] + _dot(d, w_ref[gi]) * sc_ref[:, sl]


def _pool_layer(h, g, w_pool, pool_scale):
    S, D = h.shape
    tm = min(1024, S)
    return pl.pallas_call(
        functools.partial(_pool_kernel, tm=tm),
        grid=(S // tm,),
        in_specs=[
            pl.BlockSpec((tm, D), lambda i: (i, 0)),
            pl.BlockSpec((POOL_HALO, D), lambda i: (jnp.maximum(i * (tm // POOL_HALO) - 1, 0), 0)),
            pl.BlockSpec((1, D), lambda i: (0, 0)),
            pl.BlockSpec((len(POOL_WINDOWS), POOL_GROUP, POOL_GROUP), lambda i: (0, 0, 0)),
            pl.BlockSpec((1, D), lambda i: (0, 0)),
        ],
        out_specs=pl.BlockSpec((tm, D), lambda i: (i, 0)),
        out_shape=jax.ShapeDtypeStruct((S, D), F32),
        compiler_params=_cparams(("parallel",)),
        name="pool",
    )(h, h, g.reshape(1, D), w_pool.astype(BF16), pool_scale.reshape(1, D))


def _conv_in_kernel(x_ref, g_ref, wb_ref, wc_ref, wu_ref, b_ref, z_ref, xn_ref):
    @pl.when(pl.program_id(1) == 0)
    def _():
        xn_ref[...] = _rms(x_ref[...], g_ref[...]).astype(BF16)

    xn = xn_ref[...]
    b_ref[...] = _dot(xn, wb_ref[...])
    z_ref[...] = _dot(xn, wc_ref[...]) * _dot(xn, wu_ref[...])


def _conv_out_kernel(b_ref, z_ref, zh_ref, cw_ref, w_ref, r_ref, o_ref, *, tm):
    i = pl.program_id(0)
    z = z_ref[...]
    zh = jnp.where(i > 0, zh_ref[...], 0.0)
    row = lax.broadcasted_iota(jnp.int32, (tm, 1), 0)
    prev1 = zh[CONV_HALO - 1:CONV_HALO, :]
    prev2 = zh[CONV_HALO - 2:CONV_HALO - 1, :]
    z1 = jnp.where(row == 0, prev1, pltpu.roll(z, 1, 0))
    z2 = jnp.where(row == 0, prev2, jnp.where(row == 1, prev1, pltpu.roll(z, 2, 0)))
    cw = cw_ref[...]
    conv = cw[0:1, :] * z2 + cw[1:2, :] * z1 + cw[2:3, :] * z
    y = (b_ref[...] * conv).astype(BF16)
    o_ref[...] = r_ref[...] + _dot(y, w_ref[...])


def _conv_layer(h, g, w_in, conv_w, w_out):
    S, D = h.shape
    tm = min(1024, S)
    tn = 1024
    nj = D // tn
    w_in = w_in.astype(BF16)
    b, z = pl.pallas_call(
        _conv_in_kernel,
        grid=(S // tm, nj),
        in_specs=[
            pl.BlockSpec((tm, D), lambda i, j: (i, 0)),
            pl.BlockSpec((1, D), lambda i, j: (0, 0)),
            pl.BlockSpec((D, tn), lambda i, j: (0, j)),
            pl.BlockSpec((D, tn), lambda i, j: (0, nj + j)),
            pl.BlockSpec((D, tn), lambda i, j: (0, 2 * nj + j)),
        ],
        out_specs=[pl.BlockSpec((tm, tn), lambda i, j: (i, j)),
                   pl.BlockSpec((tm, tn), lambda i, j: (i, j))],
        out_shape=[jax.ShapeDtypeStruct((S, D), F32), jax.ShapeDtypeStruct((S, D), F32)],
        scratch_shapes=[pltpu.VMEM((tm, D), BF16)],
        compiler_params=_cparams(("parallel", "arbitrary")),
        name="conv_in",
    )(h, g.reshape(1, D), w_in, w_in, w_in)
    cw = jnp.pad(conv_w, ((0, 8 - CONV_WIDTH), (0, 0)))
    return pl.pallas_call(
        functools.partial(_conv_out_kernel, tm=tm),
        grid=(S // tm,),
        in_specs=[
            pl.BlockSpec((tm, D), lambda i: (i, 0)),
            pl.BlockSpec((tm, D), lambda i: (i, 0)),
            pl.BlockSpec((CONV_HALO, D), lambda i: (jnp.maximum(i * (tm // CONV_HALO) - 1, 0), 0)),
            pl.BlockSpec((8, D), lambda i: (0, 0)),
            pl.BlockSpec((D, D), lambda i: (0, 0)),
            pl.BlockSpec((tm, D), lambda i: (i, 0)),
        ],
        out_specs=pl.BlockSpec((tm, D), lambda i: (i, 0)),
        out_shape=jax.ShapeDtypeStruct((S, D), F32),
        compiler_params=_cparams(("parallel",)),
        name="conv_out",
    )(b, z, z, cw, w_out.astype(BF16), h)


def _nsa_cmp_kernel(r_ref, w1_ref, pe_ref, w2_ref, rc_ref, rs1_ref, rs2_ref, o_ref, acc_ref, *, nc):
    kv = pl.program_id(0)
    hd = pl.program_id(1)
    half = CMP_STRIDE * HEAD_DIM
    r = r_ref[0, 0]
    w1 = w1_ref[0]
    first = _dot(r, w1[:half, :])
    second = _dot(r, w1[half:, :])
    pe_term = _dot(pe_ref[0], w1)[0:1, :]
    pre = first + pltpu.roll(second, nc - 1, 0) + pe_term
    ge = 0.5 * pre * (1.0 + jnp.tanh(0.7978845608028654 * (pre + 0.044715 * pre * pre * pre)))
    y = _dot(ge.astype(BF16), w2_ref[0, 0])

    @pl.when(hd == 0)
    def _():
        acc_ref[...] = y

    @pl.when(hd > 0)
    def _():
        acc_ref[...] += y

    @pl.when(hd == NSA_KV_HEADS - 1)
    def _():
        acc = acc_ref[...]

        @pl.when(kv == 0)
        def _():
            rc, rs1, rs2 = rc_ref[...], rs1_ref[...], rs2_ref[...]
            o_ref[0] = jnp.concatenate(
                [_rope(acc[:, :LANES], rc, rs1, rs2, 1), _rope(acc[:, LANES:], rc, rs1, rs2, 1)],
                axis=1).astype(o_ref.dtype)

        @pl.when(kv == 1)
        def _():
            o_ref[0] = acc.astype(o_ref.dtype)


def _nsa_cmp(r, w1, pe, w2p, rc, rs1, rs2):
    nc = r.shape[2]
    kvd = NSA_KV_HEADS * HEAD_DIM
    return pl.pallas_call(
        functools.partial(_nsa_cmp_kernel, nc=nc),
        grid=(2, NSA_KV_HEADS),
        in_specs=[
            pl.BlockSpec((1, 1, nc, CMP_STRIDE * HEAD_DIM), lambda a, b: (a, b, 0, 0)),
            pl.BlockSpec((1, CMP_BLOCK * HEAD_DIM, CMP_HIDDEN), lambda a, b: (a, 0, 0)),
            pl.BlockSpec((1, 8, CMP_BLOCK * HEAD_DIM), lambda a, b: (a, 0, 0)),
            pl.BlockSpec((1, 1, CMP_HIDDEN, kvd), lambda a, b: (a, b, 0, 0)),
            pl.BlockSpec((nc, LANES), lambda a, b: (0, 0)),
            pl.BlockSpec((nc, LANES), lambda a, b: (0, 0)),
            pl.BlockSpec((nc, LANES), lambda a, b: (0, 0)),
        ],
        out_specs=pl.BlockSpec((1, nc, kvd), lambda a, b: (a, 0, 0)),
        out_shape=jax.ShapeDtypeStruct((2, nc, kvd), BF16),
        scratch_shapes=[pltpu.VMEM((nc, kvd), F32)],
        compiler_params=_cparams(("parallel", "arbitrary")),
        name="nsa_cmp",
    )(r, w1, pe, w2p, rc, rs1, rs2)


def _nsa_cmp_attn_body(i, qt_ref, kc_ref, vct_ref, w_ref, oct_ref, selt_ref, *, t, n_tiles, ct, nsp):
    qpos = i * t + lax.broadcasted_iota(jnp.int32, (1, t), 1)
    any_valid = qpos >= CMP_BLOCK - 1
    low = lax.broadcasted_iota(jnp.int32, (LANES, t), 0) < HEAD_DIM
    per_tile = ct * CMP_STRIDE // SLC_BLOCK
    stat_rows = jnp.concatenate([jnp.ones((SUM_ROWS, ct), BF16), w_ref[...]], axis=0)
    cmasks = [CMP_STRIDE * (c * ct + lax.broadcasted_iota(jnp.int32, (ct, t), 0)) + (CMP_BLOCK - 1) <= qpos
              for c in range(n_tiles)]
    blk = lax.broadcasted_iota(jnp.int32, (nsp, t), 0)
    blk_f = blk.astype(F32)
    cur = jnp.right_shift(qpos, SLC_SHIFT)
    forced = jnp.logical_or(blk == 0, jnp.logical_or(blk == cur, blk == cur - 1))
    valid = blk * SLC_BLOCK <= qpos
    vals = []
    for pair in range(NSA_KV_HEADS // 2):
        lanes = slice(pair * LANES, (pair + 1) * LANES)
        lhs = [jnp.concatenate([vct_ref[lanes, c * ct:(c + 1) * ct], stat_rows], axis=0) for c in range(n_tiles)]
        imp = [[None] * n_tiles, [None] * n_tiles]
        for g in range(NSA_GROUP):
            rows = slice((pair * NSA_GROUP + g) * LANES, (pair * NSA_GROUP + g + 1) * LANES)
            qh = _head_rows(qt_ref[rows, :])
            oc = []
            for h in range(2):
                m = jnp.full((1, t), NEG, F32)
                acc = jnp.zeros((LANES + SUM_ROWS, t), F32)
                parts, maxes = [], []
                for c in range(n_tiles):
                    s = jnp.where(cmasks[c], _dot(kc_ref[0, c * ct:(c + 1) * ct, lanes], qh[h]), NEG)
                    m_new = jnp.maximum(m, jnp.max(s, axis=0, keepdims=True))
                    p = jnp.exp2(s - m_new)
                    p_hi = p.astype(BF16)
                    p_lo = (p - p_hi.astype(F32)).astype(BF16)
                    r_hi = _dot(lhs[c], p_hi)
                    r_lo = _dot(stat_rows, p_lo)
                    stats = r_hi[LANES:, :] + r_lo
                    acc = jnp.exp2(m - m_new) * acc + jnp.concatenate([r_hi[:LANES, :], stats[:SUM_ROWS, :]], axis=0)
                    parts.append(stats[SUM_ROWS:, :])
                    maxes.append(m_new)
                    m = m_new
                inv = jnp.where(any_valid, 1.0 / acc[LANES:LANES + 1, :], 0.0)
                oc.append(acc[:LANES, :] * inv)
                for c in range(n_tiles):
                    piece = parts[c] * (jnp.exp2(maxes[c] - m) * inv)
                    imp[h][c] = piece if imp[h][c] is None else imp[h][c] + piece
            oct_ref[rows, :] = jnp.where(low, oc[0], oc[1]).astype(oct_ref.dtype)
        for h in range(2):
            segs = []
            for c in range(n_tiles):
                seg = imp[h][c][:per_tile, :]
                if c > 0:
                    spill = imp[h][c - 1][per_tile:per_tile + IMP_PAD, :]
                    seg = jnp.concatenate([seg[:IMP_PAD, :] + spill, seg[IMP_PAD:, :]], axis=0)
                segs.append(seg)
            score = jnp.concatenate(segs, axis=0)[:nsp, :]
            vals.append(jnp.where(valid, jnp.where(forced, FORCE_SCORE, score), NEG))

    def pick_one(_, vals):
        out = []
        for v in vals:
            mx = jnp.max(v, axis=0, keepdims=True)
            first = jnp.min(jnp.where(v == mx, blk_f, float(nsp)), axis=0, keepdims=True)
            out.append(jnp.where(blk_f == first, -jnp.inf, v))
        return tuple(out)

    vals = lax.fori_loop(0, SLC_TOPK, pick_one, tuple(vals))
    for kvh in range(NSA_KV_HEADS):
        picked = jnp.logical_and(valid, vals[kvh] == -jnp.inf)
        selt_ref[kvh, :nsp, :] = jnp.where(picked, 1.0, 0.0).astype(selt_ref.dtype)
        if nsp < selt_ref.shape[1]:
            selt_ref[kvh, nsp:, :] = jnp.zeros((selt_ref.shape[1] - nsp, t), selt_ref.dtype)


def _nsa_cmp_attn_kernel(qt_ref, kc_ref, vct_ref, w_ref, oct_ref, selt_ref, *, t, n_steps, n_spans):
    i = pl.program_id(0)
    ct = w_ref.shape[1]
    for q in range(n_spans):
        @pl.when(jnp.logical_and(i >= q * n_steps // n_spans, i < (q + 1) * n_steps // n_spans))
        def _(q=q):
            _nsa_cmp_attn_body(i, qt_ref, kc_ref, vct_ref, w_ref, oct_ref, selt_ref, t=t, n_tiles=q + 1, ct=ct,
                               nsp=(q + 1) * ct * CMP_STRIDE // SLC_BLOCK)


def _nsa_cmp_attn(proj_t, kvc, vct, w_imp, nsp):
    S = proj_t.shape[1]
    t = NSA_TILE
    nc = kvc.shape[1]
    imp_rows, ct = w_imp.shape
    kvd = NSA_KV_HEADS * HEAD_DIM
    return pl.pallas_call(
        functools.partial(_nsa_cmp_attn_kernel, t=t, n_steps=S // t, n_spans=nc // ct),
        grid=(S // t,),
        in_specs=[
            pl.BlockSpec((D_MODEL, t), lambda i: (0, i)),
            pl.BlockSpec((1, nc, kvd), lambda i: (0, 0, 0)),
            pl.BlockSpec((kvd, nc), lambda i: (0, 0)),
            pl.BlockSpec((imp_rows, ct), lambda i: (0, 0)),
        ],
        out_specs=[pl.BlockSpec((D_MODEL, t), lambda i: (0, i)),
                   pl.BlockSpec((NSA_KV_HEADS, nsp, t), lambda i: (0, 0, i))],
        out_shape=[jax.ShapeDtypeStruct((D_MODEL, S), BF16),
                   jax.ShapeDtypeStruct((NSA_KV_HEADS, nsp, S), BF16)],
        compiler_params=_cparams(("parallel",)),
        name="nsa_cmp_attn",
    )(proj_t, kvc, vct, w_imp)


def _nsa_sel_kernel(qt_ref, ks_ref, vst_ref, selt_ref, kw0_ref, kw1_ref, kw2_ref, vwt0_ref, vwt1_ref, vwt2_ref,
                    gzt_ref, et_ref, oct_ref, ot_ref, s0_ref, s1_ref, bias_ref, m_ref, acc_ref, *, t):
    i = pl.program_id(1)
    n_heads = 2 * NSA_GROUP
    low = lax.broadcasted_iota(jnp.int32, (LANES, t), 0) < HEAD_DIM
    qs = []
    for g in range(NSA_GROUP):
        qs.extend(_head_rows(qt_ref[g * LANES:(g + 1) * LANES, :]))
    bias_ref[...] = ((selt_ref[...].astype(F32) - 1.0) * -NEG).astype(BF16)
    m_ref[...] = jnp.full(m_ref.shape, NEG, F32)
    acc_ref[...] = jnp.zeros(acc_ref.shape, F32)
    tok_blk = jnp.right_shift(lax.broadcasted_iota(jnp.int32, (t, LANES), 0), SLC_SHIFT)
    lane_col = lax.broadcasted_iota(jnp.int32, (t, LANES), 1)
    zero_rows = jnp.zeros((LANES - SEL_GROUP, t), BF16)
    tiles_per_group = SEL_GROUP * SLC_BLOCK // t
    s_bufs = (s0_ref, s1_ref)

    def scores(j, buf):
        k = ks_ref[pl.ds(pl.multiple_of(j * t, t), t), :]
        grp = j // tiles_per_group
        first_blk = (j % tiles_per_group) * (t // SLC_BLOCK)
        expand = jnp.where(lane_col == first_blk + tok_blk, 1.0, 0.0).astype(BF16)
        lhs = jnp.concatenate([k, expand], axis=1)
        for h in range(2):
            blk_bias = bias_ref[h, grp]
            for g in range(NSA_GROUP):
                rhs = jnp.concatenate([qs[2 * g + h], blk_bias, zero_rows], axis=0)
                s_bufs[buf][2 * g + h] = _dot(lhs, rhs)

    def consume(j, buf, diag):
        vt = _with_ones_rows(vst_ref[:, pl.ds(pl.multiple_of(j * t, t), t)])
        for idx in range(n_heads):
            _online_update(s_bufs[buf], vt, m_ref, acc_ref, idx, diag)

    _pipelined_tiles(i, scores, consume, pairs_per_step=4)

    n_win = WIN // t + 1
    k_win = jnp.concatenate([kw0_ref[...], kw1_ref[...], kw2_ref[...]], axis=0)
    vt_win = jnp.concatenate([vwt0_ref[...], vwt1_ref[...], vwt2_ref[...]], axis=1)
    kpos = (i - (n_win - 1)) * t + lax.broadcasted_iota(jnp.int32, (n_win * t, t), 0)
    qpos = i * t + lax.broadcasted_iota(jnp.int32, (n_win * t, t), 1)
    wmask = jnp.logical_and(jnp.logical_and(kpos <= qpos, kpos > qpos - WIN), kpos >= 0)
    ow = []
    for idx in range(n_heads):
        s = jnp.where(wmask, _dot(k_win, qs[idx]), NEG)
        p = jnp.exp2(s - jnp.max(s, axis=0, keepdims=True))
        l = jnp.sum(p, axis=0, keepdims=True)
        ow.append(_dot(vt_win, p.astype(BF16)) / l)

    gates = jax.nn.sigmoid(_dot(et_ref[0], gzt_ref[...]))
    gw = NSA_GROUP * LANES
    for g in range(NSA_GROUP):
        rows = slice(g * LANES, (g + 1) * LANES)
        o_cmp = oct_ref[rows, :].astype(F32)
        o_slc = jnp.where(low, _online_result(acc_ref, 2 * g), _online_result(acc_ref, 2 * g + 1))
        o_win = jnp.where(low, ow[2 * g], ow[2 * g + 1])
        out = (gates[g * LANES:(g + 1) * LANES, :] * o_cmp
               + gates[gw + g * LANES:gw + (g + 1) * LANES, :] * o_slc
               + gates[2 * gw + g * LANES:2 * gw + (g + 1) * LANES, :] * o_win)
        ot_ref[rows, :] = out.astype(ot_ref.dtype)


def _nsa_sel(nat, proj_t, selt, oct, gate_expand_t):
    S = nat.shape[0]
    t = NSA_TILE
    n_grp = selt.shape[1]
    n_heads = 2 * NSA_GROUP
    gl = NSA_GROUP * LANES

    def kwin_spec(back):
        return pl.BlockSpec((t, LANES), lambda p, i: (jnp.maximum(i - back, 0), NAT_K_WIN + p))

    def vwin_spec(back):
        return pl.BlockSpec((LANES, t), lambda p, i: (T_V_WIN + p, jnp.maximum(i - back, 0)))

    return pl.pallas_call(
        functools.partial(_nsa_sel_kernel, t=t),
        grid=(NSA_KV_HEADS // 2, S // t),
        in_specs=[
            pl.BlockSpec((gl, t), lambda p, i: (p, i)),
            pl.BlockSpec((S, LANES), lambda p, i: (0, NAT_K_SLC + p)),
            pl.BlockSpec((LANES, S), lambda p, i: (T_V_SLC + p, 0)),
            pl.BlockSpec((2, n_grp, SEL_GROUP, t), lambda p, i: (p, 0, 0, i)),
            kwin_spec(2), kwin_spec(1), kwin_spec(0),
            vwin_spec(2), vwin_spec(1), vwin_spec(0),
            pl.BlockSpec((NSA_PROJ_TILE, t), lambda p, i: (T_GATE * LANES // NSA_PROJ_TILE, i)),
            pl.BlockSpec((1, 3 * gl, NSA_PROJ_TILE), lambda p, i: (p, 0, 0)),
            pl.BlockSpec((gl, t), lambda p, i: (p, i)),
        ],
        out_specs=pl.BlockSpec((gl, t), lambda p, i: (p, i)),
        out_shape=jax.ShapeDtypeStruct((D_MODEL, S), BF16),
        scratch_shapes=([pltpu.VMEM((n_heads, t, t), F32)] * 2 + [pltpu.VMEM((2, n_grp, SEL_GROUP, t), BF16)]
                        + [pltpu.VMEM((n_heads, 1, t), F32)] + [pltpu.VMEM((n_heads, LANES + SUM_ROWS, t), F32)]),
        compiler_params=_cparams(("parallel", "arbitrary")),
        name="nsa_sel",
    )(proj_t, nat, proj_t, selt, nat, nat, nat, proj_t, proj_t, proj_t, proj_t, gate_expand_t, oct)


def _rope_tables(positions, key_major=False):
    half = ROT_DIM // 2
    inv = ROPE_THETA ** (-jnp.arange(0, ROT_DIM, 2, dtype=F32) / ROT_DIM)
    pos = positions.astype(F32)
    n = positions.shape[0]
    axis = 0 if key_major else 1
    ang = inv[:, None] * pos[None, :] if key_major else pos[:, None] * inv[None, :]
    cos, sin = jnp.cos(ang), jnp.sin(ang)

    def const(width, value):
        return jnp.full((width, n) if key_major else (n, width), value, F32)

    rest = HEAD_DIM - ROT_DIM
    rc = jnp.concatenate([cos, cos, const(rest, 1.0)], axis=axis)
    rs1 = jnp.concatenate([-sin, const(half, 0.0), const(rest, 0.0)], axis=axis)
    rs2 = jnp.concatenate([const(half, 0.0), sin, const(rest, 0.0)], axis=axis)
    reps = (LANES // HEAD_DIM, 1) if key_major else (1, LANES // HEAD_DIM)
    return jnp.tile(rc, reps), jnp.tile(rs1, reps), jnp.tile(rs2, reps)


def _pair_heads(w, axis):
    shape = w.shape
    w = w.reshape(shape[:axis] + (NSA_KV_HEADS // 2, 2, NSA_GROUP, HEAD_DIM) + shape[axis + 1:])
    w = jnp.swapaxes(w, axis + 1, axis + 2)
    return w.reshape(shape)


def _nsa_constants(S):
    nc = S // CMP_STRIDE
    ns = S // SLC_BLOCK
    nsp = -(-ns // LANES) * LANES
    ct = min(CMP_TILE, nc)
    imp_rows = -(-(ct * CMP_STRIDE // SLC_BLOCK + IMP_PAD) // SUM_ROWS) * SUM_ROWS
    ci = np.arange(ct)[None, :] * CMP_STRIDE
    st = np.arange(imp_rows)[:, None] * SLC_BLOCK
    w_imp = (ci < st + SLC_BLOCK) & (ci + CMP_BLOCK > st)
    e = np.zeros((NSA_KV_HEADS // 2, 3 * NSA_GROUP * LANES, NSA_PROJ_TILE), np.float32)
    for p in range(NSA_KV_HEADS // 2):
        for br in range(3):
            for g in range(NSA_GROUP):
                for hf in range(2):
                    src = br * N_HEADS + (2 * p + hf) * NSA_GROUP + g
                    r0 = br * NSA_GROUP * LANES + g * LANES + hf * HEAD_DIM
                    e[p, r0:r0 + HEAD_DIM, src] = 1.0
    cmp_end = np.minimum(np.arange(nc) * CMP_STRIDE + CMP_BLOCK - 1, S - 1)
    return nc, nsp, jnp.asarray(w_imp.astype(np.float32), BF16), jnp.asarray(e, BF16), cmp_end


def _nsa_layer(h, g, positions, w_in, pe_k, w1_k, w2_k, pe_v, w1_v, w2_v, w_o):
    S, D = h.shape
    qd = N_HEADS * HEAD_DIM
    kvd = NSA_KV_HEADS * HEAD_DIM
    nc, nsp, w_imp, gate_expand_t, cmp_end = _nsa_constants(S)
    rc, rs1, rs2 = _rope_tables(positions)

    def kv_piece(n):
        return w_in[:, qd + n * kvd:qd + (n + 1) * kvd]

    w_nat = jnp.concatenate([kv_piece(0), kv_piece(1), kv_piece(2), kv_piece(4)], axis=1).astype(BF16)
    wg = jnp.pad(w_in[:, qd + 6 * kvd:], ((0, 0), (0, NSA_T_ROWS - T_GATE * LANES - 3 * N_HEADS)))
    w_t = jnp.concatenate([_pair_heads(w_in[:, :qd], 1) * QK_SCALE, kv_piece(3), kv_piece(5), wg], axis=1).T.astype(BF16)
    nat = _proj(h, g, w_nat, "nsa_proj", key_major=False, tn=2 * NSA_PROJ_TILE,
                rope=(rc, rs1, rs2), rope_tiles=NSA_NAT_ROPE_TILES)
    proj_t = _proj(h, g, w_t, "nsa_proj_t", key_major=True, tn=2 * NSA_PROJ_TILE,
                   rope=_rope_tables(positions, key_major=True), rope_tiles=NSA_T_ROPE_TILES)

    raw = nat[:, :2 * kvd].reshape(nc, CMP_STRIDE, 2, NSA_KV_HEADS, HEAD_DIM)
    raw = raw.transpose(2, 3, 0, 1, 4).reshape(2, NSA_KV_HEADS, nc, CMP_STRIDE * HEAD_DIM)
    w1 = jnp.stack([w1_k, w1_v]).astype(BF16)
    pe = jnp.stack([pe_k.reshape(1, -1), pe_v.reshape(1, -1)])
    pe = jnp.pad(pe, ((0, 0), (0, 7), (0, 0))).astype(BF16)
    w2 = jnp.stack([w2_k, w2_v])
    eye = jnp.eye(NSA_KV_HEADS, dtype=F32)
    w2p = (w2[:, None, :, None, :] * eye[None, :, None, :, None]).reshape(2, NSA_KV_HEADS, CMP_HIDDEN, kvd).astype(BF16)
    kvc = _nsa_cmp(raw, w1, pe, w2p, *_rope_tables(positions[cmp_end]))

    oct, selt = _nsa_cmp_attn(proj_t, kvc, kvc[1].T, w_imp, nsp)
    selt = selt.reshape(NSA_KV_HEADS, -1, SEL_GROUP, S)
    ot = _nsa_sel(nat, proj_t, selt, oct, gate_expand_t)
    return _matmul_res(ot, _pair_heads(w_o, 0).astype(BF16), h, "nsa_out")


def _trunk(x2, positions, p):
    h = _fox_layer(x2, p["l0_norm_mix"], p["l0_fox_w_qkv"], p["l0_fox_w_f"], p["l0_fox_b_f"], p["l0_fox_w_o"])
    h = _mlp(h, p["l0_norm_mlp"], p["l0_mlp_w1"].astype(BF16), p["l0_mlp_w2"].astype(BF16))
    h = _pool_layer(h, p["l1_norm_mix"], p["l1_pool_w"], p["l1_pool_scale"])
    h = _mlp(h, p["l1_norm_mlp"], p["l1_mlp_w1"].astype(BF16), p["l1_mlp_w2"].astype(BF16))
    h = _conv_layer(h, p["l2_norm_mix"], p["l2_conv_w_in"], p["l2_conv_w"], p["l2_conv_w_out"])
    h = _mlp(h, p["l2_norm_mlp"], p["l2_mlp_w1"].astype(BF16), p["l2_mlp_w2"].astype(BF16))
    h = _nsa_layer(h, p["l3_norm_mix"], positions, p["l3_nsa_w_in"], p["l3_nsa_cmp_pe_k"], p["l3_nsa_cmp_w1_k"],
                   p["l3_nsa_cmp_w2_k"], p["l3_nsa_cmp_pe_v"], p["l3_nsa_cmp_w1_v"], p["l3_nsa_cmp_w2_v"],
                   p["l3_nsa_w_o"])
    return _mlp(h, p["l3_norm_mlp"], p["l3_mlp_w1"].astype(BF16), p["l3_mlp_w2"].astype(BF16), p["final_norm"])


def kernel(x, positions, l0_norm_mix, l0_fox_w_qkv, l0_fox_w_f, l0_fox_b_f, l0_fox_w_o, l0_norm_mlp, l0_mlp_w1, l0_mlp_w2, l1_norm_mix, l1_pool_w, l1_pool_scale, l1_norm_mlp, l1_mlp_w1, l1_mlp_w2, l2_norm_mix, l2_conv_w_in, l2_conv_w, l2_conv_w_out, l2_norm_mlp, l2_mlp_w1, l2_mlp_w2, l3_norm_mix, l3_nsa_w_in, l3_nsa_cmp_pe_k, l3_nsa_cmp_w1_k, l3_nsa_cmp_w2_k, l3_nsa_cmp_pe_v, l3_nsa_cmp_w1_v, l3_nsa_cmp_w2_v, l3_nsa_w_o, l3_norm_mlp, l3_mlp_w1, l3_mlp_w2, final_norm):
    params = dict(locals())
    B, S, D = x.shape
    outs = [_trunk(x[b], positions, params) for b in range(B)]
    return jnp.stack(outs, axis=0)
```

```python
import functools

import numpy as np
import jax
import jax.numpy as jnp
from jax import lax
from jax.experimental import pallas as pl
from jax.experimental.pallas import tpu as pltpu

F32 = jnp.float32
BF16 = jnp.bfloat16

D_MODEL = 1024
HEAD_DIM = 64
N_HEADS = D_MODEL // HEAD_DIM
D_FF = 4 * D_MODEL
ROPE_THETA = 500000.0
ROT_DIM = HEAD_DIM // 4
RMS_EPS = 1e-6
POOL_WINDOWS = (2, 4, 8, 16)
POOL_GROUP = D_MODEL // len(POOL_WINDOWS)
POOL_HALO = 16
CONV_WIDTH = 3
CONV_HALO = 8
NSA_KV_HEADS = 4
NSA_GROUP = N_HEADS // NSA_KV_HEADS
CMP_BLOCK = 32
CMP_STRIDE = 16
CMP_HIDDEN = 256
SLC_BLOCK = 64
SLC_SHIFT = 6
SLC_TOPK = 16
WIN = 512
FORCE_SCORE = 1e9
NEG = -1e30
LOG2E = 1.4426950408889634
QK_SCALE = HEAD_DIM ** -0.5 * LOG2E

LANES = 128
V7X_VMEM_LIMIT = 56 * 1024 * 1024
FOX_TILE = 512
FOX_BIAS_TERMS = 3
NSA_TILE = WIN // 2
SEL_GROUP = 16
SUM_ROWS = 16
CMP_TILE = 256
IMP_PAD = 8
NSA_PROJ_TILE = 2 * LANES
NSA_NAT_WIDTH = 8 * LANES
NAT_K_SLC, NAT_K_WIN = 4, 6
NSA_NAT_ROPE_TILES = (1,)
NSA_T_ROWS = 16 * LANES
T_V_SLC, T_V_WIN, T_GATE = 8, 10, 12
NSA_T_ROPE_TILES = (0, 1)


def _cparams(semantics):
    return pltpu.CompilerParams(dimension_semantics=semantics, vmem_limit_bytes=V7X_VMEM_LIMIT)


def _rms(x, g):
    return x * lax.rsqrt(jnp.mean(x * x, axis=-1, keepdims=True) + RMS_EPS) * g


def _dot(a, b):
    return jnp.dot(a, b, preferred_element_type=F32)


def _dot_nt(a, b):
    return lax.dot_general(a, b, (((1,), (1,)), ((), ())), preferred_element_type=F32)


def _dot_tn(a, b):
    return lax.dot_general(a, b, (((0,), (0,)), ((), ())), preferred_element_type=F32)


def _split3(x):
    hi = x.astype(BF16)
    r = x - hi.astype(F32)
    mid = r.astype(BF16)
    lo = (r - mid.astype(F32)).astype(BF16)
    return hi, mid, lo


def _rope(a, rc, rs1, rs2, axis):
    half = ROT_DIM // 2
    return a * rc + pltpu.roll(a, LANES - half, axis) * rs1 + pltpu.roll(a, half, axis) * rs2


def _head_rows(qt_blk):
    low = lax.broadcasted_iota(jnp.int32, qt_blk.shape, 0) < HEAD_DIM
    qf = qt_blk.astype(F32)
    return jnp.where(low, qf, 0.0).astype(BF16), jnp.where(low, 0.0, qf).astype(BF16)


def _with_ones_rows(vt):
    return jnp.concatenate([vt, jnp.ones((SUM_ROWS, vt.shape[1]), vt.dtype)], axis=0)


def _online_update(s_ref, vt_ones, m_ref, acc_ref, idx, causal):
    tk, t = s_ref.shape[1:]
    s = s_ref[idx]
    if causal:
        s = jnp.where(lax.broadcasted_iota(jnp.int32, (tk, t), 0) <= lax.broadcasted_iota(jnp.int32, (tk, t), 1),
                      s, NEG)
    m_old = m_ref[idx]
    m_new = jnp.maximum(m_old, jnp.max(s, axis=0, keepdims=True))
    p = jnp.exp2(s - m_new).astype(BF16)
    m_ref[idx] = m_new
    acc_ref[idx] = jnp.exp2(m_old - m_new) * acc_ref[idx] + _dot(vt_ones, p)


def _online_result(acc_ref, idx):
    acc = acc_ref[idx]
    return acc[:LANES, :] / acc[LANES:LANES + 1, :]


def _pipelined_tiles(i, scores, consume, pairs_per_step=1):
    scores(0, 0)

    def pair(j):
        scores(j + 1, 1)
        consume(j, 0, False)
        scores(j + 2, 0)
        consume(j + 1, 1, False)

    n_pairs = i // 2
    done = 0
    width = pairs_per_step
    while width >= 1:
        trips = (n_pairs - done) // width

        def body(jj, carry, width=width, done=done):
            for u in range(width):
                pair(2 * (done + width * jj + u))
            return carry

        lax.fori_loop(0, trips, body, 0)
        done = done + trips * width
        width //= 2

    @pl.when(i % 2 == 0)
    def _():
        consume(i, 0, True)

    @pl.when(i % 2 == 1)
    def _():
        scores(i, 1)
        consume(i - 1, 0, False)
        consume(i, 1, True)


def _mlp_kernel(x_ref, g_ref, w1_ref, w2_ref, *rest, nf, final):
    if final:
        fg_ref, o_ref, xn_ref = rest
    else:
        o_ref, xn_ref = rest
    f = pl.program_id(1)

    @pl.when(f == 0)
    def _():
        x = x_ref[...]
        xn_ref[...] = _rms(x, g_ref[...]).astype(BF16)
        o_ref[...] = x

    a = _dot(xn_ref[...], w1_ref[...])
    a = jnp.square(jnp.maximum(a, 0.0)).astype(BF16)
    o_ref[...] += _dot(a, w2_ref[...])

    if final:
        @pl.when(f == nf - 1)
        def _():
            o_ref[...] = _rms(o_ref[...], fg_ref[...])


def _mlp(h, g, w1, w2, final_g=None):
    S, D = h.shape
    F = w1.shape[1]
    tm = min(1024, S)
    tf = 2048
    nf = F // tf
    final = final_g is not None
    in_specs = [
        pl.BlockSpec((tm, D), lambda i, f: (i, 0)),
        pl.BlockSpec((1, D), lambda i, f: (0, 0)),
        pl.BlockSpec((D, tf), lambda i, f: (0, f)),
        pl.BlockSpec((tf, D), lambda i, f: (f, 0)),
    ]
    args = [h, g.reshape(1, D), w1, w2]
    if final:
        in_specs.append(pl.BlockSpec((1, D), lambda i, f: (0, 0)))
        args.append(final_g.reshape(1, D))
    return pl.pallas_call(
        functools.partial(_mlp_kernel, nf=nf, final=final),
        grid=(S // tm, nf),
        in_specs=in_specs,
        out_specs=pl.BlockSpec((tm, D), lambda i, f: (i, 0)),
        out_shape=jax.ShapeDtypeStruct((S, D), F32),
        scratch_shapes=[pltpu.VMEM((tm, D), BF16)],
        compiler_params=_cparams(("parallel", "arbitrary")),
        name="mlp",
    )(*args)


def _proj_kernel(x_ref, g_ref, w_ref, *rest, key_major, rope_tiles):
    if rope_tiles:
        rc_ref, rs1_ref, rs2_ref, o_ref, xn_ref = rest
    else:
        o_ref, xn_ref = rest
    j = pl.program_id(1)

    @pl.when(j == 0)
    def _():
        xn_ref[...] = _rms(x_ref[...], g_ref[...]).astype(BF16)

    a = _dot_nt(w_ref[...], xn_ref[...]) if key_major else _dot(xn_ref[...], w_ref[...])
    if not rope_tiles:
        o_ref[...] = a.astype(o_ref.dtype)
        return
    axis = 0 if key_major else 1
    is_rope = functools.reduce(jnp.logical_or, [j == t for t in rope_tiles])

    @pl.when(is_rope)
    def _():
        rc, rs1, rs2 = rc_ref[...], rs1_ref[...], rs2_ref[...]
        blocks = [lax.slice_in_dim(a, b * LANES, (b + 1) * LANES, axis=axis) for b in range(a.shape[axis] // LANES)]
        o_ref[...] = jnp.concatenate([_rope(blk, rc, rs1, rs2, axis) for blk in blocks], axis=axis).astype(o_ref.dtype)

    @pl.when(jnp.logical_not(is_rope))
    def _():
        o_ref[...] = a.astype(o_ref.dtype)


def _proj(h, g, w, name, *, key_major, tn, rope=None, rope_tiles=()):
    S, D = h.shape
    tm = min(1024, S)
    if key_major:
        N = w.shape[0]
        w_spec = pl.BlockSpec((tn, D), lambda i, j: (j, 0))
        rope_spec = pl.BlockSpec((LANES, tm), lambda i, j: (0, i))
        out_spec = pl.BlockSpec((tn, tm), lambda i, j: (j, i))
        out_shape = jax.ShapeDtypeStruct((N, S), BF16)
    else:
        N = w.shape[1]
        w_spec = pl.BlockSpec((D, tn), lambda i, j: (0, j))
        rope_spec = pl.BlockSpec((tm, LANES), lambda i, j: (i, 0))
        out_spec = pl.BlockSpec((tm, tn), lambda i, j: (i, j))
        out_shape = jax.ShapeDtypeStruct((S, N), BF16)
    in_specs = [pl.BlockSpec((tm, D), lambda i, j: (i, 0)), pl.BlockSpec((1, D), lambda i, j: (0, 0)), w_spec]
    args = [h, g.reshape(1, D), w]
    if rope_tiles:
        in_specs += [rope_spec] * 3
        args += list(rope)
    return pl.pallas_call(
        functools.partial(_proj_kernel, key_major=key_major, rope_tiles=tuple(rope_tiles)),
        grid=(S // tm, N // tn),
        in_specs=in_specs,
        out_specs=out_spec,
        out_shape=out_shape,
        scratch_shapes=[pltpu.VMEM((tm, D), BF16)],
        compiler_params=_cparams(("parallel", "arbitrary")),
        name=name,
    )(*args)


def _matmul_res_kernel(at_ref, w_ref, r_ref, o_ref):
    o_ref[...] = r_ref[...] + _dot_tn(at_ref[...], w_ref[...])


def _matmul_res(at, w, res, name):
    K, S = at.shape
    N = w.shape[1]
    tm = min(1024, S)
    return pl.pallas_call(
        _matmul_res_kernel,
        grid=(S // tm,),
        in_specs=[
            pl.BlockSpec((K, tm), lambda i: (0, i)),
            pl.BlockSpec((K, N), lambda i: (0, 0)),
            pl.BlockSpec((tm, N), lambda i: (i, 0)),
        ],
        out_specs=pl.BlockSpec((tm, N), lambda i: (i, 0)),
        out_shape=jax.ShapeDtypeStruct((S, N), F32),
        compiler_params=_cparams(("parallel",)),
        name=name,
    )(at, w, res)


def _fox_gate_kernel(x_ref, g_ref, wf_ref, bf_ref, place_ref, c_ref, carry_ref, *, tm):
    @pl.when(pl.program_id(0) == 0)
    def _():
        carry_ref[...] = jnp.zeros_like(carry_ref)

    xn = _rms(x_ref[...], g_ref[...]).astype(BF16)
    z = _dot(xn, wf_ref[...]) + bf_ref[...]
    logf = jnp.minimum(z, 0.0) - jnp.log(1.0 + jnp.exp(-jnp.abs(z)))
    row = lax.broadcasted_iota(jnp.int32, (tm, tm), 0)
    col = lax.broadcasted_iota(jnp.int32, (tm, tm), 1)
    tri = jnp.where(row >= col, 1.0, 0.0).astype(BF16)
    hi, mid, lo = _split3(logf)
    c = _dot(tri, hi) + _dot(tri, mid) + _dot(tri, lo) + carry_ref[...]
    carry_ref[...] = c[tm - 1:tm, :]
    terms = jnp.concatenate(_split3(c * -LOG2E), axis=1)
    c_ref[...] = _dot(terms, place_ref[...]).astype(c_ref.dtype)


def _fox_gate(h, g, wf, bf, place):
    S, D = h.shape
    tm = min(512, S)
    return pl.pallas_call(
        functools.partial(_fox_gate_kernel, tm=tm),
        grid=(S // tm,),
        in_specs=[
            pl.BlockSpec((tm, D), lambda i: (i, 0)),
            pl.BlockSpec((1, D), lambda i: (0, 0)),
            pl.BlockSpec((D, LANES), lambda i: (0, 0)),
            pl.BlockSpec((1, LANES), lambda i: (0, 0)),
            pl.BlockSpec((FOX_BIAS_TERMS * LANES, LANES), lambda i: (0, 0)),
        ],
        out_specs=pl.BlockSpec((tm, LANES), lambda i: (i, 0)),
        out_shape=jax.ShapeDtypeStruct((S, LANES), BF16),
        scratch_shapes=[pltpu.VMEM((1, LANES), F32)],
        compiler_params=_cparams(("arbitrary",)),
        name="fox_gate",
    )(h, g.reshape(1, D), wf, bf, place)


def _fox_attn_kernel(qt_ref, k_ref, c_ref, vt_ref, ot_ref, s0_ref, s1_ref, m_ref, acc_ref, *, t):
    hp = pl.program_id(0)
    i = pl.program_id(1)
    row = lax.broadcasted_iota(jnp.int32, (LANES, t), 0)
    q_heads = _head_rows(qt_ref[...])
    qa = []
    for h in range(2):
        first = FOX_BIAS_TERMS * (2 * hp + h)
        ones_rows = jnp.logical_and(row >= first, row < first + FOX_BIAS_TERMS)
        qa.append(jnp.concatenate([q_heads[h], jnp.where(ones_rows, 1.0, 0.0).astype(BF16)], axis=0))
    m_ref[...] = jnp.full(m_ref.shape, NEG, F32)
    acc_ref[...] = jnp.zeros(acc_ref.shape, F32)
    s_bufs = (s0_ref, s1_ref)

    def scores(j, buf):
        rows = pl.ds(pl.multiple_of(j * t, t), t)
        lhs = jnp.concatenate([k_ref[rows, :], c_ref[rows, :]], axis=1)
        for h in range(2):
            s_bufs[buf][h] = _dot(lhs, qa[h])

    def consume(j, buf, diag):
        vt = _with_ones_rows(vt_ref[:, pl.ds(pl.multiple_of(j * t, t), t)])
        for h in range(2):
            _online_update(s_bufs[buf], vt, m_ref, acc_ref, h, diag)

    _pipelined_tiles(i, scores, consume, pairs_per_step=8)
    ot = jnp.where(row < HEAD_DIM, _online_result(acc_ref, 0), _online_result(acc_ref, 1))
    ot_ref[...] = ot.astype(ot_ref.dtype)


def _fox_attn(qvt, k, c_terms):
    S = k.shape[0]
    t = min(FOX_TILE, S)
    n_pairs = N_HEADS // 2
    return pl.pallas_call(
        functools.partial(_fox_attn_kernel, t=t),
        grid=(n_pairs, S // t),
        in_specs=[
            pl.BlockSpec((LANES, t), lambda hp, i: (hp, i)),
            pl.BlockSpec((S, LANES), lambda hp, i: (0, hp)),
            pl.BlockSpec((S, LANES), lambda hp, i: (0, 0)),
            pl.BlockSpec((LANES, S), lambda hp, i: (n_pairs + hp, 0)),
        ],
        out_specs=pl.BlockSpec((LANES, t), lambda hp, i: (hp, i)),
        out_shape=jax.ShapeDtypeStruct((D_MODEL, S), BF16),
        scratch_shapes=([pltpu.VMEM((2, t, t), F32)] * 2 + [pltpu.VMEM((2, 1, t), F32)]
                        + [pltpu.VMEM((2, LANES + SUM_ROWS, t), F32)]),
        compiler_params=_cparams(("parallel", "arbitrary")),
        name="fox_attn",
    )(qvt, k, c_terms, qvt)


def _fox_bias_placement():
    place = np.zeros((FOX_BIAS_TERMS * LANES, LANES), np.float32)
    for head in range(N_HEADS):
        for n in range(FOX_BIAS_TERMS):
            place[n * LANES + head, FOX_BIAS_TERMS * head + n] = 1.0
    return jnp.asarray(place, BF16)


def _fox_layer(h, g, w_qkv, w_f, b_f, w_o):
    wf = jnp.pad(w_f, ((0, 0), (0, LANES - N_HEADS))).astype(BF16)
    bf = jnp.pad(b_f, (0, LANES - N_HEADS)).reshape(1, LANES)
    c_terms = _fox_gate(h, g, wf, bf, _fox_bias_placement())
    w_qv = jnp.concatenate([w_qkv[:, :D_MODEL] * QK_SCALE, w_qkv[:, 2 * D_MODEL:]], axis=1).T.astype(BF16)
    qvt = _proj(h, g, w_qv, "fox_qv", key_major=True, tn=1024)
    k = _proj(h, g, w_qkv[:, D_MODEL:2 * D_MODEL].astype(BF16), "fox_k", key_major=False, tn=1024)
    ot = _fox_attn(qvt, k, c_terms)
    return _matmul_res(ot, w_o.astype(BF16), h, "fox_out")


def _pool_kernel(x_ref, halo_ref, g_ref, w_ref, sc_ref, o_ref, *, tm):
    i = pl.program_id(0)
    x = x_ref[...]
    g = g_ref[...]
    xn = _rms(x, g)
    hn = jnp.where(i > 0, _rms(halo_ref[...], g), 0.0)
    xe = jnp.concatenate([hn, xn], axis=0)
    tpos = i * tm + lax.broadcasted_iota(jnp.int32, (tm, 1), 0)
    for gi, w in enumerate(POOL_WINDOWS):
        sl = slice(gi * POOL_GROUP, (gi + 1) * POOL_GROUP)
        s = xe[:, sl]
        k = 1
        while k < w:
            s = s + pltpu.roll(s, k, 0)
            k *= 2
        cnt = jnp.minimum(tpos + 1, w).astype(F32)
        d = (s[POOL_HALO:, :] / cnt - xn[:, sl]).astype(BF16)
        o_ref[:, sl] = x[:, sl] + _dot(d, w_ref[gi]) * sc_ref[:, sl]


def _pool_layer(h, g, w_pool, pool_scale):
    S, D = h.shape
    tm = min(1024, S)
    return pl.pallas_call(
        functools.partial(_pool_kernel, tm=tm),
        grid=(S // tm,),
        in_specs=[
            pl.BlockSpec((tm, D), lambda i: (i, 0)),
            pl.BlockSpec((POOL_HALO, D), lambda i: (jnp.maximum(i * (tm // POOL_HALO) - 1, 0), 0)),
            pl.BlockSpec((1, D), lambda i: (0, 0)),
            pl.BlockSpec((len(POOL_WINDOWS), POOL_GROUP, POOL_GROUP), lambda i: (0, 0, 0)),
            pl.BlockSpec((1, D), lambda i: (0, 0)),
        ],
        out_specs=pl.BlockSpec((tm, D), lambda i: (i, 0)),
        out_shape=jax.ShapeDtypeStruct((S, D), F32),
        compiler_params=_cparams(("parallel",)),
        name="pool",
    )(h, h, g.reshape(1, D), w_pool.astype(BF16), pool_scale.reshape(1, D))


def _conv_in_kernel(x_ref, g_ref, wb_ref, wc_ref, wu_ref, b_ref, z_ref, xn_ref):
    @pl.when(pl.program_id(1) == 0)
    def _():
        xn_ref[...] = _rms(x_ref[...], g_ref[...]).astype(BF16)

    xn = xn_ref[...]
    b_ref[...] = _dot(xn, wb_ref[...])
    z_ref[...] = _dot(xn, wc_ref[...]) * _dot(xn, wu_ref[...])


def _conv_out_kernel(b_ref, z_ref, zh_ref, cw_ref, w_ref, r_ref, o_ref, *, tm):
    i = pl.program_id(0)
    z = z_ref[...]
    zh = jnp.where(i > 0, zh_ref[...], 0.0)
    row = lax.broadcasted_iota(jnp.int32, (tm, 1), 0)
    prev1 = zh[CONV_HALO - 1:CONV_HALO, :]
    prev2 = zh[CONV_HALO - 2:CONV_HALO - 1, :]
    z1 = jnp.where(row == 0, prev1, pltpu.roll(z, 1, 0))
    z2 = jnp.where(row == 0, prev2, jnp.where(row == 1, prev1, pltpu.roll(z, 2, 0)))
    cw = cw_ref[...]
    conv = cw[0:1, :] * z2 + cw[1:2, :] * z1 + cw[2:3, :] * z
    y = (b_ref[...] * conv).astype(BF16)
    o_ref[...] = r_ref[...] + _dot(y, w_ref[...])


def _conv_layer(h, g, w_in, conv_w, w_out):
    S, D = h.shape
    tm = min(1024, S)
    tn = 1024
    nj = D // tn
    w_in = w_in.astype(BF16)
    b, z = pl.pallas_call(
        _conv_in_kernel,
        grid=(S // tm, nj),
        in_specs=[
            pl.BlockSpec((tm, D), lambda i, j: (i, 0)),
            pl.BlockSpec((1, D), lambda i, j: (0, 0)),
            pl.BlockSpec((D, tn), lambda i, j: (0, j)),
            pl.BlockSpec((D, tn), lambda i, j: (0, nj + j)),
            pl.BlockSpec((D, tn), lambda i, j: (0, 2 * nj + j)),
        ],
        out_specs=[pl.BlockSpec((tm, tn), lambda i, j: (i, j)),
                   pl.BlockSpec((tm, tn), lambda i, j: (i, j))],
        out_shape=[jax.ShapeDtypeStruct((S, D), F32), jax.ShapeDtypeStruct((S, D), F32)],
        scratch_shapes=[pltpu.VMEM((tm, D), BF16)],
        compiler_params=_cparams(("parallel", "arbitrary")),
        name="conv_in",
    )(h, g.reshape(1, D), w_in, w_in, w_in)
    cw = jnp.pad(conv_w, ((0, 8 - CONV_WIDTH), (0, 0)))
    return pl.pallas_call(
        functools.partial(_conv_out_kernel, tm=tm),
        grid=(S // tm,),
        in_specs=[
            pl.BlockSpec((tm, D), lambda i: (i, 0)),
            pl.BlockSpec((tm, D), lambda i: (i, 0)),
            pl.BlockSpec((CONV_HALO, D), lambda i: (jnp.maximum(i * (tm // CONV_HALO) - 1, 0), 0)),
            pl.BlockSpec((8, D), lambda i: (0, 0)),
            pl.BlockSpec((D, D), lambda i: (0, 0)),
            pl.BlockSpec((tm, D), lambda i: (i, 0)),
        ],
        out_specs=pl.BlockSpec((tm, D), lambda i: (i, 0)),
        out_shape=jax.ShapeDtypeStruct((S, D), F32),
        compiler_params=_cparams(("parallel",)),
        name="conv_out",
    )(b, z, z, cw, w_out.astype(BF16), h)


def _nsa_cmp_kernel(r_ref, w1_ref, pe_ref, w2_ref, rc_ref, rs1_ref, rs2_ref, o_ref, acc_ref, *, nc):
    kv = pl.program_id(0)
    hd = pl.program_id(1)
    half = CMP_STRIDE * HEAD_DIM
    r = r_ref[0, 0]
    w1 = w1_ref[0]
    first = _dot(r, w1[:half, :])
    second = _dot(r, w1[half:, :])
    pe_term = _dot(pe_ref[0], w1)[0:1, :]
    pre = first + pltpu.roll(second, nc - 1, 0) + pe_term
    ge = 0.5 * pre * (1.0 + jnp.tanh(0.7978845608028654 * (pre + 0.044715 * pre * pre * pre)))
    y = _dot(ge.astype(BF16), w2_ref[0, 0])

    @pl.when(hd == 0)
    def _():
        acc_ref[...] = y

    @pl.when(hd > 0)
    def _():
        acc_ref[...] += y

    @pl.when(hd == NSA_KV_HEADS - 1)
    def _():
        acc = acc_ref[...]

        @pl.when(kv == 0)
        def _():
            rc, rs1, rs2 = rc_ref[...], rs1_ref[...], rs2_ref[...]
            o_ref[0] = jnp.concatenate(
                [_rope(acc[:, :LANES], rc, rs1, rs2, 1), _rope(acc[:, LANES:], rc, rs1, rs2, 1)],
                axis=1).astype(o_ref.dtype)

        @pl.when(kv == 1)
        def _():
            o_ref[0] = acc.astype(o_ref.dtype)


def _nsa_cmp(r, w1, pe, w2p, rc, rs1, rs2):
    nc = r.shape[2]
    kvd = NSA_KV_HEADS * HEAD_DIM
    return pl.pallas_call(
        functools.partial(_nsa_cmp_kernel, nc=nc),
        grid=(2, NSA_KV_HEADS),
        in_specs=[
            pl.BlockSpec((1, 1, nc, CMP_STRIDE * HEAD_DIM), lambda a, b: (a, b, 0, 0)),
            pl.BlockSpec((1, CMP_BLOCK * HEAD_DIM, CMP_HIDDEN), lambda a, b: (a, 0, 0)),
            pl.BlockSpec((1, 8, CMP_BLOCK * HEAD_DIM), lambda a, b: (a, 0, 0)),
            pl.BlockSpec((1, 1, CMP_HIDDEN, kvd), lambda a, b: (a, b, 0, 0)),
            pl.BlockSpec((nc, LANES), lambda a, b: (0, 0)),
            pl.BlockSpec((nc, LANES), lambda a, b: (0, 0)),
            pl.BlockSpec((nc, LANES), lambda a, b: (0, 0)),
        ],
        out_specs=pl.BlockSpec((1, nc, kvd), lambda a, b: (a, 0, 0)),
        out_shape=jax.ShapeDtypeStruct((2, nc, kvd), BF16),
        scratch_shapes=[pltpu.VMEM((nc, kvd), F32)],
        compiler_params=_cparams(("parallel", "arbitrary")),
        name="nsa_cmp",
    )(r, w1, pe, w2p, rc, rs1, rs2)


def _nsa_cmp_attn_body(i, qt_ref, kc_ref, vct_ref, w_ref, oct_ref, selt_ref, *, t, n_tiles, ct, nsp):
    qpos = i * t + lax.broadcasted_iota(jnp.int32, (1, t), 1)
    any_valid = qpos >= CMP_BLOCK - 1
    low = lax.broadcasted_iota(jnp.int32, (LANES, t), 0) < HEAD_DIM
    per_tile = ct * CMP_STRIDE // SLC_BLOCK
    stat_rows = jnp.concatenate([jnp.ones((SUM_ROWS, ct), BF16), w_ref[...]], axis=0)
    cmasks = [CMP_STRIDE * (c * ct + lax.broadcasted_iota(jnp.int32, (ct, t), 0)) + (CMP_BLOCK - 1) <= qpos
              for c in range(n_tiles)]
    blk = lax.broadcasted_iota(jnp.int32, (nsp, t), 0)
    blk_f = blk.astype(F32)
    cur = jnp.right_shift(qpos, SLC_SHIFT)
    forced = jnp.logical_or(blk == 0, jnp.logical_or(blk == cur, blk == cur - 1))
    valid = blk * SLC_BLOCK <= qpos
    vals = []
    for pair in range(NSA_KV_HEADS // 2):
        lanes = slice(pair * LANES, (pair + 1) * LANES)
        lhs = [jnp.concatenate([vct_ref[lanes, c * ct:(c + 1) * ct], stat_rows], axis=0) for c in range(n_tiles)]
        imp = [[None] * n_tiles, [None] * n_tiles]
        for g in range(NSA_GROUP):
            rows = slice((pair * NSA_GROUP + g) * LANES, (pair * NSA_GROUP + g + 1) * LANES)
            qh = _head_rows(qt_ref[rows, :])
            oc = []
            for h in range(2):
                m = jnp.full((1, t), NEG, F32)
                acc = jnp.zeros((LANES + SUM_ROWS, t), F32)
                parts, maxes = [], []
                for c in range(n_tiles):
                    s = jnp.where(cmasks[c], _dot(kc_ref[0, c * ct:(c + 1) * ct, lanes], qh[h]), NEG)
                    m_new = jnp.maximum(m, jnp.max(s, axis=0, keepdims=True))
                    p = jnp.exp2(s - m_new)
                    p_hi = p.astype(BF16)
                    p_lo = (p - p_hi.astype(F32)).astype(BF16)
                    r_hi = _dot(lhs[c], p_hi)
                    r_lo = _dot(stat_rows, p_lo)
                    stats = r_hi[LANES:, :] + r_lo
                    acc = jnp.exp2(m - m_new) * acc + jnp.concatenate([r_hi[:LANES, :], stats[:SUM_ROWS, :]], axis=0)
                    parts.append(stats[SUM_ROWS:, :])
                    maxes.append(m_new)
                    m = m_new
                inv = jnp.where(any_valid, 1.0 / acc[LANES:LANES + 1, :], 0.0)
                oc.append(acc[:LANES, :] * inv)
                for c in range(n_tiles):
                    piece = parts[c] * (jnp.exp2(maxes[c] - m) * inv)
                    imp[h][c] = piece if imp[h][c] is None else imp[h][c] + piece
            oct_ref[rows, :] = jnp.where(low, oc[0], oc[1]).astype(oct_ref.dtype)
        for h in range(2):
            segs = []
            for c in range(n_tiles):
                seg = imp[h][c][:per_tile, :]
                if c > 0:
                    spill = imp[h][c - 1][per_tile:per_tile + IMP_PAD, :]
                    seg = jnp.concatenate([seg[:IMP_PAD, :] + spill, seg[IMP_PAD:, :]], axis=0)
                segs.append(seg)
            score = jnp.concatenate(segs, axis=0)[:nsp, :]
            vals.append(jnp.where(valid, jnp.where(forced, FORCE_SCORE, score), NEG))

    def pick_one(_, vals):
        out = []
        for v in vals:
            mx = jnp.max(v, axis=0, keepdims=True)
            first = jnp.min(jnp.where(v == mx, blk_f, float(nsp)), axis=0, keepdims=True)
            out.append(jnp.where(blk_f == first, -jnp.inf, v))
        return tuple(out)

    vals = lax.fori_loop(0, SLC_TOPK, pick_one, tuple(vals))
    for kvh in range(NSA_KV_HEADS):
        picked = jnp.logical_and(valid, vals[kvh] == -jnp.inf)
        selt_ref[kvh, :nsp, :] = jnp.where(picked, 1.0, 0.0).astype(selt_ref.dtype)
        if nsp < selt_ref.shape[1]:
            selt_ref[kvh, nsp:, :] = jnp.zeros((selt_ref.shape[1] - nsp, t), selt_ref.dtype)


def _nsa_cmp_attn_kernel(qt_ref, kc_ref, vct_ref, w_ref, oct_ref, selt_ref, *, t, n_steps, n_spans):
    i = pl.program_id(0)
    ct = w_ref.shape[1]
    for q in range(n_spans):
        @pl.when(jnp.logical_and(i >= q * n_steps // n_spans, i < (q + 1) * n_steps // n_spans))
        def _(q=q):
            _nsa_cmp_attn_body(i, qt_ref, kc_ref, vct_ref, w_ref, oct_ref, selt_ref, t=t, n_tiles=q + 1, ct=ct,
                               nsp=(q + 1) * ct * CMP_STRIDE // SLC_BLOCK)


def _nsa_cmp_attn(proj_t, kvc, vct, w_imp, nsp):
    S = proj_t.shape[1]
    t = NSA_TILE
    nc = kvc.shape[1]
    imp_rows, ct = w_imp.shape
    kvd = NSA_KV_HEADS * HEAD_DIM
    return pl.pallas_call(
        functools.partial(_nsa_cmp_attn_kernel, t=t, n_steps=S // t, n_spans=nc // ct),
        grid=(S // t,),
        in_specs=[
            pl.BlockSpec((D_MODEL, t), lambda i: (0, i)),
            pl.BlockSpec((1, nc, kvd), lambda i: (0, 0, 0)),
            pl.BlockSpec((kvd, nc), lambda i: (0, 0)),
            pl.BlockSpec((imp_rows, ct), lambda i: (0, 0)),
        ],
        out_specs=[pl.BlockSpec((D_MODEL, t), lambda i: (0, i)),
                   pl.BlockSpec((NSA_KV_HEADS, nsp, t), lambda i: (0, 0, i))],
        out_shape=[jax.ShapeDtypeStruct((D_MODEL, S), BF16),
                   jax.ShapeDtypeStruct((NSA_KV_HEADS, nsp, S), BF16)],
        compiler_params=_cparams(("parallel",)),
        name="nsa_cmp_attn",
    )(proj_t, kvc, vct, w_imp)


def _nsa_sel_kernel(qt_ref, ks_ref, vst_ref, selt_ref, kw0_ref, kw1_ref, kw2_ref, vwt0_ref, vwt1_ref, vwt2_ref,
                    gzt_ref, et_ref, oct_ref, ot_ref, s0_ref, s1_ref, bias_ref, m_ref, acc_ref, *, t):
    i = pl.program_id(1)
    n_heads = 2 * NSA_GROUP
    low = lax.broadcasted_iota(jnp.int32, (LANES, t), 0) < HEAD_DIM
    qs = []
    for g in range(NSA_GROUP):
        qs.extend(_head_rows(qt_ref[g * LANES:(g + 1) * LANES, :]))
    bias_ref[...] = ((selt_ref[...].astype(F32) - 1.0) * -NEG).astype(BF16)
    m_ref[...] = jnp.full(m_ref.shape, NEG, F32)
    acc_ref[...] = jnp.zeros(acc_ref.shape, F32)
    tok_blk = jnp.right_shift(lax.broadcasted_iota(jnp.int32, (t, LANES), 0), SLC_SHIFT)
    lane_col = lax.broadcasted_iota(jnp.int32, (t, LANES), 1)
    zero_rows = jnp.zeros((LANES - SEL_GROUP, t), BF16)
    tiles_per_group = SEL_GROUP * SLC_BLOCK // t
    s_bufs = (s0_ref, s1_ref)

    def scores(j, buf):
        k = ks_ref[pl.ds(pl.multiple_of(j * t, t), t), :]
        grp = j // tiles_per_group
        first_blk = (j % tiles_per_group) * (t // SLC_BLOCK)
        expand = jnp.where(lane_col == first_blk + tok_blk, 1.0, 0.0).astype(BF16)
        lhs = jnp.concatenate([k, expand], axis=1)
        for h in range(2):
            blk_bias = bias_ref[h, grp]
            for g in range(NSA_GROUP):
                rhs = jnp.concatenate([qs[2 * g + h], blk_bias, zero_rows], axis=0)
                s_bufs[buf][2 * g + h] = _dot(lhs, rhs)

    def consume(j, buf, diag):
        vt = _with_ones_rows(vst_ref[:, pl.ds(pl.multiple_of(j * t, t), t)])
        for idx in range(n_heads):
            _online_update(s_bufs[buf], vt, m_ref, acc_ref, idx, diag)

    _pipelined_tiles(i, scores, consume, pairs_per_step=2)

    n_win = WIN // t + 1
    k_win = jnp.concatenate([kw0_ref[...], kw1_ref[...], kw2_ref[...]], axis=0)
    vt_win = jnp.concatenate([vwt0_ref[...], vwt1_ref[...], vwt2_ref[...]], axis=1)
    kpos = (i - (n_win - 1)) * t + lax.broadcasted_iota(jnp.int32, (n_win * t, t), 0)
    qpos = i * t + lax.broadcasted_iota(jnp.int32, (n_win * t, t), 1)
    wmask = jnp.logical_and(jnp.logical_and(kpos <= qpos, kpos > qpos - WIN), kpos >= 0)
    ow = []
    for idx in range(n_heads):
        s = jnp.where(wmask, _dot(k_win, qs[idx]), NEG)
        p = jnp.exp2(s - jnp.max(s, axis=0, keepdims=True))
        l = jnp.sum(p, axis=0, keepdims=True)
        ow.append(_dot(vt_win, p.astype(BF16)) / l)

    gates = jax.nn.sigmoid(_dot(et_ref[0], gzt_ref[...]))
    gw = NSA_GROUP * LANES
    for g in range(NSA_GROUP):
        rows = slice(g * LANES, (g + 1) * LANES)
        o_cmp = oct_ref[rows, :].astype(F32)
        o_slc = jnp.where(low, _online_result(acc_ref, 2 * g), _online_result(acc_ref, 2 * g + 1))
        o_win = jnp.where(low, ow[2 * g], ow[2 * g + 1])
        out = (gates[g * LANES:(g + 1) * LANES, :] * o_cmp
               + gates[gw + g * LANES:gw + (g + 1) * LANES, :] * o_slc
               + gates[2 * gw + g * LANES:2 * gw + (g + 1) * LANES, :] * o_win)
        ot_ref[rows, :] = out.astype(ot_ref.dtype)


def _nsa_sel(nat, proj_t, selt, oct, gate_expand_t):
    S = nat.shape[0]
    t = NSA_TILE
    n_grp = selt.shape[1]
    n_heads = 2 * NSA_GROUP
    gl = NSA_GROUP * LANES

    def kwin_spec(back):
        return pl.BlockSpec((t, LANES), lambda p, i: (jnp.maximum(i - back, 0), NAT_K_WIN + p))

    def vwin_spec(back):
        return pl.BlockSpec((LANES, t), lambda p, i: (T_V_WIN + p, jnp.maximum(i - back, 0)))

    return pl.pallas_call(
        functools.partial(_nsa_sel_kernel, t=t),
        grid=(NSA_KV_HEADS // 2, S // t),
        in_specs=[
            pl.BlockSpec((gl, t), lambda p, i: (p, i)),
            pl.BlockSpec((S, LANES), lambda p, i: (0, NAT_K_SLC + p)),
            pl.BlockSpec((LANES, S), lambda p, i: (T_V_SLC + p, 0)),
            pl.BlockSpec((2, n_grp, SEL_GROUP, t), lambda p, i: (p, 0, 0, i)),
            kwin_spec(2), kwin_spec(1), kwin_spec(0),
            vwin_spec(2), vwin_spec(1), vwin_spec(0),
            pl.BlockSpec((NSA_PROJ_TILE, t), lambda p, i: (T_GATE * LANES // NSA_PROJ_TILE, i)),
            pl.BlockSpec((1, 3 * gl, NSA_PROJ_TILE), lambda p, i: (p, 0, 0)),
            pl.BlockSpec((gl, t), lambda p, i: (p, i)),
        ],
        out_specs=pl.BlockSpec((gl, t), lambda p, i: (p, i)),
        out_shape=jax.ShapeDtypeStruct((D_MODEL, S), BF16),
        scratch_shapes=([pltpu.VMEM((n_heads, t, t), F32)] * 2 + [pltpu.VMEM((2, n_grp, SEL_GROUP, t), BF16)]
                        + [pltpu.VMEM((n_heads, 1, t), F32)] + [pltpu.VMEM((n_heads, LANES + SUM_ROWS, t), F32)]),
        compiler_params=_cparams(("parallel", "arbitrary")),
        name="nsa_sel",
    )(proj_t, nat, proj_t, selt, nat, nat, nat, proj_t, proj_t, proj_t, proj_t, gate_expand_t, oct)


def _rope_tables(positions, key_major=False):
    half = ROT_DIM // 2
    inv = ROPE_THETA ** (-jnp.arange(0, ROT_DIM, 2, dtype=F32) / ROT_DIM)
    pos = positions.astype(F32)
    n = positions.shape[0]
    axis = 0 if key_major else 1
    ang = inv[:, None] * pos[None, :] if key_major else pos[:, None] * inv[None, :]
    cos, sin = jnp.cos(ang), jnp.sin(ang)

    def const(width, value):
        return jnp.full((width, n) if key_major else (n, width), value, F32)

    rest = HEAD_DIM - ROT_DIM
    rc = jnp.concatenate([cos, cos, const(rest, 1.0)], axis=axis)
    rs1 = jnp.concatenate([-sin, const(half, 0.0), const(rest, 0.0)], axis=axis)
    rs2 = jnp.concatenate([const(half, 0.0), sin, const(rest, 0.0)], axis=axis)
    reps = (LANES // HEAD_DIM, 1) if key_major else (1, LANES // HEAD_DIM)
    return jnp.tile(rc, reps), jnp.tile(rs1, reps), jnp.tile(rs2, reps)


def _pair_heads(w, axis):
    shape = w.shape
    w = w.reshape(shape[:axis] + (NSA_KV_HEADS // 2, 2, NSA_GROUP, HEAD_DIM) + shape[axis + 1:])
    w = jnp.swapaxes(w, axis + 1, axis + 2)
    return w.reshape(shape)


def _nsa_constants(S):
    nc = S // CMP_STRIDE
    ns = S // SLC_BLOCK
    nsp = -(-ns // LANES) * LANES
    ct = min(CMP_TILE, nc)
    imp_rows = -(-(ct * CMP_STRIDE // SLC_BLOCK + IMP_PAD) // SUM_ROWS) * SUM_ROWS
    ci = np.arange(ct)[None, :] * CMP_STRIDE
    st = np.arange(imp_rows)[:, None] * SLC_BLOCK
    w_imp = (ci < st + SLC_BLOCK) & (ci + CMP_BLOCK > st)
    e = np.zeros((NSA_KV_HEADS // 2, 3 * NSA_GROUP * LANES, NSA_PROJ_TILE), np.float32)
    for p in range(NSA_KV_HEADS // 2):
        for br in range(3):
            for g in range(NSA_GROUP):
                for hf in range(2):
                    src = br * N_HEADS + (2 * p + hf) * NSA_GROUP + g
                    r0 = br * NSA_GROUP * LANES + g * LANES + hf * HEAD_DIM
                    e[p, r0:r0 + HEAD_DIM, src] = 1.0
    cmp_end = np.minimum(np.arange(nc) * CMP_STRIDE + CMP_BLOCK - 1, S - 1)
    return nc, nsp, jnp.asarray(w_imp.astype(np.float32), BF16), jnp.asarray(e, BF16), cmp_end


def _nsa_layer(h, g, positions, w_in, pe_k, w1_k, w2_k, pe_v, w1_v, w2_v, w_o):
    S, D = h.shape
    qd = N_HEADS * HEAD_DIM
    kvd = NSA_KV_HEADS * HEAD_DIM
    nc, nsp, w_imp, gate_expand_t, cmp_end = _nsa_constants(S)
    rc, rs1, rs2 = _rope_tables(positions)

    def kv_piece(n):
        return w_in[:, qd + n * kvd:qd + (n + 1) * kvd]

    w_nat = jnp.concatenate([kv_piece(0), kv_piece(1), kv_piece(2), kv_piece(4)], axis=1).astype(BF16)
    wg = jnp.pad(w_in[:, qd + 6 * kvd:], ((0, 0), (0, NSA_T_ROWS - T_GATE * LANES - 3 * N_HEADS)))
    w_t = jnp.concatenate([_pair_heads(w_in[:, :qd], 1) * QK_SCALE, kv_piece(3), kv_piece(5), wg], axis=1).T.astype(BF16)
    nat = _proj(h, g, w_nat, "nsa_proj", key_major=False, tn=2 * NSA_PROJ_TILE,
                rope=(rc, rs1, rs2), rope_tiles=NSA_NAT_ROPE_TILES)
    proj_t = _proj(h, g, w_t, "nsa_proj_t", key_major=True, tn=2 * NSA_PROJ_TILE,
                   rope=_rope_tables(positions, key_major=True), rope_tiles=NSA_T_ROPE_TILES)

    raw = nat[:, :2 * kvd].reshape(nc, CMP_STRIDE, 2, NSA_KV_HEADS, HEAD_DIM)
    raw = raw.transpose(2, 3, 0, 1, 4).reshape(2, NSA_KV_HEADS, nc, CMP_STRIDE * HEAD_DIM)
    w1 = jnp.stack([w1_k, w1_v]).astype(BF16)
    pe = jnp.stack([pe_k.reshape(1, -1), pe_v.reshape(1, -1)])
    pe = jnp.pad(pe, ((0, 0), (0, 7), (0, 0))).astype(BF16)
    w2 = jnp.stack([w2_k, w2_v])
    eye = jnp.eye(NSA_KV_HEADS, dtype=F32)
    w2p = (w2[:, None, :, None, :] * eye[None, :, None, :, None]).reshape(2, NSA_KV_HEADS, CMP_HIDDEN, kvd).astype(BF16)
    kvc = _nsa_cmp(raw, w1, pe, w2p, *_rope_tables(positions[cmp_end]))

    oct, selt = _nsa_cmp_attn(proj_t, kvc, kvc[1].T, w_imp, nsp)
    selt = selt.reshape(NSA_KV_HEADS, -1, SEL_GROUP, S)
    ot = _nsa_sel(nat, proj_t, selt, oct, gate_expand_t)
    return _matmul_res(ot, _pair_heads(w_o, 0).astype(BF16), h, "nsa_out")


def _trunk(x2, positions, p):
    h = _fox_layer(x2, p["l0_norm_mix"], p["l0_fox_w_qkv"], p["l0_fox_w_f"], p["l0_fox_b_f"], p["l0_fox_w_o"])
    h = _mlp(h, p["l0_norm_mlp"], p["l0_mlp_w1"].astype(BF16), p["l0_mlp_w2"].astype(BF16))
    h = _pool_layer(h, p["l1_norm_mix"], p["l1_pool_w"], p["l1_pool_scale"])
    h = _mlp(h, p["l1_norm_mlp"], p["l1_mlp_w1"].astype(BF16), p["l1_mlp_w2"].astype(BF16))
    h = _conv_layer(h, p["l2_norm_mix"], p["l2_conv_w_in"], p["l2_conv_w"], p["l2_conv_w_out"])
    h = _mlp(h, p["l2_norm_mlp"], p["l2_mlp_w1"].astype(BF16), p["l2_mlp_w2"].astype(BF16))
    h = _nsa_layer(h, p["l3_norm_mix"], positions, p["l3_nsa_w_in"], p["l3_nsa_cmp_pe_k"], p["l3_nsa_cmp_w1_k"],
                   p["l3_nsa_cmp_w2_k"], p["l3_nsa_cmp_pe_v"], p["l3_nsa_cmp_w1_v"], p["l3_nsa_cmp_w2_v"],
                   p["l3_nsa_w_o"])
    return _mlp(h, p["l3_norm_mlp"], p["l3_mlp_w1"].astype(BF16), p["l3_mlp_w2"].astype(BF16), p["final_norm"])


def kernel(x, positions, l0_norm_mix, l0_fox_w_qkv, l0_fox_w_f, l0_fox_b_f, l0_fox_w_o, l0_norm_mlp, l0_mlp_w1, l0_mlp_w2, l1_norm_mix, l1_pool_w, l1_pool_scale, l1_norm_mlp, l1_mlp_w1, l1_mlp_w2, l2_norm_mix, l2_conv_w_in, l2_conv_w, l2_conv_w_out, l2_norm_mlp, l2_mlp_w1, l2_mlp_w2, l3_norm_mix, l3_nsa_w_in, l3_nsa_cmp_pe_k, l3_nsa_cmp_w1_k, l3_nsa_cmp_w2_k, l3_nsa_cmp_pe_v, l3_nsa_cmp_w1_v, l3_nsa_cmp_w2_v, l3_nsa_w_o, l3_norm_mlp, l3_mlp_w1, l3_mlp_w2, final_norm):
    params = dict(locals())
    B, S, D = x.shape
    outs = [_trunk(x[b], positions, params) for b in range(B)]
    return jnp.stack(outs, axis=0)
```

```python
import functools

import numpy as np
import jax
import jax.numpy as jnp
from jax import lax
from jax.experimental import pallas as pl
from jax.experimental.pallas import tpu as pltpu

F32 = jnp.float32
BF16 = jnp.bfloat16

D_MODEL = 1024
HEAD_DIM = 64
N_HEADS = D_MODEL // HEAD_DIM
D_FF = 4 * D_MODEL
ROPE_THETA = 500000.0
ROT_DIM = HEAD_DIM // 4
RMS_EPS = 1e-6
POOL_WINDOWS = (2, 4, 8, 16)
POOL_GROUP = D_MODEL // len(POOL_WINDOWS)
POOL_HALO = 16
CONV_WIDTH = 3
CONV_HALO = 8
NSA_KV_HEADS = 4
NSA_GROUP = N_HEADS // NSA_KV_HEADS
CMP_BLOCK = 32
CMP_STRIDE = 16
CMP_HIDDEN = 256
SLC_BLOCK = 64
SLC_SHIFT = 6
SLC_TOPK = 16
WIN = 512
FORCE_SCORE = 1e9
NEG = -1e30
LOG2E = 1.4426950408889634
QK_SCALE = HEAD_DIM ** -0.5 * LOG2E

LANES = 128
V7X_VMEM_LIMIT = 56 * 1024 * 1024
FOX_TILE = 512
FOX_BIAS_TERMS = 3
NSA_TILE = WIN // 2
SEL_GROUP = 16
SUM_ROWS = 16
CMP_TILE = 256
IMP_PAD = 8
NSA_PROJ_TILE = 2 * LANES
NSA_NAT_WIDTH = 8 * LANES
NAT_K_SLC, NAT_K_WIN = 4, 6
NSA_NAT_ROPE_BLOCKS = (4, 5, 6, 7)
NSA_T_ROWS = 16 * LANES
T_V_SLC, T_V_WIN, T_GATE = 8, 10, 12
NSA_T_ROPE_BLOCKS = tuple(range(8))


def _cparams(semantics):
    return pltpu.CompilerParams(dimension_semantics=semantics, vmem_limit_bytes=V7X_VMEM_LIMIT)


def _rms(x, g):
    return x * lax.rsqrt(jnp.mean(x * x, axis=-1, keepdims=True) + RMS_EPS) * g


def _dot(a, b):
    return jnp.dot(a, b, preferred_element_type=F32)


def _dot_nt(a, b):
    return lax.dot_general(a, b, (((1,), (1,)), ((), ())), preferred_element_type=F32)


def _dot_tn(a, b):
    return lax.dot_general(a, b, (((0,), (0,)), ((), ())), preferred_element_type=F32)


def _split3(x):
    hi = x.astype(BF16)
    r = x - hi.astype(F32)
    mid = r.astype(BF16)
    lo = (r - mid.astype(F32)).astype(BF16)
    return hi, mid, lo


def _rope(a, rc, rs1, rs2, axis):
    half = ROT_DIM // 2
    return a * rc + pltpu.roll(a, LANES - half, axis) * rs1 + pltpu.roll(a, half, axis) * rs2


def _head_rows(qt_blk):
    low = lax.broadcasted_iota(jnp.int32, qt_blk.shape, 0) < HEAD_DIM
    qf = qt_blk.astype(F32)
    return jnp.where(low, qf, 0.0).astype(BF16), jnp.where(low, 0.0, qf).astype(BF16)


def _with_ones_rows(vt):
    return jnp.concatenate([vt, jnp.ones((SUM_ROWS, vt.shape[1]), vt.dtype)], axis=0)


def _online_update(s_ref, vt_ones, m_ref, acc_ref, idx, causal):
    tk, t = s_ref.shape[1:]
    s = s_ref[idx]
    if causal:
        s = jnp.where(lax.broadcasted_iota(jnp.int32, (tk, t), 0) <= lax.broadcasted_iota(jnp.int32, (tk, t), 1),
                      s, NEG)
    m_old = m_ref[idx]
    m_new = jnp.maximum(m_old, jnp.max(s, axis=0, keepdims=True))
    p = jnp.exp2(s - m_new).astype(BF16)
    m_ref[idx] = m_new
    acc_ref[idx] = jnp.exp2(m_old - m_new) * acc_ref[idx] + _dot(vt_ones, p)


def _online_result(acc_ref, idx):
    acc = acc_ref[idx]
    return acc[:LANES, :] / acc[LANES:LANES + 1, :]


def _pipelined_tiles(i, scores, consume, pairs_per_step=1):
    scores(0, 0)

    def pair(j):
        scores(j + 1, 1)
        consume(j, 0, False)
        scores(j + 2, 0)
        consume(j + 1, 1, False)

    n_pairs = i // 2
    done = 0
    width = pairs_per_step
    while width >= 1:
        trips = (n_pairs - done) // width

        def body(jj, carry, width=width, done=done):
            for u in range(width):
                pair(2 * (done + width * jj + u))
            return carry

        lax.fori_loop(0, trips, body, 0)
        done = done + trips * width
        width //= 2

    @pl.when(i % 2 == 0)
    def _():
        consume(i, 0, True)

    @pl.when(i % 2 == 1)
    def _():
        scores(i, 1)
        consume(i - 1, 0, False)
        consume(i, 1, True)


def _mlp_kernel(x_ref, g_ref, w1_ref, w2_ref, *rest, nf, final):
    if final:
        fg_ref, o_ref, xn_ref = rest
    else:
        o_ref, xn_ref = rest
    f = pl.program_id(1)

    @pl.when(f == 0)
    def _():
        x = x_ref[...]
        xn_ref[...] = _rms(x, g_ref[...]).astype(BF16)
        o_ref[...] = x

    a = _dot(xn_ref[...], w1_ref[...])
    a = jnp.square(jnp.maximum(a, 0.0)).astype(BF16)
    o_ref[...] += _dot(a, w2_ref[...])

    if final:
        @pl.when(f == nf - 1)
        def _():
            o_ref[...] = _rms(o_ref[...], fg_ref[...])


def _mlp(h, g, w1, w2, final_g=None):
    S, D = h.shape
    F = w1.shape[1]
    tm = min(1024, S)
    tf = 2048
    nf = F // tf
    final = final_g is not None
    in_specs = [
        pl.BlockSpec((tm, D), lambda i, f: (i, 0)),
        pl.BlockSpec((1, D), lambda i, f: (0, 0)),
        pl.BlockSpec((D, tf), lambda i, f: (0, f)),
        pl.BlockSpec((tf, D), lambda i, f: (f, 0)),
    ]
    args = [h, g.reshape(1, D), w1, w2]
    if final:
        in_specs.append(pl.BlockSpec((1, D), lambda i, f: (0, 0)))
        args.append(final_g.reshape(1, D))
    return pl.pallas_call(
        functools.partial(_mlp_kernel, nf=nf, final=final),
        grid=(S // tm, nf),
        in_specs=in_specs,
        out_specs=pl.BlockSpec((tm, D), lambda i, f: (i, 0)),
        out_shape=jax.ShapeDtypeStruct((S, D), F32),
        scratch_shapes=[pltpu.VMEM((tm, D), BF16)],
        compiler_params=_cparams(("parallel", "arbitrary")),
        name="mlp",
    )(*args)


def _proj_kernel(x_ref, g_ref, w_ref, *rest, key_major, rope_blocks):
    if rope_blocks:
        rc_ref, rs1_ref, rs2_ref, o_ref, xn_ref = rest
    else:
        o_ref, xn_ref = rest

    @pl.when(pl.program_id(1) == 0)
    def _():
        xn_ref[...] = _rms(x_ref[...], g_ref[...]).astype(BF16)

    a = _dot_nt(w_ref[...], xn_ref[...]) if key_major else _dot(xn_ref[...], w_ref[...])
    if not rope_blocks:
        o_ref[...] = a.astype(o_ref.dtype)
        return
    axis = 0 if key_major else 1
    rc, rs1, rs2 = rc_ref[...], rs1_ref[...], rs2_ref[...]
    blocks = [lax.slice_in_dim(a, b * LANES, (b + 1) * LANES, axis=axis) for b in range(a.shape[axis] // LANES)]
    blocks = [_rope(blk, rc, rs1, rs2, axis) if b in rope_blocks else blk for b, blk in enumerate(blocks)]
    o_ref[...] = jnp.concatenate(blocks, axis=axis).astype(o_ref.dtype)


def _proj(h, g, w, name, *, key_major, tn, rope=None, rope_blocks=()):
    S, D = h.shape
    tm = min(1024, S)
    if key_major:
        N = w.shape[0]
        w_spec = pl.BlockSpec((tn, D), lambda i, j: (j, 0))
        rope_spec = pl.BlockSpec((LANES, tm), lambda i, j: (0, i))
        out_spec = pl.BlockSpec((tn, tm), lambda i, j: (j, i))
        out_shape = jax.ShapeDtypeStruct((N, S), BF16)
    else:
        N = w.shape[1]
        w_spec = pl.BlockSpec((D, tn), lambda i, j: (0, j))
        rope_spec = pl.BlockSpec((tm, LANES), lambda i, j: (i, 0))
        out_spec = pl.BlockSpec((tm, tn), lambda i, j: (i, j))
        out_shape = jax.ShapeDtypeStruct((S, N), BF16)
    in_specs = [pl.BlockSpec((tm, D), lambda i, j: (i, 0)), pl.BlockSpec((1, D), lambda i, j: (0, 0)), w_spec]
    args = [h, g.reshape(1, D), w]
    if rope_blocks:
        assert tn == N
        in_specs += [rope_spec] * 3
        args += list(rope)
    return pl.pallas_call(
        functools.partial(_proj_kernel, key_major=key_major, rope_blocks=tuple(rope_blocks)),
        grid=(S // tm, N // tn),
        in_specs=in_specs,
        out_specs=out_spec,
        out_shape=out_shape,
        scratch_shapes=[pltpu.VMEM((tm, D), BF16)],
        compiler_params=_cparams(("parallel", "arbitrary")),
        name=name,
    )(*args)


def _matmul_res_kernel(at_ref, w_ref, r_ref, o_ref):
    o_ref[...] = r_ref[...] + _dot_tn(at_ref[...], w_ref[...])


def _matmul_res(at, w, res, name):
    K, S = at.shape
    N = w.shape[1]
    tm = min(1024, S)
    return pl.pallas_call(
        _matmul_res_kernel,
        grid=(S // tm,),
        in_specs=[
            pl.BlockSpec((K, tm), lambda i: (0, i)),
            pl.BlockSpec((K, N), lambda i: (0, 0)),
            pl.BlockSpec((tm, N), lambda i: (i, 0)),
        ],
        out_specs=pl.BlockSpec((tm, N), lambda i: (i, 0)),
        out_shape=jax.ShapeDtypeStruct((S, N), F32),
        compiler_params=_cparams(("parallel",)),
        name=name,
    )(at, w, res)


def _fox_gate_kernel(x_ref, g_ref, wf_ref, bf_ref, place_ref, c_ref, carry_ref, *, tm):
    @pl.when(pl.program_id(0) == 0)
    def _():
        carry_ref[...] = jnp.zeros_like(carry_ref)

    xn = _rms(x_ref[...], g_ref[...]).astype(BF16)
    z = _dot(xn, wf_ref[...]) + bf_ref[...]
    logf = jnp.minimum(z, 0.0) - jnp.log(1.0 + jnp.exp(-jnp.abs(z)))
    row = lax.broadcasted_iota(jnp.int32, (tm, tm), 0)
    col = lax.broadcasted_iota(jnp.int32, (tm, tm), 1)
    tri = jnp.where(row >= col, 1.0, 0.0).astype(BF16)
    hi, mid, lo = _split3(logf)
    c = _dot(tri, hi) + _dot(tri, mid) + _dot(tri, lo) + carry_ref[...]
    carry_ref[...] = c[tm - 1:tm, :]
    terms = jnp.concatenate(_split3(c * -LOG2E), axis=1)
    c_ref[...] = _dot(terms, place_ref[...]).astype(c_ref.dtype)


def _fox_gate(h, g, wf, bf, place):
    S, D = h.shape
    tm = min(512, S)
    return pl.pallas_call(
        functools.partial(_fox_gate_kernel, tm=tm),
        grid=(S // tm,),
        in_specs=[
            pl.BlockSpec((tm, D), lambda i: (i, 0)),
            pl.BlockSpec((1, D), lambda i: (0, 0)),
            pl.BlockSpec((D, LANES), lambda i: (0, 0)),
            pl.BlockSpec((1, LANES), lambda i: (0, 0)),
            pl.BlockSpec((FOX_BIAS_TERMS * LANES, LANES), lambda i: (0, 0)),
        ],
        out_specs=pl.BlockSpec((tm, LANES), lambda i: (i, 0)),
        out_shape=jax.ShapeDtypeStruct((S, LANES), BF16),
        scratch_shapes=[pltpu.VMEM((1, LANES), F32)],
        compiler_params=_cparams(("arbitrary",)),
        name="fox_gate",
    )(h, g.reshape(1, D), wf, bf, place)


def _fox_attn_kernel(qt_ref, k_ref, c_ref, vt_ref, ot_ref, s0_ref, s1_ref, m_ref, acc_ref, *, t):
    hp = pl.program_id(0)
    i = pl.program_id(1)
    row = lax.broadcasted_iota(jnp.int32, (LANES, t), 0)
    q_heads = _head_rows(qt_ref[...])
    qa = []
    for h in range(2):
        first = FOX_BIAS_TERMS * (2 * hp + h)
        ones_rows = jnp.logical_and(row >= first, row < first + FOX_BIAS_TERMS)
        qa.append(jnp.concatenate([q_heads[h], jnp.where(ones_rows, 1.0, 0.0).astype(BF16)], axis=0))
    m_ref[...] = jnp.full(m_ref.shape, NEG, F32)
    acc_ref[...] = jnp.zeros(acc_ref.shape, F32)
    s_bufs = (s0_ref, s1_ref)

    def scores(j, buf):
        rows = pl.ds(pl.multiple_of(j * t, t), t)
        lhs = jnp.concatenate([k_ref[rows, :], c_ref[rows, :]], axis=1)
        for h in range(2):
            s_bufs[buf][h] = _dot(lhs, qa[h])

    def consume(j, buf, diag):
        vt = _with_ones_rows(vt_ref[:, pl.ds(pl.multiple_of(j * t, t), t)])
        for h in range(2):
            _online_update(s_bufs[buf], vt, m_ref, acc_ref, h, diag)

    _pipelined_tiles(i, scores, consume, pairs_per_step=8)
    ot = jnp.where(row < HEAD_DIM, _online_result(acc_ref, 0), _online_result(acc_ref, 1))
    ot_ref[...] = ot.astype(ot_ref.dtype)


def _fox_attn(qvt, k, c_terms):
    S = k.shape[0]
    t = min(FOX_TILE, S)
    n_pairs = N_HEADS // 2
    return pl.pallas_call(
        functools.partial(_fox_attn_kernel, t=t),
        grid=(n_pairs, S // t),
        in_specs=[
            pl.BlockSpec((LANES, t), lambda hp, i: (hp, i)),
            pl.BlockSpec((S, LANES), lambda hp, i: (0, hp)),
            pl.BlockSpec((S, LANES), lambda hp, i: (0, 0)),
            pl.BlockSpec((LANES, S), lambda hp, i: (n_pairs + hp, 0)),
        ],
        out_specs=pl.BlockSpec((LANES, t), lambda hp, i: (hp, i)),
        out_shape=jax.ShapeDtypeStruct((D_MODEL, S), BF16),
        scratch_shapes=([pltpu.VMEM((2, t, t), F32)] * 2 + [pltpu.VMEM((2, 1, t), F32)]
                        + [pltpu.VMEM((2, LANES + SUM_ROWS, t), F32)]),
        compiler_params=_cparams(("parallel", "arbitrary")),
        name="fox_attn",
    )(qvt, k, c_terms, qvt)


def _fox_bias_placement():
    place = np.zeros((FOX_BIAS_TERMS * LANES, LANES), np.float32)
    for head in range(N_HEADS):
        for n in range(FOX_BIAS_TERMS):
            place[n * LANES + head, FOX_BIAS_TERMS * head + n] = 1.0
    return jnp.asarray(place, BF16)


def _fox_layer(h, g, w_qkv, w_f, b_f, w_o):
    wf = jnp.pad(w_f, ((0, 0), (0, LANES - N_HEADS))).astype(BF16)
    bf = jnp.pad(b_f, (0, LANES - N_HEADS)).reshape(1, LANES)
    c_terms = _fox_gate(h, g, wf, bf, _fox_bias_placement())
    w_qv = jnp.concatenate([w_qkv[:, :D_MODEL] * QK_SCALE, w_qkv[:, 2 * D_MODEL:]], axis=1).T.astype(BF16)
    qvt = _proj(h, g, w_qv, "fox_qv", key_major=True, tn=2 * D_MODEL)
    k = _proj(h, g, w_qkv[:, D_MODEL:2 * D_MODEL].astype(BF16), "fox_k", key_major=False, tn=1024)
    ot = _fox_attn(qvt, k, c_terms)
    return _matmul_res(ot, w_o.astype(BF16), h, "fox_out")


def _pool_kernel(x_ref, halo_ref, g_ref, w_ref, sc_ref, o_ref, *, tm):
    i = pl.program_id(0)
    x = x_ref[...]
    g = g_ref[...]
    xn = _rms(x, g)
    hn = jnp.where(i > 0, _rms(halo_ref[...], g), 0.0)
    xe = jnp.concatenate([hn, xn], axis=0)
    tpos = i * tm + lax.broadcasted_iota(jnp.int32, (tm, 1), 0)
    for gi, w in enumerate(POOL_WINDOWS):
        sl = slice(gi * POOL_GROUP, (gi + 1) * POOL_GROUP)
        s = xe[:, sl]
        k = 1
        while k < w:
            s = s + pltpu.roll(s, k, 0)
            k *= 2
        cnt = jnp.minimum(tpos + 1, w).astype(F32)
        d = (s[POOL_HALO:, :] / cnt - xn[:, sl]).astype(BF16)
        o_ref[:, sl] = x[:, sl] + _dot(d, w_ref[gi]) * sc_ref[:, sl]


def _pool_layer(h, g, w_pool, pool_scale):
    S, D = h.shape
    tm = min(1024, S)
    return pl.pallas_call(
        functools.partial(_pool_kernel, tm=tm),
        grid=(S // tm,),
        in_specs=[
            pl.BlockSpec((tm, D), lambda i: (i, 0)),
            pl.BlockSpec((POOL_HALO, D), lambda i: (jnp.maximum(i * (tm // POOL_HALO) - 1, 0), 0)),
            pl.BlockSpec((1, D), lambda i: (0, 0)),
            pl.BlockSpec((len(POOL_WINDOWS), POOL_GROUP, POOL_GROUP), lambda i: (0, 0, 0)),
            pl.BlockSpec((1, D), lambda i: (0, 0)),
        ],
        out_specs=pl.BlockSpec((tm, D), lambda i: (i, 0)),
        out_shape=jax.ShapeDtypeStruct((S, D), F32),
        compiler_params=_cparams(("parallel",)),
        name="pool",
    )(h, h, g.reshape(1, D), w_pool.astype(BF16), pool_scale.reshape(1, D))


def _conv_in_kernel(x_ref, g_ref, wb_ref, wc_ref, wu_ref, b_ref, z_ref, xn_ref):
    @pl.when(pl.program_id(1) == 0)
    def _():
        xn_ref[...] = _rms(x_ref[...], g_ref[...]).astype(BF16)

    xn = xn_ref[...]
    b_ref[...] = _dot(xn, wb_ref[...])
    z_ref[...] = _dot(xn, wc_ref[...]) * _dot(xn, wu_ref[...])


def _conv_out_kernel(b_ref, z_ref, zh_ref, cw_ref, w_ref, r_ref, o_ref, *, tm):
    i = pl.program_id(0)
    z = z_ref[...]
    zh = jnp.where(i > 0, zh_ref[...], 0.0)
    row = lax.broadcasted_iota(jnp.int32, (tm, 1), 0)
    prev1 = zh[CONV_HALO - 1:CONV_HALO, :]
    prev2 = zh[CONV_HALO - 2:CONV_HALO - 1, :]
    z1 = jnp.where(row == 0, prev1, pltpu.roll(z, 1, 0))
    z2 = jnp.where(row == 0, prev2, jnp.where(row == 1, prev1, pltpu.roll(z, 2, 0)))
    cw = cw_ref[...]
    conv = cw[0:1, :] * z2 + cw[1:2, :] * z1 + cw[2:3, :] * z
    y = (b_ref[...] * conv).astype(BF16)
    o_ref[...] = r_ref[...] + _dot(y, w_ref[...])


def _conv_layer(h, g, w_in, conv_w, w_out):
    S, D = h.shape
    tm = min(1024, S)
    tn = 1024
    nj = D // tn
    w_in = w_in.astype(BF16)
    b, z = pl.pallas_call(
        _conv_in_kernel,
        grid=(S // tm, nj),
        in_specs=[
            pl.BlockSpec((tm, D), lambda i, j: (i, 0)),
            pl.BlockSpec((1, D), lambda i, j: (0, 0)),
            pl.BlockSpec((D, tn), lambda i, j: (0, j)),
            pl.BlockSpec((D, tn), lambda i, j: (0, nj + j)),
            pl.BlockSpec((D, tn), lambda i, j: (0, 2 * nj + j)),
        ],
        out_specs=[pl.BlockSpec((tm, tn), lambda i, j: (i, j)),
                   pl.BlockSpec((tm, tn), lambda i, j: (i, j))],
        out_shape=[jax.ShapeDtypeStruct((S, D), F32), jax.ShapeDtypeStruct((S, D), F32)],
        scratch_shapes=[pltpu.VMEM((tm, D), BF16)],
        compiler_params=_cparams(("parallel", "arbitrary")),
        name="conv_in",
    )(h, g.reshape(1, D), w_in, w_in, w_in)
    cw = jnp.pad(conv_w, ((0, 8 - CONV_WIDTH), (0, 0)))
    return pl.pallas_call(
        functools.partial(_conv_out_kernel, tm=tm),
        grid=(S // tm,),
        in_specs=[
            pl.BlockSpec((tm, D), lambda i: (i, 0)),
            pl.BlockSpec((tm, D), lambda i: (i, 0)),
            pl.BlockSpec((CONV_HALO, D), lambda i: (jnp.maximum(i * (tm // CONV_HALO) - 1, 0), 0)),
            pl.BlockSpec((8, D), lambda i: (0, 0)),
            pl.BlockSpec((D, D), lambda i: (0, 0)),
            pl.BlockSpec((tm, D), lambda i: (i, 0)),
        ],
        out_specs=pl.BlockSpec((tm, D), lambda i: (i, 0)),
        out_shape=jax.ShapeDtypeStruct((S, D), F32),
        compiler_params=_cparams(("parallel",)),
        name="conv_out",
    )(b, z, z, cw, w_out.astype(BF16), h)


def _nsa_cmp_kernel(r_ref, w1_ref, pe_ref, w2_ref, rc_ref, rs1_ref, rs2_ref, o_ref, acc_ref, *, nc):
    kv = pl.program_id(0)
    hd = pl.program_id(1)
    half = CMP_STRIDE * HEAD_DIM
    r = r_ref[0, 0]
    w1 = w1_ref[0]
    first = _dot(r, w1[:half, :])
    second = _dot(r, w1[half:, :])
    pe_term = _dot(pe_ref[0], w1)[0:1, :]
    pre = first + pltpu.roll(second, nc - 1, 0) + pe_term
    ge = 0.5 * pre * (1.0 + jnp.tanh(0.7978845608028654 * (pre + 0.044715 * pre * pre * pre)))
    y = _dot(ge.astype(BF16), w2_ref[0, 0])

    @pl.when(hd == 0)
    def _():
        acc_ref[...] = y

    @pl.when(hd > 0)
    def _():
        acc_ref[...] += y

    @pl.when(hd == NSA_KV_HEADS - 1)
    def _():
        acc = acc_ref[...]

        @pl.when(kv == 0)
        def _():
            rc, rs1, rs2 = rc_ref[...], rs1_ref[...], rs2_ref[...]
            o_ref[0] = jnp.concatenate(
                [_rope(acc[:, :LANES], rc, rs1, rs2, 1), _rope(acc[:, LANES:], rc, rs1, rs2, 1)],
                axis=1).astype(o_ref.dtype)

        @pl.when(kv == 1)
        def _():
            o_ref[0] = acc.astype(o_ref.dtype)


def _nsa_cmp(r, w1, pe, w2p, rc, rs1, rs2):
    nc = r.shape[2]
    kvd = NSA_KV_HEADS * HEAD_DIM
    return pl.pallas_call(
        functools.partial(_nsa_cmp_kernel, nc=nc),
        grid=(2, NSA_KV_HEADS),
        in_specs=[
            pl.BlockSpec((1, 1, nc, CMP_STRIDE * HEAD_DIM), lambda a, b: (a, b, 0, 0)),
            pl.BlockSpec((1, CMP_BLOCK * HEAD_DIM, CMP_HIDDEN), lambda a, b: (a, 0, 0)),
            pl.BlockSpec((1, 8, CMP_BLOCK * HEAD_DIM), lambda a, b: (a, 0, 0)),
            pl.BlockSpec((1, 1, CMP_HIDDEN, kvd), lambda a, b: (a, b, 0, 0)),
            pl.BlockSpec((nc, LANES), lambda a, b: (0, 0)),
            pl.BlockSpec((nc, LANES), lambda a, b: (0, 0)),
            pl.BlockSpec((nc, LANES), lambda a, b: (0, 0)),
        ],
        out_specs=pl.BlockSpec((1, nc, kvd), lambda a, b: (a, 0, 0)),
        out_shape=jax.ShapeDtypeStruct((2, nc, kvd), BF16),
        scratch_shapes=[pltpu.VMEM((nc, kvd), F32)],
        compiler_params=_cparams(("parallel", "arbitrary")),
        name="nsa_cmp",
    )(r, w1, pe, w2p, rc, rs1, rs2)


def _nsa_cmp_attn_body(i, qt_ref, kc_ref, vct_ref, w_ref, oct_ref, selt_ref, *, t, n_tiles, ct, nsp):
    qpos = i * t + lax.broadcasted_iota(jnp.int32, (1, t), 1)
    any_valid = qpos >= CMP_BLOCK - 1
    low = lax.broadcasted_iota(jnp.int32, (LANES, t), 0) < HEAD_DIM
    per_tile = ct * CMP_STRIDE // SLC_BLOCK
    stat_rows = jnp.concatenate([jnp.ones((SUM_ROWS, ct), BF16), w_ref[...]], axis=0)
    cmasks = [CMP_STRIDE * (c * ct + lax.broadcasted_iota(jnp.int32, (ct, t), 0)) + (CMP_BLOCK - 1) <= qpos
              for c in range(n_tiles)]
    blk = lax.broadcasted_iota(jnp.int32, (nsp, t), 0)
    blk_f = blk.astype(F32)
    cur = jnp.right_shift(qpos, SLC_SHIFT)
    forced = jnp.logical_or(blk == 0, jnp.logical_or(blk == cur, blk == cur - 1))
    valid = blk * SLC_BLOCK <= qpos
    vals = []
    for pair in range(NSA_KV_HEADS // 2):
        lanes = slice(pair * LANES, (pair + 1) * LANES)
        lhs = [jnp.concatenate([vct_ref[lanes, c * ct:(c + 1) * ct], stat_rows], axis=0) for c in range(n_tiles)]
        imp = [[None] * n_tiles, [None] * n_tiles]
        for g in range(NSA_GROUP):
            rows = slice((pair * NSA_GROUP + g) * LANES, (pair * NSA_GROUP + g + 1) * LANES)
            qh = _head_rows(qt_ref[rows, :])
            oc = []
            for h in range(2):
                m = jnp.full((1, t), NEG, F32)
                acc = jnp.zeros((LANES + SUM_ROWS, t), F32)
                parts, maxes = [], []
                for c in range(n_tiles):
                    s = jnp.where(cmasks[c], _dot(kc_ref[0, c * ct:(c + 1) * ct, lanes], qh[h]), NEG)
                    m_new = jnp.maximum(m, jnp.max(s, axis=0, keepdims=True))
                    p = jnp.exp2(s - m_new)
                    p_hi = p.astype(BF16)
                    p_lo = (p - p_hi.astype(F32)).astype(BF16)
                    r_hi = _dot(lhs[c], p_hi)
                    r_lo = _dot(stat_rows, p_lo)
                    stats = r_hi[LANES:, :] + r_lo
                    acc = jnp.exp2(m - m_new) * acc + jnp.concatenate([r_hi[:LANES, :], stats[:SUM_ROWS, :]], axis=0)
                    parts.append(stats[SUM_ROWS:, :])
                    maxes.append(m_new)
                    m = m_new
                inv = jnp.where(any_valid, 1.0 / acc[LANES:LANES + 1, :], 0.0)
                oc.append(acc[:LANES, :] * inv)
                for c in range(n_tiles):
                    piece = parts[c] * (jnp.exp2(maxes[c] - m) * inv)
                    imp[h][c] = piece if imp[h][c] is None else imp[h][c] + piece
            oct_ref[rows, :] = jnp.where(low, oc[0], oc[1]).astype(oct_ref.dtype)
        for h in range(2):
            segs = []
            for c in range(n_tiles):
                seg = imp[h][c][:per_tile, :]
                if c > 0:
                    spill = imp[h][c - 1][per_tile:per_tile + IMP_PAD, :]
                    seg = jnp.concatenate([seg[:IMP_PAD, :] + spill, seg[IMP_PAD:, :]], axis=0)
                segs.append(seg)
            score = jnp.concatenate(segs, axis=0)[:nsp, :]
            vals.append(jnp.where(valid, jnp.where(forced, FORCE_SCORE, score), NEG))

    def pick_one(_, vals):
        out = []
        for v in vals:
            mx = jnp.max(v, axis=0, keepdims=True)
            first = jnp.min(jnp.where(v == mx, blk_f, float(nsp)), axis=0, keepdims=True)
            out.append(jnp.where(blk_f == first, -jnp.inf, v))
        return tuple(out)

    vals = lax.fori_loop(0, SLC_TOPK, pick_one, tuple(vals))
    for kvh in range(NSA_KV_HEADS):
        picked = jnp.logical_and(valid, vals[kvh] == -jnp.inf)
        selt_ref[kvh, :nsp, :] = jnp.where(picked, 1.0, 0.0).astype(selt_ref.dtype)
        if nsp < selt_ref.shape[1]:
            selt_ref[kvh, nsp:, :] = jnp.zeros((selt_ref.shape[1] - nsp, t), selt_ref.dtype)


def _nsa_cmp_attn_kernel(qt_ref, kc_ref, vct_ref, w_ref, oct_ref, selt_ref, *, t, n_steps, n_spans):
    i = pl.program_id(0)
    ct = w_ref.shape[1]
    for q in range(n_spans):
        @pl.when(jnp.logical_and(i >= q * n_steps // n_spans, i < (q + 1) * n_steps // n_spans))
        def _(q=q):
            _nsa_cmp_attn_body(i, qt_ref, kc_ref, vct_ref, w_ref, oct_ref, selt_ref, t=t, n_tiles=q + 1, ct=ct,
                               nsp=(q + 1) * ct * CMP_STRIDE // SLC_BLOCK)


def _nsa_cmp_attn(proj_t, kvc, vct, w_imp, nsp):
    S = proj_t.shape[1]
    t = NSA_TILE
    nc = kvc.shape[1]
    imp_rows, ct = w_imp.shape
    kvd = NSA_KV_HEADS * HEAD_DIM
    return pl.pallas_call(
        functools.partial(_nsa_cmp_attn_kernel, t=t, n_steps=S // t, n_spans=nc // ct),
        grid=(S // t,),
        in_specs=[
            pl.BlockSpec((D_MODEL, t), lambda i: (0, i)),
            pl.BlockSpec((1, nc, kvd), lambda i: (0, 0, 0)),
            pl.BlockSpec((kvd, nc), lambda i: (0, 0)),
            pl.BlockSpec((imp_rows, ct), lambda i: (0, 0)),
        ],
        out_specs=[pl.BlockSpec((D_MODEL, t), lambda i: (0, i)),
                   pl.BlockSpec((NSA_KV_HEADS, nsp, t), lambda i: (0, 0, i))],
        out_shape=[jax.ShapeDtypeStruct((D_MODEL, S), BF16),
                   jax.ShapeDtypeStruct((NSA_KV_HEADS, nsp, S), BF16)],
        compiler_params=_cparams(("parallel",)),
        name="nsa_cmp_attn",
    )(proj_t, kvc, vct, w_imp)


def _nsa_sel_kernel(qt_ref, ks_ref, vst_ref, selt_ref, kw0_ref, kw1_ref, kw2_ref, vwt0_ref, vwt1_ref, vwt2_ref,
                    gzt_ref, et_ref, oct_ref, ot_ref, s0_ref, s1_ref, bias_ref, m_ref, acc_ref, *, t):
    i = pl.program_id(1)
    n_heads = 2 * NSA_GROUP
    low = lax.broadcasted_iota(jnp.int32, (LANES, t), 0) < HEAD_DIM
    qs = []
    for g in range(NSA_GROUP):
        qs.extend(_head_rows(qt_ref[g * LANES:(g + 1) * LANES, :]))
    bias_ref[...] = ((selt_ref[...].astype(F32) - 1.0) * -NEG).astype(BF16)
    m_ref[...] = jnp.full(m_ref.shape, NEG, F32)
    acc_ref[...] = jnp.zeros(acc_ref.shape, F32)
    tok_blk = jnp.right_shift(lax.broadcasted_iota(jnp.int32, (t, LANES), 0), SLC_SHIFT)
    lane_col = lax.broadcasted_iota(jnp.int32, (t, LANES), 1)
    zero_rows = jnp.zeros((LANES - SEL_GROUP, t), BF16)
    tiles_per_group = SEL_GROUP * SLC_BLOCK // t
    s_bufs = (s0_ref, s1_ref)

    def scores(j, buf):
        k = ks_ref[pl.ds(pl.multiple_of(j * t, t), t), :]
        grp = j // tiles_per_group
        first_blk = (j % tiles_per_group) * (t // SLC_BLOCK)
        expand = jnp.where(lane_col == first_blk + tok_blk, 1.0, 0.0).astype(BF16)
        lhs = jnp.concatenate([k, expand], axis=1)
        for h in range(2):
            blk_bias = bias_ref[h, grp]
            for g in range(NSA_GROUP):
                rhs = jnp.concatenate([qs[2 * g + h], blk_bias, zero_rows], axis=0)
                s_bufs[buf][2 * g + h] = _dot(lhs, rhs)

    def consume(j, buf, diag):
        vt = _with_ones_rows(vst_ref[:, pl.ds(pl.multiple_of(j * t, t), t)])
        for idx in range(n_heads):
            _online_update(s_bufs[buf], vt, m_ref, acc_ref, idx, diag)

    _pipelined_tiles(i, scores, consume, pairs_per_step=2)

    n_win = WIN // t + 1
    k_win = jnp.concatenate([kw0_ref[...], kw1_ref[...], kw2_ref[...]], axis=0)
    vt_win = jnp.concatenate([vwt0_ref[...], vwt1_ref[...], vwt2_ref[...]], axis=1)
    kpos = (i - (n_win - 1)) * t + lax.broadcasted_iota(jnp.int32, (n_win * t, t), 0)
    qpos = i * t + lax.broadcasted_iota(jnp.int32, (n_win * t, t), 1)
    wmask = jnp.logical_and(jnp.logical_and(kpos <= qpos, kpos > qpos - WIN), kpos >= 0)
    ow = []
    for idx in range(n_heads):
        s = jnp.where(wmask, _dot(k_win, qs[idx]), NEG)
        p = jnp.exp2(s - jnp.max(s, axis=0, keepdims=True))
        l = jnp.sum(p, axis=0, keepdims=True)
        ow.append(_dot(vt_win, p.astype(BF16)) / l)

    gates = jax.nn.sigmoid(_dot(et_ref[0], gzt_ref[...]))
    gw = NSA_GROUP * LANES
    for g in range(NSA_GROUP):
        rows = slice(g * LANES, (g + 1) * LANES)
        o_cmp = oct_ref[rows, :].astype(F32)
        o_slc = jnp.where(low, _online_result(acc_ref, 2 * g), _online_result(acc_ref, 2 * g + 1))
        o_win = jnp.where(low, ow[2 * g], ow[2 * g + 1])
        out = (gates[g * LANES:(g + 1) * LANES, :] * o_cmp
               + gates[gw + g * LANES:gw + (g + 1) * LANES, :] * o_slc
               + gates[2 * gw + g * LANES:2 * gw + (g + 1) * LANES, :] * o_win)
        ot_ref[rows, :] = out.astype(ot_ref.dtype)


def _nsa_sel(nat, proj_t, selt, oct, gate_expand_t):
    S = nat.shape[0]
    t = NSA_TILE
    n_grp = selt.shape[1]
    n_heads = 2 * NSA_GROUP
    gl = NSA_GROUP * LANES

    def kwin_spec(back):
        return pl.BlockSpec((t, LANES), lambda p, i: (jnp.maximum(i - back, 0), NAT_K_WIN + p))

    def vwin_spec(back):
        return pl.BlockSpec((LANES, t), lambda p, i: (T_V_WIN + p, jnp.maximum(i - back, 0)))

    return pl.pallas_call(
        functools.partial(_nsa_sel_kernel, t=t),
        grid=(NSA_KV_HEADS // 2, S // t),
        in_specs=[
            pl.BlockSpec((gl, t), lambda p, i: (p, i)),
            pl.BlockSpec((S, LANES), lambda p, i: (0, NAT_K_SLC + p)),
            pl.BlockSpec((LANES, S), lambda p, i: (T_V_SLC + p, 0)),
            pl.BlockSpec((2, n_grp, SEL_GROUP, t), lambda p, i: (p, 0, 0, i)),
            kwin_spec(2), kwin_spec(1), kwin_spec(0),
            vwin_spec(2), vwin_spec(1), vwin_spec(0),
            pl.BlockSpec((NSA_PROJ_TILE, t), lambda p, i: (T_GATE * LANES // NSA_PROJ_TILE, i)),
            pl.BlockSpec((1, 3 * gl, NSA_PROJ_TILE), lambda p, i: (p, 0, 0)),
            pl.BlockSpec((gl, t), lambda p, i: (p, i)),
        ],
        out_specs=pl.BlockSpec((gl, t), lambda p, i: (p, i)),
        out_shape=jax.ShapeDtypeStruct((D_MODEL, S), BF16),
        scratch_shapes=([pltpu.VMEM((n_heads, t, t), F32)] * 2 + [pltpu.VMEM((2, n_grp, SEL_GROUP, t), BF16)]
                        + [pltpu.VMEM((n_heads, 1, t), F32)] + [pltpu.VMEM((n_heads, LANES + SUM_ROWS, t), F32)]),
        compiler_params=_cparams(("parallel", "arbitrary")),
        name="nsa_sel",
    )(proj_t, nat, proj_t, selt, nat, nat, nat, proj_t, proj_t, proj_t, proj_t, gate_expand_t, oct)


def _rope_tables(positions, key_major=False):
    half = ROT_DIM // 2
    inv = ROPE_THETA ** (-jnp.arange(0, ROT_DIM, 2, dtype=F32) / ROT_DIM)
    pos = positions.astype(F32)
    n = positions.shape[0]
    axis = 0 if key_major else 1
    ang = inv[:, None] * pos[None, :] if key_major else pos[:, None] * inv[None, :]
    cos, sin = jnp.cos(ang), jnp.sin(ang)

    def const(width, value):
        return jnp.full((width, n) if key_major else (n, width), value, F32)

    rest = HEAD_DIM - ROT_DIM
    rc = jnp.concatenate([cos, cos, const(rest, 1.0)], axis=axis)
    rs1 = jnp.concatenate([-sin, const(half, 0.0), const(rest, 0.0)], axis=axis)
    rs2 = jnp.concatenate([const(half, 0.0), sin, const(rest, 0.0)], axis=axis)
    reps = (LANES // HEAD_DIM, 1) if key_major else (1, LANES // HEAD_DIM)
    return jnp.tile(rc, reps), jnp.tile(rs1, reps), jnp.tile(rs2, reps)


def _pair_heads(w, axis):
    shape = w.shape
    w = w.reshape(shape[:axis] + (NSA_KV_HEADS // 2, 2, NSA_GROUP, HEAD_DIM) + shape[axis + 1:])
    w = jnp.swapaxes(w, axis + 1, axis + 2)
    return w.reshape(shape)


def _nsa_constants(S):
    nc = S // CMP_STRIDE
    ns = S // SLC_BLOCK
    nsp = -(-ns // LANES) * LANES
    ct = min(CMP_TILE, nc)
    imp_rows = -(-(ct * CMP_STRIDE // SLC_BLOCK + IMP_PAD) // SUM_ROWS) * SUM_ROWS
    ci = np.arange(ct)[None, :] * CMP_STRIDE
    st = np.arange(imp_rows)[:, None] * SLC_BLOCK
    w_imp = (ci < st + SLC_BLOCK) & (ci + CMP_BLOCK > st)
    e = np.zeros((NSA_KV_HEADS // 2, 3 * NSA_GROUP * LANES, NSA_PROJ_TILE), np.float32)
    for p in range(NSA_KV_HEADS // 2):
        for br in range(3):
            for g in range(NSA_GROUP):
                for hf in range(2):
                    src = br * N_HEADS + (2 * p + hf) * NSA_GROUP + g
                    r0 = br * NSA_GROUP * LANES + g * LANES + hf * HEAD_DIM
                    e[p, r0:r0 + HEAD_DIM, src] = 1.0
    cmp_end = np.minimum(np.arange(nc) * CMP_STRIDE + CMP_BLOCK - 1, S - 1)
    return nc, nsp, jnp.asarray(w_imp.astype(np.float32), BF16), jnp.asarray(e, BF16), cmp_end


def _nsa_layer(h, g, positions, w_in, pe_k, w1_k, w2_k, pe_v, w1_v, w2_v, w_o):
    S, D = h.shape
    qd = N_HEADS * HEAD_DIM
    kvd = NSA_KV_HEADS * HEAD_DIM
    nc, nsp, w_imp, gate_expand_t, cmp_end = _nsa_constants(S)
    rc, rs1, rs2 = _rope_tables(positions)

    def kv_piece(n):
        return w_in[:, qd + n * kvd:qd + (n + 1) * kvd]

    w_nat = jnp.concatenate([kv_piece(0), kv_piece(1), kv_piece(2), kv_piece(4)], axis=1).astype(BF16)
    wg = jnp.pad(w_in[:, qd + 6 * kvd:], ((0, 0), (0, NSA_T_ROWS - T_GATE * LANES - 3 * N_HEADS)))
    w_t = jnp.concatenate([_pair_heads(w_in[:, :qd], 1) * QK_SCALE, kv_piece(3), kv_piece(5), wg], axis=1).T.astype(BF16)
    nat = _proj(h, g, w_nat, "nsa_proj", key_major=False, tn=NSA_NAT_WIDTH,
                rope=(rc, rs1, rs2), rope_blocks=NSA_NAT_ROPE_BLOCKS)
    proj_t = _proj(h, g, w_t, "nsa_proj_t", key_major=True, tn=NSA_T_ROWS,
                   rope=_rope_tables(positions, key_major=True), rope_blocks=NSA_T_ROPE_BLOCKS)

    raw = nat[:, :2 * kvd].reshape(nc, CMP_STRIDE, 2, NSA_KV_HEADS, HEAD_DIM)
    raw = raw.transpose(2, 3, 0, 1, 4).reshape(2, NSA_KV_HEADS, nc, CMP_STRIDE * HEAD_DIM)
    w1 = jnp.stack([w1_k, w1_v]).astype(BF16)
    pe = jnp.stack([pe_k.reshape(1, -1), pe_v.reshape(1, -1)])
    pe = jnp.pad(pe, ((0, 0), (0, 7), (0, 0))).astype(BF16)
    w2 = jnp.stack([w2_k, w2_v])
    eye = jnp.eye(NSA_KV_HEADS, dtype=F32)
    w2p = (w2[:, None, :, None, :] * eye[None, :, None, :, None]).reshape(2, NSA_KV_HEADS, CMP_HIDDEN, kvd).astype(BF16)
    kvc = _nsa_cmp(raw, w1, pe, w2p, *_rope_tables(positions[cmp_end]))

    oct, selt = _nsa_cmp_attn(proj_t, kvc, kvc[1].T, w_imp, nsp)
    selt = selt.reshape(NSA_KV_HEADS, -1, SEL_GROUP, S)
    ot = _nsa_sel(nat, proj_t, selt, oct, gate_expand_t)
    return _matmul_res(ot, _pair_heads(w_o, 0).astype(BF16), h, "nsa_out")


def _trunk(x2, positions, p):
    h = _fox_layer(x2, p["l0_norm_mix"], p["l0_fox_w_qkv"], p["l0_fox_w_f"], p["l0_fox_b_f"], p["l0_fox_w_o"])
    h = _mlp(h, p["l0_norm_mlp"], p["l0_mlp_w1"].astype(BF16), p["l0_mlp_w2"].astype(BF16))
    h = _pool_layer(h, p["l1_norm_mix"], p["l1_pool_w"], p["l1_pool_scale"])
    h = _mlp(h, p["l1_norm_mlp"], p["l1_mlp_w1"].astype(BF16), p["l1_mlp_w2"].astype(BF16))
    h = _conv_layer(h, p["l2_norm_mix"], p["l2_conv_w_in"], p["l2_conv_w"], p["l2_conv_w_out"])
    h = _mlp(h, p["l2_norm_mlp"], p["l2_mlp_w1"].astype(BF16), p["l2_mlp_w2"].astype(BF16))
    h = _nsa_layer(h, p["l3_norm_mix"], positions, p["l3_nsa_w_in"], p["l3_nsa_cmp_pe_k"], p["l3_nsa_cmp_w1_k"],
                   p["l3_nsa_cmp_w2_k"], p["l3_nsa_cmp_pe_v"], p["l3_nsa_cmp_w1_v"], p["l3_nsa_cmp_w2_v"],
                   p["l3_nsa_w_o"])
    return _mlp(h, p["l3_norm_mlp"], p["l3_mlp_w1"].astype(BF16), p["l3_mlp_w2"].astype(BF16), p["final_norm"])


def kernel(x, positions, l0_norm_mix, l0_fox_w_qkv, l0_fox_w_f, l0_fox_b_f, l0_fox_w_o, l0_norm_mlp, l0_mlp_w1, l0_mlp_w2, l1_norm_mix, l1_pool_w, l1_pool_scale, l1_norm_mlp, l1_mlp_w1, l1_mlp_w2, l2_norm_mix, l2_conv_w_in, l2_conv_w, l2_conv_w_out, l2_norm_mlp, l2_mlp_w1, l2_mlp_w2, l3_norm_mix, l3_nsa_w_in, l3_nsa_cmp_pe_k, l3_nsa_cmp_w1_k, l3_nsa_cmp_w2_k, l3_nsa_cmp_pe_v, l3_nsa_cmp_w1_v, l3_nsa_cmp_w2_v, l3_nsa_w_o, l3_norm_mlp, l3_mlp_w1, l3_mlp_w2, final_norm):
    params = dict(locals())
    B, S, D = x.shape
    outs = [_trunk(x[b], positions, params) for b in range(B)]
    return jnp.stack(outs, axis=0)
```
